```python
import jax, jax.numpy as jnp
from jax import lax
import numpy as np

D_MODEL = 2048
BATCH = 8
SEQ = 2048
DEPTH = 4
DEC_BATCH = 32
DEC_SEQ = 16
PAST_LEN = 1024

CHUNK = 64
N_MIXERS = 4
MEM_LEN = 256
ROPE_THETA = 500000.0
EPS = 1e-6
NEG_INF = -1e30
QBLOCK = 128

MLA_HEADS = 16
MLA_NOPE = 128
MLA_ROPE = 64
MLA_V = 128
MLA_Q_LORA = 512
MLA_KV_LORA = 512
MLA_SCALE = (MLA_NOPE + MLA_ROPE) ** -0.5

BAND_HEADS = 16
BAND_HEAD_DIM = D_MODEL // BAND_HEADS
BAND_PREV_CHUNKS = 8
MAX_REL = 128

DSA_HEADS = 16
DSA_KV_HEADS = 4
DSA_HEAD_DIM = 128
DSA_ROT = DSA_HEAD_DIM // 4
IDX_HEADS = 16
IDX_DIM = 64
IDX_ROT = IDX_DIM // 4
DSA_TOPK = 256
DSA_QBLOCK = 64
DSA_OQ = DSA_HEADS * DSA_HEAD_DIM
DSA_OK = DSA_OQ + DSA_KV_HEADS * DSA_HEAD_DIM
DSA_OV = DSA_OK + DSA_KV_HEADS * DSA_HEAD_DIM
DSA_OIQ = DSA_OV + IDX_HEADS * IDX_DIM
DSA_OIK = DSA_OIQ + IDX_DIM
DSA_IN = DSA_OIK + IDX_HEADS

SWA_HEADS = 32
SWA_KV_HEADS = 8
SWA_HEAD_DIM = 64
SWA_ROT = SWA_HEAD_DIM // 4
SWA_WINDOW = 128
SWA_PREV_CHUNKS = SWA_WINDOW // CHUNK

XA_HEADS = 4
XA_HEAD_DIM = 128

D_FF = 5632
CONV_W = 3

kernel_name = 'hybrid_streaming_encoder_step'


def rms_norm(x, g):
    xf = x.astype(jnp.float32)
    y = xf * lax.rsqrt(jnp.mean(xf * xf, axis=-1, keepdims=True) + EPS)
    return (y * g.astype(jnp.float32)).astype(x.dtype)


def apply_rope(x, pos, rot):
    half = rot // 2
    inv = ROPE_THETA ** (-jnp.arange(half, dtype=jnp.float32) / half)
    ang = pos.astype(jnp.float32)[:, None] * inv[None, :]
    c, s = jnp.cos(ang)[:, None, :], jnp.sin(ang)[:, None, :]
    xf = x[..., :rot].astype(jnp.float32)
    x1, x2 = xf[..., :half], xf[..., half:]
    xr = jnp.concatenate([x1 * c - x2 * s, x2 * c + x1 * s], axis=-1).astype(x.dtype)
    return jnp.concatenate([xr, x[..., rot:]], axis=-1)


def chunk_visible(qpos, kpos):
    qc, kc = qpos[:, None] // CHUNK, kpos[None, :] // CHUNK
    return (kpos[None, :] >= 0) & (kc <= qc)


def band_visible(qpos, kpos, n_prev):
    qc, kc = qpos[:, None] // CHUNK, kpos[None, :] // CHUNK
    return (kpos[None, :] >= 0) & (kc <= qc) & (kc >= qc - n_prev)


def to_blocks(x, bs):
    return x.reshape((x.shape[0], x.shape[1] // bs, bs) + x.shape[2:]).swapaxes(0, 1)


def from_blocks(o):
    return o.swapaxes(0, 1).reshape((o.shape[1], o.shape[0] * o.shape[2]) + o.shape[3:])


def attend(q, k, v, mask=None, bias=None, sinks=None):
    B, Q, Hq, d = q.shape
    Hkv = k.shape[2]
    g = Hq // Hkv
    s = jnp.einsum('bqkgd,bskd->bkgqs', q.reshape(B, Q, Hkv, g, d), k).astype(jnp.float32) * d ** -0.5
    if bias is not None:
        s = s + bias.reshape(Hkv, g, Q, -1).astype(jnp.float32)
    if mask is not None:
        s = jnp.where(mask[None, None, None], s, NEG_INF)
    if sinks is None:
        p = jax.nn.softmax(s, axis=-1)
    else:
        sk = jnp.broadcast_to(sinks.astype(jnp.float32).reshape(1, Hkv, g, 1, 1), s.shape[:-1] + (1,))
        p = jax.nn.softmax(jnp.concatenate([s, sk], axis=-1), axis=-1)[..., :-1]
    o = jnp.einsum('bkgqs,bskd->bqkgd', p.astype(v.dtype), v)
    return o.reshape(B, Q, Hq, v.shape[-1])


def mla_project(h, pos, w_in, q_lora_norm, kv_lora_norm, w_uq, q_norm, k_norm):
    B, L, _ = h.shape
    u = h @ w_in
    cq = rms_norm(u[..., :MLA_Q_LORA], q_lora_norm)
    ckv = rms_norm(u[..., MLA_Q_LORA:MLA_Q_LORA + MLA_KV_LORA], kv_lora_norm)
    kr = u[..., MLA_Q_LORA + MLA_KV_LORA:]
    q = (cq @ w_uq).reshape(B, L, MLA_HEADS, MLA_NOPE + MLA_ROPE)
    qn = rms_norm(q[..., :MLA_NOPE], q_norm[:MLA_NOPE])
    qr = apply_rope(rms_norm(q[..., MLA_NOPE:], q_norm[MLA_NOPE:]), pos, MLA_ROPE)
    kr = apply_rope(rms_norm(kr, k_norm[MLA_NOPE:])[:, :, None], pos, MLA_ROPE)[:, :, 0]
    return qn, qr, ckv, kr


def mla_expand(ckv, w_ukv, k_norm):
    B, L, _ = ckv.shape
    kv = (ckv @ w_ukv).reshape(B, L, MLA_HEADS, MLA_NOPE + MLA_V)
    return rms_norm(kv[..., :MLA_NOPE], k_norm[:MLA_NOPE]), kv[..., MLA_NOPE:]


def mla_attend(qn, qr, kn, kr, v, mask):
    s = (jnp.einsum('bqhd,bkhd->bhqk', qn, kn) + jnp.einsum('bqhr,bkr->bhqk', qr, kr)).astype(jnp.float32) * MLA_SCALE
    s = jnp.where(mask[None, None], s, NEG_INF)
    p = jax.nn.softmax(s, axis=-1).astype(v.dtype)
    return jnp.einsum('bhqk,bkhd->bqhd', p, v)


def mla_prompt(qn, qr, ckv, kr, w_ukv, k_norm):
    L = qn.shape[1]
    kn, v = mla_expand(ckv, w_ukv, k_norm)
    kpos = jnp.arange(L, dtype=jnp.int32)

    def one_block(args):
        start, qn_b, qr_b = args
        qpos = start + jnp.arange(QBLOCK, dtype=jnp.int32)
        return mla_attend(qn_b, qr_b, kn, kr, v, chunk_visible(qpos, kpos))

    starts = jnp.arange(L // QBLOCK, dtype=jnp.int32) * QBLOCK
    return from_blocks(lax.map(one_block, (starts, to_blocks(qn, QBLOCK), to_blocks(qr, QBLOCK))))


def mla_sample(qn, qr, ckv_all, kr_all, qpos, w_ukv, k_norm):
    kn, v = mla_expand(ckv_all, w_ukv, k_norm)
    kpos = jnp.arange(ckv_all.shape[1], dtype=jnp.int32)
    return mla_attend(qn, qr, kn, kr_all, v, chunk_visible(qpos, kpos))


def rel_bias(table, qpos, kpos):
    idx = jnp.clip(qpos[:, None] - kpos[None, :], -MAX_REL, MAX_REL) + MAX_REL
    return jnp.transpose(table[idx], (2, 0, 1))


def band_prompt(q, k, v, n_prev, bias_table=None, sinks=None):
    L = q.shape[1]
    pad = n_prev * CHUNK
    width = pad + CHUNK
    kp = jnp.pad(k, ((0, 0), (pad, 0), (0, 0), (0, 0)))
    vp = jnp.pad(v, ((0, 0), (pad, 0), (0, 0), (0, 0)))

    def one_chunk(args):
        start, q_c = args
        qpos = start + jnp.arange(CHUNK, dtype=jnp.int32)
        kpos = start - pad + jnp.arange(width, dtype=jnp.int32)
        k_c = lax.dynamic_slice_in_dim(kp, start, width, axis=1)
        v_c = lax.dynamic_slice_in_dim(vp, start, width, axis=1)
        bias = None if bias_table is None else rel_bias(bias_table, qpos, kpos)
        return attend(q_c, k_c, v_c, band_visible(qpos, kpos, n_prev), bias, sinks)

    starts = jnp.arange(L // CHUNK, dtype=jnp.int32) * CHUNK
    return from_blocks(lax.map(one_chunk, (starts, to_blocks(q, CHUNK))))


def band_sample(q, k, v, k_cache, v_cache, past_len, n_prev, bias_table=None, sinks=None):
    n = q.shape[1]
    rows = k_cache.shape[1]
    k_all = jnp.concatenate([k_cache, k], axis=1)
    v_all = jnp.concatenate([v_cache, v], axis=1)
    qpos = past_len + jnp.arange(n, dtype=jnp.int32)
    kpos = past_len - rows + jnp.arange(rows + n, dtype=jnp.int32)
    bias = None if bias_table is None else rel_bias(bias_table, qpos, kpos)
    return attend(q, k_all, v_all, band_visible(qpos, kpos, n_prev), bias, sinks)


def band_project(h, w_in, q_norm, k_norm):
    B, L, _ = h.shape
    u = (h @ w_in).reshape(B, L, 3, BAND_HEADS, BAND_HEAD_DIM)
    return rms_norm(u[:, :, 0], q_norm), rms_norm(u[:, :, 1], k_norm), u[:, :, 2]


def swa_project(h, pos, w_in, q_norm, k_norm):
    B, L, _ = h.shape
    u = h @ w_in
    nq, nk = SWA_HEADS * SWA_HEAD_DIM, SWA_KV_HEADS * SWA_HEAD_DIM
    q = apply_rope(rms_norm(u[..., :nq].reshape(B, L, SWA_HEADS, SWA_HEAD_DIM), q_norm), pos, SWA_ROT)
    k = apply_rope(rms_norm(u[..., nq:nq + nk].reshape(B, L, SWA_KV_HEADS, SWA_HEAD_DIM), k_norm), pos, SWA_ROT)
    v = u[..., nq + nk:].reshape(B, L, SWA_KV_HEADS, SWA_HEAD_DIM)
    return q, k, v


def dsa_project(h, pos, w_in, q_norm, k_norm, idx_k_norm):
    B, L, _ = h.shape
    u = h @ w_in
    q = apply_rope(rms_norm(u[..., :DSA_OQ].reshape(B, L, DSA_HEADS, DSA_HEAD_DIM), q_norm), pos, DSA_ROT)
    k = apply_rope(rms_norm(u[..., DSA_OQ:DSA_OK].reshape(B, L, DSA_KV_HEADS, DSA_HEAD_DIM), k_norm), pos, DSA_ROT)
    v = u[..., DSA_OK:DSA_OV].reshape(B, L, DSA_KV_HEADS, DSA_HEAD_DIM)
    iq = apply_rope(u[..., DSA_OV:DSA_OIQ].reshape(B, L, IDX_HEADS, IDX_DIM), pos, IDX_ROT)
    ik = apply_rope(rms_norm(u[..., DSA_OIQ:DSA_OIK], idx_k_norm)[:, :, None], pos, IDX_ROT)[:, :, 0]
    iw = u[..., DSA_OIK:] * IDX_HEADS ** -0.5
    return q, k, v, iq, ik, iw


def dsa_attend(q, iq, iw, qpos, k, v, ik, n_sel):
    B, Q, Hq, d = q.shape
    S, Hkv = k.shape[1], k.shape[2]
    g = Hq // Hkv
    kpos = jnp.arange(S, dtype=jnp.int32)
    logits = jnp.einsum('bqhe,bse->bqhs', iq, ik).astype(jnp.float32) * IDX_DIM ** -0.5
    score = jnp.einsum('bqh,bqhs->bqs', iw.astype(jnp.float32), jax.nn.relu(logits))
    score = jnp.where(chunk_visible(qpos, kpos)[None], score, NEG_INF)
    _, idx = lax.top_k(score, n_sel)
    gather = jax.vmap(lambda a, i: a[i])
    k_sel, v_sel = gather(k, idx), gather(v, idx)
    sel_vis = (idx // CHUNK) <= (qpos // CHUNK)[None, :, None]
    s = jnp.einsum('bqkgd,bqskd->bkgqs', q.reshape(B, Q, Hkv, g, d), k_sel).astype(jnp.float32) * d ** -0.5
    s = jnp.where(sel_vis[:, None, None], s, NEG_INF)
    p = jax.nn.softmax(s, axis=-1).astype(v.dtype)
    return jnp.einsum('bkgqs,bqskd->bqkgd', p, v_sel).reshape(B, Q, Hq, d)


def dsa_prompt(q, k, v, iq, ik, iw):
    L = q.shape[1]
    n_sel = min(DSA_TOPK, L // 4)

    def one_block(args):
        start, q_b, iq_b, iw_b = args
        qpos = start + jnp.arange(DSA_QBLOCK, dtype=jnp.int32)
        return dsa_attend(q_b, iq_b, iw_b, qpos, k, v, ik, n_sel)

    starts = jnp.arange(L // DSA_QBLOCK, dtype=jnp.int32) * DSA_QBLOCK
    blocks = (starts, to_blocks(q, DSA_QBLOCK), to_blocks(iq, DSA_QBLOCK), to_blocks(iw, DSA_QBLOCK))
    return from_blocks(lax.map(one_block, blocks))


def mem_kv(mem, g, w_kv, k_norm):
    B, M, _ = mem.shape
    kv = (rms_norm(mem, g) @ w_kv).reshape(B, M, 2, XA_HEADS, XA_HEAD_DIM)
    return rms_norm(kv[:, :, 0], k_norm), kv[:, :, 1]


def cross_attn(h, mk, mv, w_q, q_norm, w_o):
    B, L, _ = h.shape
    q = rms_norm((h @ w_q).reshape(B, L, XA_HEADS, XA_HEAD_DIM), q_norm)
    return attend(q, mk, mv).reshape(B, L, -1) @ w_o


def conv_ffn(h, prev, w_up, conv_w, conv_b, w_down):
    L = h.shape[1]
    u = h @ w_up
    gate, val = u[..., :D_FF], u[..., D_FF:]
    gp = jnp.concatenate([prev.astype(gate.dtype), gate], axis=1)
    c = sum(conv_w[j] * gp[:, j:j + L] for j in range(CONV_W)) + conv_b
    return (jax.nn.silu(c) * val) @ w_down, gp[:, -(CONV_W - 1):]


def setup_inputs(seed: int = 0) -> dict:
    key = jax.random.key(seed)
    keys = iter(jax.random.split(key, 64))
    NP = DEPTH // N_MIXERS
    b_rows = min(BAND_PREV_CHUNKS * CHUNK, PAST_LEN)
    d_rows = min(SWA_WINDOW, PAST_LEN)

    def nrm(shape, scale=1.0):
        return jax.random.normal(next(keys), shape, jnp.float32) * scale

    def gain(shape):
        return 1.0 + 0.02 * nrm(shape)

    return {
        'x_prompt': nrm((BATCH, SEQ, D_MODEL)),
        'x_sample': nrm((DEC_BATCH, DEC_SEQ, D_MODEL)),
        'mem_prompt': nrm((BATCH, MEM_LEN, D_MODEL)),
        'cache_a_ckv': nrm((NP, DEC_BATCH, PAST_LEN, MLA_KV_LORA)),
        'cache_a_krope': nrm((NP, DEC_BATCH, PAST_LEN, MLA_ROPE)),
        'cache_b_k': nrm((NP, DEC_BATCH, b_rows, BAND_HEADS, BAND_HEAD_DIM)),
        'cache_b_v': nrm((NP, DEC_BATCH, b_rows, BAND_HEADS, BAND_HEAD_DIM)),
        'cache_c_k': nrm((NP, DEC_BATCH, PAST_LEN, DSA_KV_HEADS, DSA_HEAD_DIM)),
        'cache_c_v': nrm((NP, DEC_BATCH, PAST_LEN, DSA_KV_HEADS, DSA_HEAD_DIM)),
        'cache_c_idxk': nrm((NP, DEC_BATCH, PAST_LEN, IDX_DIM)),
        'cache_d_k': nrm((NP, DEC_BATCH, d_rows, SWA_KV_HEADS, SWA_HEAD_DIM)),
        'cache_d_v': nrm((NP, DEC_BATCH, d_rows, SWA_KV_HEADS, SWA_HEAD_DIM)),
        'cache_mem_k': nrm((DEPTH, DEC_BATCH, MEM_LEN, XA_HEADS, XA_HEAD_DIM)),
        'cache_mem_v': nrm((DEPTH, DEC_BATCH, MEM_LEN, XA_HEADS, XA_HEAD_DIM)),
        'state_ffn_conv': nrm((DEPTH, DEC_BATCH, CONV_W - 1, D_FF)),
        'norm_mix': gain((DEPTH, D_MODEL)),
        'norm_xattn': gain((DEPTH, D_MODEL)),
        'norm_mem': gain((DEPTH, D_MODEL)),
        'norm_ffn': gain((DEPTH, D_MODEL)),
        'a_w_in': nrm((NP, D_MODEL, MLA_Q_LORA + MLA_KV_LORA + MLA_ROPE), D_MODEL ** -0.5),
        'a_q_lora_norm': gain((NP, MLA_Q_LORA)),
        'a_kv_lora_norm': gain((NP, MLA_KV_LORA)),
        'a_w_uq': nrm((NP, MLA_Q_LORA, MLA_HEADS * (MLA_NOPE + MLA_ROPE)), MLA_Q_LORA ** -0.5),
        'a_w_ukv': nrm((NP, MLA_KV_LORA, MLA_HEADS * (MLA_NOPE + MLA_V)), MLA_KV_LORA ** -0.5),
        'a_q_norm': gain((NP, MLA_NOPE + MLA_ROPE)),
        'a_k_norm': gain((NP, MLA_NOPE + MLA_ROPE)),
        'a_w_out': nrm((NP, MLA_HEADS * MLA_V, D_MODEL), (MLA_HEADS * MLA_V) ** -0.5),
        'b_w_in': nrm((NP, D_MODEL, 3 * BAND_HEADS * BAND_HEAD_DIM), D_MODEL ** -0.5),
        'b_q_norm': gain((NP, BAND_HEAD_DIM)),
        'b_k_norm': gain((NP, BAND_HEAD_DIM)),
        'b_rel_bias': nrm((NP, 2 * MAX_REL + 1, BAND_HEADS), 0.1),
        'b_w_out': nrm((NP, BAND_HEADS * BAND_HEAD_DIM, D_MODEL), (BAND_HEADS * BAND_HEAD_DIM) ** -0.5),
        'c_w_in': nrm((NP, D_MODEL, DSA_IN), D_MODEL ** -0.5),
        'c_q_norm': gain((NP, DSA_HEAD_DIM)),
        'c_k_norm': gain((NP, DSA_HEAD_DIM)),
        'c_idx_k_norm': gain((NP, IDX_DIM)),
        'c_w_out': nrm((NP, DSA_HEADS * DSA_HEAD_DIM, D_MODEL), (DSA_HEADS * DSA_HEAD_DIM) ** -0.5),
        'd_w_in': nrm((NP, D_MODEL, (SWA_HEADS + 2 * SWA_KV_HEADS) * SWA_HEAD_DIM), D_MODEL ** -0.5),
        'd_q_norm': gain((NP, SWA_HEAD_DIM)),
        'd_k_norm': gain((NP, SWA_HEAD_DIM)),
        'd_sinks': nrm((NP, SWA_HEADS)),
        'd_w_out': nrm((NP, SWA_HEADS * SWA_HEAD_DIM, D_MODEL), (SWA_HEADS * SWA_HEAD_DIM) ** -0.5),
        'xa_w_q': nrm((DEPTH, D_MODEL, XA_HEADS * XA_HEAD_DIM), D_MODEL ** -0.5),
        'xa_w_kv': nrm((DEPTH, D_MODEL, 2 * XA_HEADS * XA_HEAD_DIM), D_MODEL ** -0.5),
        'xa_q_norm': gain((DEPTH, XA_HEAD_DIM)),
        'xa_k_norm': gain((DEPTH, XA_HEAD_DIM)),
        'xa_w_o': nrm((DEPTH, XA_HEADS * XA_HEAD_DIM, D_MODEL), (XA_HEADS * XA_HEAD_DIM) ** -0.5),
        'ffn_w_up': nrm((DEPTH, D_MODEL, 2 * D_FF), D_MODEL ** -0.5),
        'ffn_conv_w': nrm((DEPTH, CONV_W, D_FF), CONV_W ** -0.5),
        'ffn_conv_b': nrm((DEPTH, D_FF), 0.02),
        'ffn_w_down': nrm((DEPTH, D_FF, D_MODEL), D_FF ** -0.5),
    }


def reference(x_prompt, x_sample, mem_prompt,
              cache_a_ckv, cache_a_krope, cache_b_k, cache_b_v,
              cache_c_k, cache_c_v, cache_c_idxk, cache_d_k, cache_d_v,
              cache_mem_k, cache_mem_v, state_ffn_conv,
              norm_mix, norm_xattn, norm_mem, norm_ffn,
              a_w_in, a_q_lora_norm, a_kv_lora_norm, a_w_uq, a_w_ukv, a_q_norm, a_k_norm, a_w_out,
              b_w_in, b_q_norm, b_k_norm, b_rel_bias, b_w_out,
              c_w_in, c_q_norm, c_k_norm, c_idx_k_norm, c_w_out,
              d_w_in, d_q_norm, d_k_norm, d_sinks, d_w_out,
              xa_w_q, xa_w_kv, xa_q_norm, xa_k_norm, xa_w_o,
              ffn_w_up, ffn_conv_w, ffn_conv_b, ffn_w_down):
    B, S, _ = x_prompt.shape
    Bd, n, _ = x_sample.shape
    P = cache_a_ckv.shape[2]
    pos_p = jnp.arange(S, dtype=jnp.int32)
    pos_s = P + jnp.arange(n, dtype=jnp.int32)
    names = ('a_ckv_p', 'a_krope_p', 'b_k_p', 'b_v_p', 'c_k_p', 'c_v_p', 'c_idxk_p', 'd_k_p', 'd_v_p',
             'mem_k_p', 'mem_v_p', 'conv_p', 'a_ckv_s', 'a_krope_s', 'b_k_s', 'b_v_s', 'c_k_s', 'c_v_s',
             'c_idxk_s', 'd_k_s', 'd_v_s', 'conv_s')
    st = {name: [] for name in names}
    xp, xs = x_prompt, x_sample
    for i in range(DEPTH):
        m, j = i % N_MIXERS, i // N_MIXERS
        hp, hs = rms_norm(xp, norm_mix[i]), rms_norm(xs, norm_mix[i])
        if m == 0:
            wa = (a_w_in[j], a_q_lora_norm[j], a_kv_lora_norm[j], a_w_uq[j], a_q_norm[j], a_k_norm[j])
            qn, qr, ckv, kr = mla_project(hp, pos_p, *wa)
            yp = mla_prompt(qn, qr, ckv, kr, a_w_ukv[j], a_k_norm[j]).reshape(B, S, -1) @ a_w_out[j]
            st['a_ckv_p'].append(ckv)
            st['a_krope_p'].append(kr)
            qn, qr, ckv, kr = mla_project(hs, pos_s, *wa)
            ckv_all = jnp.concatenate([cache_a_ckv[j], ckv], axis=1)
            kr_all = jnp.concatenate([cache_a_krope[j], kr], axis=1)
            ys = mla_sample(qn, qr, ckv_all, kr_all, pos_s, a_w_ukv[j], a_k_norm[j]).reshape(Bd, n, -1) @ a_w_out[j]
            st['a_ckv_s'].append(ckv)
            st['a_krope_s'].append(kr)
        elif m == 1:
            wb = (b_w_in[j], b_q_norm[j], b_k_norm[j])
            q, k, v = band_project(hp, *wb)
            yp = band_prompt(q, k, v, BAND_PREV_CHUNKS, bias_table=b_rel_bias[j]).reshape(B, S, -1) @ b_w_out[j]
            r = min(BAND_PREV_CHUNKS * CHUNK, S)
            st['b_k_p'].append(k[:, S - r:])
            st['b_v_p'].append(v[:, S - r:])
            q, k, v = band_project(hs, *wb)
            ys = band_sample(q, k, v, cache_b_k[j], cache_b_v[j], P, BAND_PREV_CHUNKS,
                             bias_table=b_rel_bias[j]).reshape(Bd, n, -1) @ b_w_out[j]
            st['b_k_s'].append(k)
            st['b_v_s'].append(v)
        elif m == 2:
            wc = (c_w_in[j], c_q_norm[j], c_k_norm[j], c_idx_k_norm[j])
            q, k, v, iq, ik, iw = dsa_project(hp, pos_p, *wc)
            yp = dsa_prompt(q, k, v, iq, ik, iw).reshape(B, S, -1) @ c_w_out[j]
            st['c_k_p'].append(k)
            st['c_v_p'].append(v)
            st['c_idxk_p'].append(ik)
            q, k, v, iq, ik, iw = dsa_project(hs, pos_s, *wc)
            k_all = jnp.concatenate([cache_c_k[j], k], axis=1)
            v_all = jnp.concatenate([cache_c_v[j], v], axis=1)
            ik_all = jnp.concatenate([cache_c_idxk[j], ik], axis=1)
            ys = dsa_attend(q, iq, iw, pos_s, k_all, v_all, ik_all,
                            min(DSA_TOPK, (P + n) // 4)).reshape(Bd, n, -1) @ c_w_out[j]
            st['c_k_s'].append(k)
            st['c_v_s'].append(v)
            st['c_idxk_s'].append(ik)
        else:
            wd = (d_w_in[j], d_q_norm[j], d_k_norm[j])
            q, k, v = swa_project(hp, pos_p, *wd)
            yp = band_prompt(q, k, v, SWA_PREV_CHUNKS, sinks=d_sinks[j]).reshape(B, S, -1) @ d_w_out[j]
            r = min(SWA_WINDOW, S)
            st['d_k_p'].append(k[:, S - r:])
            st['d_v_p'].append(v[:, S - r:])
            q, k, v = swa_project(hs, pos_s, *wd)
            ys = band_sample(q, k, v, cache_d_k[j], cache_d_v[j], P, SWA_PREV_CHUNKS,
                             sinks=d_sinks[j]).reshape(Bd, n, -1) @ d_w_out[j]
            st['d_k_s'].append(k)
            st['d_v_s'].append(v)
        xp, xs = xp + yp, xs + ys
        mk, mv = mem_kv(mem_prompt, norm_mem[i], xa_w_kv[i], xa_k_norm[i])
        st['mem_k_p'].append(mk)
        st['mem_v_p'].append(mv)
        xp = xp + cross_attn(rms_norm(xp, norm_xattn[i]), mk, mv, xa_w_q[i], xa_q_norm[i], xa_w_o[i])
        xs = xs + cross_attn(rms_norm(xs, norm_xattn[i]), cache_mem_k[i], cache_mem_v[i],
                             xa_w_q[i], xa_q_norm[i], xa_w_o[i])
        zero_prev = jnp.zeros((B, CONV_W - 1, D_FF), xp.dtype)
        fp, conv_p = conv_ffn(rms_norm(xp, norm_ffn[i]), zero_prev, ffn_w_up[i], ffn_conv_w[i], ffn_conv_b[i], ffn_w_down[i])
        fs, conv_s = conv_ffn(rms_norm(xs, norm_ffn[i]), state_ffn_conv[i], ffn_w_up[i], ffn_conv_w[i], ffn_conv_b[i], ffn_w_down[i])
        st['conv_p'].append(conv_p)
        st['conv_s'].append(conv_s)
        xp, xs = xp + fp, xs + fs
    return (xp, xs,
            jnp.stack(st['a_ckv_p']), jnp.stack(st['a_krope_p']),
            jnp.stack(st['b_k_p']), jnp.stack(st['b_v_p']),
            jnp.stack(st['c_k_p']), jnp.stack(st['c_v_p']), jnp.stack(st['c_idxk_p']),
            jnp.stack(st['d_k_p']), jnp.stack(st['d_v_p']),
            jnp.stack(st['mem_k_p']), jnp.stack(st['mem_v_p']), jnp.stack(st['conv_p']),
            jnp.stack(st['a_ckv_s']), jnp.stack(st['a_krope_s']),
            jnp.stack(st['b_k_s']), jnp.stack(st['b_v_s']),
            jnp.stack(st['c_k_s']), jnp.stack(st['c_v_s']), jnp.stack(st['c_idxk_s']),
            jnp.stack(st['d_k_s']), jnp.stack(st['d_v_s']),
            jnp.stack(st['conv_s']))
```

```python
import functools

import numpy as np
import jax
import jax.numpy as jnp
from jax import lax
from jax.experimental import pallas as pl
from jax.experimental.pallas import tpu as pltpu

F32, BF16, I32 = jnp.float32, jnp.bfloat16, jnp.int32

CHUNK = 64
CHUNK_SHIFT = 6
ROPE_THETA = 500000.0
EPS = 1e-6
NEG_INF = -1e30
MLA_HEADS, MLA_NOPE, MLA_ROPE, MLA_V = 16, 128, 64, 128
MLA_Q_LORA, MLA_KV_LORA = 512, 512
BAND_HEADS, BAND_PREV_CHUNKS, MAX_REL = 16, 8, 128
DSA_HEADS, DSA_KV_HEADS, DSA_HEAD_DIM = 16, 4, 128
IDX_HEADS, IDX_DIM, DSA_TOPK = 16, 64, 256
SWA_HEADS, SWA_KV_HEADS, SWA_HEAD_DIM, SWA_PREV_CHUNKS = 32, 8, 64, 2
XA_HEADS, XA_HEAD_DIM = 4, 128
CONV_W = 3

LANES = 128
VMEM_LIMIT_MAX = 56 * 1024 * 1024
VMEM_LIMIT_MIN = 32 * 1024 * 1024

M_INIT = -3.0e38
INT_MIN = -2 ** 31
BIG_COL = 2 ** 30


def _vmem_limit(block_bytes, temp_bytes=0):
    need = 2 * block_bytes + temp_bytes + (4 << 20)
    return int(min(max(need, VMEM_LIMIT_MIN), VMEM_LIMIT_MAX))


def _pick(n, cands):
    for c in cands:
        if n % c == 0:
            return c
    raise ValueError(f"no tile for {n}")


def _rmsnorm_kernel(x_ref, g_ref, o_ref):
    x = x_ref[...]
    ms = jnp.mean(x * x, axis=-1, keepdims=True)
    o_ref[...] = (x * lax.rsqrt(ms + EPS) * g_ref[...]).astype(o_ref.dtype)


def rmsnorm_cast(x, g):
    M, K = x.shape
    tm = _pick(M, (512, 256, 128, 64, 16, 8))
    return pl.pallas_call(
        _rmsnorm_kernel,
        grid=(M // tm,),
        in_specs=[pl.BlockSpec((tm, K), lambda i: (i, 0)), pl.BlockSpec((1, K), lambda i: (0, 0))],
        out_specs=pl.BlockSpec((tm, K), lambda i: (i, 0)),
        out_shape=jax.ShapeDtypeStruct((M, K), BF16),
        compiler_params=pltpu.CompilerParams(
            dimension_semantics=("parallel",), vmem_limit_bytes=_vmem_limit(tm * K * 6, tm * K * 8)),
        name="rmsnorm_cast",
    )(x, g.reshape(1, K).astype(F32))


def _group_norm_rope(y, gain_ref, rope_refs, gs, norm, half):
    tm, tn = y.shape
    if norm and gs == tn:
        ss = jnp.sum(y * y, axis=-1, keepdims=True)
        return y * lax.rsqrt(ss * (1.0 / gs) + EPS) * gain_ref[...]
    outs = []
    for c in range(tn // LANES):
        yc = y[:, c * LANES:(c + 1) * LANES]
        if norm:
            yy = yc * yc
            if gs == LANES:
                ss = jnp.sum(yy, axis=-1, keepdims=True)
            else:
                lo = lax.broadcasted_iota(I32, yc.shape, 1) < gs
                s_lo = jnp.sum(jnp.where(lo, yy, 0.0), axis=-1, keepdims=True)
                s_hi = jnp.sum(jnp.where(lo, 0.0, yy), axis=-1, keepdims=True)
                ss = jnp.where(lo, s_lo, s_hi)
            yc = yc * lax.rsqrt(ss * (1.0 / gs) + EPS) * gain_ref[:, c * LANES:(c + 1) * LANES]
        elif gain_ref is not None:
            yc = yc * gain_ref[:, c * LANES:(c + 1) * LANES]
        if half:
            c_ref, s1_ref, s2_ref = rope_refs
            yc = (yc * c_ref[...] + pltpu.roll(yc, LANES - half, 1) * s1_ref[...]
                  + pltpu.roll(yc, half, 1) * s2_ref[...])
        outs.append(yc)
    return outs[0] if len(outs) == 1 else jnp.concatenate(outs, axis=1)


def _mm_kernel(*refs, has_gain, gs, norm, half, has_res):
    a_ref, w_ref = refs[0], refs[1]
    pos = 2
    gain_ref = rope_refs = res_ref = None
    if has_gain:
        gain_ref = refs[pos]
        pos += 1
    if half:
        rope_refs = refs[pos:pos + 3]
        pos += 3
    if has_res:
        res_ref = refs[pos]
        pos += 1
    o_ref = refs[pos]
    y = jnp.dot(a_ref[...], w_ref[...], preferred_element_type=F32)
    if has_gain or half:
        y = _group_norm_rope(y, gain_ref, rope_refs, gs, norm, half)
    if has_res:
        y = res_ref[...] + y
    o_ref[...] = y.astype(o_ref.dtype)


def mm(a, w, *, gain=None, gs=LANES, norm=False, rope=None, res=None, out_dtype=F32, tn=None,
       name="mm"):
    M, K = a.shape
    N = w.shape[1]
    assert a.dtype == BF16 and w.dtype == BF16 and N % LANES == 0
    if tn is None:
        tn = gs if (norm and gs > LANES) else _pick(N, (512, 256, 128))
    tm_c = (1024, 512, 256, 128, 64, 16, 8) if K <= 2048 else (512, 256, 128, 64, 16, 8)
    rows_period = M if rope is None else rope[1].shape[0]
    tm = _pick(np.gcd(M, rows_period), tm_c)
    half = 0
    in_specs = [pl.BlockSpec((tm, K), lambda i, j: (i, 0)), pl.BlockSpec((K, tn), lambda i, j: (0, j))]
    args = [a, w]
    blk = tm * K * 2 + K * tn * 2 + tm * tn * 4
    if gain is not None:
        in_specs.append(pl.BlockSpec((1, tn), lambda i, j: (0, j)))
        args.append(gain.reshape(1, N).astype(F32))
    if rope is not None:
        half, tabs = rope[0], rope[1:]
        nrb = rows_period // tm
        for t in tabs:
            in_specs.append(pl.BlockSpec((tm, LANES), lambda i, j, nrb=nrb: (i % nrb, 0)))
            args.append(t)
        blk += 3 * tm * LANES * 4
    if res is not None:
        in_specs.append(pl.BlockSpec((tm, tn), lambda i, j: (i, j)))
        args.append(res)
        blk += tm * tn * 4
    kern = functools.partial(_mm_kernel, has_gain=gain is not None, gs=gs, norm=norm, half=half,
                             has_res=res is not None)
    return pl.pallas_call(
        kern,
        grid=(M // tm, N // tn),
        in_specs=in_specs,
        out_specs=pl.BlockSpec((tm, tn), lambda i, j: (i, j)),
        out_shape=jax.ShapeDtypeStruct((M, N), out_dtype),
        compiler_params=pltpu.CompilerParams(
            dimension_semantics=("parallel", "arbitrary"),
            vmem_limit_bytes=_vmem_limit(blk, 3 * tm * tn * 4)),
        name=name,
    )(*args)


def _ffn_up_kernel(h_ref, wg_ref, wv_ref, cw_ref, cb_ref, prev_ref, act_ref, last_ref, carry_ref, *,
                   tiles_per_seq, seq_rows):
    i = pl.program_id(0)
    h = h_ref[...]
    g = jnp.dot(h, wg_ref[...], preferred_element_type=F32)
    val = jnp.dot(h, wv_ref[...], preferred_element_type=F32)
    tm, tn = g.shape
    row = lax.broadcasted_iota(I32, (tm, tn), 0)
    g1 = pltpu.roll(g, 1, 0)
    g2 = pltpu.roll(g, 2, 0)
    if tiles_per_seq >= 1:
        first = (i % tiles_per_seq) == 0
        j = pl.program_id(1)
        p_prev = prev_ref[0]
        p_carry = carry_ref[j]
        p2 = jnp.where(first, p_prev[0:1, :], p_carry[6:7, :])
        p1 = jnp.where(first, p_prev[1:2, :], p_carry[7:8, :])
        g1 = jnp.where(row == 0, p1, g1)
        g2 = jnp.where(row == 0, p2, jnp.where(row == 1, p1, g2))
        carry_ref[j] = g[tm - 8:tm, :]
        last_ref[0] = g[tm - 8:tm, :]
    else:
        rmod = row % seq_rows
        g1 = jnp.where(rmod == 0, prev_ref[0], g1)
        g2 = jnp.where(rmod < 2, prev_ref[1], g2)
        last_ref[...] = g
    c = cw_ref[0:1, :] * g2 + cw_ref[1:2, :] * g1 + cw_ref[2:3, :] * g + cb_ref[...]
    act_ref[...] = (c * (1.0 / (1.0 + jnp.exp(-c))) * val).astype(act_ref.dtype)


def ffn_up(h, wg, wv, conv_w, conv_b, prev, *, n_seq, seq_rows):
    M, D = h.shape
    Fd = wg.shape[1]
    tn = _pick(Fd, (512, 256, 128))
    nj = Fd // tn
    if seq_rows >= 256:
        tm = _pick(seq_rows, (1024, 512, 256))
        tps = seq_rows // tm
        prev_arg = prev
        prev_spec = pl.BlockSpec((1, 2, tn), lambda i, j: (i // tps, 0, j))
        last_shape = jax.ShapeDtypeStruct((M // tm, 8, Fd), F32)
        last_spec = pl.BlockSpec((1, 8, tn), lambda i, j: (i, 0, j))
    else:
        tm = M
        tps = 0
        z = jnp.zeros((n_seq, seq_rows - 1, Fd), F32)
        inj1 = jnp.concatenate([prev[:, 1:2], z], axis=1).reshape(M, Fd)
        inj2 = jnp.concatenate([prev[:, 0:2], z[:, 1:]], axis=1).reshape(M, Fd)
        prev_arg = jnp.stack([inj1, inj2])
        prev_spec = pl.BlockSpec((2, tm, tn), lambda i, j: (0, i, j))
        last_shape = jax.ShapeDtypeStruct((M, Fd), F32)
        last_spec = pl.BlockSpec((tm, tn), lambda i, j: (i, j))
    kern = functools.partial(_ffn_up_kernel, tiles_per_seq=tps, seq_rows=seq_rows)
    blk = tm * D * 2 + 2 * D * tn * 2 + tm * tn * 2 + 4 * tn * 4 + 2 * tm * tn * 4
    act, last = pl.pallas_call(
        kern,
        grid=(M // tm, nj),
        in_specs=[pl.BlockSpec((tm, D), lambda i, j: (i, 0)),
                  pl.BlockSpec((D, tn), lambda i, j: (0, j)),
                  pl.BlockSpec((D, tn), lambda i, j: (0, j)),
                  pl.BlockSpec((CONV_W, tn), lambda i, j: (0, j)),
                  pl.BlockSpec((1, tn), lambda i, j: (0, j)),
                  prev_spec],
        out_specs=[pl.BlockSpec((tm, tn), lambda i, j: (i, j)), last_spec],
        out_shape=[jax.ShapeDtypeStruct((M, Fd), BF16), last_shape],
        scratch_shapes=[pltpu.VMEM((nj, 8, tn), F32)],
        compiler_params=pltpu.CompilerParams(
            dimension_semantics=("arbitrary", "arbitrary"),
            vmem_limit_bytes=_vmem_limit(blk, 8 * tm * tn * 4)),
        name="ffn_up",
    )(h, wg, wv, conv_w.astype(F32), conv_b.reshape(1, Fd).astype(F32), prev_arg)
    if tps == 0:
        last = last.reshape(n_seq, seq_rows, Fd)[:, seq_rows - 2:]
    else:
        last = last[tps - 1::tps, 6:8]
    return act, last


def _attn_kernel(*refs, tq, tk, nq, sk_valid, window, n_prev, causal, masked, qbase, kbase, slots,
                 outs, scale, has_k2, has_bias, has_sinks, sk):
    q_ref, k_ref, v_ref = refs[0], refs[1], refs[2]
    pos = 3
    q2_ref = k2_ref = bias_ref = sinks_ref = None
    if has_k2:
        q2_ref, k2_ref = refs[pos], refs[pos + 1]
        pos += 2
    if has_bias:
        bias_ref = refs[pos]
        pos += 1
    if has_sinks:
        sinks_ref = refs[pos]
        pos += 1
    o_ref = refs[pos]
    m_ref, l_ref, acc_ref = refs[pos + 1], refs[pos + 2], refs[pos + 3]

    g = pl.program_id(1)
    i = pl.program_id(2)
    q0 = i * tq
    ns = len(slots)
    R = ns * tq
    lane = lax.broadcasted_iota(I32, (tq, LANES), 1)

    qs = []
    for (t, mode) in slots:
        qt = q_ref[:, t * LANES:(t + 1) * LANES]
        if mode == "lo":
            qt = jnp.where(lane < 64, qt, 0.0)
        elif mode == "hi":
            qt = jnp.where(lane >= 64, qt, 0.0)
        if has_k2:
            q2 = jnp.where((lane >> 6) == (g % 2), q2_ref[...], 0.0)
            qt = jnp.concatenate([qt, q2], axis=1)
        qs.append(qt.astype(BF16))
    Q = qs[0] if ns == 1 else jnp.concatenate(qs, axis=0)

    qpos1 = qbase + q0 + lax.broadcasted_iota(I32, (tq, 1), 0)
    qc1 = qpos1 >> CHUNK_SHIFT
    qc = qc1 if ns == 1 else jnp.concatenate([qc1] * ns, axis=0)

    if window is not None:
        lo_row = ((qbase + q0) >> CHUNK_SHIFT) * CHUNK - n_prev * CHUNK - kbase
        start = jnp.clip(lo_row, 0, sk - window)
        nb = window // tk
    elif causal:
        start = 0
        hi_row = (((qbase + q0 + tq - 1) >> CHUNK_SHIFT) + 1) * CHUNK - kbase
        nb = (jnp.minimum(hi_row, sk_valid) + tk - 1) // tk
    else:
        start = 0
        nb = sk // tk

    m_ref[...] = jnp.full((R, 1), M_INIT, F32)
    l_ref[...] = jnp.zeros((R, 1), F32)
    acc_ref[...] = jnp.zeros((R, LANES), F32)

    def body(jb, carry):
        r0 = pl.multiple_of(start + jb * tk, 8)
        kb = k_ref[pl.ds(r0, tk), :]
        if has_k2:
            kb = jnp.concatenate([kb, k2_ref[pl.ds(r0, tk), :]], axis=1)
        s = lax.dot_general(Q, kb.astype(BF16), (((1,), (1,)), ((), ())),
                            preferred_element_type=F32) * scale
        if has_bias:
            s = s + bias_ref[0, 0]
        if masked:
            krow = r0 + lax.broadcasted_iota(I32, (1, tk), 1)
            kc = (kbase + krow) >> CHUNK_SHIFT
            s = jnp.where(kc <= qc, s, NEG_INF)
            if n_prev is not None:
                s = jnp.where(kc >= qc - n_prev, s, NEG_INF)
            if sk_valid < sk:
                s = jnp.where(krow < sk_valid, s, NEG_INF)
        m_old = m_ref[...]
        m_new = jnp.maximum(m_old, jnp.max(s, axis=-1, keepdims=True))
        p = jnp.exp(s - m_new)
        alpha = jnp.exp(m_old - m_new)
        l_ref[...] = alpha * l_ref[...] + jnp.sum(p, axis=-1, keepdims=True)
        acc_ref[...] = alpha * acc_ref[...] + jnp.dot(
            p.astype(BF16), v_ref[pl.ds(r0, tk), :].astype(BF16), preferred_element_type=F32)
        m_ref[...] = m_new
        return carry

    lax.fori_loop(0, nb, body, 0)

    m = m_ref[...]
    l = l_ref[...]
    acc = acc_ref[...]
    if has_sinks:
        sk_rows = [jnp.full((tq, 1), sinks_ref[g * ns + s], F32) for s in range(ns)]
        sink = sk_rows[0] if ns == 1 else jnp.concatenate(sk_rows, axis=0)
        m_t = jnp.maximum(m, sink)
        a = jnp.exp(m - m_t)
        l = l * a + jnp.exp(sink - m_t)
        acc = acc * a
    o = acc / l
    for t, spec in enumerate(outs):
        if spec[0] == "slot":
            ot = o[spec[1] * tq:(spec[1] + 1) * tq]
        else:
            ot = jnp.where(lane < 64, o[spec[1] * tq:(spec[1] + 1) * tq],
                           o[spec[2] * tq:(spec[2] + 1) * tq])
        o_ref[:, t * LANES:(t + 1) * LANES] = ot.astype(o_ref.dtype)


def attention(q, k, v, *, B, Sq, Sk, sk_valid, tq, tk, G, qw, slots, outs, scale, window=None,
              n_prev=None, causal=False, masked=True, qbase=0, kbase=0, q2=None, k2=None, bias=None,
              sinks=None, name="attn"):
    nq = Sq // tq
    ns = len(slots)
    R = ns * tq
    ow = len(outs)
    in_specs = [pl.BlockSpec((tq, qw * LANES), lambda b, g, i: (b * nq + i, g)),
                pl.BlockSpec((Sk, LANES), lambda b, g, i: (b, g)),
                pl.BlockSpec((Sk, LANES), lambda b, g, i: (b, g))]
    args = [q, k, v]
    blk = tq * qw * LANES * 4 + 2 * Sk * LANES * 4 + tq * ow * LANES * 2
    if q2 is not None:
        in_specs += [pl.BlockSpec((tq, LANES), lambda b, g, i: (b * nq + i, g // 2)),
                     pl.BlockSpec((Sk, LANES), lambda b, g, i: (b, 0))]
        args += [q2, k2]
        blk += tq * LANES * 4 + Sk * LANES * 4
    if bias is not None:
        nv = bias.shape[0]
        assert window == tk
        in_specs.append(pl.BlockSpec((1, 1, tq, window),
                                     lambda b, g, i, nv=nv: (jnp.minimum(i, nv - 1), g, 0, 0)))
        args.append(bias)
        blk += tq * window * 4
    if sinks is not None:
        in_specs.append(pl.BlockSpec(memory_space=pltpu.SMEM))
        args.append(sinks.astype(F32))
    kern = functools.partial(
        _attn_kernel, tq=tq, tk=tk, nq=nq, sk_valid=sk_valid, window=window, n_prev=n_prev,
        causal=causal, masked=masked, qbase=qbase, kbase=kbase, slots=tuple(slots), outs=tuple(outs),
        scale=scale, has_k2=q2 is not None, has_bias=bias is not None, has_sinks=sinks is not None,
        sk=Sk)
    return pl.pallas_call(
        kern,
        grid=(B, G, nq),
        in_specs=in_specs,
        out_specs=pl.BlockSpec((tq, ow * LANES), lambda b, g, i: (b * nq + i, g)),
        out_shape=jax.ShapeDtypeStruct((B * Sq, G * ow * LANES), BF16),
        scratch_shapes=[pltpu.VMEM((R, 1), F32), pltpu.VMEM((R, 1), F32), pltpu.VMEM((R, LANES), F32)],
        compiler_params=pltpu.CompilerParams(
            dimension_semantics=("parallel", "parallel", "arbitrary"),
            vmem_limit_bytes=_vmem_limit(blk, 6 * R * tk * 4)),
        name=name,
    )(*args)


def _dsa_kernel(q_ref, k_ref, v_ref, iq_ref, ik_ref, iw_ref, o_ref, key_ref, mb_ref, m_ref, l_ref,
                acc_ref, *, tq, tk, sk, sk_valid, qbase, n_sel, scale, idx_scale):
    i = pl.program_id(1)
    q0 = i * tq
    qpos = qbase + q0 + lax.broadcasted_iota(I32, (tq, 1), 0)
    qc = qpos >> CHUNK_SHIFT
    hi_row = (((qbase + q0 + tq - 1) >> CHUNK_SHIFT) + 1) * CHUNK
    nb = (jnp.minimum(hi_row, sk_valid) + tk - 1) // tk
    lane = lax.broadcasted_iota(I32, (tq, LANES), 1)
    n_idx_tiles = iq_ref.shape[1] // LANES

    def col_of(jb):
        return jb * tk + lax.broadcasted_iota(I32, (1, tk), 1)

    def mask_invisible(x, col):
        x = jnp.where((col >> CHUNK_SHIFT) <= qc, x, NEG_INF)
        if sk_valid < sk:
            x = jnp.where(col < sk_valid, x, NEG_INF)
        return x

    def score_body(jb, carry):
        r0 = pl.multiple_of(jb * tk, 8)
        ikb = ik_ref[pl.ds(r0, tk), :].astype(BF16)
        acc = jnp.zeros((tq, tk), F32)
        for p in range(n_idx_tiles):
            t = iq_ref[:, p * LANES:(p + 1) * LANES]
            for hf in range(2):
                qm = jnp.where((lane < 64) if hf == 0 else (lane >= 64), t, 0.0).astype(BF16)
                lg = lax.dot_general(qm, ikb, (((1,), (1,)), ((), ())),
                                     preferred_element_type=F32) * idx_scale
                h = 2 * p + hf
                acc = acc + iw_ref[:, h:h + 1] * jnp.maximum(lg, 0.0)
        sc = mask_invisible(acc, col_of(jb))
        bits = lax.bitcast_convert_type(sc, I32)
        key_ref[jb] = bits ^ ((bits >> 31) & 0x7FFFFFFF)
        return carry

    lax.fori_loop(0, nb, score_body, 0)

    def count(pred):
        def cb(jb, c):
            return c + jnp.sum(jnp.where(pred(key_ref[jb], jb), 1.0, 0.0), axis=1, keepdims=True)
        return lax.fori_loop(0, nb, cb, jnp.zeros((tq, 1), F32))

    def count_eq_before(jtrial):
        def cb(jb, c):
            hit = jnp.where(key_ref[jb] == thr, jnp.where(col_of(jb) < jtrial, 1.0, 0.0), 0.0)
            return c + jnp.sum(hit, axis=1, keepdims=True)
        return lax.fori_loop(0, nb, cb, jnp.zeros((tq, 1), F32))

    def bit_body(it, cand):
        trial = cand | (jnp.int32(1) << (31 - it))
        trial_s = trial ^ INT_MIN
        cnt = count(lambda kk, jb: kk >= trial_s)
        return jnp.where(cnt >= n_sel, trial, cand)

    cand = lax.fori_loop(0, 32, bit_body, jnp.zeros((tq, 1), I32))
    thr = cand ^ INT_MIN

    cnt_gt = count(lambda kk, jb: kk > thr)
    n_eq = count(lambda kk, jb: kk == thr)
    need = n_sel - cnt_gt
    neg_bits = int(np.float32(NEG_INF).view(np.int32))
    neg_key = neg_bits ^ ((neg_bits >> 31) & 0x7FFFFFFF)
    tie = jnp.where(thr > neg_key, jnp.where(n_eq > need, 1.0, 0.0), 0.0)
    any_tie = jnp.max(tie) > 0.0

    def tie_cols():
        def jbit(it, cj):
            trial = cj | (jnp.int32(1) << (30 - it))
            c = count_eq_before(trial)
            return jnp.where(c < need, trial, cj)
        return lax.fori_loop(0, 31, jbit, jnp.zeros((tq, 1), I32))

    jmax = lax.cond(any_tie, tie_cols, lambda: jnp.full((tq, 1), BIG_COL, I32))

    def mask_body(jb, carry):
        kk = key_ref[jb]
        col = col_of(jb)
        sel = jnp.where(kk > thr, 0.0, jnp.where(kk == thr, jnp.where(col <= jmax, 0.0, NEG_INF), NEG_INF))
        mb_ref[jb] = mask_invisible(sel, col)
        return carry

    lax.fori_loop(0, nb, mask_body, 0)

    n_kv = k_ref.shape[1] // LANES
    grp = (q_ref.shape[1] // LANES) // n_kv
    R = grp * tq
    for kvh in range(n_kv):
        Q = jnp.concatenate(
            [q_ref[:, (kvh * grp + gi) * LANES:(kvh * grp + gi + 1) * LANES] for gi in range(grp)],
            axis=0).astype(BF16)
        m_ref[...] = jnp.full((R, 1), M_INIT, F32)
        l_ref[...] = jnp.zeros((R, 1), F32)
        acc_ref[...] = jnp.zeros((R, LANES), F32)

        def body(jb, carry, kvh=kvh, Q=Q):
            r0 = pl.multiple_of(jb * tk, 8)
            kb = k_ref[pl.ds(r0, tk), kvh * LANES:(kvh + 1) * LANES].astype(BF16)
            vb = v_ref[pl.ds(r0, tk), kvh * LANES:(kvh + 1) * LANES].astype(BF16)
            s = lax.dot_general(Q, kb, (((1,), (1,)), ((), ())), preferred_element_type=F32) * scale
            mb = mb_ref[jb]
            s = jnp.where(jnp.concatenate([mb] * grp, axis=0) < 0.0, NEG_INF, s)
            m_old = m_ref[...]
            m_new = jnp.maximum(m_old, jnp.max(s, axis=-1, keepdims=True))
            p = jnp.exp(s - m_new)
            alpha = jnp.exp(m_old - m_new)
            l_ref[...] = alpha * l_ref[...] + jnp.sum(p, axis=-1, keepdims=True)
            acc_ref[...] = alpha * acc_ref[...] + jnp.dot(p.astype(BF16), vb,
                                                          preferred_element_type=F32)
            m_ref[...] = m_new
            return carry

        lax.fori_loop(0, nb, body, 0)
        o = acc_ref[...] / l_ref[...]
        for gi in range(grp):
            hh = kvh * grp + gi
            o_ref[:, hh * LANES:(hh + 1) * LANES] = o[gi * tq:(gi + 1) * tq].astype(o_ref.dtype)


def dsa_attention(q, k, v, iq, ik2, iw, *, B, Sq, Sk, sk_valid, tq, tk, qbase, n_sel):
    nq = Sq // tq
    nkb = Sk // tk
    Hq = q.shape[1]
    Hk = k.shape[1]
    grp = Hq // Hk
    R = grp * tq
    kern = functools.partial(_dsa_kernel, tq=tq, tk=tk, sk=Sk, sk_valid=sk_valid, qbase=qbase, n_sel=n_sel,
                             scale=DSA_HEAD_DIM ** -0.5, idx_scale=IDX_DIM ** -0.5)
    blk = tq * (Hq + iq.shape[1] + LANES) * 4 + Sk * (2 * Hk + LANES) * 4 + tq * Hq * 2
    return pl.pallas_call(
        kern,
        grid=(B, nq),
        in_specs=[pl.BlockSpec((tq, Hq), lambda b, i: (b * nq + i, 0)),
                  pl.BlockSpec((Sk, Hk), lambda b, i: (b, 0)),
                  pl.BlockSpec((Sk, Hk), lambda b, i: (b, 0)),
                  pl.BlockSpec((tq, iq.shape[1]), lambda b, i: (b * nq + i, 0)),
                  pl.BlockSpec((Sk, LANES), lambda b, i: (b, 0)),
                  pl.BlockSpec((tq, LANES), lambda b, i: (b * nq + i, 0))],
        out_specs=pl.BlockSpec((tq, Hq), lambda b, i: (b * nq + i, 0)),
        out_shape=jax.ShapeDtypeStruct((B * Sq, Hq), BF16),
        scratch_shapes=[pltpu.VMEM((nkb, tq, tk), I32), pltpu.VMEM((nkb, tq, tk), F32),
                        pltpu.VMEM((R, 1), F32), pltpu.VMEM((R, 1), F32), pltpu.VMEM((R, LANES), F32)],
        compiler_params=pltpu.CompilerParams(
            dimension_semantics=("parallel", "arbitrary"),
            vmem_limit_bytes=_vmem_limit(blk, 2 * tq * Sk * 4 + 6 * R * tk * 4)),
        name="dsa_attention",
    )(q, k, v, iq, ik2, iw)


def _rope_tables(pos, rot, group):
    half = rot // 2
    inv = ROPE_THETA ** (-jnp.arange(half, dtype=F32) / half)
    ang = pos.astype(F32)[:, None] * inv[None, :]
    c, s = jnp.cos(ang), jnp.sin(ang)
    lane = np.arange(LANES) % group
    is_lo = lane < half
    is_hi = (lane >= half) & (lane < rot)
    idx = np.where(is_lo, lane, np.where(is_hi, lane - half, 0))
    cg, sg = c[:, idx], s[:, idx]
    C = jnp.where(is_lo | is_hi, cg, 1.0)
    S1 = jnp.where(is_lo, -sg, 0.0)
    S2 = jnp.where(is_hi, sg, 0.0)
    return half, C, S1, S2


def _pad_cols(w, n):
    return jnp.pad(w, ((0, 0), (0, n - w.shape[1])))


def _tile_gain(g, n):
    return jnp.tile(g.astype(F32), n // g.shape[0])


def _band_bias(table, tq, window, offs):
    qi = np.arange(tq)[:, None]
    kj = np.arange(window)[None, :]
    idx = np.stack([np.clip(qi - kj + off, -MAX_REL, MAX_REL) + MAX_REL for off in offs])
    return jnp.transpose(table.astype(F32)[idx], (0, 3, 1, 2))


def _pad_rows(x, B, rows, rows_pad):
    if rows == rows_pad:
        return x
    x = x.reshape(B, rows, -1)
    x = jnp.pad(x, ((0, 0), (0, rows_pad - rows), (0, 0)))
    return x.reshape(B * rows_pad, -1)


def _with_cache(cache, new, B, rows_pad):
    n = new.shape[0] // B
    P = cache.shape[1]
    parts = [cache.reshape(B, P, -1), new.reshape(B, n, -1)]
    if rows_pad > P + n:
        parts.append(jnp.zeros((B, rows_pad - P - n, new.shape[1]), F32))
    return jnp.concatenate(parts, axis=1).reshape(B * rows_pad, -1)


def _round_up(x, m):
    return (x + m - 1) // m * m


_SLOT1 = ((0, None),)
_OUT1 = (("slot", 0),)


def _mla_weights(w_in, q_lora_norm, kv_lora_norm, w_uq, w_ukv, q_norm, k_norm):
    H = MLA_HEADS
    wq3 = w_uq.reshape(MLA_Q_LORA, H, MLA_NOPE + MLA_ROPE)
    wkv3 = w_ukv.reshape(MLA_KV_LORA, H, MLA_NOPE + MLA_V)
    return dict(
        w_cq=w_in[:, :MLA_Q_LORA].astype(BF16),
        w_ckv=w_in[:, MLA_Q_LORA:MLA_Q_LORA + MLA_KV_LORA].astype(BF16),
        w_kr=_pad_cols(w_in[:, MLA_Q_LORA + MLA_KV_LORA:], LANES).astype(BF16),
        g_cq=q_lora_norm, g_ckv=kv_lora_norm,
        g_kr=_tile_gain(k_norm[MLA_NOPE:], LANES),
        w_qn=wq3[:, :, :MLA_NOPE].reshape(MLA_Q_LORA, H * MLA_NOPE).astype(BF16),
        w_qr=wq3[:, :, MLA_NOPE:].reshape(MLA_Q_LORA, H * MLA_ROPE).astype(BF16),
        g_qn=_tile_gain(q_norm[:MLA_NOPE], H * MLA_NOPE),
        g_qr=_tile_gain(q_norm[MLA_NOPE:], H * MLA_ROPE),
        w_kn=wkv3[:, :, :MLA_NOPE].reshape(MLA_KV_LORA, H * MLA_NOPE).astype(BF16),
        w_v=wkv3[:, :, MLA_NOPE:].reshape(MLA_KV_LORA, H * MLA_V).astype(BF16),
        g_kn=_tile_gain(k_norm[:MLA_NOPE], H * MLA_NOPE),
    )


def _mla_project(h, W, rope64):
    cq = mm(h, W["w_cq"], gain=W["g_cq"], gs=MLA_Q_LORA, norm=True, out_dtype=BF16, name="mla_cq")
    ckv = mm(h, W["w_ckv"], gain=W["g_ckv"], gs=MLA_KV_LORA, norm=True, name="mla_ckv")
    kr = mm(h, W["w_kr"], gain=W["g_kr"], gs=64, norm=True, rope=rope64, name="mla_kr")
    qn = mm(cq, W["w_qn"], gain=W["g_qn"], gs=LANES, norm=True, name="mla_qn")
    qr = mm(cq, W["w_qr"], gain=W["g_qr"], gs=64, norm=True, rope=rope64, name="mla_qr")
    return qn, qr, ckv, kr


def _mla_expand(ckv_bf, W):
    kn = mm(ckv_bf, W["w_kn"], gain=W["g_kn"], gs=LANES, norm=True, name="mla_kn")
    v = mm(ckv_bf, W["w_v"], name="mla_v")
    return kn, v


def _mla_attend(qn, qr, kn, kr, v, *, B, Sq, Sk, sk_valid, tq, tk, qbase):
    kr2 = jnp.concatenate([kr[:, :MLA_ROPE], kr[:, :MLA_ROPE]], axis=1)
    return attention(qn, kn, v, B=B, Sq=Sq, Sk=Sk, sk_valid=sk_valid, tq=tq, tk=tk, G=MLA_HEADS, qw=1,
                     slots=_SLOT1, outs=_OUT1, scale=(MLA_NOPE + MLA_ROPE) ** -0.5, causal=True,
                     qbase=qbase, q2=qr, k2=kr2, name="mla_attn")


def kernel(x_prompt, x_sample, mem_prompt, cache_a_ckv, cache_a_krope, cache_b_k, cache_b_v, cache_c_k, cache_c_v, cache_c_idxk, cache_d_k, cache_d_v, cache_mem_k, cache_mem_v, state_ffn_conv, norm_mix, norm_xattn, norm_mem, norm_ffn, a_w_in, a_q_lora_norm, a_kv_lora_norm, a_w_uq, a_w_ukv, a_q_norm, a_k_norm, a_w_out, b_w_in, b_q_norm, b_k_norm, b_rel_bias, b_w_out, c_w_in, c_q_norm, c_k_norm, c_idx_k_norm, c_w_out, d_w_in, d_q_norm, d_k_norm, d_sinks, d_w_out, xa_w_q, xa_w_kv, xa_q_norm, xa_k_norm, xa_w_o, ffn_w_up, ffn_conv_w, ffn_conv_b, ffn_w_down):
    B, S, D = x_prompt.shape
    Bd, n, _ = x_sample.shape
    P = cache_a_ckv.shape[2]
    depth = norm_mix.shape[0]
    Fd = ffn_conv_b.shape[1]
    Mp, Ms = B * S, Bd * n
    pos_p = jnp.arange(S, dtype=I32)
    pos_s = jnp.tile(P + jnp.arange(n, dtype=I32), Bd)
    names = ('a_ckv_p', 'a_krope_p', 'b_k_p', 'b_v_p', 'c_k_p', 'c_v_p', 'c_idxk_p', 'd_k_p', 'd_v_p',
             'mem_k_p', 'mem_v_p', 'conv_p', 'a_ckv_s', 'a_krope_s', 'b_k_s', 'b_v_s', 'c_k_s', 'c_v_s',
             'c_idxk_s', 'd_k_s', 'd_v_s', 'conv_s')
    st = {name: [] for name in names}
    xp = x_prompt.reshape(Mp, D)
    xs = x_sample.reshape(Ms, D)
    tq_p = 128

    def rope_pair(rot, group):
        return _rope_tables(pos_p, rot, group), _rope_tables(pos_s, rot, group)

    for i in range(depth):
        m, j = i % 4, i // 4
        hp = rmsnorm_cast(xp, norm_mix[i])
        hs = rmsnorm_cast(xs, norm_mix[i])
        if m == 0:
            W = _mla_weights(a_w_in[j], a_q_lora_norm[j], a_kv_lora_norm[j], a_w_uq[j], a_w_ukv[j],
                             a_q_norm[j], a_k_norm[j])
            rp, rs = rope_pair(MLA_ROPE, 64)
            w_out = a_w_out[j].astype(BF16)
            qn, qr, ckv, kr = _mla_project(hp, W, rp)
            kn, v = _mla_expand(ckv.astype(BF16), W)
            op = _mla_attend(qn, qr, kn, kr, v, B=B, Sq=S, Sk=S, sk_valid=S, tq=tq_p,
                             tk=_pick(S, (512, 256, 128)), qbase=0)
            xp = mm(op, w_out, res=xp, name="mix_out")
            st['a_ckv_p'].append(ckv.reshape(B, S, MLA_KV_LORA))
            st['a_krope_p'].append(kr[:, :MLA_ROPE].reshape(B, S, MLA_ROPE))
            qn, qr, ckv, kr = _mla_project(hs, W, rs)
            sk_valid = P + n
            Skp = _round_up(sk_valid, LANES)
            ckv_all = _with_cache(cache_a_ckv[j], ckv, Bd, Skp)
            kr_all = _with_cache(cache_a_krope[j], kr[:, :MLA_ROPE], Bd, Skp)
            kn, v = _mla_expand(ckv_all.astype(BF16), W)
            os_ = _mla_attend(qn, qr, kn, kr_all, v, B=Bd, Sq=n, Sk=Skp, sk_valid=sk_valid, tq=n,
                              tk=_pick(Skp, (512, 384, 256, 128)), qbase=P)
            xs = mm(os_, w_out, res=xs, name="mix_out")
            st['a_ckv_s'].append(ckv.reshape(Bd, n, MLA_KV_LORA))
            st['a_krope_s'].append(kr[:, :MLA_ROPE].reshape(Bd, n, MLA_ROPE))
        elif m == 1:
            H, dh = BAND_HEADS, D // BAND_HEADS
            w_in = b_w_in[j]
            wq, wk, wv = (w_in[:, t * H * dh:(t + 1) * H * dh].astype(BF16) for t in range(3))
            gq, gk = _tile_gain(b_q_norm[j], H * dh), _tile_gain(b_k_norm[j], H * dh)
            w_out = b_w_out[j].astype(BF16)
            npv = BAND_PREV_CHUNKS
            q = mm(hp, wq, gain=gq, gs=dh, norm=True, name="band_q")
            k = mm(hp, wk, gain=gk, gs=dh, norm=True, name="band_k")
            v = mm(hp, wv, name="band_v")
            win = npv * CHUNK + tq_p
            offs = [t * tq_p for t in range(npv * CHUNK // tq_p)] + [npv * CHUNK]
            bias = _band_bias(b_rel_bias[j], tq_p, win, offs)
            op = attention(q, k, v, B=B, Sq=S, Sk=S, sk_valid=S, tq=tq_p, tk=win, G=H, qw=1,
                           slots=_SLOT1, outs=_OUT1, scale=dh ** -0.5, window=win, n_prev=npv,
                           bias=bias, name="band_attn")
            xp = mm(op, w_out, res=xp, name="mix_out")
            r = min(npv * CHUNK, S)
            st['b_k_p'].append(k.reshape(B, S, H, dh)[:, S - r:])
            st['b_v_p'].append(v.reshape(B, S, H, dh)[:, S - r:])
            q = mm(hs, wq, gain=gq, gs=dh, norm=True, name="band_q")
            k = mm(hs, wk, gain=gk, gs=dh, norm=True, name="band_k")
            v = mm(hs, wv, name="band_v")
            rows = cache_b_k.shape[2]
            sk_valid = rows + n
            Skp = _round_up(sk_valid, LANES)
            k_all = _with_cache(cache_b_k[j].reshape(Bd, rows, H * dh), k, Bd, Skp)
            v_all = _with_cache(cache_b_v[j].reshape(Bd, rows, H * dh), v, Bd, Skp)
            bias = _band_bias(b_rel_bias[j], n, Skp, [rows])
            os_ = attention(q, k_all, v_all, B=Bd, Sq=n, Sk=Skp, sk_valid=sk_valid, tq=n, tk=Skp, G=H,
                            qw=1, slots=_SLOT1, outs=_OUT1, scale=dh ** -0.5, window=Skp, n_prev=npv,
                            qbase=P, kbase=P - rows, bias=bias, name="band_attn")
            xs = mm(os_, w_out, res=xs, name="mix_out")
            st['b_k_s'].append(k.reshape(Bd, n, H, dh))
            st['b_v_s'].append(v.reshape(Bd, n, H, dh))
        elif m == 2:
            Hq, Hk, dh = DSA_HEADS, DSA_KV_HEADS, DSA_HEAD_DIM
            w_in = c_w_in[j]
            oq, ok_, ov = Hq * dh, Hq * dh + Hk * dh, Hq * dh + 2 * Hk * dh
            oiq = ov + IDX_HEADS * IDX_DIM
            oik = oiq + IDX_DIM
            wq = w_in[:, :oq].astype(BF16)
            wk = w_in[:, oq:ok_].astype(BF16)
            wv = w_in[:, ok_:ov].astype(BF16)
            wiq = w_in[:, ov:oiq].astype(BF16)
            wik = _pad_cols(w_in[:, oiq:oik], LANES).astype(BF16)
            wiw = _pad_cols(w_in[:, oik:], LANES).astype(BF16)
            gq, gk = _tile_gain(c_q_norm[j], oq), _tile_gain(c_k_norm[j], Hk * dh)
            gik = _tile_gain(c_idx_k_norm[j], LANES)
            giw = jnp.full((LANES,), IDX_HEADS ** -0.5, F32)
            w_out = c_w_out[j].astype(BF16)
            rp128, rs128 = rope_pair(dh // 4, LANES)
            rp64, rs64 = rope_pair(IDX_DIM // 4, 64)

            def project(h, r128, r64):
                q = mm(h, wq, gain=gq, gs=dh, norm=True, rope=r128, name="dsa_q")
                k = mm(h, wk, gain=gk, gs=dh, norm=True, rope=r128, name="dsa_k")
                v = mm(h, wv, name="dsa_v")
                iq = mm(h, wiq, rope=r64, name="dsa_iq")
                ik = mm(h, wik, gain=gik, gs=64, norm=True, rope=r64, name="dsa_ik")
                iw = mm(h, wiw, gain=giw, name="dsa_iw")
                return q, k, v, iq, ik, iw

            q, k, v, iq, ik, iw = project(hp, rp128, rp64)
            ik2 = jnp.concatenate([ik[:, :IDX_DIM], ik[:, :IDX_DIM]], axis=1)
            op = dsa_attention(q, k, v, iq, ik2, iw, B=B, Sq=S, Sk=S, sk_valid=S, tq=tq_p,
                               tk=_pick(S, (512, 256, 128)), qbase=0, n_sel=min(DSA_TOPK, S // 4))
            xp = mm(op, w_out, res=xp, name="mix_out")
            st['c_k_p'].append(k.reshape(B, S, Hk, dh))
            st['c_v_p'].append(v.reshape(B, S, Hk, dh))
            st['c_idxk_p'].append(ik[:, :IDX_DIM].reshape(B, S, IDX_DIM))
            q, k, v, iq, ik, iw = project(hs, rs128, rs64)
            sk_valid = P + n
            Skp = _round_up(sk_valid, LANES)
            k_all = _with_cache(cache_c_k[j].reshape(Bd, P, Hk * dh), k, Bd, Skp)
            v_all = _with_cache(cache_c_v[j].reshape(Bd, P, Hk * dh), v, Bd, Skp)
            ik_all = _with_cache(cache_c_idxk[j], ik[:, :IDX_DIM], Bd, Skp)
            ik2 = jnp.concatenate([ik_all, ik_all], axis=1)
            os_ = dsa_attention(q, k_all, v_all, iq, ik2, iw, B=Bd, Sq=n, Sk=Skp, sk_valid=sk_valid,
                                tq=n, tk=_pick(Skp, (512, 384, 256, 128)), qbase=P,
                                n_sel=min(DSA_TOPK, sk_valid // 4))
            xs = mm(os_, w_out, res=xs, name="mix_out")
            st['c_k_s'].append(k.reshape(Bd, n, Hk, dh))
            st['c_v_s'].append(v.reshape(Bd, n, Hk, dh))
            st['c_idxk_s'].append(ik[:, :IDX_DIM].reshape(Bd, n, IDX_DIM))
        else:
            Hq, Hk, dh = SWA_HEADS, SWA_KV_HEADS, SWA_HEAD_DIM
            grp = Hq // Hk
            npairs = Hk // 2
            perm = np.array([[[2 * p * grp + t, (2 * p + 1) * grp + t] for t in range(grp)]
                             for p in range(npairs)]).reshape(-1)
            w_in = d_w_in[j]
            nqc, nkc = Hq * dh, Hk * dh
            wq = w_in[:, :nqc].reshape(D, Hq, dh)[:, perm].reshape(D, nqc).astype(BF16)
            wk = w_in[:, nqc:nqc + nkc].astype(BF16)
            wv = w_in[:, nqc + nkc:].astype(BF16)
            gq, gk = _tile_gain(d_q_norm[j], nqc), _tile_gain(d_k_norm[j], nkc)
            w_out = d_w_out[j].reshape(Hq, dh, D)[perm].reshape(nqc, D).astype(BF16)
            sinks = d_sinks[j][perm]
            rp, rs = rope_pair(dh // 4, 64)
            npv = SWA_PREV_CHUNKS
            slots = tuple((t, hf) for t in range(grp) for hf in ("lo", "hi"))
            outs = tuple(("pair", 2 * t, 2 * t + 1) for t in range(grp))

            def project(h, r):
                q = mm(h, wq, gain=gq, gs=64, norm=True, rope=r, name="swa_q")
                k = mm(h, wk, gain=gk, gs=64, norm=True, rope=r, name="swa_k")
                v = mm(h, wv, name="swa_v")
                return q, k, v

            q, k, v = project(hp, rp)
            win = npv * CHUNK + tq_p
            op = attention(q, k, v, B=B, Sq=S, Sk=S, sk_valid=S, tq=tq_p, tk=win, G=npairs, qw=grp,
                           slots=slots, outs=outs, scale=dh ** -0.5, window=win, n_prev=npv,
                           sinks=sinks, name="swa_attn")
            xp = mm(op, w_out, res=xp, name="mix_out")
            r = min(npv * CHUNK, S)
            st['d_k_p'].append(k.reshape(B, S, Hk, dh)[:, S - r:])
            st['d_v_p'].append(v.reshape(B, S, Hk, dh)[:, S - r:])
            q, k, v = project(hs, rs)
            rows = cache_d_k.shape[2]
            sk_valid = rows + n
            Skp = _round_up(sk_valid, LANES)
            k_all = _with_cache(cache_d_k[j].reshape(Bd, rows, nkc), k, Bd, Skp)
            v_all = _with_cache(cache_d_v[j].reshape(Bd, rows, nkc), v, Bd, Skp)
            os_ = attention(q, k_all, v_all, B=Bd, Sq=n, Sk=Skp, sk_valid=sk_valid, tq=n, tk=Skp,
                            G=npairs, qw=grp, slots=slots, outs=outs, scale=dh ** -0.5, window=Skp,
                            n_prev=npv, qbase=P, kbase=P - rows, sinks=sinks, name="swa_attn")
            xs = mm(os_, w_out, res=xs, name="mix_out")
            st['d_k_s'].append(k.reshape(Bd, n, Hk, dh))
            st['d_v_s'].append(v.reshape(Bd, n, Hk, dh))

        Hx, dx = XA_HEADS, XA_HEAD_DIM
        Mm = mem_prompt.shape[1]
        hm = rmsnorm_cast(mem_prompt.reshape(B * Mm, D), norm_mem[i])
        w_kv = xa_w_kv[i]
        mk = mm(hm, w_kv[:, :Hx * dx].astype(BF16), gain=_tile_gain(xa_k_norm[i], Hx * dx), gs=dx,
                norm=True, name="xa_k")
        mv = mm(hm, w_kv[:, Hx * dx:].astype(BF16), name="xa_v")
        st['mem_k_p'].append(mk.reshape(B, Mm, Hx, dx))
        st['mem_v_p'].append(mv.reshape(B, Mm, Hx, dx))
        wxq = xa_w_q[i].astype(BF16)
        gxq = _tile_gain(xa_q_norm[i], Hx * dx)
        wxo = xa_w_o[i].astype(BF16)

        def cross(x, mk2, mv2, Bx, Sx, tq):
            hx = rmsnorm_cast(x, norm_xattn[i])
            qx = mm(hx, wxq, gain=gxq, gs=dx, norm=True, name="xa_q")
            ox = attention(qx, mk2, mv2, B=Bx, Sq=Sx, Sk=Mm, sk_valid=Mm, tq=tq, tk=Mm, G=Hx, qw=1,
                           slots=_SLOT1, outs=_OUT1, scale=dx ** -0.5, masked=False, name="xattn")
            return mm(ox, wxo, res=x, name="xa_out")

        xp = cross(xp, mk, mv, B, S, _pick(S, (512, 256, 128)))
        xs = cross(xs, cache_mem_k[i].reshape(Bd * Mm, Hx * dx), cache_mem_v[i].reshape(Bd * Mm, Hx * dx),
                   Bd, n, n)

        w_up = ffn_w_up[i]
        wg, wv_ = w_up[:, :Fd].astype(BF16), w_up[:, Fd:].astype(BF16)
        w_dn = ffn_w_down[i].astype(BF16)
        hf = rmsnorm_cast(xp, norm_ffn[i])
        act, conv_p = ffn_up(hf, wg, wv_, ffn_conv_w[i], ffn_conv_b[i], jnp.zeros((B, CONV_W - 1, Fd), F32),
                             n_seq=B, seq_rows=S)
        xp = mm(act, w_dn, res=xp, name="ffn_down")
        hf = rmsnorm_cast(xs, norm_ffn[i])
        act, conv_s = ffn_up(hf, wg, wv_, ffn_conv_w[i], ffn_conv_b[i], state_ffn_conv[i],
                             n_seq=Bd, seq_rows=n)
        xs = mm(act, w_dn, res=xs, name="ffn_down")
        st['conv_p'].append(conv_p)
        st['conv_s'].append(conv_s)

    order = ('a_ckv_p', 'a_krope_p', 'b_k_p', 'b_v_p', 'c_k_p', 'c_v_p', 'c_idxk_p', 'd_k_p', 'd_v_p',
             'mem_k_p', 'mem_v_p', 'conv_p', 'a_ckv_s', 'a_krope_s', 'b_k_s', 'b_v_s', 'c_k_s', 'c_v_s',
             'c_idxk_s', 'd_k_s', 'd_v_s', 'conv_s')
    return (xp.reshape(B, S, D), xs.reshape(Bd, n, D)) + tuple(jnp.stack(st[nm]) for nm in order)
```

```python
import functools

import numpy as np
import jax
import jax.numpy as jnp
from jax import lax
from jax.experimental import pallas as pl
from jax.experimental.pallas import tpu as pltpu

F32, BF16, I32 = jnp.float32, jnp.bfloat16, jnp.int32

CHUNK = 64
CHUNK_SHIFT = 6
ROPE_THETA = 500000.0
EPS = 1e-6
NEG_INF = -1e30
MLA_HEADS, MLA_NOPE, MLA_ROPE, MLA_V = 16, 128, 64, 128
MLA_Q_LORA, MLA_KV_LORA = 512, 512
BAND_HEADS, BAND_PREV_CHUNKS, MAX_REL = 16, 8, 128
DSA_HEADS, DSA_KV_HEADS, DSA_HEAD_DIM = 16, 4, 128
IDX_HEADS, IDX_DIM, DSA_TOPK = 16, 64, 256
SWA_HEADS, SWA_KV_HEADS, SWA_HEAD_DIM, SWA_PREV_CHUNKS = 32, 8, 64, 2
XA_HEADS, XA_HEAD_DIM = 4, 128
CONV_W = 3

LANES = 128
VMEM_LIMIT_MAX = 56 * 1024 * 1024
VMEM_LIMIT_MIN = 32 * 1024 * 1024

M_INIT = -3.0e38
INT_MIN = -2 ** 31
BIG_COL = 2 ** 30


def _vmem_limit(block_bytes, temp_bytes=0):
    need = 2 * block_bytes + temp_bytes + (4 << 20)
    return int(min(max(need, VMEM_LIMIT_MIN), VMEM_LIMIT_MAX))


def _pick(n, cands):
    for c in cands:
        if n % c == 0:
            return c
    raise ValueError(f"no tile for {n}")


def _rmsnorm_kernel(x_ref, g_ref, o_ref):
    x = x_ref[...]
    ms = jnp.mean(x * x, axis=-1, keepdims=True)
    o_ref[...] = (x * lax.rsqrt(ms + EPS) * g_ref[...]).astype(o_ref.dtype)


def rmsnorm_cast(x, g):
    M, K = x.shape
    tm = _pick(M, (512, 256, 128, 64, 16, 8))
    return pl.pallas_call(
        _rmsnorm_kernel,
        grid=(M // tm,),
        in_specs=[pl.BlockSpec((tm, K), lambda i: (i, 0)), pl.BlockSpec((1, K), lambda i: (0, 0))],
        out_specs=pl.BlockSpec((tm, K), lambda i: (i, 0)),
        out_shape=jax.ShapeDtypeStruct((M, K), BF16),
        compiler_params=pltpu.CompilerParams(
            dimension_semantics=("parallel",), vmem_limit_bytes=_vmem_limit(tm * K * 6, tm * K * 8)),
        name="rmsnorm_cast",
    )(x, g.reshape(1, K).astype(F32))


def _group_norm_rope(y, gain_ref, rope_refs, gs, norm, half):
    tm, tn = y.shape
    if norm and gs == tn:
        ss = jnp.sum(y * y, axis=-1, keepdims=True)
        return y * lax.rsqrt(ss * (1.0 / gs) + EPS) * gain_ref[...]
    outs = []
    for c in range(tn // LANES):
        yc = y[:, c * LANES:(c + 1) * LANES]
        if norm:
            yy = yc * yc
            if gs == LANES:
                ss = jnp.sum(yy, axis=-1, keepdims=True)
            else:
                lo = lax.broadcasted_iota(I32, yc.shape, 1) < gs
                s_lo = jnp.sum(jnp.where(lo, yy, 0.0), axis=-1, keepdims=True)
                s_hi = jnp.sum(jnp.where(lo, 0.0, yy), axis=-1, keepdims=True)
                ss = jnp.where(lo, s_lo, s_hi)
            yc = yc * lax.rsqrt(ss * (1.0 / gs) + EPS) * gain_ref[:, c * LANES:(c + 1) * LANES]
        elif gain_ref is not None:
            yc = yc * gain_ref[:, c * LANES:(c + 1) * LANES]
        if half:
            c_ref, s1_ref, s2_ref = rope_refs
            yc = (yc * c_ref[...] + pltpu.roll(yc, LANES - half, 1) * s1_ref[...]
                  + pltpu.roll(yc, half, 1) * s2_ref[...])
        outs.append(yc)
    return outs[0] if len(outs) == 1 else jnp.concatenate(outs, axis=1)


def _mm_kernel(*refs, has_gain, gs, norm, half, has_res):
    a_ref, w_ref = refs[0], refs[1]
    pos = 2
    gain_ref = rope_refs = res_ref = None
    if has_gain:
        gain_ref = refs[pos]
        pos += 1
    if half:
        rope_refs = refs[pos:pos + 3]
        pos += 3
    if has_res:
        res_ref = refs[pos]
        pos += 1
    y = jnp.dot(a_ref[...], w_ref[...], preferred_element_type=F32)
    if has_gain or half:
        y = _group_norm_rope(y, gain_ref, rope_refs, gs, norm, half)
    if has_res:
        y = res_ref[...] + y
    for o_ref in refs[pos:]:
        o_ref[...] = y.astype(o_ref.dtype)


def mm(a, w, *, gain=None, gs=LANES, norm=False, rope=None, res=None, out_dtype=F32, tn=None,
       name="mm"):
    M, K = a.shape
    N = w.shape[1]
    assert a.dtype == BF16 and w.dtype == BF16 and N % LANES == 0
    if tn is None:
        tn = gs if (norm and gs > LANES) else _pick(N, (512, 256, 128))
    tm_c = (1024, 512, 256, 128, 64, 16, 8) if K <= 2048 else (512, 256, 128, 64, 16, 8)
    rows_period = M if rope is None else rope[1].shape[0]
    tm = _pick(np.gcd(M, rows_period), tm_c)
    half = 0
    in_specs = [pl.BlockSpec((tm, K), lambda i, j: (i, 0)), pl.BlockSpec((K, tn), lambda i, j: (0, j))]
    args = [a, w]
    blk = tm * K * 2 + K * tn * 2 + tm * tn * 4
    if gain is not None:
        in_specs.append(pl.BlockSpec((1, tn), lambda i, j: (0, j)))
        args.append(gain.reshape(1, N).astype(F32))
    if rope is not None:
        half, tabs = rope[0], rope[1:]
        nrb = rows_period // tm
        for t in tabs:
            in_specs.append(pl.BlockSpec((tm, LANES), lambda i, j, nrb=nrb: (i % nrb, 0)))
            args.append(t)
        blk += 3 * tm * LANES * 4
    if res is not None:
        in_specs.append(pl.BlockSpec((tm, tn), lambda i, j: (i, j)))
        args.append(res)
        blk += tm * tn * 4
    kern = functools.partial(_mm_kernel, has_gain=gain is not None, gs=gs, norm=norm, half=half,
                             has_res=res is not None)
    dtypes = out_dtype if isinstance(out_dtype, tuple) else (out_dtype,)
    outs = pl.pallas_call(
        kern,
        grid=(M // tm, N // tn),
        in_specs=in_specs,
        out_specs=[pl.BlockSpec((tm, tn), lambda i, j: (i, j)) for _ in dtypes],
        out_shape=[jax.ShapeDtypeStruct((M, N), dt) for dt in dtypes],
        compiler_params=pltpu.CompilerParams(
            dimension_semantics=("parallel", "arbitrary"),
            vmem_limit_bytes=_vmem_limit(blk + (len(dtypes) - 1) * tm * tn * 4, 3 * tm * tn * 4)),
        name=name,
    )(*args)
    return tuple(outs) if isinstance(out_dtype, tuple) else outs[0]


def _ffn_up_kernel(h_ref, wg_ref, wv_ref, cw_ref, cb_ref, prev_ref, act_ref, last_ref, carry_ref, *,
                   tiles_per_seq, seq_rows):
    i = pl.program_id(0)
    h = h_ref[...]
    g = jnp.dot(h, wg_ref[...], preferred_element_type=F32)
    val = jnp.dot(h, wv_ref[...], preferred_element_type=F32)
    tm, tn = g.shape
    row = lax.broadcasted_iota(I32, (tm, tn), 0)
    g1 = pltpu.roll(g, 1, 0)
    g2 = pltpu.roll(g, 2, 0)
    if tiles_per_seq >= 1:
        first = (i % tiles_per_seq) == 0
        j = pl.program_id(1)
        p_prev = prev_ref[0]
        p_carry = carry_ref[j]
        p2 = jnp.where(first, p_prev[0:1, :], p_carry[6:7, :])
        p1 = jnp.where(first, p_prev[1:2, :], p_carry[7:8, :])
        g1 = jnp.where(row == 0, p1, g1)
        g2 = jnp.where(row == 0, p2, jnp.where(row == 1, p1, g2))
        carry_ref[j] = g[tm - 8:tm, :]
        last_ref[0] = g[tm - 8:tm, :]
    else:
        rmod = row % seq_rows
        g1 = jnp.where(rmod == 0, prev_ref[0], g1)
        g2 = jnp.where(rmod < 2, prev_ref[1], g2)
        last_ref[...] = g
    c = cw_ref[0:1, :] * g2 + cw_ref[1:2, :] * g1 + cw_ref[2:3, :] * g + cb_ref[...]
    act_ref[...] = (c * (1.0 / (1.0 + jnp.exp(-c))) * val).astype(act_ref.dtype)


def ffn_up(h, wg, wv, conv_w, conv_b, prev, *, n_seq, seq_rows):
    M, D = h.shape
    Fd = wg.shape[1]
    tn = _pick(Fd, (512, 256, 128))
    nj = Fd // tn
    if seq_rows >= 256:
        tm = _pick(seq_rows, (1024, 512, 256))
        tps = seq_rows // tm
        prev_arg = prev
        prev_spec = pl.BlockSpec((1, 2, tn), lambda i, j: (i // tps, 0, j))
        last_shape = jax.ShapeDtypeStruct((M // tm, 8, Fd), F32)
        last_spec = pl.BlockSpec((1, 8, tn), lambda i, j: (i, 0, j))
    else:
        tm = M
        tps = 0
        z = jnp.zeros((n_seq, seq_rows - 1, Fd), F32)
        inj1 = jnp.concatenate([prev[:, 1:2], z], axis=1).reshape(M, Fd)
        inj2 = jnp.concatenate([prev[:, 0:2], z[:, 1:]], axis=1).reshape(M, Fd)
        prev_arg = jnp.stack([inj1, inj2])
        prev_spec = pl.BlockSpec((2, tm, tn), lambda i, j: (0, i, j))
        last_shape = jax.ShapeDtypeStruct((M, Fd), F32)
        last_spec = pl.BlockSpec((tm, tn), lambda i, j: (i, j))
    kern = functools.partial(_ffn_up_kernel, tiles_per_seq=tps, seq_rows=seq_rows)
    blk = tm * D * 2 + 2 * D * tn * 2 + tm * tn * 2 + 4 * tn * 4 + 2 * tm * tn * 4
    act, last = pl.pallas_call(
        kern,
        grid=(M // tm, nj),
        in_specs=[pl.BlockSpec((tm, D), lambda i, j: (i, 0)),
                  pl.BlockSpec((D, tn), lambda i, j: (0, j)),
                  pl.BlockSpec((D, tn), lambda i, j: (0, j)),
                  pl.BlockSpec((CONV_W, tn), lambda i, j: (0, j)),
                  pl.BlockSpec((1, tn), lambda i, j: (0, j)),
                  prev_spec],
        out_specs=[pl.BlockSpec((tm, tn), lambda i, j: (i, j)), last_spec],
        out_shape=[jax.ShapeDtypeStruct((M, Fd), BF16), last_shape],
        scratch_shapes=[pltpu.VMEM((nj, 8, tn), F32)],
        compiler_params=pltpu.CompilerParams(
            dimension_semantics=("arbitrary", "arbitrary"),
            vmem_limit_bytes=_vmem_limit(blk, 8 * tm * tn * 4)),
        name="ffn_up",
    )(h, wg, wv, conv_w.astype(F32), conv_b.reshape(1, Fd).astype(F32), prev_arg)
    if tps == 0:
        last = last.reshape(n_seq, seq_rows, Fd)[:, seq_rows - 2:]
    else:
        last = last[tps - 1::tps, 6:8]
    return act, last


def _attn_kernel(*refs, tq, tk, nq, sk_valid, window, n_prev, causal, masked, qbase, kbase, slots,
                 outs, scale, has_k2, has_bias, has_sinks, sk):
    q_ref, k_ref, v_ref = refs[0], refs[1], refs[2]
    pos = 3
    q2_ref = k2_ref = bias_ref = sinks_ref = None
    if has_k2:
        q2_ref, k2_ref = refs[pos], refs[pos + 1]
        pos += 2
    if has_bias:
        bias_ref = refs[pos]
        pos += 1
    if has_sinks:
        sinks_ref = refs[pos]
        pos += 1
    o_ref = refs[pos]
    m_ref, l_ref, acc_ref = refs[pos + 1], refs[pos + 2], refs[pos + 3]

    g = pl.program_id(1)
    i = pl.program_id(2)
    q0 = i * tq
    ns = len(slots)
    R = ns * tq
    lane = lax.broadcasted_iota(I32, (tq, LANES), 1)

    qs = []
    for (t, mode) in slots:
        qt = q_ref[:, t * LANES:(t + 1) * LANES]
        if mode == "lo":
            qt = jnp.where(lane < 64, qt, 0.0)
        elif mode == "hi":
            qt = jnp.where(lane >= 64, qt, 0.0)
        if has_k2:
            q2 = jnp.where((lane >> 6) == (g % 2), q2_ref[...], 0.0)
            qt = jnp.concatenate([qt, q2], axis=1)
        qs.append(qt.astype(BF16))
    Q = qs[0] if ns == 1 else jnp.concatenate(qs, axis=0)

    qpos1 = qbase + q0 + lax.broadcasted_iota(I32, (tq, 1), 0)
    qc1 = qpos1 >> CHUNK_SHIFT
    qc = qc1 if ns == 1 else jnp.concatenate([qc1] * ns, axis=0)

    if window is not None:
        lo_row = ((qbase + q0) >> CHUNK_SHIFT) * CHUNK - n_prev * CHUNK - kbase
        start = jnp.clip(lo_row, 0, sk - window)
        nb = window // tk
    elif causal:
        start = 0
        hi_row = (((qbase + q0 + tq - 1) >> CHUNK_SHIFT) + 1) * CHUNK - kbase
        nb = (jnp.minimum(hi_row, sk_valid) + tk - 1) // tk
    else:
        start = 0
        nb = sk // tk

    m_ref[...] = jnp.full((R, 1), M_INIT, F32)
    l_ref[...] = jnp.zeros((R, 1), F32)
    acc_ref[...] = jnp.zeros((R, LANES), F32)

    def body(jb, carry):
        r0 = pl.multiple_of(start + jb * tk, 8)
        kb = k_ref[pl.ds(r0, tk), :]
        if has_k2:
            kb = jnp.concatenate([kb, k2_ref[pl.ds(r0, tk), :]], axis=1)
        s = lax.dot_general(Q, kb.astype(BF16), (((1,), (1,)), ((), ())),
                            preferred_element_type=F32) * scale
        if has_bias:
            s = s + bias_ref[0, 0]
        if masked:
            krow = r0 + lax.broadcasted_iota(I32, (1, tk), 1)
            kc = (kbase + krow) >> CHUNK_SHIFT
            s = jnp.where(kc <= qc, s, NEG_INF)
            if n_prev is not None:
                s = jnp.where(kc >= qc - n_prev, s, NEG_INF)
            if sk_valid < sk:
                s = jnp.where(krow < sk_valid, s, NEG_INF)
        m_old = m_ref[...]
        m_new = jnp.maximum(m_old, jnp.max(s, axis=-1, keepdims=True))
        p = jnp.exp(s - m_new)
        alpha = jnp.exp(m_old - m_new)
        l_ref[...] = alpha * l_ref[...] + jnp.sum(p, axis=-1, keepdims=True)
        acc_ref[...] = alpha * acc_ref[...] + jnp.dot(
            p.astype(BF16), v_ref[pl.ds(r0, tk), :].astype(BF16), preferred_element_type=F32)
        m_ref[...] = m_new
        return carry

    lax.fori_loop(0, nb, body, 0)

    m = m_ref[...]
    l = l_ref[...]
    acc = acc_ref[...]
    if has_sinks:
        sk_rows = [jnp.full((tq, 1), sinks_ref[g * ns + s], F32) for s in range(ns)]
        sink = sk_rows[0] if ns == 1 else jnp.concatenate(sk_rows, axis=0)
        m_t = jnp.maximum(m, sink)
        a = jnp.exp(m - m_t)
        l = l * a + jnp.exp(sink - m_t)
        acc = acc * a
    o = acc / l
    for t, spec in enumerate(outs):
        if spec[0] == "slot":
            ot = o[spec[1] * tq:(spec[1] + 1) * tq]
        else:
            ot = jnp.where(lane < 64, o[spec[1] * tq:(spec[1] + 1) * tq],
                           o[spec[2] * tq:(spec[2] + 1) * tq])
        o_ref[:, t * LANES:(t + 1) * LANES] = ot.astype(o_ref.dtype)


def attention(q, k, v, *, B, Sq, Sk, sk_valid, tq, tk, G, qw, slots, outs, scale, window=None,
              n_prev=None, causal=False, masked=True, qbase=0, kbase=0, q2=None, k2=None, bias=None,
              sinks=None, name="attn"):
    nq = Sq // tq
    ns = len(slots)
    R = ns * tq
    ow = len(outs)
    in_specs = [pl.BlockSpec((tq, qw * LANES), lambda b, g, i: (b * nq + i, g)),
                pl.BlockSpec((Sk, LANES), lambda b, g, i: (b, g)),
                pl.BlockSpec((Sk, LANES), lambda b, g, i: (b, g))]
    args = [q, k, v]
    blk = tq * qw * LANES * 4 + 2 * Sk * LANES * 4 + tq * ow * LANES * 2
    if q2 is not None:
        in_specs += [pl.BlockSpec((tq, LANES), lambda b, g, i: (b * nq + i, g // 2)),
                     pl.BlockSpec((Sk, LANES), lambda b, g, i: (b, 0))]
        args += [q2, k2]
        blk += tq * LANES * 4 + Sk * LANES * 4
    if bias is not None:
        nv = bias.shape[0]
        assert window == tk
        in_specs.append(pl.BlockSpec((1, 1, tq, window),
                                     lambda b, g, i, nv=nv: (jnp.minimum(i, nv - 1), g, 0, 0)))
        args.append(bias)
        blk += tq * window * 4
    if sinks is not None:
        in_specs.append(pl.BlockSpec(memory_space=pltpu.SMEM))
        args.append(sinks.astype(F32))
    kern = functools.partial(
        _attn_kernel, tq=tq, tk=tk, nq=nq, sk_valid=sk_valid, window=window, n_prev=n_prev,
        causal=causal, masked=masked, qbase=qbase, kbase=kbase, slots=tuple(slots), outs=tuple(outs),
        scale=scale, has_k2=q2 is not None, has_bias=bias is not None, has_sinks=sinks is not None,
        sk=Sk)
    return pl.pallas_call(
        kern,
        grid=(B, G, nq),
        in_specs=in_specs,
        out_specs=pl.BlockSpec((tq, ow * LANES), lambda b, g, i: (b * nq + i, g)),
        out_shape=jax.ShapeDtypeStruct((B * Sq, G * ow * LANES), BF16),
        scratch_shapes=[pltpu.VMEM((R, 1), F32), pltpu.VMEM((R, 1), F32), pltpu.VMEM((R, LANES), F32)],
        compiler_params=pltpu.CompilerParams(
            dimension_semantics=("parallel", "parallel", "arbitrary"),
            vmem_limit_bytes=_vmem_limit(blk, 6 * R * tk * 4)),
        name=name,
    )(*args)


def _unit(slots, k_tile, outs, q2=None, bias=None, sinks=()):
    return (tuple(slots), k_tile, tuple(outs), q2, bias, tuple(sinks))


def _attn2_kernel(*refs, tq, nq, units, mode, win, sk, sk_valid, n_prev, masked, qbase, kbase, scale,
                  has_q2, has_bias, has_sinks, sinks_per_group):
    q_ref, k_ref, v_ref = refs[0], refs[1], refs[2]
    pos = 3
    q2_ref = k2_ref = bias_ref = sinks_ref = None
    if has_q2:
        q2_ref, k2_ref = refs[pos], refs[pos + 1]
        pos += 2
    if has_bias:
        bias_ref = refs[pos]
        pos += 1
    if has_sinks:
        sinks_ref = refs[pos]
        pos += 1
    o_ref = refs[pos]

    g = pl.program_id(1)
    i = pl.program_id(2)
    q0 = i * tq
    lane = lax.broadcasted_iota(I32, (tq, LANES), 1)
    half_mask = {"lo": jnp.where(lane < 64, 1.0, 0.0).astype(BF16),
                 "hi": jnp.where(lane >= 64, 1.0, 0.0).astype(BF16)}
    qc1 = (qbase + q0 + lax.broadcasted_iota(I32, (tq, 1), 0)) >> CHUNK_SHIFT

    def compute(r0, L):
        if masked:
            krow = r0 + lax.broadcasted_iota(I32, (1, L), 1)
            kc = (kbase + krow) >> CHUNK_SHIFT
        rows = pl.ds(r0, L) if not isinstance(r0, int) else slice(r0, r0 + L)
        for (slots, k_tile, outs, q2, bias_idx, sink_idx) in units:
            ns = len(slots)
            qs = []
            for (t, mk) in slots:
                qt = q_ref[:, t * LANES:(t + 1) * LANES]
                if mk is not None:
                    qt = qt * half_mask[mk]
                if q2 is not None:
                    q2t = q2_ref[:, q2[0] * LANES:(q2[0] + 1) * LANES] * half_mask[q2[1]]
                    qt = jnp.concatenate([qt, q2t], axis=1)
                qs.append(qt)
            Q = qs[0] if ns == 1 else jnp.concatenate(qs, axis=0)
            kb = k_ref[rows, k_tile * LANES:(k_tile + 1) * LANES]
            if q2 is not None:
                kb = jnp.concatenate([kb, k2_ref[rows, :]], axis=1)
            vb = v_ref[rows, k_tile * LANES:(k_tile + 1) * LANES]
            s = lax.dot_general(Q, kb, (((1,), (1,)), ((), ())), preferred_element_type=F32) * scale
            if bias_idx is not None:
                s = s + bias_ref[0, bias_idx]
            if masked:
                qc = qc1 if ns == 1 else jnp.concatenate([qc1] * ns, axis=0)
                s = jnp.where(kc <= qc, s, NEG_INF)
                if n_prev is not None:
                    s = jnp.where(kc >= qc - n_prev, s, NEG_INF)
                if sk_valid < sk:
                    s = jnp.where(krow < sk_valid, s, NEG_INF)
            m = jnp.max(s, axis=-1, keepdims=True)
            if sink_idx:
                sk_rows = [jnp.full((tq, 1), sinks_ref[g * sinks_per_group + si], F32) for si in sink_idx]
                sink = sk_rows[0] if ns == 1 else jnp.concatenate(sk_rows, axis=0)
                m = jnp.maximum(m, sink)
            p = jnp.exp(s - m)
            l = jnp.sum(p, axis=-1, keepdims=True)
            if sink_idx:
                l = l + jnp.exp(sink - m)
            o = jnp.dot(p.astype(BF16), vb, preferred_element_type=F32) / l
            for spec in outs:
                t = spec[0]
                if spec[1] == "slot":
                    ot = o[spec[2] * tq:(spec[2] + 1) * tq]
                else:
                    ot = jnp.where(lane < 64, o[spec[2] * tq:(spec[2] + 1) * tq],
                                   o[spec[3] * tq:(spec[3] + 1) * tq])
                o_ref[:, t * LANES:(t + 1) * LANES] = ot.astype(o_ref.dtype)

    if mode == "window":
        lo_row = ((qbase + q0) >> CHUNK_SHIFT) * CHUNK - n_prev * CHUNK - kbase
        compute(pl.multiple_of(jnp.clip(lo_row, 0, sk - win), 16), win)
    elif mode == "full":
        compute(0, sk)
    else:
        for c in range(nq):
            pl.when(i == c)(functools.partial(compute, 0, min(sk, (c + 1) * tq)))


def attention2(q, k, v, *, B, Sq, Sk, sk_valid, tq, G, qw, kw, ow, units, scale, mode, win=None,
               n_prev=None, masked=True, qbase=0, kbase=0, q2=None, q2w=0, k2=None, bias=None,
               sinks=None, sinks_per_group=0, name="attn"):
    nq = Sq // tq
    assert mode != "causal" or (qbase == 0 and kbase == 0 and tq % CHUNK == 0)
    in_specs = [pl.BlockSpec((tq, qw * LANES), lambda b, g, i: (b * nq + i, g)),
                pl.BlockSpec((Sk, kw * LANES), lambda b, g, i: (b, g)),
                pl.BlockSpec((Sk, kw * LANES), lambda b, g, i: (b, g))]
    args = [q, k, v]
    blk = tq * qw * LANES * 2 + 2 * Sk * kw * LANES * 2 + tq * ow * LANES * 2
    if q2 is not None:
        in_specs += [pl.BlockSpec((tq, q2w * LANES), lambda b, g, i: (b * nq + i, g)),
                     pl.BlockSpec((Sk, LANES), lambda b, g, i: (b, 0))]
        args += [q2, k2]
        blk += tq * q2w * LANES * 2 + Sk * LANES * 2
    if bias is not None:
        nv, nbh = bias.shape[0], bias.shape[1] // G
        in_specs.append(pl.BlockSpec((1, nbh, tq, win),
                                     lambda b, g, i, nv=nv: (jnp.minimum(i, nv - 1), g, 0, 0)))
        args.append(bias)
        blk += nbh * tq * win * 4
    if sinks is not None:
        in_specs.append(pl.BlockSpec(memory_space=pltpu.SMEM))
        args.append(sinks.astype(F32))
    L = win if mode == "window" else Sk
    temp = sum(len(u[0]) for u in units) * tq * L * 12
    kern = functools.partial(
        _attn2_kernel, tq=tq, nq=nq, units=tuple(units), mode=mode, win=win, sk=Sk, sk_valid=sk_valid,
        n_prev=n_prev, masked=masked, qbase=qbase, kbase=kbase, scale=scale, has_q2=q2 is not None,
        has_bias=bias is not None, has_sinks=sinks is not None, sinks_per_group=sinks_per_group)
    return pl.pallas_call(
        kern,
        grid=(B, G, nq),
        in_specs=in_specs,
        out_specs=pl.BlockSpec((tq, ow * LANES), lambda b, g, i: (b * nq + i, g)),
        out_shape=jax.ShapeDtypeStruct((B * Sq, G * ow * LANES), BF16),
        compiler_params=pltpu.CompilerParams(
            dimension_semantics=("parallel", "parallel", "arbitrary"),
            vmem_limit_bytes=_vmem_limit(blk, temp)),
        name=name,
    )(*args)


def _toeplitz_kernel(e_ref, o_ref, *, tq, win):
    wp = e_ref.shape[-1]
    x = jnp.broadcast_to(e_ref[0, 0, 0:1, :], (tq, wp))
    y = pltpu.roll(x, 0, 1, stride=1, stride_axis=0)
    o_ref[0, 0] = y[:, :win]


def band_bias(table, tq, win, offs):
    H = table.shape[1]
    wp = _round_up(win + tq - 1, LANES)
    m = np.arange(wp)
    m = np.where(m < win, m, m - wp)
    idx = np.stack([np.clip(off - m, -MAX_REL, MAX_REL) + MAX_REL for off in offs])
    e = jnp.transpose(table.astype(F32)[idx], (0, 2, 1))
    e = jnp.broadcast_to(e[:, :, None, :], (len(offs), H, 8, wp))
    return pl.pallas_call(
        functools.partial(_toeplitz_kernel, tq=tq, win=win),
        grid=(len(offs), H),
        in_specs=[pl.BlockSpec((1, 1, 8, wp), lambda v, h: (v, h, 0, 0))],
        out_specs=pl.BlockSpec((1, 1, tq, win), lambda v, h: (v, h, 0, 0)),
        out_shape=jax.ShapeDtypeStruct((len(offs), H, tq, win), F32),
        compiler_params=pltpu.CompilerParams(dimension_semantics=("parallel", "parallel")),
        name="band_bias",
    )(e)


def _dsa_kernel(q_ref, k_ref, v_ref, iq_ref, ik_ref, iw_ref, o_ref, key_ref, mb_ref, m_ref, l_ref,
                acc_ref, *, tq, tk, sk, sk_valid, qbase, n_sel, scale, idx_scale):
    i = pl.program_id(1)
    q0 = i * tq
    qpos = qbase + q0 + lax.broadcasted_iota(I32, (tq, 1), 0)
    qc = qpos >> CHUNK_SHIFT
    hi_row = (((qbase + q0 + tq - 1) >> CHUNK_SHIFT) + 1) * CHUNK
    nb = (jnp.minimum(hi_row, sk_valid) + tk - 1) // tk
    lane = lax.broadcasted_iota(I32, (tq, LANES), 1)
    n_idx_tiles = iq_ref.shape[1] // LANES

    def col_of(jb):
        return jb * tk + lax.broadcasted_iota(I32, (1, tk), 1)

    def mask_invisible(x, col):
        x = jnp.where((col >> CHUNK_SHIFT) <= qc, x, NEG_INF)
        if sk_valid < sk:
            x = jnp.where(col < sk_valid, x, NEG_INF)
        return x

    def score_body(jb, carry):
        r0 = pl.multiple_of(jb * tk, 8)
        ikb = ik_ref[pl.ds(r0, tk), :].astype(BF16)
        acc = jnp.zeros((tq, tk), F32)
        for p in range(n_idx_tiles):
            t = iq_ref[:, p * LANES:(p + 1) * LANES]
            for hf in range(2):
                qm = jnp.where((lane < 64) if hf == 0 else (lane >= 64), t, 0.0).astype(BF16)
                lg = lax.dot_general(qm, ikb, (((1,), (1,)), ((), ())),
                                     preferred_element_type=F32) * idx_scale
                h = 2 * p + hf
                acc = acc + iw_ref[:, h:h + 1] * jnp.maximum(lg, 0.0)
        sc = mask_invisible(acc, col_of(jb))
        bits = lax.bitcast_convert_type(sc, I32)
        key_ref[jb] = bits ^ ((bits >> 31) & 0x7FFFFFFF)
        return carry

    lax.fori_loop(0, nb, score_body, 0)

    def count(pred):
        def cb(jb, c):
            return c + jnp.sum(jnp.where(pred(key_ref[jb], jb), 1.0, 0.0), axis=1, keepdims=True)
        return lax.fori_loop(0, nb, cb, jnp.zeros((tq, 1), F32))

    def count_eq_before(jtrial):
        def cb(jb, c):
            hit = jnp.where(key_ref[jb] == thr, jnp.where(col_of(jb) < jtrial, 1.0, 0.0), 0.0)
            return c + jnp.sum(hit, axis=1, keepdims=True)
        return lax.fori_loop(0, nb, cb, jnp.zeros((tq, 1), F32))

    def bit_body(it, cand):
        trial = cand | (jnp.int32(1) << (31 - it))
        trial_s = trial ^ INT_MIN
        cnt = count(lambda kk, jb: kk >= trial_s)
        return jnp.where(cnt >= n_sel, trial, cand)

    cand = lax.fori_loop(0, 32, bit_body, jnp.zeros((tq, 1), I32))
    thr = cand ^ INT_MIN

    cnt_gt = count(lambda kk, jb: kk > thr)
    n_eq = count(lambda kk, jb: kk == thr)
    need = n_sel - cnt_gt
    neg_bits = int(np.float32(NEG_INF).view(np.int32))
    neg_key = neg_bits ^ ((neg_bits >> 31) & 0x7FFFFFFF)
    tie = jnp.where(thr > neg_key, jnp.where(n_eq > need, 1.0, 0.0), 0.0)
    any_tie = jnp.max(tie) > 0.0

    def tie_cols():
        def jbit(it, cj):
            trial = cj | (jnp.int32(1) << (30 - it))
            c = count_eq_before(trial)
            return jnp.where(c < need, trial, cj)
        return lax.fori_loop(0, 31, jbit, jnp.zeros((tq, 1), I32))

    jmax = lax.cond(any_tie, tie_cols, lambda: jnp.full((tq, 1), BIG_COL, I32))

    def mask_body(jb, carry):
        kk = key_ref[jb]
        col = col_of(jb)
        sel = jnp.where(kk > thr, 0.0, jnp.where(kk == thr, jnp.where(col <= jmax, 0.0, NEG_INF), NEG_INF))
        mb_ref[jb] = mask_invisible(sel, col)
        return carry

    lax.fori_loop(0, nb, mask_body, 0)

    n_kv = k_ref.shape[1] // LANES
    grp = (q_ref.shape[1] // LANES) // n_kv
    R = grp * tq
    for kvh in range(n_kv):
        Q = jnp.concatenate(
            [q_ref[:, (kvh * grp + gi) * LANES:(kvh * grp + gi + 1) * LANES] for gi in range(grp)],
            axis=0).astype(BF16)
        m_ref[...] = jnp.full((R, 1), M_INIT, F32)
        l_ref[...] = jnp.zeros((R, 1), F32)
        acc_ref[...] = jnp.zeros((R, LANES), F32)

        def body(jb, carry, kvh=kvh, Q=Q):
            r0 = pl.multiple_of(jb * tk, 8)
            kb = k_ref[pl.ds(r0, tk), kvh * LANES:(kvh + 1) * LANES].astype(BF16)
            vb = v_ref[pl.ds(r0, tk), kvh * LANES:(kvh + 1) * LANES].astype(BF16)
            s = lax.dot_general(Q, kb, (((1,), (1,)), ((), ())), preferred_element_type=F32) * scale
            mb = mb_ref[jb]
            s = jnp.where(jnp.concatenate([mb] * grp, axis=0) < 0.0, NEG_INF, s)
            m_old = m_ref[...]
            m_new = jnp.maximum(m_old, jnp.max(s, axis=-1, keepdims=True))
            p = jnp.exp(s - m_new)
            alpha = jnp.exp(m_old - m_new)
            l_ref[...] = alpha * l_ref[...] + jnp.sum(p, axis=-1, keepdims=True)
            acc_ref[...] = alpha * acc_ref[...] + jnp.dot(p.astype(BF16), vb,
                                                          preferred_element_type=F32)
            m_ref[...] = m_new
            return carry

        lax.fori_loop(0, nb, body, 0)
        o = acc_ref[...] / l_ref[...]
        for gi in range(grp):
            hh = kvh * grp + gi
            o_ref[:, hh * LANES:(hh + 1) * LANES] = o[gi * tq:(gi + 1) * tq].astype(o_ref.dtype)


def dsa_attention(q, k, v, iq, ik2, iw, *, B, Sq, Sk, sk_valid, tq, tk, qbase, n_sel):
    nq = Sq // tq
    nkb = Sk // tk
    Hq = q.shape[1]
    Hk = k.shape[1]
    grp = Hq // Hk
    R = grp * tq
    kern = functools.partial(_dsa_kernel, tq=tq, tk=tk, sk=Sk, sk_valid=sk_valid, qbase=qbase, n_sel=n_sel,
                             scale=DSA_HEAD_DIM ** -0.5, idx_scale=IDX_DIM ** -0.5)
    blk = tq * (Hq + iq.shape[1] + LANES) * 4 + Sk * (2 * Hk + LANES) * 4 + tq * Hq * 2
    return pl.pallas_call(
        kern,
        grid=(B, nq),
        in_specs=[pl.BlockSpec((tq, Hq), lambda b, i: (b * nq + i, 0)),
                  pl.BlockSpec((Sk, Hk), lambda b, i: (b, 0)),
                  pl.BlockSpec((Sk, Hk), lambda b, i: (b, 0)),
                  pl.BlockSpec((tq, iq.shape[1]), lambda b, i: (b * nq + i, 0)),
                  pl.BlockSpec((Sk, LANES), lambda b, i: (b, 0)),
                  pl.BlockSpec((tq, LANES), lambda b, i: (b * nq + i, 0))],
        out_specs=pl.BlockSpec((tq, Hq), lambda b, i: (b * nq + i, 0)),
        out_shape=jax.ShapeDtypeStruct((B * Sq, Hq), BF16),
        scratch_shapes=[pltpu.VMEM((nkb, tq, tk), I32), pltpu.VMEM((nkb, tq, tk), F32),
                        pltpu.VMEM((R, 1), F32), pltpu.VMEM((R, 1), F32), pltpu.VMEM((R, LANES), F32)],
        compiler_params=pltpu.CompilerParams(
            dimension_semantics=("parallel", "arbitrary"),
            vmem_limit_bytes=_vmem_limit(blk, 2 * tq * Sk * 4 + 6 * R * tk * 4)),
        name="dsa_attention",
    )(q, k, v, iq, ik2, iw)


def _rope_tables(pos, rot, group):
    half = rot // 2
    inv = ROPE_THETA ** (-jnp.arange(half, dtype=F32) / half)
    ang = pos.astype(F32)[:, None] * inv[None, :]
    c, s = jnp.cos(ang), jnp.sin(ang)
    lane = np.arange(LANES) % group
    is_lo = lane < half
    is_hi = (lane >= half) & (lane < rot)
    idx = np.where(is_lo, lane, np.where(is_hi, lane - half, 0))
    cg, sg = c[:, idx], s[:, idx]
    C = jnp.where(is_lo | is_hi, cg, 1.0)
    S1 = jnp.where(is_lo, -sg, 0.0)
    S2 = jnp.where(is_hi, sg, 0.0)
    return half, C, S1, S2


def _pad_cols(w, n):
    return jnp.pad(w, ((0, 0), (0, n - w.shape[1])))


def _tile_gain(g, n):
    return jnp.tile(g.astype(F32), n // g.shape[0])


def _band_bias(table, tq, window, offs):
    qi = np.arange(tq)[:, None]
    kj = np.arange(window)[None, :]
    idx = np.stack([np.clip(qi - kj + off, -MAX_REL, MAX_REL) + MAX_REL for off in offs])
    return jnp.transpose(table.astype(F32)[idx], (0, 3, 1, 2))


def _pad_rows(x, B, rows, rows_pad):
    if rows == rows_pad:
        return x
    x = x.reshape(B, rows, -1)
    x = jnp.pad(x, ((0, 0), (0, rows_pad - rows), (0, 0)))
    return x.reshape(B * rows_pad, -1)


def _with_cache(cache, new, B, rows_pad, dtype=F32):
    n = new.shape[0] // B
    P = cache.shape[1]
    parts = [cache.reshape(B, P, -1).astype(dtype), new.reshape(B, n, -1).astype(dtype)]
    if rows_pad > P + n:
        parts.append(jnp.zeros((B, rows_pad - P - n, new.shape[1]), dtype))
    return jnp.concatenate(parts, axis=1).reshape(B * rows_pad, -1)


def _round_up(x, m):
    return (x + m - 1) // m * m


_SLOT1 = ((0, None),)
_OUT1 = (("slot", 0),)


def _mla_weights(w_in, q_lora_norm, kv_lora_norm, w_uq, w_ukv, q_norm, k_norm):
    H = MLA_HEADS
    wq3 = w_uq.reshape(MLA_Q_LORA, H, MLA_NOPE + MLA_ROPE)
    wkv3 = w_ukv.reshape(MLA_KV_LORA, H, MLA_NOPE + MLA_V)
    return dict(
        w_cq=w_in[:, :MLA_Q_LORA].astype(BF16),
        w_ckv=w_in[:, MLA_Q_LORA:MLA_Q_LORA + MLA_KV_LORA].astype(BF16),
        w_kr=_pad_cols(w_in[:, MLA_Q_LORA + MLA_KV_LORA:], LANES).astype(BF16),
        g_cq=q_lora_norm, g_ckv=kv_lora_norm,
        g_kr=_tile_gain(k_norm[MLA_NOPE:], LANES),
        w_qn=wq3[:, :, :MLA_NOPE].reshape(MLA_Q_LORA, H * MLA_NOPE).astype(BF16),
        w_qr=wq3[:, :, MLA_NOPE:].reshape(MLA_Q_LORA, H * MLA_ROPE).astype(BF16),
        g_qn=_tile_gain(q_norm[:MLA_NOPE], H * MLA_NOPE),
        g_qr=_tile_gain(q_norm[MLA_NOPE:], H * MLA_ROPE),
        w_kn=wkv3[:, :, :MLA_NOPE].reshape(MLA_KV_LORA, H * MLA_NOPE).astype(BF16),
        w_v=wkv3[:, :, MLA_NOPE:].reshape(MLA_KV_LORA, H * MLA_V).astype(BF16),
        g_kn=_tile_gain(k_norm[:MLA_NOPE], H * MLA_NOPE),
    )


def _mla_project(h, W, rope64):
    cq = mm(h, W["w_cq"], gain=W["g_cq"], gs=MLA_Q_LORA, norm=True, out_dtype=BF16, name="mla_cq")
    ckv, ckv_bf = mm(h, W["w_ckv"], gain=W["g_ckv"], gs=MLA_KV_LORA, norm=True, out_dtype=(F32, BF16),
                     name="mla_ckv")
    kr = mm(h, W["w_kr"], gain=W["g_kr"], gs=64, norm=True, rope=rope64, name="mla_kr")
    qn = mm(cq, W["w_qn"], gain=W["g_qn"], gs=LANES, norm=True, out_dtype=BF16, name="mla_qn")
    qr = mm(cq, W["w_qr"], gain=W["g_qr"], gs=64, norm=True, rope=rope64, out_dtype=BF16, name="mla_qr")
    return qn, qr, ckv, ckv_bf, kr


def _mla_expand(ckv_bf, W):
    kn = mm(ckv_bf, W["w_kn"], gain=W["g_kn"], gs=LANES, norm=True, out_dtype=BF16, name="mla_kn")
    v = mm(ckv_bf, W["w_v"], out_dtype=BF16, name="mla_v")
    return kn, v


def _mla_attend(qn, qr, kn, kr, v, *, B, Sq, Sk, sk_valid, tq, hps, mode, qbase):
    kr2 = jnp.concatenate([kr[:, :MLA_ROPE], kr[:, :MLA_ROPE]], axis=1).astype(BF16)
    units = tuple(_unit(((h, None),), h, ((h, "slot", 0),), q2=(h // 2, "lo" if h % 2 == 0 else "hi"))
                  for h in range(hps))
    return attention2(qn, kn, v, B=B, Sq=Sq, Sk=Sk, sk_valid=sk_valid, tq=tq, G=MLA_HEADS // hps,
                      qw=hps, kw=hps, ow=hps, units=units, scale=(MLA_NOPE + MLA_ROPE) ** -0.5,
                      mode=mode, qbase=qbase, q2=qr, q2w=hps // 2, k2=kr2, name="mla_attn")


def kernel(x_prompt, x_sample, mem_prompt, cache_a_ckv, cache_a_krope, cache_b_k, cache_b_v, cache_c_k, cache_c_v, cache_c_idxk, cache_d_k, cache_d_v, cache_mem_k, cache_mem_v, state_ffn_conv, norm_mix, norm_xattn, norm_mem, norm_ffn, a_w_in, a_q_lora_norm, a_kv_lora_norm, a_w_uq, a_w_ukv, a_q_norm, a_k_norm, a_w_out, b_w_in, b_q_norm, b_k_norm, b_rel_bias, b_w_out, c_w_in, c_q_norm, c_k_norm, c_idx_k_norm, c_w_out, d_w_in, d_q_norm, d_k_norm, d_sinks, d_w_out, xa_w_q, xa_w_kv, xa_q_norm, xa_k_norm, xa_w_o, ffn_w_up, ffn_conv_w, ffn_conv_b, ffn_w_down):
    B, S, D = x_prompt.shape
    Bd, n, _ = x_sample.shape
    P = cache_a_ckv.shape[2]
    depth = norm_mix.shape[0]
    Fd = ffn_conv_b.shape[1]
    Mp, Ms = B * S, Bd * n
    pos_p = jnp.arange(S, dtype=I32)
    pos_s = jnp.tile(P + jnp.arange(n, dtype=I32), Bd)
    names = ('a_ckv_p', 'a_krope_p', 'b_k_p', 'b_v_p', 'c_k_p', 'c_v_p', 'c_idxk_p', 'd_k_p', 'd_v_p',
             'mem_k_p', 'mem_v_p', 'conv_p', 'a_ckv_s', 'a_krope_s', 'b_k_s', 'b_v_s', 'c_k_s', 'c_v_s',
             'c_idxk_s', 'd_k_s', 'd_v_s', 'conv_s')
    st = {name: [] for name in names}
    xp = x_prompt.reshape(Mp, D)
    xs = x_sample.reshape(Ms, D)
    tq_p = 128

    def rope_pair(rot, group):
        return _rope_tables(pos_p, rot, group), _rope_tables(pos_s, rot, group)

    for i in range(depth):
        m, j = i % 4, i // 4
        hp = rmsnorm_cast(xp, norm_mix[i])
        hs = rmsnorm_cast(xs, norm_mix[i])
        if m == 0:
            W = _mla_weights(a_w_in[j], a_q_lora_norm[j], a_kv_lora_norm[j], a_w_uq[j], a_w_ukv[j],
                             a_q_norm[j], a_k_norm[j])
            rp, rs = rope_pair(MLA_ROPE, 64)
            w_out = a_w_out[j].astype(BF16)
            qn, qr, ckv, ckv_bf, kr = _mla_project(hp, W, rp)
            kn, v = _mla_expand(ckv_bf, W)
            op = _mla_attend(qn, qr, kn, kr, v, B=B, Sq=S, Sk=S, sk_valid=S, tq=_pick(S, (256, 128)),
                             hps=2, mode="causal", qbase=0)
            xp = mm(op, w_out, res=xp, name="mix_out")
            st['a_ckv_p'].append(ckv.reshape(B, S, MLA_KV_LORA))
            st['a_krope_p'].append(kr[:, :MLA_ROPE].reshape(B, S, MLA_ROPE))
            qn, qr, ckv, ckv_bf, kr = _mla_project(hs, W, rs)
            sk_valid = P + n
            Skp = _round_up(sk_valid, LANES)
            ckv_all = _with_cache(cache_a_ckv[j], ckv_bf, Bd, Skp, BF16)
            kr_all = _with_cache(cache_a_krope[j], kr[:, :MLA_ROPE], Bd, Skp)
            kn, v = _mla_expand(ckv_all, W)
            os_ = _mla_attend(qn, qr, kn, kr_all, v, B=Bd, Sq=n, Sk=Skp, sk_valid=sk_valid, tq=n,
                              hps=MLA_HEADS, mode="full", qbase=P)
            xs = mm(os_, w_out, res=xs, name="mix_out")
            st['a_ckv_s'].append(ckv.reshape(Bd, n, MLA_KV_LORA))
            st['a_krope_s'].append(kr[:, :MLA_ROPE].reshape(Bd, n, MLA_ROPE))
        elif m == 1:
            H, dh = BAND_HEADS, D // BAND_HEADS
            w_in = b_w_in[j]
            wq, wk, wv = (w_in[:, t * H * dh:(t + 1) * H * dh].astype(BF16) for t in range(3))
            gq, gk = _tile_gain(b_q_norm[j], H * dh), _tile_gain(b_k_norm[j], H * dh)
            w_out = b_w_out[j].astype(BF16)
            npv = BAND_PREV_CHUNKS
            def band_units(nh):
                return tuple(_unit(((h, None),), h, ((h, "slot", 0),), bias=h) for h in range(nh))

            q = mm(hp, wq, gain=gq, gs=dh, norm=True, out_dtype=BF16, name="band_q")
            k, k_bf = mm(hp, wk, gain=gk, gs=dh, norm=True, out_dtype=(F32, BF16), name="band_k")
            v, v_bf = mm(hp, wv, out_dtype=(F32, BF16), name="band_v")
            win = npv * CHUNK + tq_p
            offs = [t * tq_p for t in range(npv * CHUNK // tq_p)] + [npv * CHUNK]
            bias = band_bias(b_rel_bias[j], tq_p, win, offs)
            hps = 4
            op = attention2(q, k_bf, v_bf, B=B, Sq=S, Sk=S, sk_valid=S, tq=tq_p, G=H // hps, qw=hps,
                            kw=hps, ow=hps, units=band_units(hps), scale=dh ** -0.5, mode="window",
                            win=win, n_prev=npv, bias=bias, name="band_attn")
            xp = mm(op, w_out, res=xp, name="mix_out")
            r = min(npv * CHUNK, S)
            st['b_k_p'].append(k.reshape(B, S, H, dh)[:, S - r:])
            st['b_v_p'].append(v.reshape(B, S, H, dh)[:, S - r:])
            q = mm(hs, wq, gain=gq, gs=dh, norm=True, out_dtype=BF16, name="band_q")
            k = mm(hs, wk, gain=gk, gs=dh, norm=True, name="band_k")
            v = mm(hs, wv, name="band_v")
            rows = cache_b_k.shape[2]
            sk_valid = rows + n
            Skp = _round_up(sk_valid, LANES)
            k_all = _with_cache(cache_b_k[j].reshape(Bd, rows, H * dh), k, Bd, Skp, BF16)
            v_all = _with_cache(cache_b_v[j].reshape(Bd, rows, H * dh), v, Bd, Skp, BF16)
            bias = band_bias(b_rel_bias[j], n, Skp, [rows])
            os_ = attention2(q, k_all, v_all, B=Bd, Sq=n, Sk=Skp, sk_valid=sk_valid, tq=n, G=1, qw=H,
                             kw=H, ow=H, units=band_units(H), scale=dh ** -0.5, mode="window", win=Skp,
                             n_prev=npv, qbase=P, kbase=P - rows, bias=bias, name="band_attn")
            xs = mm(os_, w_out, res=xs, name="mix_out")
            st['b_k_s'].append(k.reshape(Bd, n, H, dh))
            st['b_v_s'].append(v.reshape(Bd, n, H, dh))
        elif m == 2:
            Hq, Hk, dh = DSA_HEADS, DSA_KV_HEADS, DSA_HEAD_DIM
            w_in = c_w_in[j]
            oq, ok_, ov = Hq * dh, Hq * dh + Hk * dh, Hq * dh + 2 * Hk * dh
            oiq = ov + IDX_HEADS * IDX_DIM
            oik = oiq + IDX_DIM
            wq = w_in[:, :oq].astype(BF16)
            wk = w_in[:, oq:ok_].astype(BF16)
            wv = w_in[:, ok_:ov].astype(BF16)
            wiq = w_in[:, ov:oiq].astype(BF16)
            wik = _pad_cols(w_in[:, oiq:oik], LANES).astype(BF16)
            wiw = _pad_cols(w_in[:, oik:], LANES).astype(BF16)
            gq, gk = _tile_gain(c_q_norm[j], oq), _tile_gain(c_k_norm[j], Hk * dh)
            gik = _tile_gain(c_idx_k_norm[j], LANES)
            giw = jnp.full((LANES,), IDX_HEADS ** -0.5, F32)
            w_out = c_w_out[j].astype(BF16)
            rp128, rs128 = rope_pair(dh // 4, LANES)
            rp64, rs64 = rope_pair(IDX_DIM // 4, 64)

            def project(h, r128, r64):
                q = mm(h, wq, gain=gq, gs=dh, norm=True, rope=r128, name="dsa_q")
                k = mm(h, wk, gain=gk, gs=dh, norm=True, rope=r128, name="dsa_k")
                v = mm(h, wv, name="dsa_v")
                iq = mm(h, wiq, rope=r64, name="dsa_iq")
                ik = mm(h, wik, gain=gik, gs=64, norm=True, rope=r64, name="dsa_ik")
                iw = mm(h, wiw, gain=giw, name="dsa_iw")
                return q, k, v, iq, ik, iw

            q, k, v, iq, ik, iw = project(hp, rp128, rp64)
            ik2 = jnp.concatenate([ik[:, :IDX_DIM], ik[:, :IDX_DIM]], axis=1)
            op = dsa_attention(q, k, v, iq, ik2, iw, B=B, Sq=S, Sk=S, sk_valid=S, tq=tq_p,
                               tk=_pick(S, (512, 256, 128)), qbase=0, n_sel=min(DSA_TOPK, S // 4))
            xp = mm(op, w_out, res=xp, name="mix_out")
            st['c_k_p'].append(k.reshape(B, S, Hk, dh))
            st['c_v_p'].append(v.reshape(B, S, Hk, dh))
            st['c_idxk_p'].append(ik[:, :IDX_DIM].reshape(B, S, IDX_DIM))
            q, k, v, iq, ik, iw = project(hs, rs128, rs64)
            sk_valid = P + n
            Skp = _round_up(sk_valid, LANES)
            k_all = _with_cache(cache_c_k[j].reshape(Bd, P, Hk * dh), k, Bd, Skp)
            v_all = _with_cache(cache_c_v[j].reshape(Bd, P, Hk * dh), v, Bd, Skp)
            ik_all = _with_cache(cache_c_idxk[j], ik[:, :IDX_DIM], Bd, Skp)
            ik2 = jnp.concatenate([ik_all, ik_all], axis=1)
            os_ = dsa_attention(q, k_all, v_all, iq, ik2, iw, B=Bd, Sq=n, Sk=Skp, sk_valid=sk_valid,
                                tq=n, tk=_pick(Skp, (512, 384, 256, 128)), qbase=P,
                                n_sel=min(DSA_TOPK, sk_valid // 4))
            xs = mm(os_, w_out, res=xs, name="mix_out")
            st['c_k_s'].append(k.reshape(Bd, n, Hk, dh))
            st['c_v_s'].append(v.reshape(Bd, n, Hk, dh))
            st['c_idxk_s'].append(ik[:, :IDX_DIM].reshape(Bd, n, IDX_DIM))
        else:
            Hq, Hk, dh = SWA_HEADS, SWA_KV_HEADS, SWA_HEAD_DIM
            grp = Hq // Hk
            npairs = Hk // 2
            perm = np.array([[[2 * p * grp + t, (2 * p + 1) * grp + t] for t in range(grp)]
                             for p in range(npairs)]).reshape(-1)
            w_in = d_w_in[j]
            nqc, nkc = Hq * dh, Hk * dh
            wq = w_in[:, :nqc].reshape(D, Hq, dh)[:, perm].reshape(D, nqc).astype(BF16)
            wk = w_in[:, nqc:nqc + nkc].astype(BF16)
            wv = w_in[:, nqc + nkc:].astype(BF16)
            gq, gk = _tile_gain(d_q_norm[j], nqc), _tile_gain(d_k_norm[j], nkc)
            w_out = d_w_out[j].reshape(Hq, dh, D)[perm].reshape(nqc, D).astype(BF16)
            sinks = d_sinks[j][perm]
            rp, rs = rope_pair(dh // 4, 64)
            npv = SWA_PREV_CHUNKS
            def swa_units(npr):
                us = []
                for p in range(npr):
                    slots = tuple((p * grp + t, hf) for t in range(grp) for hf in ("lo", "hi"))
                    outs = tuple((p * grp + t, "pair", 2 * t, 2 * t + 1) for t in range(grp))
                    us.append(_unit(slots, p, outs, sinks=range(2 * grp * p, 2 * grp * (p + 1))))
                return tuple(us)

            def project(h, r):
                q = mm(h, wq, gain=gq, gs=64, norm=True, rope=r, out_dtype=BF16, name="swa_q")
                k, k_bf = mm(h, wk, gain=gk, gs=64, norm=True, rope=r, out_dtype=(F32, BF16), name="swa_k")
                v, v_bf = mm(h, wv, out_dtype=(F32, BF16), name="swa_v")
                return q, k, v, k_bf, v_bf

            q, k, v, k_bf, v_bf = project(hp, rp)
            win = npv * CHUNK + tq_p
            pps = 2
            op = attention2(q, k_bf, v_bf, B=B, Sq=S, Sk=S, sk_valid=S, tq=tq_p, G=npairs // pps,
                            qw=pps * grp, kw=pps, ow=pps * grp, units=swa_units(pps), scale=dh ** -0.5,
                            mode="window", win=win, n_prev=npv, sinks=sinks,
                            sinks_per_group=2 * grp * pps, name="swa_attn")
            xp = mm(op, w_out, res=xp, name="mix_out")
            r = min(npv * CHUNK, S)
            st['d_k_p'].append(k.reshape(B, S, Hk, dh)[:, S - r:])
            st['d_v_p'].append(v.reshape(B, S, Hk, dh)[:, S - r:])
            q, k, v, _, _ = project(hs, rs)
            rows = cache_d_k.shape[2]
            sk_valid = rows + n
            Skp = _round_up(sk_valid, LANES)
            k_all = _with_cache(cache_d_k[j].reshape(Bd, rows, nkc), k, Bd, Skp, BF16)
            v_all = _with_cache(cache_d_v[j].reshape(Bd, rows, nkc), v, Bd, Skp, BF16)
            os_ = attention2(q, k_all, v_all, B=Bd, Sq=n, Sk=Skp, sk_valid=sk_valid, tq=n, G=1,
                             qw=npairs * grp, kw=npairs, ow=npairs * grp, units=swa_units(npairs),
                             scale=dh ** -0.5, mode="window", win=Skp, n_prev=npv, qbase=P,
                             kbase=P - rows, sinks=sinks, sinks_per_group=2 * grp * npairs,
                             name="swa_attn")
            xs = mm(os_, w_out, res=xs, name="mix_out")
            st['d_k_s'].append(k.reshape(Bd, n, Hk, dh))
            st['d_v_s'].append(v.reshape(Bd, n, Hk, dh))

        Hx, dx = XA_HEADS, XA_HEAD_DIM
        Mm = mem_prompt.shape[1]
        hm = rmsnorm_cast(mem_prompt.reshape(B * Mm, D), norm_mem[i])
        w_kv = xa_w_kv[i]
        mk, mk_bf = mm(hm, w_kv[:, :Hx * dx].astype(BF16), gain=_tile_gain(xa_k_norm[i], Hx * dx), gs=dx,
                       norm=True, out_dtype=(F32, BF16), name="xa_k")
        mv, mv_bf = mm(hm, w_kv[:, Hx * dx:].astype(BF16), out_dtype=(F32, BF16), name="xa_v")
        st['mem_k_p'].append(mk.reshape(B, Mm, Hx, dx))
        st['mem_v_p'].append(mv.reshape(B, Mm, Hx, dx))
        wxq = xa_w_q[i].astype(BF16)
        gxq = _tile_gain(xa_q_norm[i], Hx * dx)
        wxo = xa_w_o[i].astype(BF16)
        xa_units = tuple(_unit(((h, None),), h, ((h, "slot", 0),)) for h in range(Hx))

        def cross(x, mk2, mv2, Bx, Sx, tq):
            hx = rmsnorm_cast(x, norm_xattn[i])
            qx = mm(hx, wxq, gain=gxq, gs=dx, norm=True, out_dtype=BF16, name="xa_q")
            ox = attention2(qx, mk2, mv2, B=Bx, Sq=Sx, Sk=Mm, sk_valid=Mm, tq=tq, G=1, qw=Hx, kw=Hx,
                            ow=Hx, units=xa_units, scale=dx ** -0.5, mode="full", masked=False,
                            name="xattn")
            return mm(ox, wxo, res=x, name="xa_out")

        xp = cross(xp, mk_bf, mv_bf, B, S, _pick(S, (512, 256, 128)))
        xs = cross(xs, cache_mem_k[i].reshape(Bd * Mm, Hx * dx).astype(BF16),
                   cache_mem_v[i].reshape(Bd * Mm, Hx * dx).astype(BF16), Bd, n, n)

        w_up = ffn_w_up[i]
        wg, wv_ = w_up[:, :Fd].astype(BF16), w_up[:, Fd:].astype(BF16)
        w_dn = ffn_w_down[i].astype(BF16)
        hf = rmsnorm_cast(xp, norm_ffn[i])
        act, conv_p = ffn_up(hf, wg, wv_, ffn_conv_w[i], ffn_conv_b[i], jnp.zeros((B, CONV_W - 1, Fd), F32),
                             n_seq=B, seq_rows=S)
        xp = mm(act, w_dn, res=xp, name="ffn_down")
        hf = rmsnorm_cast(xs, norm_ffn[i])
        act, conv_s = ffn_up(hf, wg, wv_, ffn_conv_w[i], ffn_conv_b[i], state_ffn_conv[i],
                             n_seq=Bd, seq_rows=n)
        xs = mm(act, w_dn, res=xs, name="ffn_down")
        st['conv_p'].append(conv_p)
        st['conv_s'].append(conv_s)

    order = ('a_ckv_p', 'a_krope_p', 'b_k_p', 'b_v_p', 'c_k_p', 'c_v_p', 'c_idxk_p', 'd_k_p', 'd_v_p',
             'mem_k_p', 'mem_v_p', 'conv_p', 'a_ckv_s', 'a_krope_s', 'b_k_s', 'b_v_s', 'c_k_s', 'c_v_s',
             'c_idxk_s', 'd_k_s', 'd_v_s', 'conv_s')
    return (xp.reshape(B, S, D), xs.reshape(Bd, n, D)) + tuple(jnp.stack(st[nm]) for nm in order)
```

```python
import functools

import numpy as np
import jax
import jax.numpy as jnp
from jax import lax
from jax.experimental import pallas as pl
from jax.experimental.pallas import tpu as pltpu

F32, BF16, I32 = jnp.float32, jnp.bfloat16, jnp.int32

CHUNK = 64
CHUNK_SHIFT = 6
ROPE_THETA = 500000.0
EPS = 1e-6
NEG_INF = -1e30
MLA_HEADS, MLA_NOPE, MLA_ROPE, MLA_V = 16, 128, 64, 128
MLA_Q_LORA, MLA_KV_LORA = 512, 512
BAND_HEADS, BAND_PREV_CHUNKS, MAX_REL = 16, 8, 128
DSA_HEADS, DSA_KV_HEADS, DSA_HEAD_DIM = 16, 4, 128
IDX_HEADS, IDX_DIM, DSA_TOPK = 16, 64, 256
SWA_HEADS, SWA_KV_HEADS, SWA_HEAD_DIM, SWA_PREV_CHUNKS = 32, 8, 64, 2
XA_HEADS, XA_HEAD_DIM = 4, 128
CONV_W = 3

LANES = 128
VMEM_LIMIT_MAX = 56 * 1024 * 1024
VMEM_LIMIT_MIN = 32 * 1024 * 1024

M_INIT = -3.0e38
INT_MIN = -2 ** 31
BIG_COL = 2 ** 30
SOFTMAX_ROWS = 64


def _vmem_limit(block_bytes, temp_bytes=0):
    need = 2 * block_bytes + temp_bytes + (4 << 20)
    return int(min(max(need, VMEM_LIMIT_MIN), VMEM_LIMIT_MAX))


def _pick(n, cands):
    for c in cands:
        if n % c == 0:
            return c
    raise ValueError(f"no tile for {n}")


def _rmsnorm_kernel(x_ref, g_ref, o_ref):
    x = x_ref[...]
    ms = jnp.mean(x * x, axis=-1, keepdims=True)
    o_ref[...] = (x * lax.rsqrt(ms + EPS) * g_ref[...]).astype(o_ref.dtype)


def rmsnorm_cast(x, g):
    M, K = x.shape
    tm = _pick(M, (512, 256, 128, 64, 16, 8))
    return pl.pallas_call(
        _rmsnorm_kernel,
        grid=(M // tm,),
        in_specs=[pl.BlockSpec((tm, K), lambda i: (i, 0)), pl.BlockSpec((1, K), lambda i: (0, 0))],
        out_specs=pl.BlockSpec((tm, K), lambda i: (i, 0)),
        out_shape=jax.ShapeDtypeStruct((M, K), BF16),
        compiler_params=pltpu.CompilerParams(
            dimension_semantics=("parallel",), vmem_limit_bytes=_vmem_limit(tm * K * 6, tm * K * 8)),
        name="rmsnorm_cast",
    )(x, g.reshape(1, K).astype(F32))


def _group_norm_rope(y, gain_ref, rope_refs, gs, norm, half):
    tm, tn = y.shape
    if norm and gs == tn:
        ss = jnp.sum(y * y, axis=-1, keepdims=True)
        return y * lax.rsqrt(ss * (1.0 / gs) + EPS) * gain_ref[...]
    outs = []
    for c in range(tn // LANES):
        yc = y[:, c * LANES:(c + 1) * LANES]
        if norm:
            yy = yc * yc
            if gs == LANES:
                ss = jnp.sum(yy, axis=-1, keepdims=True)
            else:
                lo = lax.broadcasted_iota(I32, yc.shape, 1) < gs
                s_lo = jnp.sum(jnp.where(lo, yy, 0.0), axis=-1, keepdims=True)
                s_hi = jnp.sum(jnp.where(lo, 0.0, yy), axis=-1, keepdims=True)
                ss = jnp.where(lo, s_lo, s_hi)
            yc = yc * lax.rsqrt(ss * (1.0 / gs) + EPS) * gain_ref[:, c * LANES:(c + 1) * LANES]
        elif gain_ref is not None:
            yc = yc * gain_ref[:, c * LANES:(c + 1) * LANES]
        outs.append(yc)
    y = outs[0] if len(outs) == 1 else jnp.concatenate(outs, axis=1)
    if not half:
        return y
    c_ref, s_ref, p_ref = rope_refs
    pw = p_ref.shape[0]
    outs = []
    for b in range(tn // pw):
        yb = y[:, b * pw:(b + 1) * pw]
        hi = yb.astype(BF16)
        lo = (yb - hi.astype(F32)).astype(BF16)
        partner = (jnp.dot(hi, p_ref[...], preferred_element_type=F32)
                   + jnp.dot(lo, p_ref[...], preferred_element_type=F32))
        outs.append(yb * c_ref[...] + partner * s_ref[...])
    return outs[0] if len(outs) == 1 else jnp.concatenate(outs, axis=1)


def _mm_kernel(*refs, has_gain, gs, norm, half, has_res):
    a_ref, w_ref = refs[0], refs[1]
    pos = 2
    gain_ref = rope_refs = res_ref = None
    if has_gain:
        gain_ref = refs[pos]
        pos += 1
    if half:
        rope_refs = refs[pos:pos + 3]
        pos += 3
    if has_res:
        res_ref = refs[pos]
        pos += 1
    y = jnp.dot(a_ref[...], w_ref[...], preferred_element_type=F32)
    if has_gain or half:
        y = _group_norm_rope(y, gain_ref, rope_refs, gs, norm, half)
    if has_res:
        y = res_ref[...] + y
    for o_ref in refs[pos:]:
        o_ref[...] = y.astype(o_ref.dtype)


def mm(a, w, *, gain=None, gs=LANES, norm=False, rope=None, res=None, out_dtype=F32, tn=None,
       name="mm"):
    M, K = a.shape
    N = w.shape[1]
    assert a.dtype == BF16 and w.dtype == BF16 and N % LANES == 0
    if tn is None:
        tn = gs if (norm and gs > LANES) else _pick(N, (512, 256, 128))
    if K > 2048 and tn > 256 and N % 256 == 0:
        tn = 256
    tm_c = (1024, 512, 256, 128, 64, 16, 8)
    rows_period = M if rope is None else rope[1].shape[0]
    tm = _pick(np.gcd(M, rows_period), tm_c)
    half = 0
    in_specs = [pl.BlockSpec((tm, K), lambda i, j: (i, 0)), pl.BlockSpec((K, tn), lambda i, j: (0, j))]
    args = [a, w]
    blk = tm * K * 2 + K * tn * 2 + tm * tn * 4
    if gain is not None:
        in_specs.append(pl.BlockSpec((1, tn), lambda i, j: (0, j)))
        args.append(gain.reshape(1, N).astype(F32))
    if rope is not None:
        half, tab_c, tab_s, perm = rope
        if tn % perm.shape[0] != 0:
            tab_c, tab_s, perm = tab_c[:, :LANES], tab_s[:, :LANES], perm[:LANES, :LANES]
        pw = perm.shape[0]
        assert tn % pw == 0
        nrb = rows_period // tm
        for t in (tab_c, tab_s):
            in_specs.append(pl.BlockSpec((tm, pw), lambda i, j, nrb=nrb: (i % nrb, 0)))
            args.append(t)
        in_specs.append(pl.BlockSpec((pw, pw), lambda i, j: (0, 0)))
        args.append(perm)
        blk += 2 * tm * pw * 4 + pw * pw * 2
    if res is not None:
        in_specs.append(pl.BlockSpec((tm, tn), lambda i, j: (i, j)))
        args.append(res)
        blk += tm * tn * 4
    kern = functools.partial(_mm_kernel, has_gain=gain is not None, gs=gs, norm=norm, half=half,
                             has_res=res is not None)
    dtypes = out_dtype if isinstance(out_dtype, tuple) else (out_dtype,)
    outs = pl.pallas_call(
        kern,
        grid=(M // tm, N // tn),
        in_specs=in_specs,
        out_specs=[pl.BlockSpec((tm, tn), lambda i, j: (i, j)) for _ in dtypes],
        out_shape=[jax.ShapeDtypeStruct((M, N), dt) for dt in dtypes],
        compiler_params=pltpu.CompilerParams(
            dimension_semantics=("parallel", "arbitrary"),
            vmem_limit_bytes=_vmem_limit(blk + (len(dtypes) - 1) * tm * tn * 4, 3 * tm * tn * 4)),
        name=name,
    )(*args)
    return tuple(outs) if isinstance(out_dtype, tuple) else outs[0]


def _ffn_up_kernel(h_ref, wg_ref, wv_ref, cw_ref, cb_ref, prev_ref, act_ref, last_ref, carry_ref, *,
                   tiles_per_seq, seq_rows):
    i = pl.program_id(0)
    h = h_ref[...]
    g = jnp.dot(h, wg_ref[...], preferred_element_type=F32)
    val = jnp.dot(h, wv_ref[...], preferred_element_type=F32)
    tm, tn = g.shape
    row = lax.broadcasted_iota(I32, (tm, tn), 0)
    g1 = pltpu.roll(g, 1, 0)
    g2 = pltpu.roll(g, 2, 0)
    if tiles_per_seq >= 1:
        first = (i % tiles_per_seq) == 0
        j = pl.program_id(1)
        p_prev = prev_ref[0]
        p_carry = carry_ref[j]
        p2 = jnp.where(first, p_prev[0:1, :], p_carry[6:7, :])
        p1 = jnp.where(first, p_prev[1:2, :], p_carry[7:8, :])
        g1 = jnp.where(row == 0, p1, g1)
        g2 = jnp.where(row == 0, p2, jnp.where(row == 1, p1, g2))
        carry_ref[j] = g[tm - 8:tm, :]
        last_ref[0] = g[tm - 8:tm, :]
    else:
        rmod = row % seq_rows
        g1 = jnp.where(rmod == 0, prev_ref[0], g1)
        g2 = jnp.where(rmod < 2, prev_ref[1], g2)
        last_ref[...] = g
    c = cw_ref[0:1, :] * g2 + cw_ref[1:2, :] * g1 + cw_ref[2:3, :] * g + cb_ref[...]
    act_ref[...] = (c * (1.0 / (1.0 + jnp.exp(-c))) * val).astype(act_ref.dtype)


def ffn_up(h, wg, wv, conv_w, conv_b, prev, *, n_seq, seq_rows):
    M, D = h.shape
    Fd = wg.shape[1]
    tn = _pick(Fd, (512, 256, 128))
    nj = Fd // tn
    if seq_rows >= 256:
        tm = _pick(seq_rows, (1024, 512, 256))
        tps = seq_rows // tm
        prev_arg = prev
        prev_spec = pl.BlockSpec((1, 2, tn), lambda i, j: (i // tps, 0, j))
        last_shape = jax.ShapeDtypeStruct((M // tm, 8, Fd), F32)
        last_spec = pl.BlockSpec((1, 8, tn), lambda i, j: (i, 0, j))
    else:
        tm = M
        tps = 0
        z = jnp.zeros((n_seq, seq_rows - 1, Fd), F32)
        inj1 = jnp.concatenate([prev[:, 1:2], z], axis=1).reshape(M, Fd)
        inj2 = jnp.concatenate([prev[:, 0:2], z[:, 1:]], axis=1).reshape(M, Fd)
        prev_arg = jnp.stack([inj1, inj2])
        prev_spec = pl.BlockSpec((2, tm, tn), lambda i, j: (0, i, j))
        last_shape = jax.ShapeDtypeStruct((M, Fd), F32)
        last_spec = pl.BlockSpec((tm, tn), lambda i, j: (i, j))
    kern = functools.partial(_ffn_up_kernel, tiles_per_seq=tps, seq_rows=seq_rows)
    blk = tm * D * 2 + 2 * D * tn * 2 + tm * tn * 2 + 4 * tn * 4 + 2 * tm * tn * 4
    act, last = pl.pallas_call(
        kern,
        grid=(M // tm, nj),
        in_specs=[pl.BlockSpec((tm, D), lambda i, j: (i, 0)),
                  pl.BlockSpec((D, tn), lambda i, j: (0, j)),
                  pl.BlockSpec((D, tn), lambda i, j: (0, j)),
                  pl.BlockSpec((CONV_W, tn), lambda i, j: (0, j)),
                  pl.BlockSpec((1, tn), lambda i, j: (0, j)),
                  prev_spec],
        out_specs=[pl.BlockSpec((tm, tn), lambda i, j: (i, j)), last_spec],
        out_shape=[jax.ShapeDtypeStruct((M, Fd), BF16), last_shape],
        scratch_shapes=[pltpu.VMEM((nj, 8, tn), F32)],
        compiler_params=pltpu.CompilerParams(
            dimension_semantics=("arbitrary", "arbitrary"),
            vmem_limit_bytes=_vmem_limit(blk, 8 * tm * tn * 4)),
        name="ffn_up",
    )(h, wg, wv, conv_w.astype(F32), conv_b.reshape(1, Fd).astype(F32), prev_arg)
    if tps == 0:
        last = last.reshape(n_seq, seq_rows, Fd)[:, seq_rows - 2:]
    else:
        last = last[tps - 1::tps, 6:8]
    return act, last


def _attn_kernel(*refs, tq, tk, nq, sk_valid, window, n_prev, causal, masked, qbase, kbase, slots,
                 outs, scale, has_k2, has_bias, has_sinks, sk):
    q_ref, k_ref, v_ref = refs[0], refs[1], refs[2]
    pos = 3
    q2_ref = k2_ref = bias_ref = sinks_ref = None
    if has_k2:
        q2_ref, k2_ref = refs[pos], refs[pos + 1]
        pos += 2
    if has_bias:
        bias_ref = refs[pos]
        pos += 1
    if has_sinks:
        sinks_ref = refs[pos]
        pos += 1
    o_ref = refs[pos]
    m_ref, l_ref, acc_ref = refs[pos + 1], refs[pos + 2], refs[pos + 3]

    g = pl.program_id(1)
    i = pl.program_id(2)
    q0 = i * tq
    ns = len(slots)
    R = ns * tq
    lane = lax.broadcasted_iota(I32, (tq, LANES), 1)

    qs = []
    for (t, mode) in slots:
        qt = q_ref[:, t * LANES:(t + 1) * LANES]
        if mode == "lo":
            qt = jnp.where(lane < 64, qt, 0.0)
        elif mode == "hi":
            qt = jnp.where(lane >= 64, qt, 0.0)
        if has_k2:
            q2 = jnp.where((lane >> 6) == (g % 2), q2_ref[...], 0.0)
            qt = jnp.concatenate([qt, q2], axis=1)
        qs.append(qt.astype(BF16))
    Q = qs[0] if ns == 1 else jnp.concatenate(qs, axis=0)

    qpos1 = qbase + q0 + lax.broadcasted_iota(I32, (tq, 1), 0)
    qc1 = qpos1 >> CHUNK_SHIFT
    qc = qc1 if ns == 1 else jnp.concatenate([qc1] * ns, axis=0)

    if window is not None:
        lo_row = ((qbase + q0) >> CHUNK_SHIFT) * CHUNK - n_prev * CHUNK - kbase
        start = jnp.clip(lo_row, 0, sk - window)
        nb = window // tk
    elif causal:
        start = 0
        hi_row = (((qbase + q0 + tq - 1) >> CHUNK_SHIFT) + 1) * CHUNK - kbase
        nb = (jnp.minimum(hi_row, sk_valid) + tk - 1) // tk
    else:
        start = 0
        nb = sk // tk

    m_ref[...] = jnp.full((R, 1), M_INIT, F32)
    l_ref[...] = jnp.zeros((R, 1), F32)
    acc_ref[...] = jnp.zeros((R, LANES), F32)

    def body(jb, carry):
        r0 = pl.multiple_of(start + jb * tk, 8)
        kb = k_ref[pl.ds(r0, tk), :]
        if has_k2:
            kb = jnp.concatenate([kb, k2_ref[pl.ds(r0, tk), :]], axis=1)
        s = lax.dot_general(Q, kb.astype(BF16), (((1,), (1,)), ((), ())),
                            preferred_element_type=F32) * scale
        if has_bias:
            s = s + bias_ref[0, 0]
        if masked:
            krow = r0 + lax.broadcasted_iota(I32, (1, tk), 1)
            kc = (kbase + krow) >> CHUNK_SHIFT
            s = jnp.where(kc <= qc, s, NEG_INF)
            if n_prev is not None:
                s = jnp.where(kc >= qc - n_prev, s, NEG_INF)
            if sk_valid < sk:
                s = jnp.where(krow < sk_valid, s, NEG_INF)
        m_old = m_ref[...]
        m_new = jnp.maximum(m_old, jnp.max(s, axis=-1, keepdims=True))
        p = jnp.exp(s - m_new)
        alpha = jnp.exp(m_old - m_new)
        l_ref[...] = alpha * l_ref[...] + jnp.sum(p, axis=-1, keepdims=True)
        acc_ref[...] = alpha * acc_ref[...] + jnp.dot(
            p.astype(BF16), v_ref[pl.ds(r0, tk), :].astype(BF16), preferred_element_type=F32)
        m_ref[...] = m_new
        return carry

    lax.fori_loop(0, nb, body, 0)

    m = m_ref[...]
    l = l_ref[...]
    acc = acc_ref[...]
    if has_sinks:
        sk_rows = [jnp.full((tq, 1), sinks_ref[g * ns + s], F32) for s in range(ns)]
        sink = sk_rows[0] if ns == 1 else jnp.concatenate(sk_rows, axis=0)
        m_t = jnp.maximum(m, sink)
        a = jnp.exp(m - m_t)
        l = l * a + jnp.exp(sink - m_t)
        acc = acc * a
    o = acc / l
    for t, spec in enumerate(outs):
        if spec[0] == "slot":
            ot = o[spec[1] * tq:(spec[1] + 1) * tq]
        else:
            ot = jnp.where(lane < 64, o[spec[1] * tq:(spec[1] + 1) * tq],
                           o[spec[2] * tq:(spec[2] + 1) * tq])
        o_ref[:, t * LANES:(t + 1) * LANES] = ot.astype(o_ref.dtype)


def attention(q, k, v, *, B, Sq, Sk, sk_valid, tq, tk, G, qw, slots, outs, scale, window=None,
              n_prev=None, causal=False, masked=True, qbase=0, kbase=0, q2=None, k2=None, bias=None,
              sinks=None, name="attn"):
    nq = Sq // tq
    ns = len(slots)
    R = ns * tq
    ow = len(outs)
    in_specs = [pl.BlockSpec((tq, qw * LANES), lambda b, g, i: (b * nq + i, g)),
                pl.BlockSpec((Sk, LANES), lambda b, g, i: (b, g)),
                pl.BlockSpec((Sk, LANES), lambda b, g, i: (b, g))]
    args = [q, k, v]
    blk = tq * qw * LANES * 4 + 2 * Sk * LANES * 4 + tq * ow * LANES * 2
    if q2 is not None:
        in_specs += [pl.BlockSpec((tq, LANES), lambda b, g, i: (b * nq + i, g // 2)),
                     pl.BlockSpec((Sk, LANES), lambda b, g, i: (b, 0))]
        args += [q2, k2]
        blk += tq * LANES * 4 + Sk * LANES * 4
    if bias is not None:
        nv = bias.shape[0]
        assert window == tk
        in_specs.append(pl.BlockSpec((1, 1, tq, window),
                                     lambda b, g, i, nv=nv: (jnp.minimum(i, nv - 1), g, 0, 0)))
        args.append(bias)
        blk += tq * window * 4
    if sinks is not None:
        in_specs.append(pl.BlockSpec(memory_space=pltpu.SMEM))
        args.append(sinks.astype(F32))
    kern = functools.partial(
        _attn_kernel, tq=tq, tk=tk, nq=nq, sk_valid=sk_valid, window=window, n_prev=n_prev,
        causal=causal, masked=masked, qbase=qbase, kbase=kbase, slots=tuple(slots), outs=tuple(outs),
        scale=scale, has_k2=q2 is not None, has_bias=bias is not None, has_sinks=sinks is not None,
        sk=Sk)
    return pl.pallas_call(
        kern,
        grid=(B, G, nq),
        in_specs=in_specs,
        out_specs=pl.BlockSpec((tq, ow * LANES), lambda b, g, i: (b * nq + i, g)),
        out_shape=jax.ShapeDtypeStruct((B * Sq, G * ow * LANES), BF16),
        scratch_shapes=[pltpu.VMEM((R, 1), F32), pltpu.VMEM((R, 1), F32), pltpu.VMEM((R, LANES), F32)],
        compiler_params=pltpu.CompilerParams(
            dimension_semantics=("parallel", "parallel", "arbitrary"),
            vmem_limit_bytes=_vmem_limit(blk, 6 * R * tk * 4)),
        name=name,
    )(*args)


def _unit(slots, k_tile, outs, q2=None, bias=None, sinks=()):
    return (tuple(slots), k_tile, tuple(outs), q2, bias, tuple(sinks))


def _attn2_kernel(*refs, tq, nq, units, mode, win, sk, sk_valid, n_prev, masked, qbase, kbase, scale,
                  has_q2, has_bias, has_sinks, sinks_per_group):
    q_ref, k_ref, v_ref = refs[0], refs[1], refs[2]
    pos = 3
    q2_ref = k2_ref = bias_ref = sinks_ref = None
    if has_q2:
        q2_ref, k2_ref = refs[pos], refs[pos + 1]
        pos += 2
    if has_bias:
        bias_ref = refs[pos]
        pos += 1
    if has_sinks:
        sinks_ref = refs[pos]
        pos += 1
    o_ref = refs[pos]

    g = pl.program_id(1)
    i = pl.program_id(2)
    q0 = i * tq
    lane = lax.broadcasted_iota(I32, (tq, LANES), 1)
    half_mask = {"lo": jnp.where(lane < 64, 1.0, 0.0).astype(BF16),
                 "hi": jnp.where(lane >= 64, 1.0, 0.0).astype(BF16)}
    qc1 = (qbase + q0 + lax.broadcasted_iota(I32, (tq, 1), 0)) >> CHUNK_SHIFT

    def compute(r0, L):
        if masked:
            krow = r0 + lax.broadcasted_iota(I32, (1, L), 1)
            kc = (kbase + krow) >> CHUNK_SHIFT
        rows = pl.ds(r0, L) if not isinstance(r0, int) else slice(r0, r0 + L)
        for (slots, k_tile, outs, q2, bias_idx, sink_idx) in units:
            ns = len(slots)
            qs = []
            for (t, mk) in slots:
                qt = q_ref[:, t * LANES:(t + 1) * LANES]
                if mk is not None:
                    qt = qt * half_mask[mk]
                if q2 is not None:
                    q2t = q2_ref[:, q2[0] * LANES:(q2[0] + 1) * LANES] * half_mask[q2[1]]
                    qt = jnp.concatenate([qt, q2t], axis=1)
                qs.append(qt)
            Q = qs[0] if ns == 1 else jnp.concatenate(qs, axis=0)
            kb = k_ref[rows, k_tile * LANES:(k_tile + 1) * LANES]
            if q2 is not None:
                kb = jnp.concatenate([kb, k2_ref[rows, :]], axis=1)
            vb = v_ref[rows, k_tile * LANES:(k_tile + 1) * LANES]
            s = lax.dot_general(Q, kb, (((1,), (1,)), ((), ())), preferred_element_type=F32) * scale
            if bias_idx is not None:
                s = s + bias_ref[0, bias_idx]
            if masked:
                qc = qc1 if ns == 1 else jnp.concatenate([qc1] * ns, axis=0)
                s = jnp.where(kc <= qc, s, NEG_INF)
                if n_prev is not None:
                    s = jnp.where(kc >= qc - n_prev, s, NEG_INF)
                if sk_valid < sk:
                    s = jnp.where(krow < sk_valid, s, NEG_INF)
            m = jnp.max(s, axis=-1, keepdims=True)
            if sink_idx:
                sk_rows = [jnp.full((tq, 1), sinks_ref[g * sinks_per_group + si], F32) for si in sink_idx]
                sink = sk_rows[0] if ns == 1 else jnp.concatenate(sk_rows, axis=0)
                m = jnp.maximum(m, sink)
            p = jnp.exp(s - m)
            l = jnp.sum(p, axis=-1, keepdims=True)
            if sink_idx:
                l = l + jnp.exp(sink - m)
            o = jnp.dot(p.astype(BF16), vb, preferred_element_type=F32) / l
            for spec in outs:
                t = spec[0]
                if spec[1] == "slot":
                    ot = o[spec[2] * tq:(spec[2] + 1) * tq]
                else:
                    ot = jnp.where(lane < 64, o[spec[2] * tq:(spec[2] + 1) * tq],
                                   o[spec[3] * tq:(spec[3] + 1) * tq])
                o_ref[:, t * LANES:(t + 1) * LANES] = ot.astype(o_ref.dtype)

    if mode == "window":
        lo_row = ((qbase + q0) >> CHUNK_SHIFT) * CHUNK - n_prev * CHUNK - kbase
        compute(pl.multiple_of(jnp.clip(lo_row, 0, sk - win), 16), win)
    elif mode == "full":
        compute(0, sk)
    else:
        for c in range(nq):
            pl.when(i == c)(functools.partial(compute, 0, min(sk, (c + 1) * tq)))


def attention2(q, k, v, *, B, Sq, Sk, sk_valid, tq, G, qw, kw, ow, units, scale, mode, win=None,
               n_prev=None, masked=True, qbase=0, kbase=0, q2=None, q2w=0, k2=None, bias=None,
               sinks=None, sinks_per_group=0, name="attn"):
    nq = Sq // tq
    assert mode != "causal" or (qbase == 0 and kbase == 0 and tq % CHUNK == 0)
    in_specs = [pl.BlockSpec((tq, qw * LANES), lambda b, g, i: (b * nq + i, g)),
                pl.BlockSpec((Sk, kw * LANES), lambda b, g, i: (b, g)),
                pl.BlockSpec((Sk, kw * LANES), lambda b, g, i: (b, g))]
    args = [q, k, v]
    blk = tq * qw * LANES * 2 + 2 * Sk * kw * LANES * 2 + tq * ow * LANES * 2
    if q2 is not None:
        in_specs += [pl.BlockSpec((tq, q2w * LANES), lambda b, g, i: (b * nq + i, g)),
                     pl.BlockSpec((Sk, LANES), lambda b, g, i: (b, 0))]
        args += [q2, k2]
        blk += tq * q2w * LANES * 2 + Sk * LANES * 2
    if bias is not None:
        nv, nbh = bias.shape[0], bias.shape[1] // G
        in_specs.append(pl.BlockSpec((1, nbh, tq, win),
                                     lambda b, g, i, nv=nv: (jnp.minimum(i, nv - 1), g, 0, 0)))
        args.append(bias)
        blk += nbh * tq * win * 4
    if sinks is not None:
        in_specs.append(pl.BlockSpec(memory_space=pltpu.SMEM))
        args.append(sinks.astype(F32))
    L = win if mode == "window" else Sk
    temp = sum(len(u[0]) for u in units) * tq * L * 12
    kern = functools.partial(
        _attn2_kernel, tq=tq, nq=nq, units=tuple(units), mode=mode, win=win, sk=Sk, sk_valid=sk_valid,
        n_prev=n_prev, masked=masked, qbase=qbase, kbase=kbase, scale=scale, has_q2=q2 is not None,
        has_bias=bias is not None, has_sinks=sinks is not None, sinks_per_group=sinks_per_group)
    return pl.pallas_call(
        kern,
        grid=(B, G, nq),
        in_specs=in_specs,
        out_specs=pl.BlockSpec((tq, ow * LANES), lambda b, g, i: (b * nq + i, g)),
        out_shape=jax.ShapeDtypeStruct((B * Sq, G * ow * LANES), BF16),
        compiler_params=pltpu.CompilerParams(
            dimension_semantics=("parallel", "parallel", "arbitrary"),
            vmem_limit_bytes=_vmem_limit(blk, temp)),
        name=name,
    )(*args)


def _toeplitz_kernel(e_ref, o_ref, *, tq, win):
    wp = e_ref.shape[-1]
    x = jnp.broadcast_to(e_ref[0, 0, 0:1, :], (tq, wp))
    y = pltpu.roll(x, 0, 1, stride=1, stride_axis=0)
    o_ref[0, 0] = y[:, :win]


def band_bias(table, tq, win, offs):
    H = table.shape[1]
    wp = _round_up(win + tq - 1, LANES)
    m = np.arange(wp)
    m = np.where(m < win, m, m - wp)
    idx = np.stack([np.clip(off - m, -MAX_REL, MAX_REL) + MAX_REL for off in offs])
    e = jnp.transpose(table.astype(F32)[idx], (0, 2, 1))
    e = jnp.broadcast_to(e[:, :, None, :], (len(offs), H, 8, wp))
    return pl.pallas_call(
        functools.partial(_toeplitz_kernel, tq=tq, win=win),
        grid=(len(offs), H),
        in_specs=[pl.BlockSpec((1, 1, 8, wp), lambda v, h: (v, h, 0, 0))],
        out_specs=pl.BlockSpec((1, 1, tq, win), lambda v, h: (v, h, 0, 0)),
        out_shape=jax.ShapeDtypeStruct((len(offs), H, tq, win), F32),
        compiler_params=pltpu.CompilerParams(dimension_semantics=("parallel", "parallel")),
        name="band_bias",
    )(e)


def _dsa_kernel(q_ref, k_ref, v_ref, iq_ref, ik_ref, iw_ref, o_ref, key_ref, mb_ref, m_ref, l_ref,
                acc_ref, *, tq, tk, sk, sk_valid, qbase, n_sel, scale, idx_scale):
    i = pl.program_id(1)
    q0 = i * tq
    qpos = qbase + q0 + lax.broadcasted_iota(I32, (tq, 1), 0)
    qc = qpos >> CHUNK_SHIFT
    hi_row = (((qbase + q0 + tq - 1) >> CHUNK_SHIFT) + 1) * CHUNK
    nb = (jnp.minimum(hi_row, sk_valid) + tk - 1) // tk
    lane = lax.broadcasted_iota(I32, (tq, LANES), 1)
    n_idx_tiles = iq_ref.shape[1] // LANES

    def col_of(jb):
        return jb * tk + lax.broadcasted_iota(I32, (1, tk), 1)

    def mask_invisible(x, col):
        x = jnp.where((col >> CHUNK_SHIFT) <= qc, x, NEG_INF)
        if sk_valid < sk:
            x = jnp.where(col < sk_valid, x, NEG_INF)
        return x

    def score_body(jb, carry):
        r0 = pl.multiple_of(jb * tk, 8)
        ikb = ik_ref[pl.ds(r0, tk), :].astype(BF16)
        acc = jnp.zeros((tq, tk), F32)
        for p in range(n_idx_tiles):
            t = iq_ref[:, p * LANES:(p + 1) * LANES]
            for hf in range(2):
                qm = jnp.where((lane < 64) if hf == 0 else (lane >= 64), t, 0.0).astype(BF16)
                lg = lax.dot_general(qm, ikb, (((1,), (1,)), ((), ())),
                                     preferred_element_type=F32) * idx_scale
                h = 2 * p + hf
                acc = acc + iw_ref[:, h:h + 1] * jnp.maximum(lg, 0.0)
        sc = mask_invisible(acc, col_of(jb))
        bits = lax.bitcast_convert_type(sc, I32)
        key_ref[jb] = bits ^ ((bits >> 31) & 0x7FFFFFFF)
        return carry

    lax.fori_loop(0, nb, score_body, 0)

    def count(pred):
        def cb(jb, c):
            return c + jnp.sum(jnp.where(pred(key_ref[jb], jb), 1.0, 0.0), axis=1, keepdims=True)
        return lax.fori_loop(0, nb, cb, jnp.zeros((tq, 1), F32))

    def count_eq_before(jtrial):
        def cb(jb, c):
            hit = jnp.where(key_ref[jb] == thr, jnp.where(col_of(jb) < jtrial, 1.0, 0.0), 0.0)
            return c + jnp.sum(hit, axis=1, keepdims=True)
        return lax.fori_loop(0, nb, cb, jnp.zeros((tq, 1), F32))

    def bit_body(it, cand):
        trial = cand | (jnp.int32(1) << (31 - it))
        trial_s = trial ^ INT_MIN
        cnt = count(lambda kk, jb: kk >= trial_s)
        return jnp.where(cnt >= n_sel, trial, cand)

    cand = lax.fori_loop(0, 32, bit_body, jnp.zeros((tq, 1), I32))
    thr = cand ^ INT_MIN

    cnt_gt = count(lambda kk, jb: kk > thr)
    n_eq = count(lambda kk, jb: kk == thr)
    need = n_sel - cnt_gt
    neg_bits = int(np.float32(NEG_INF).view(np.int32))
    neg_key = neg_bits ^ ((neg_bits >> 31) & 0x7FFFFFFF)
    tie = jnp.where(thr > neg_key, jnp.where(n_eq > need, 1.0, 0.0), 0.0)
    any_tie = jnp.max(tie) > 0.0

    def tie_cols():
        def jbit(it, cj):
            trial = cj | (jnp.int32(1) << (30 - it))
            c = count_eq_before(trial)
            return jnp.where(c < need, trial, cj)
        return lax.fori_loop(0, 31, jbit, jnp.zeros((tq, 1), I32))

    jmax = lax.cond(any_tie, tie_cols, lambda: jnp.full((tq, 1), BIG_COL, I32))

    def mask_body(jb, carry):
        kk = key_ref[jb]
        col = col_of(jb)
        sel = jnp.where(kk > thr, 0.0, jnp.where(kk == thr, jnp.where(col <= jmax, 0.0, NEG_INF), NEG_INF))
        mb_ref[jb] = mask_invisible(sel, col)
        return carry

    lax.fori_loop(0, nb, mask_body, 0)

    n_kv = k_ref.shape[1] // LANES
    grp = (q_ref.shape[1] // LANES) // n_kv
    R = grp * tq
    for kvh in range(n_kv):
        Q = jnp.concatenate(
            [q_ref[:, (kvh * grp + gi) * LANES:(kvh * grp + gi + 1) * LANES] for gi in range(grp)],
            axis=0).astype(BF16)
        m_ref[...] = jnp.full((R, 1), M_INIT, F32)
        l_ref[...] = jnp.zeros((R, 1), F32)
        acc_ref[...] = jnp.zeros((R, LANES), F32)

        def body(jb, carry, kvh=kvh, Q=Q):
            r0 = pl.multiple_of(jb * tk, 8)
            kb = k_ref[pl.ds(r0, tk), kvh * LANES:(kvh + 1) * LANES].astype(BF16)
            vb = v_ref[pl.ds(r0, tk), kvh * LANES:(kvh + 1) * LANES].astype(BF16)
            s = lax.dot_general(Q, kb, (((1,), (1,)), ((), ())), preferred_element_type=F32) * scale
            mb = mb_ref[jb]
            s = jnp.where(jnp.concatenate([mb] * grp, axis=0) < 0.0, NEG_INF, s)
            m_old = m_ref[...]
            m_new = jnp.maximum(m_old, jnp.max(s, axis=-1, keepdims=True))
            p = jnp.exp(s - m_new)
            alpha = jnp.exp(m_old - m_new)
            l_ref[...] = alpha * l_ref[...] + jnp.sum(p, axis=-1, keepdims=True)
            acc_ref[...] = alpha * acc_ref[...] + jnp.dot(p.astype(BF16), vb,
                                                          preferred_element_type=F32)
            m_ref[...] = m_new
            return carry

        lax.fori_loop(0, nb, body, 0)
        o = acc_ref[...] / l_ref[...]
        for gi in range(grp):
            hh = kvh * grp + gi
            o_ref[:, hh * LANES:(hh + 1) * LANES] = o[gi * tq:(gi + 1) * tq].astype(o_ref.dtype)


def dsa_attention(q, k, v, iq, ik2, iw, *, B, Sq, Sk, sk_valid, tq, tk, qbase, n_sel):
    nq = Sq // tq
    nkb = Sk // tk
    Hq = q.shape[1]
    Hk = k.shape[1]
    grp = Hq // Hk
    R = grp * tq
    kern = functools.partial(_dsa_kernel, tq=tq, tk=tk, sk=Sk, sk_valid=sk_valid, qbase=qbase, n_sel=n_sel,
                             scale=DSA_HEAD_DIM ** -0.5, idx_scale=IDX_DIM ** -0.5)
    blk = tq * (Hq + iq.shape[1] + LANES) * 4 + Sk * (2 * Hk + LANES) * 4 + tq * Hq * 2
    return pl.pallas_call(
        kern,
        grid=(B, nq),
        in_specs=[pl.BlockSpec((tq, Hq), lambda b, i: (b * nq + i, 0)),
                  pl.BlockSpec((Sk, Hk), lambda b, i: (b, 0)),
                  pl.BlockSpec((Sk, Hk), lambda b, i: (b, 0)),
                  pl.BlockSpec((tq, iq.shape[1]), lambda b, i: (b * nq + i, 0)),
                  pl.BlockSpec((Sk, LANES), lambda b, i: (b, 0)),
                  pl.BlockSpec((tq, LANES), lambda b, i: (b * nq + i, 0))],
        out_specs=pl.BlockSpec((tq, Hq), lambda b, i: (b * nq + i, 0)),
        out_shape=jax.ShapeDtypeStruct((B * Sq, Hq), BF16),
        scratch_shapes=[pltpu.VMEM((nkb, tq, tk), I32), pltpu.VMEM((nkb, tq, tk), F32),
                        pltpu.VMEM((R, 1), F32), pltpu.VMEM((R, 1), F32), pltpu.VMEM((R, LANES), F32)],
        compiler_params=pltpu.CompilerParams(
            dimension_semantics=("parallel", "arbitrary"),
            vmem_limit_bytes=_vmem_limit(blk, 2 * tq * Sk * 4 + 6 * R * tk * 4)),
        name="dsa_attention",
    )(q, k, v, iq, ik2, iw)


def _dsa2_kernel(q_ref, k_ref, v_ref, iq_ref, ik_ref, iw_ref, o_ref, key_ref, mb_ref, m_ref, l_ref,
                 acc_ref, alpha_ref, s_ref, p_ref, *, tq, tk, sk, sk_valid, qbase, n_sel, scale):
    i = pl.program_id(1)
    q0 = i * tq
    qc = (qbase + q0 + lax.broadcasted_iota(I32, (tq, 1), 0)) >> CHUNK_SHIFT
    hi_row = (((qbase + q0 + tq - 1) >> CHUNK_SHIFT) + 1) * CHUNK
    nb = (jnp.minimum(hi_row, sk_valid) + tk - 1) // tk
    lane = lax.broadcasted_iota(I32, (tq, LANES), 1)
    half_mask = (jnp.where(lane < 64, 1.0, 0.0).astype(BF16), jnp.where(lane >= 64, 1.0, 0.0).astype(BF16))
    n_idx_tiles = iq_ref.shape[1] // LANES
    n_lane_tiles = tk // LANES

    def col_of(jb):
        return jb * tk + lax.broadcasted_iota(I32, (1, tk), 1)

    def mask_invisible(x, col):
        x = jnp.where((col >> CHUNK_SHIFT) <= qc, x, NEG_INF)
        if sk_valid < sk:
            x = jnp.where(col < sk_valid, x, NEG_INF)
        return x

    def lane_fold(x):
        acc = x[:, 0:LANES]
        for t in range(1, n_lane_tiles):
            acc = acc + x[:, t * LANES:(t + 1) * LANES]
        return acc

    def score_body(jb, carry):
        r0 = pl.multiple_of(jb * tk, 16)
        ikb = ik_ref[pl.ds(r0, tk), :]
        acc = jnp.zeros((tq, tk), F32)
        for p in range(n_idx_tiles):
            t = iq_ref[:, p * LANES:(p + 1) * LANES]
            for hf in range(2):
                lg = lax.dot_general(t * half_mask[hf], ikb, (((1,), (1,)), ((), ())),
                                     preferred_element_type=F32)
                h = 2 * p + hf
                acc = acc + iw_ref[:, h:h + 1] * jnp.maximum(lg, 0.0)
        sc = mask_invisible(acc, col_of(jb))
        bits = lax.bitcast_convert_type(sc, I32)
        key_ref[jb] = bits ^ ((bits >> 31) & 0x7FFFFFFF)
        return carry

    lax.fori_loop(0, nb, score_body, 0)

    def count(hit):
        def cb(jb, c):
            return c + lane_fold(hit(key_ref[jb], jb))
        part = lax.fori_loop(0, nb, cb, jnp.zeros((tq, LANES), F32))
        return jnp.sum(part, axis=1, keepdims=True)

    def bit_body(it, cand):
        trial = cand | (jnp.int32(1) << (31 - it))
        trial_s = trial ^ INT_MIN
        cnt = count(lambda kk, jb: jnp.where(kk >= trial_s, 1.0, 0.0))
        return jnp.where(cnt >= n_sel, trial, cand)

    cand = lax.fori_loop(0, 32, bit_body, jnp.zeros((tq, 1), I32))
    thr = cand ^ INT_MIN

    cnt_gt = count(lambda kk, jb: jnp.where(kk > thr, 1.0, 0.0))
    n_eq = count(lambda kk, jb: jnp.where(kk == thr, 1.0, 0.0))
    need = n_sel - cnt_gt
    neg_bits = int(np.float32(NEG_INF).view(np.int32))
    neg_key = neg_bits ^ ((neg_bits >> 31) & 0x7FFFFFFF)
    tie = jnp.where(thr > neg_key, jnp.where(n_eq > need, 1.0, 0.0), 0.0)
    any_tie = jnp.max(tie) > 0.0

    def tie_cols():
        def jbit(it, cj):
            trial = cj | (jnp.int32(1) << (30 - it))
            c = count(lambda kk, jb: jnp.where(kk == thr, jnp.where(col_of(jb) < trial, 1.0, 0.0), 0.0))
            return jnp.where(c < need, trial, cj)
        return lax.fori_loop(0, 31, jbit, jnp.zeros((tq, 1), I32))

    jmax = lax.cond(any_tie, tie_cols, lambda: jnp.full((tq, 1), BIG_COL, I32))

    def mask_body(jb, carry):
        kk = key_ref[jb]
        col = col_of(jb)
        sel = jnp.where(kk > thr, 0.0, jnp.where(kk == thr, jnp.where(col <= jmax, 0.0, NEG_INF), NEG_INF))
        mb_ref[jb] = mask_invisible(sel, col)
        return carry

    lax.fori_loop(0, nb, mask_body, 0)

    n_kv = k_ref.shape[1] // LANES
    grp = (q_ref.shape[1] // LANES) // n_kv
    R = grp * tq
    m_ref[...] = jnp.full(m_ref.shape, M_INIT, F32)
    l_ref[...] = jnp.zeros(l_ref.shape, F32)
    acc_ref[...] = jnp.zeros(acc_ref.shape, F32)

    ch = min(SOFTMAX_ROWS, tq)
    n_ch = R // ch

    def body(jb, carry):
        r0 = pl.multiple_of(jb * tk, 16)
        for kvh in range(n_kv):
            Q = jnp.concatenate(
                [q_ref[:, (kvh * grp + gi) * LANES:(kvh * grp + gi + 1) * LANES] for gi in range(grp)], axis=0)
            kb = k_ref[pl.ds(r0, tk), kvh * LANES:(kvh + 1) * LANES]
            vb = v_ref[pl.ds(r0, tk), kvh * LANES:(kvh + 1) * LANES]
            s_ref[kvh] = lax.dot_general(Q, kb, (((1,), (1,)), ((), ())), preferred_element_type=F32)

            def rows_body(c, carry2, kvh=kvh):
                rr = pl.multiple_of(c * ch, ch)
                qr = pl.multiple_of((c * ch) % tq, ch)
                s = s_ref[kvh, pl.ds(rr, ch), :] * scale + mb_ref[jb, pl.ds(qr, ch), :]
                m_old = m_ref[kvh, pl.ds(rr, ch), :]
                m_new = jnp.maximum(m_old, jnp.max(s, axis=-1, keepdims=True))
                p = jnp.exp(s - m_new)
                alpha = jnp.exp(m_old - m_new)
                l_ref[kvh, pl.ds(rr, ch), :] = alpha * l_ref[kvh, pl.ds(rr, ch), :] + jnp.sum(
                    p, axis=-1, keepdims=True)
                m_ref[kvh, pl.ds(rr, ch), :] = m_new
                alpha_ref[kvh, pl.ds(rr, ch), :] = alpha
                p_ref[kvh, pl.ds(rr, ch), :] = p.astype(BF16)
                return carry2

            lax.fori_loop(0, n_ch, rows_body, 0, unroll=min(4, n_ch))
            acc_ref[kvh] = alpha_ref[kvh] * acc_ref[kvh] + jnp.dot(p_ref[kvh], vb,
                                                                   preferred_element_type=F32)
        return carry

    lax.fori_loop(0, nb, body, 0)
    for kvh in range(n_kv):
        o = acc_ref[kvh] / l_ref[kvh]
        for gi in range(grp):
            hh = kvh * grp + gi
            o_ref[:, hh * LANES:(hh + 1) * LANES] = o[gi * tq:(gi + 1) * tq].astype(o_ref.dtype)


def dsa_attention2(q, k, v, iq, ik2, iw, *, B, Sq, Sk, sk_valid, tq, tk, qbase, n_sel):
    nq = Sq // tq
    nkb = Sk // tk
    Hq = q.shape[1]
    Hk = k.shape[1]
    n_kv = Hk // LANES
    R = (Hq // Hk) * tq
    kern = functools.partial(_dsa2_kernel, tq=tq, tk=tk, sk=Sk, sk_valid=sk_valid, qbase=qbase,
                             n_sel=n_sel, scale=DSA_HEAD_DIM ** -0.5)
    blk = tq * (Hq + iq.shape[1]) * 2 + tq * LANES * 4 + Sk * (2 * Hk + LANES) * 2 + tq * Hq * 2
    scratch = 2 * nkb * tq * tk * 4 + 3 * n_kv * R * LANES * 4
    return pl.pallas_call(
        kern,
        grid=(B, nq),
        in_specs=[pl.BlockSpec((tq, Hq), lambda b, i: (b * nq + i, 0)),
                  pl.BlockSpec((Sk, Hk), lambda b, i: (b, 0)),
                  pl.BlockSpec((Sk, Hk), lambda b, i: (b, 0)),
                  pl.BlockSpec((tq, iq.shape[1]), lambda b, i: (b * nq + i, 0)),
                  pl.BlockSpec((Sk, LANES), lambda b, i: (b, 0)),
                  pl.BlockSpec((tq, LANES), lambda b, i: (b * nq + i, 0))],
        out_specs=pl.BlockSpec((tq, Hq), lambda b, i: (b * nq + i, 0)),
        out_shape=jax.ShapeDtypeStruct((B * Sq, Hq), BF16),
        scratch_shapes=[pltpu.VMEM((nkb, tq, tk), I32), pltpu.VMEM((nkb, tq, tk), F32),
                        pltpu.VMEM((n_kv, R, 1), F32), pltpu.VMEM((n_kv, R, 1), F32),
                        pltpu.VMEM((n_kv, R, LANES), F32), pltpu.VMEM((n_kv, R, 1), F32),
                        pltpu.VMEM((n_kv, R, tk), F32), pltpu.VMEM((n_kv, R, tk), BF16)],
        compiler_params=pltpu.CompilerParams(
            dimension_semantics=("parallel", "arbitrary"),
            vmem_limit_bytes=_vmem_limit(blk, scratch + n_kv * R * (LANES * 4 + tk * 6) + (8 << 20))),
        name="dsa_attention",
    )(q, k, v, iq, ik2, iw)


def _rope_tables(pos, rot, group):
    width = 2 * LANES
    half = rot // 2
    inv = ROPE_THETA ** (-jnp.arange(half, dtype=F32) / half)
    ang = pos.astype(F32)[:, None] * inv[None, :]
    c, s = jnp.cos(ang), jnp.sin(ang)
    lane = np.arange(width) % group
    is_lo = lane < half
    is_hi = (lane >= half) & (lane < rot)
    idx = np.where(is_lo, lane, np.where(is_hi, lane - half, 0))
    cg, sg = c[:, idx], s[:, idx]
    C = jnp.where(is_lo | is_hi, cg, 1.0)
    S = jnp.where(is_lo, -sg, jnp.where(is_hi, sg, 0.0))
    perm = np.zeros((width, width), np.float32)
    dst = np.arange(width)
    src = np.where(is_lo, dst + half, dst - half)
    perm[src[is_lo | is_hi], dst[is_lo | is_hi]] = 1.0
    return half, C, S, jnp.asarray(perm, BF16)


def _pad_cols(w, n):
    return jnp.pad(w, ((0, 0), (0, n - w.shape[1])))


def _tile_gain(g, n):
    return jnp.tile(g.astype(F32), n // g.shape[0])


def _band_bias(table, tq, window, offs):
    qi = np.arange(tq)[:, None]
    kj = np.arange(window)[None, :]
    idx = np.stack([np.clip(qi - kj + off, -MAX_REL, MAX_REL) + MAX_REL for off in offs])
    return jnp.transpose(table.astype(F32)[idx], (0, 3, 1, 2))


def _pad_rows(x, B, rows, rows_pad):
    if rows == rows_pad:
        return x
    x = x.reshape(B, rows, -1)
    x = jnp.pad(x, ((0, 0), (0, rows_pad - rows), (0, 0)))
    return x.reshape(B * rows_pad, -1)


def _with_cache(cache, new, B, rows_pad, dtype=F32):
    n = new.shape[0] // B
    P = cache.shape[1]
    parts = [cache.reshape(B, P, -1).astype(dtype), new.reshape(B, n, -1).astype(dtype)]
    if rows_pad > P + n:
        parts.append(jnp.zeros((B, rows_pad - P - n, new.shape[1]), dtype))
    return jnp.concatenate(parts, axis=1).reshape(B * rows_pad, -1)


def _round_up(x, m):
    return (x + m - 1) // m * m


_SLOT1 = ((0, None),)
_OUT1 = (("slot", 0),)


def _mla_weights(w_in, q_lora_norm, kv_lora_norm, w_uq, w_ukv, q_norm, k_norm):
    H = MLA_HEADS
    wq3 = w_uq.reshape(MLA_Q_LORA, H, MLA_NOPE + MLA_ROPE)
    wkv3 = w_ukv.reshape(MLA_KV_LORA, H, MLA_NOPE + MLA_V)
    return dict(
        w_cq=w_in[:, :MLA_Q_LORA].astype(BF16),
        w_ckv=w_in[:, MLA_Q_LORA:MLA_Q_LORA + MLA_KV_LORA].astype(BF16),
        w_kr=_pad_cols(w_in[:, MLA_Q_LORA + MLA_KV_LORA:], LANES).astype(BF16),
        g_cq=q_lora_norm, g_ckv=kv_lora_norm,
        g_kr=_tile_gain(k_norm[MLA_NOPE:], LANES),
        w_qn=wq3[:, :, :MLA_NOPE].reshape(MLA_Q_LORA, H * MLA_NOPE).astype(BF16),
        w_qr=wq3[:, :, MLA_NOPE:].reshape(MLA_Q_LORA, H * MLA_ROPE).astype(BF16),
        g_qn=_tile_gain(q_norm[:MLA_NOPE], H * MLA_NOPE),
        g_qr=_tile_gain(q_norm[MLA_NOPE:], H * MLA_ROPE),
        w_kn=wkv3[:, :, :MLA_NOPE].reshape(MLA_KV_LORA, H * MLA_NOPE).astype(BF16),
        w_v=wkv3[:, :, MLA_NOPE:].reshape(MLA_KV_LORA, H * MLA_V).astype(BF16),
        g_kn=_tile_gain(k_norm[:MLA_NOPE], H * MLA_NOPE),
    )


def _mla_project(h, W, rope64):
    cq = mm(h, W["w_cq"], gain=W["g_cq"], gs=MLA_Q_LORA, norm=True, out_dtype=BF16, name="mla_cq")
    ckv, ckv_bf = mm(h, W["w_ckv"], gain=W["g_ckv"], gs=MLA_KV_LORA, norm=True, out_dtype=(F32, BF16),
                     name="mla_ckv")
    kr = mm(h, W["w_kr"], gain=W["g_kr"], gs=64, norm=True, rope=rope64, name="mla_kr")
    qn = mm(cq, W["w_qn"], gain=W["g_qn"], gs=LANES, norm=True, out_dtype=BF16, name="mla_qn")
    qr = mm(cq, W["w_qr"], gain=W["g_qr"], gs=64, norm=True, rope=rope64, out_dtype=BF16, name="mla_qr")
    return qn, qr, ckv, ckv_bf, kr


def _mla_expand(ckv_bf, W):
    kn = mm(ckv_bf, W["w_kn"], gain=W["g_kn"], gs=LANES, norm=True, out_dtype=BF16, name="mla_kn")
    v = mm(ckv_bf, W["w_v"], out_dtype=BF16, name="mla_v")
    return kn, v


def _mla_attend(qn, qr, kn, kr, v, *, B, Sq, Sk, sk_valid, tq, hps, mode, qbase):
    kr2 = jnp.concatenate([kr[:, :MLA_ROPE], kr[:, :MLA_ROPE]], axis=1).astype(BF16)
    units = tuple(_unit(((h, None),), h, ((h, "slot", 0),), q2=(h // 2, "lo" if h % 2 == 0 else "hi"))
                  for h in range(hps))
    return attention2(qn, kn, v, B=B, Sq=Sq, Sk=Sk, sk_valid=sk_valid, tq=tq, G=MLA_HEADS // hps,
                      qw=hps, kw=hps, ow=hps, units=units, scale=(MLA_NOPE + MLA_ROPE) ** -0.5,
                      mode=mode, qbase=qbase, q2=qr, q2w=hps // 2, k2=kr2, name="mla_attn")


def kernel(x_prompt, x_sample, mem_prompt, cache_a_ckv, cache_a_krope, cache_b_k, cache_b_v, cache_c_k, cache_c_v, cache_c_idxk, cache_d_k, cache_d_v, cache_mem_k, cache_mem_v, state_ffn_conv, norm_mix, norm_xattn, norm_mem, norm_ffn, a_w_in, a_q_lora_norm, a_kv_lora_norm, a_w_uq, a_w_ukv, a_q_norm, a_k_norm, a_w_out, b_w_in, b_q_norm, b_k_norm, b_rel_bias, b_w_out, c_w_in, c_q_norm, c_k_norm, c_idx_k_norm, c_w_out, d_w_in, d_q_norm, d_k_norm, d_sinks, d_w_out, xa_w_q, xa_w_kv, xa_q_norm, xa_k_norm, xa_w_o, ffn_w_up, ffn_conv_w, ffn_conv_b, ffn_w_down):
    B, S, D = x_prompt.shape
    Bd, n, _ = x_sample.shape
    P = cache_a_ckv.shape[2]
    depth = norm_mix.shape[0]
    Fd = ffn_conv_b.shape[1]
    Mp, Ms = B * S, Bd * n
    pos_p = jnp.arange(S, dtype=I32)
    pos_s = jnp.tile(P + jnp.arange(n, dtype=I32), Bd)
    names = ('a_ckv_p', 'a_krope_p', 'b_k_p', 'b_v_p', 'c_k_p', 'c_v_p', 'c_idxk_p', 'd_k_p', 'd_v_p',
             'mem_k_p', 'mem_v_p', 'conv_p', 'a_ckv_s', 'a_krope_s', 'b_k_s', 'b_v_s', 'c_k_s', 'c_v_s',
             'c_idxk_s', 'd_k_s', 'd_v_s', 'conv_s')
    st = {name: [] for name in names}
    xp = x_prompt.reshape(Mp, D)
    xs = x_sample.reshape(Ms, D)
    tq_p = 128

    def rope_pair(rot, group):
        return _rope_tables(pos_p, rot, group), _rope_tables(pos_s, rot, group)

    for i in range(depth):
        m, j = i % 4, i // 4
        hp = rmsnorm_cast(xp, norm_mix[i])
        hs = rmsnorm_cast(xs, norm_mix[i])
        if m == 0:
            W = _mla_weights(a_w_in[j], a_q_lora_norm[j], a_kv_lora_norm[j], a_w_uq[j], a_w_ukv[j],
                             a_q_norm[j], a_k_norm[j])
            rp, rs = rope_pair(MLA_ROPE, 64)
            w_out = a_w_out[j].astype(BF16)
            qn, qr, ckv, ckv_bf, kr = _mla_project(hp, W, rp)
            kn, v = _mla_expand(ckv_bf, W)
            op = _mla_attend(qn, qr, kn, kr, v, B=B, Sq=S, Sk=S, sk_valid=S, tq=_pick(S, (256, 128)),
                             hps=2, mode="causal", qbase=0)
            xp = mm(op, w_out, res=xp, name="mix_out")
            st['a_ckv_p'].append(ckv.reshape(B, S, MLA_KV_LORA))
            st['a_krope_p'].append(kr[:, :MLA_ROPE].reshape(B, S, MLA_ROPE))
            qn, qr, ckv, ckv_bf, kr = _mla_project(hs, W, rs)
            sk_valid = P + n
            Skp = _round_up(sk_valid, LANES)
            ckv_all = _with_cache(cache_a_ckv[j], ckv_bf, Bd, Skp, BF16)
            kr_all = _with_cache(cache_a_krope[j], kr[:, :MLA_ROPE], Bd, Skp)
            kn, v = _mla_expand(ckv_all, W)
            os_ = _mla_attend(qn, qr, kn, kr_all, v, B=Bd, Sq=n, Sk=Skp, sk_valid=sk_valid, tq=n,
                              hps=MLA_HEADS, mode="full", qbase=P)
            xs = mm(os_, w_out, res=xs, name="mix_out")
            st['a_ckv_s'].append(ckv.reshape(Bd, n, MLA_KV_LORA))
            st['a_krope_s'].append(kr[:, :MLA_ROPE].reshape(Bd, n, MLA_ROPE))
        elif m == 1:
            H, dh = BAND_HEADS, D // BAND_HEADS
            w_in = b_w_in[j]
            wq, wk, wv = (w_in[:, t * H * dh:(t + 1) * H * dh].astype(BF16) for t in range(3))
            gq, gk = _tile_gain(b_q_norm[j], H * dh), _tile_gain(b_k_norm[j], H * dh)
            w_out = b_w_out[j].astype(BF16)
            npv = BAND_PREV_CHUNKS
            def band_units(nh):
                return tuple(_unit(((h, None),), h, ((h, "slot", 0),), bias=h) for h in range(nh))

            q = mm(hp, wq, gain=gq, gs=dh, norm=True, out_dtype=BF16, name="band_q")
            k, k_bf = mm(hp, wk, gain=gk, gs=dh, norm=True, out_dtype=(F32, BF16), name="band_k")
            v, v_bf = mm(hp, wv, out_dtype=(F32, BF16), name="band_v")
            win = npv * CHUNK + tq_p
            offs = [t * tq_p for t in range(npv * CHUNK // tq_p)] + [npv * CHUNK]
            bias = band_bias(b_rel_bias[j], tq_p, win, offs)
            hps = 4
            op = attention2(q, k_bf, v_bf, B=B, Sq=S, Sk=S, sk_valid=S, tq=tq_p, G=H // hps, qw=hps,
                            kw=hps, ow=hps, units=band_units(hps), scale=dh ** -0.5, mode="window",
                            win=win, n_prev=npv, bias=bias, name="band_attn")
            xp = mm(op, w_out, res=xp, name="mix_out")
            r = min(npv * CHUNK, S)
            st['b_k_p'].append(k.reshape(B, S, H, dh)[:, S - r:])
            st['b_v_p'].append(v.reshape(B, S, H, dh)[:, S - r:])
            q = mm(hs, wq, gain=gq, gs=dh, norm=True, out_dtype=BF16, name="band_q")
            k = mm(hs, wk, gain=gk, gs=dh, norm=True, name="band_k")
            v = mm(hs, wv, name="band_v")
            rows = cache_b_k.shape[2]
            sk_valid = rows + n
            Skp = _round_up(sk_valid, LANES)
            k_all = _with_cache(cache_b_k[j].reshape(Bd, rows, H * dh), k, Bd, Skp, BF16)
            v_all = _with_cache(cache_b_v[j].reshape(Bd, rows, H * dh), v, Bd, Skp, BF16)
            bias = band_bias(b_rel_bias[j], n, Skp, [rows])
            os_ = attention2(q, k_all, v_all, B=Bd, Sq=n, Sk=Skp, sk_valid=sk_valid, tq=n, G=1, qw=H,
                             kw=H, ow=H, units=band_units(H), scale=dh ** -0.5, mode="window", win=Skp,
                             n_prev=npv, qbase=P, kbase=P - rows, bias=bias, name="band_attn")
            xs = mm(os_, w_out, res=xs, name="mix_out")
            st['b_k_s'].append(k.reshape(Bd, n, H, dh))
            st['b_v_s'].append(v.reshape(Bd, n, H, dh))
        elif m == 2:
            Hq, Hk, dh = DSA_HEADS, DSA_KV_HEADS, DSA_HEAD_DIM
            w_in = c_w_in[j]
            oq, ok_, ov = Hq * dh, Hq * dh + Hk * dh, Hq * dh + 2 * Hk * dh
            oiq = ov + IDX_HEADS * IDX_DIM
            oik = oiq + IDX_DIM
            wq = w_in[:, :oq].astype(BF16)
            wk = w_in[:, oq:ok_].astype(BF16)
            wv = w_in[:, ok_:ov].astype(BF16)
            wiq = w_in[:, ov:oiq].astype(BF16)
            wik = _pad_cols(w_in[:, oiq:oik], LANES).astype(BF16)
            wiw = _pad_cols(w_in[:, oik:], LANES).astype(BF16)
            gq, gk = _tile_gain(c_q_norm[j], oq), _tile_gain(c_k_norm[j], Hk * dh)
            gik = _tile_gain(c_idx_k_norm[j], LANES)
            giw = jnp.full((LANES,), IDX_HEADS ** -0.5 * IDX_DIM ** -0.5, F32)
            w_out = c_w_out[j].astype(BF16)
            rp128, rs128 = rope_pair(dh // 4, LANES)
            rp64, rs64 = rope_pair(IDX_DIM // 4, 64)

            def project(h, r128, r64):
                q = mm(h, wq, gain=gq, gs=dh, norm=True, rope=r128, out_dtype=BF16, name="dsa_q")
                k, k_bf = mm(h, wk, gain=gk, gs=dh, norm=True, rope=r128, out_dtype=(F32, BF16), name="dsa_k")
                v, v_bf = mm(h, wv, out_dtype=(F32, BF16), name="dsa_v")
                iq = mm(h, wiq, rope=r64, out_dtype=BF16, name="dsa_iq")
                ik = mm(h, wik, gain=gik, gs=64, norm=True, rope=r64, name="dsa_ik")
                iw = mm(h, wiw, gain=giw, name="dsa_iw")
                return q, k, v, k_bf, v_bf, iq, ik, iw

            q, k, v, k_bf, v_bf, iq, ik, iw = project(hp, rp128, rp64)
            ik2 = jnp.concatenate([ik[:, :IDX_DIM], ik[:, :IDX_DIM]], axis=1).astype(BF16)
            op = dsa_attention2(q, k_bf, v_bf, iq, ik2, iw, B=B, Sq=S, Sk=S, sk_valid=S,
                                tq=_pick(S, (256, 128)), tk=_pick(S, (512, 256, 128)), qbase=0,
                                n_sel=min(DSA_TOPK, S // 4))
            xp = mm(op, w_out, res=xp, name="mix_out")
            st['c_k_p'].append(k.reshape(B, S, Hk, dh))
            st['c_v_p'].append(v.reshape(B, S, Hk, dh))
            st['c_idxk_p'].append(ik[:, :IDX_DIM].reshape(B, S, IDX_DIM))
            q, k, v, _, _, iq, ik, iw = project(hs, rs128, rs64)
            sk_valid = P + n
            Skp = _round_up(sk_valid, LANES)
            k_all = _with_cache(cache_c_k[j].reshape(Bd, P, Hk * dh), k, Bd, Skp, BF16)
            v_all = _with_cache(cache_c_v[j].reshape(Bd, P, Hk * dh), v, Bd, Skp, BF16)
            ik_all = _with_cache(cache_c_idxk[j], ik[:, :IDX_DIM], Bd, Skp, BF16)
            ik2 = jnp.concatenate([ik_all, ik_all], axis=1)
            os_ = dsa_attention2(q, k_all, v_all, iq, ik2, iw, B=Bd, Sq=n, Sk=Skp, sk_valid=sk_valid,
                                 tq=n, tk=_pick(Skp, (512, 384, 256, 128)), qbase=P,
                                 n_sel=min(DSA_TOPK, sk_valid // 4))
            xs = mm(os_, w_out, res=xs, name="mix_out")
            st['c_k_s'].append(k.reshape(Bd, n, Hk, dh))
            st['c_v_s'].append(v.reshape(Bd, n, Hk, dh))
            st['c_idxk_s'].append(ik[:, :IDX_DIM].reshape(Bd, n, IDX_DIM))
        else:
            Hq, Hk, dh = SWA_HEADS, SWA_KV_HEADS, SWA_HEAD_DIM
            grp = Hq // Hk
            npairs = Hk // 2
            perm = np.array([[[2 * p * grp + t, (2 * p + 1) * grp + t] for t in range(grp)]
                             for p in range(npairs)]).reshape(-1)
            w_in = d_w_in[j]
            nqc, nkc = Hq * dh, Hk * dh
            wq = w_in[:, :nqc].reshape(D, Hq, dh)[:, perm].reshape(D, nqc).astype(BF16)
            wk = w_in[:, nqc:nqc + nkc].astype(BF16)
            wv = w_in[:, nqc + nkc:].astype(BF16)
            gq, gk = _tile_gain(d_q_norm[j], nqc), _tile_gain(d_k_norm[j], nkc)
            w_out = d_w_out[j].reshape(Hq, dh, D)[perm].reshape(nqc, D).astype(BF16)
            sinks = d_sinks[j][perm]
            rp, rs = rope_pair(dh // 4, 64)
            npv = SWA_PREV_CHUNKS
            def swa_units(npr):
                us = []
                for p in range(npr):
                    slots = tuple((p * grp + t, hf) for t in range(grp) for hf in ("lo", "hi"))
                    outs = tuple((p * grp + t, "pair", 2 * t, 2 * t + 1) for t in range(grp))
                    us.append(_unit(slots, p, outs, sinks=range(2 * grp * p, 2 * grp * (p + 1))))
                return tuple(us)

            def project(h, r):
                q = mm(h, wq, gain=gq, gs=64, norm=True, rope=r, out_dtype=BF16, name="swa_q")
                k, k_bf = mm(h, wk, gain=gk, gs=64, norm=True, rope=r, out_dtype=(F32, BF16), name="swa_k")
                v, v_bf = mm(h, wv, out_dtype=(F32, BF16), name="swa_v")
                return q, k, v, k_bf, v_bf

            q, k, v, k_bf, v_bf = project(hp, rp)
            win = npv * CHUNK + tq_p
            pps = 2
            op = attention2(q, k_bf, v_bf, B=B, Sq=S, Sk=S, sk_valid=S, tq=tq_p, G=npairs // pps,
                            qw=pps * grp, kw=pps, ow=pps * grp, units=swa_units(pps), scale=dh ** -0.5,
                            mode="window", win=win, n_prev=npv, sinks=sinks,
                            sinks_per_group=2 * grp * pps, name="swa_attn")
            xp = mm(op, w_out, res=xp, name="mix_out")
            r = min(npv * CHUNK, S)
            st['d_k_p'].append(k.reshape(B, S, Hk, dh)[:, S - r:])
            st['d_v_p'].append(v.reshape(B, S, Hk, dh)[:, S - r:])
            q, k, v, _, _ = project(hs, rs)
            rows = cache_d_k.shape[2]
            sk_valid = rows + n
            Skp = _round_up(sk_valid, LANES)
            k_all = _with_cache(cache_d_k[j].reshape(Bd, rows, nkc), k, Bd, Skp, BF16)
            v_all = _with_cache(cache_d_v[j].reshape(Bd, rows, nkc), v, Bd, Skp, BF16)
            os_ = attention2(q, k_all, v_all, B=Bd, Sq=n, Sk=Skp, sk_valid=sk_valid, tq=n, G=1,
                             qw=npairs * grp, kw=npairs, ow=npairs * grp, units=swa_units(npairs),
                             scale=dh ** -0.5, mode="window", win=Skp, n_prev=npv, qbase=P,
                             kbase=P - rows, sinks=sinks, sinks_per_group=2 * grp * npairs,
                             name="swa_attn")
            xs = mm(os_, w_out, res=xs, name="mix_out")
            st['d_k_s'].append(k.reshape(Bd, n, Hk, dh))
            st['d_v_s'].append(v.reshape(Bd, n, Hk, dh))

        Hx, dx = XA_HEADS, XA_HEAD_DIM
        Mm = mem_prompt.shape[1]
        hm = rmsnorm_cast(mem_prompt.reshape(B * Mm, D), norm_mem[i])
        w_kv = xa_w_kv[i]
        mk, mk_bf = mm(hm, w_kv[:, :Hx * dx].astype(BF16), gain=_tile_gain(xa_k_norm[i], Hx * dx), gs=dx,
                       norm=True, out_dtype=(F32, BF16), name="xa_k")
        mv, mv_bf = mm(hm, w_kv[:, Hx * dx:].astype(BF16), out_dtype=(F32, BF16), name="xa_v")
        st['mem_k_p'].append(mk.reshape(B, Mm, Hx, dx))
        st['mem_v_p'].append(mv.reshape(B, Mm, Hx, dx))
        wxq = xa_w_q[i].astype(BF16)
        gxq = _tile_gain(xa_q_norm[i], Hx * dx)
        wxo = xa_w_o[i].astype(BF16)
        xa_units = tuple(_unit(((h, None),), h, ((h, "slot", 0),)) for h in range(Hx))

        def cross(x, mk2, mv2, Bx, Sx, tq):
            hx = rmsnorm_cast(x, norm_xattn[i])
            qx = mm(hx, wxq, gain=gxq, gs=dx, norm=True, out_dtype=BF16, name="xa_q")
            ox = attention2(qx, mk2, mv2, B=Bx, Sq=Sx, Sk=Mm, sk_valid=Mm, tq=tq, G=1, qw=Hx, kw=Hx,
                            ow=Hx, units=xa_units, scale=dx ** -0.5, mode="full", masked=False,
                            name="xattn")
            return mm(ox, wxo, res=x, name="xa_out")

        xp = cross(xp, mk_bf, mv_bf, B, S, _pick(S, (512, 256, 128)))
        xs = cross(xs, cache_mem_k[i].reshape(Bd * Mm, Hx * dx).astype(BF16),
                   cache_mem_v[i].reshape(Bd * Mm, Hx * dx).astype(BF16), Bd, n, n)

        w_up = ffn_w_up[i]
        wg, wv_ = w_up[:, :Fd].astype(BF16), w_up[:, Fd:].astype(BF16)
        w_dn = ffn_w_down[i].astype(BF16)
        hf = rmsnorm_cast(xp, norm_ffn[i])
        act, conv_p = ffn_up(hf, wg, wv_, ffn_conv_w[i], ffn_conv_b[i], jnp.zeros((B, CONV_W - 1, Fd), F32),
                             n_seq=B, seq_rows=S)
        xp = mm(act, w_dn, res=xp, name="ffn_down")
        hf = rmsnorm_cast(xs, norm_ffn[i])
        act, conv_s = ffn_up(hf, wg, wv_, ffn_conv_w[i], ffn_conv_b[i], state_ffn_conv[i],
                             n_seq=Bd, seq_rows=n)
        xs = mm(act, w_dn, res=xs, name="ffn_down")
        st['conv_p'].append(conv_p)
        st['conv_s'].append(conv_s)

    order = ('a_ckv_p', 'a_krope_p', 'b_k_p', 'b_v_p', 'c_k_p', 'c_v_p', 'c_idxk_p', 'd_k_p', 'd_v_p',
             'mem_k_p', 'mem_v_p', 'conv_p', 'a_ckv_s', 'a_krope_s', 'b_k_s', 'b_v_s', 'c_k_s', 'c_v_s',
             'c_idxk_s', 'd_k_s', 'd_v_s', 'conv_s')
    return (xp.reshape(B, S, D), xs.reshape(Bd, n, D)) + tuple(jnp.stack(st[nm]) for nm in order)
```

```python
import functools

import numpy as np
import jax
import jax.numpy as jnp
from jax import lax
from jax.experimental import pallas as pl
from jax.experimental.pallas import tpu as pltpu

F32, BF16, I32 = jnp.float32, jnp.bfloat16, jnp.int32

CHUNK = 64
CHUNK_SHIFT = 6
ROPE_THETA = 500000.0
EPS = 1e-6
NEG_INF = -1e30
MLA_HEADS, MLA_NOPE, MLA_ROPE, MLA_V = 16, 128, 64, 128
MLA_Q_LORA, MLA_KV_LORA = 512, 512
BAND_HEADS, BAND_PREV_CHUNKS, MAX_REL = 16, 8, 128
DSA_HEADS, DSA_KV_HEADS, DSA_HEAD_DIM = 16, 4, 128
IDX_HEADS, IDX_DIM, DSA_TOPK = 16, 64, 256
SWA_HEADS, SWA_KV_HEADS, SWA_HEAD_DIM, SWA_PREV_CHUNKS = 32, 8, 64, 2
XA_HEADS, XA_HEAD_DIM = 4, 128
CONV_W = 3

LANES = 128
VMEM_LIMIT_MAX = 56 * 1024 * 1024
VMEM_LIMIT_MIN = 32 * 1024 * 1024

M_INIT = -3.0e38
INT_MIN = -2 ** 31
BIG_COL = 2 ** 30
LOG2E = 1.4426950408889634
SOFTMAX_ROWS = 64


def _vmem_limit(block_bytes, temp_bytes=0):
    need = 2 * block_bytes + temp_bytes + (4 << 20)
    return int(min(max(need, VMEM_LIMIT_MIN), VMEM_LIMIT_MAX))


def _pick(n, cands):
    for c in cands:
        if n % c == 0:
            return c
    raise ValueError(f"no tile for {n}")


def _rmsnorm_kernel(x_ref, g_ref, o_ref):
    x = x_ref[...]
    ms = jnp.mean(x * x, axis=-1, keepdims=True)
    o_ref[...] = (x * lax.rsqrt(ms + EPS) * g_ref[...]).astype(o_ref.dtype)


def rmsnorm_cast(x, g):
    M, K = x.shape
    tm = _pick(M, (512, 256, 128, 64, 16, 8))
    return pl.pallas_call(
        _rmsnorm_kernel,
        grid=(M // tm,),
        in_specs=[pl.BlockSpec((tm, K), lambda i: (i, 0)), pl.BlockSpec((1, K), lambda i: (0, 0))],
        out_specs=pl.BlockSpec((tm, K), lambda i: (i, 0)),
        out_shape=jax.ShapeDtypeStruct((M, K), BF16),
        compiler_params=pltpu.CompilerParams(
            dimension_semantics=("parallel",), vmem_limit_bytes=_vmem_limit(tm * K * 6, tm * K * 8)),
        name="rmsnorm_cast",
    )(x, g.reshape(1, K).astype(F32))


def _group_norm_rope(y, gain_ref, rope_refs, gs, norm, half):
    tm, tn = y.shape
    if norm and gs == tn:
        ss = jnp.sum(y * y, axis=-1, keepdims=True)
        return y * lax.rsqrt(ss * (1.0 / gs) + EPS) * gain_ref[...]
    outs = []
    for c in range(tn // LANES):
        yc = y[:, c * LANES:(c + 1) * LANES]
        if norm:
            yy = yc * yc
            if gs == LANES:
                ss = jnp.sum(yy, axis=-1, keepdims=True)
            else:
                lo = lax.broadcasted_iota(I32, yc.shape, 1) < gs
                s_lo = jnp.sum(jnp.where(lo, yy, 0.0), axis=-1, keepdims=True)
                s_hi = jnp.sum(jnp.where(lo, 0.0, yy), axis=-1, keepdims=True)
                ss = jnp.where(lo, s_lo, s_hi)
            yc = yc * lax.rsqrt(ss * (1.0 / gs) + EPS) * gain_ref[:, c * LANES:(c + 1) * LANES]
        elif gain_ref is not None:
            yc = yc * gain_ref[:, c * LANES:(c + 1) * LANES]
        outs.append(yc)
    y = outs[0] if len(outs) == 1 else jnp.concatenate(outs, axis=1)
    if not half:
        return y
    c_ref, s_ref, p_ref = rope_refs
    pw = p_ref.shape[0]
    outs = []
    for b in range(tn // pw):
        yb = y[:, b * pw:(b + 1) * pw]
        hi = yb.astype(BF16)
        lo = (yb - hi.astype(F32)).astype(BF16)
        partner = (jnp.dot(hi, p_ref[...], preferred_element_type=F32)
                   + jnp.dot(lo, p_ref[...], preferred_element_type=F32))
        outs.append(yb * c_ref[...] + partner * s_ref[...])
    return outs[0] if len(outs) == 1 else jnp.concatenate(outs, axis=1)


def _normed_rows(x_ref, g_ref, h_ref):
    @pl.when(pl.program_id(1) == 0)
    def _():
        x = x_ref[...]
        ms = jnp.mean(x * x, axis=-1, keepdims=True)
        h_ref[...] = (x * lax.rsqrt(ms + EPS) * g_ref[...]).astype(h_ref.dtype)
    return h_ref[...]


def _mm_kernel(*refs, has_gain, gs, norm, half, has_res, has_in_norm, n_out):
    a_ref, w_ref = refs[0], refs[1]
    pos = 2
    in_gain_ref = gain_ref = rope_refs = res_ref = None
    if has_in_norm:
        in_gain_ref = refs[pos]
        pos += 1
    if has_gain:
        gain_ref = refs[pos]
        pos += 1
    if half:
        rope_refs = refs[pos:pos + 3]
        pos += 3
    if has_res:
        res_ref = refs[pos]
        pos += 1
    a = _normed_rows(a_ref, in_gain_ref, refs[pos + n_out]) if has_in_norm else a_ref[...]
    y = jnp.dot(a, w_ref[...], preferred_element_type=F32)
    if has_gain or half:
        y = _group_norm_rope(y, gain_ref, rope_refs, gs, norm, half)
    if has_res:
        y = res_ref[...] + y
    for o_ref in refs[pos:pos + n_out]:
        o_ref[...] = y.astype(o_ref.dtype)


def mm(a, w, *, gain=None, gs=LANES, norm=False, rope=None, res=None, out_dtype=F32, tn=None,
       in_gain=None, name="mm"):
    M, K = a.shape
    N = w.shape[1]
    assert a.dtype == (BF16 if in_gain is None else F32) and w.dtype == BF16 and N % LANES == 0
    if tn is None:
        tn = gs if (norm and gs > LANES) else _pick(N, (512, 256, 128))
    if K > 2048 and tn > 256 and N % 256 == 0:
        tn = 256
    tm_c = (1024, 512, 256, 128, 64, 16, 8)
    rows_period = M if rope is None else rope[1].shape[0]
    tm = _pick(np.gcd(M, rows_period), tm_c)
    half = 0
    in_specs = [pl.BlockSpec((tm, K), lambda i, j: (i, 0)), pl.BlockSpec((K, tn), lambda i, j: (0, j))]
    args = [a, w]
    blk = tm * K * 2 + K * tn * 2 + tm * tn * 4
    scratch = []
    if in_gain is not None:
        in_specs.append(pl.BlockSpec((1, K), lambda i, j: (0, 0)))
        args.append(in_gain.reshape(1, K).astype(F32))
        scratch.append(pltpu.VMEM((tm, K), BF16))
        blk += tm * K * 4
    if gain is not None:
        in_specs.append(pl.BlockSpec((1, tn), lambda i, j: (0, j)))
        args.append(gain.reshape(1, N).astype(F32))
    if rope is not None:
        half, tab_c, tab_s, perm = rope
        if tn % perm.shape[0] != 0:
            tab_c, tab_s, perm = tab_c[:, :LANES], tab_s[:, :LANES], perm[:LANES, :LANES]
        pw = perm.shape[0]
        assert tn % pw == 0
        nrb = rows_period // tm
        for t in (tab_c, tab_s):
            in_specs.append(pl.BlockSpec((tm, pw), lambda i, j, nrb=nrb: (i % nrb, 0)))
            args.append(t)
        in_specs.append(pl.BlockSpec((pw, pw), lambda i, j: (0, 0)))
        args.append(perm)
        blk += 2 * tm * pw * 4 + pw * pw * 2
    if res is not None:
        in_specs.append(pl.BlockSpec((tm, tn), lambda i, j: (i, j)))
        args.append(res)
        blk += tm * tn * 4
    dtypes = out_dtype if isinstance(out_dtype, tuple) else (out_dtype,)
    kern = functools.partial(_mm_kernel, has_gain=gain is not None, gs=gs, norm=norm, half=half,
                             has_res=res is not None, has_in_norm=in_gain is not None, n_out=len(dtypes))
    outs = pl.pallas_call(
        kern,
        grid=(M // tm, N // tn),
        in_specs=in_specs,
        out_specs=[pl.BlockSpec((tm, tn), lambda i, j: (i, j)) for _ in dtypes],
        out_shape=[jax.ShapeDtypeStruct((M, N), dt) for dt in dtypes],
        scratch_shapes=scratch,
        compiler_params=pltpu.CompilerParams(
            dimension_semantics=("parallel", "arbitrary"),
            vmem_limit_bytes=_vmem_limit(blk + (len(dtypes) - 1) * tm * tn * 4, 3 * tm * tn * 4)),
        name=name,
    )(*args)
    return tuple(outs) if isinstance(out_dtype, tuple) else outs[0]


def _ffn_up_kernel(x_ref, gx_ref, wg_ref, wv_ref, cw_ref, cb_ref, prev_ref, act_ref, last_ref, carry_ref,
                   h_ref, *, tiles_per_seq, seq_rows):
    i = pl.program_id(0)
    h = _normed_rows(x_ref, gx_ref, h_ref)
    g = jnp.dot(h, wg_ref[...], preferred_element_type=F32)
    val = jnp.dot(h, wv_ref[...], preferred_element_type=F32)
    tm, tn = g.shape
    row = lax.broadcasted_iota(I32, (tm, tn), 0)
    g1 = pltpu.roll(g, 1, 0)
    g2 = pltpu.roll(g, 2, 0)
    if tiles_per_seq >= 1:
        first = (i % tiles_per_seq) == 0
        j = pl.program_id(1)
        p_prev = prev_ref[0]
        p_carry = carry_ref[j]
        p2 = jnp.where(first, p_prev[0:1, :], p_carry[6:7, :])
        p1 = jnp.where(first, p_prev[1:2, :], p_carry[7:8, :])
        g1 = jnp.where(row == 0, p1, g1)
        g2 = jnp.where(row == 0, p2, jnp.where(row == 1, p1, g2))
        carry_ref[j] = g[tm - 8:tm, :]
        last_ref[0] = g[tm - 8:tm, :]
    else:
        rmod = row % seq_rows
        g1 = jnp.where(rmod == 0, prev_ref[0], g1)
        g2 = jnp.where(rmod < 2, prev_ref[1], g2)
        last_ref[...] = g
    c = cw_ref[0:1, :] * g2 + cw_ref[1:2, :] * g1 + cw_ref[2:3, :] * g + cb_ref[...]
    act_ref[...] = (c * (1.0 / (1.0 + jnp.exp(-c))) * val).astype(act_ref.dtype)


def ffn_up(x, gx, wg, wv, conv_w, conv_b, prev, *, n_seq, seq_rows):
    M, D = x.shape
    Fd = wg.shape[1]
    tn = _pick(Fd, (512, 256, 128))
    nj = Fd // tn
    if seq_rows >= 256:
        tm = _pick(seq_rows, (1024, 512, 256))
        tps = seq_rows // tm
        prev_arg = prev
        prev_spec = pl.BlockSpec((1, 2, tn), lambda i, j: (i // tps, 0, j))
        last_shape = jax.ShapeDtypeStruct((M // tm, 8, Fd), F32)
        last_spec = pl.BlockSpec((1, 8, tn), lambda i, j: (i, 0, j))
    else:
        tm = M
        tps = 0
        z = jnp.zeros((n_seq, seq_rows - 1, Fd), F32)
        inj1 = jnp.concatenate([prev[:, 1:2], z], axis=1).reshape(M, Fd)
        inj2 = jnp.concatenate([prev[:, 0:2], z[:, 1:]], axis=1).reshape(M, Fd)
        prev_arg = jnp.stack([inj1, inj2])
        prev_spec = pl.BlockSpec((2, tm, tn), lambda i, j: (0, i, j))
        last_shape = jax.ShapeDtypeStruct((M, Fd), F32)
        last_spec = pl.BlockSpec((tm, tn), lambda i, j: (i, j))
    kern = functools.partial(_ffn_up_kernel, tiles_per_seq=tps, seq_rows=seq_rows)
    blk = tm * D * 5 + 2 * D * tn * 2 + tm * tn * 2 + 4 * tn * 4 + 2 * tm * tn * 4
    act, last = pl.pallas_call(
        kern,
        grid=(M // tm, nj),
        in_specs=[pl.BlockSpec((tm, D), lambda i, j: (i, 0)),
                  pl.BlockSpec((1, D), lambda i, j: (0, 0)),
                  pl.BlockSpec((D, tn), lambda i, j: (0, j)),
                  pl.BlockSpec((D, tn), lambda i, j: (0, j)),
                  pl.BlockSpec((CONV_W, tn), lambda i, j: (0, j)),
                  pl.BlockSpec((1, tn), lambda i, j: (0, j)),
                  prev_spec],
        out_specs=[pl.BlockSpec((tm, tn), lambda i, j: (i, j)), last_spec],
        out_shape=[jax.ShapeDtypeStruct((M, Fd), BF16), last_shape],
        scratch_shapes=[pltpu.VMEM((nj, 8, tn), F32), pltpu.VMEM((tm, D), BF16)],
        compiler_params=pltpu.CompilerParams(
            dimension_semantics=("arbitrary", "arbitrary"),
            vmem_limit_bytes=_vmem_limit(blk, 6 * tm * tn * 4)),
        name="ffn_up",
    )(x, gx.reshape(1, D).astype(F32), wg, wv, conv_w.astype(F32), conv_b.reshape(1, Fd).astype(F32), prev_arg)
    if tps == 0:
        last = last.reshape(n_seq, seq_rows, Fd)[:, seq_rows - 2:]
    else:
        last = last[tps - 1::tps, 6:8]
    return act, last


def _attn_kernel(*refs, tq, tk, nq, sk_valid, window, n_prev, causal, masked, qbase, kbase, slots,
                 outs, scale, has_k2, has_bias, has_sinks, sk):
    q_ref, k_ref, v_ref = refs[0], refs[1], refs[2]
    pos = 3
    q2_ref = k2_ref = bias_ref = sinks_ref = None
    if has_k2:
        q2_ref, k2_ref = refs[pos], refs[pos + 1]
        pos += 2
    if has_bias:
        bias_ref = refs[pos]
        pos += 1
    if has_sinks:
        sinks_ref = refs[pos]
        pos += 1
    o_ref = refs[pos]
    m_ref, l_ref, acc_ref = refs[pos + 1], refs[pos + 2], refs[pos + 3]

    g = pl.program_id(1)
    i = pl.program_id(2)
    q0 = i * tq
    ns = len(slots)
    R = ns * tq
    lane = lax.broadcasted_iota(I32, (tq, LANES), 1)

    qs = []
    for (t, mode) in slots:
        qt = q_ref[:, t * LANES:(t + 1) * LANES]
        if mode == "lo":
            qt = jnp.where(lane < 64, qt, 0.0)
        elif mode == "hi":
            qt = jnp.where(lane >= 64, qt, 0.0)
        if has_k2:
            q2 = jnp.where((lane >> 6) == (g % 2), q2_ref[...], 0.0)
            qt = jnp.concatenate([qt, q2], axis=1)
        qs.append(qt.astype(BF16))
    Q = qs[0] if ns == 1 else jnp.concatenate(qs, axis=0)

    qpos1 = qbase + q0 + lax.broadcasted_iota(I32, (tq, 1), 0)
    qc1 = qpos1 >> CHUNK_SHIFT
    qc = qc1 if ns == 1 else jnp.concatenate([qc1] * ns, axis=0)

    if window is not None:
        lo_row = ((qbase + q0) >> CHUNK_SHIFT) * CHUNK - n_prev * CHUNK - kbase
        start = jnp.clip(lo_row, 0, sk - window)
        nb = window // tk
    elif causal:
        start = 0
        hi_row = (((qbase + q0 + tq - 1) >> CHUNK_SHIFT) + 1) * CHUNK - kbase
        nb = (jnp.minimum(hi_row, sk_valid) + tk - 1) // tk
    else:
        start = 0
        nb = sk // tk

    m_ref[...] = jnp.full((R, 1), M_INIT, F32)
    l_ref[...] = jnp.zeros((R, 1), F32)
    acc_ref[...] = jnp.zeros((R, LANES), F32)

    def body(jb, carry):
        r0 = pl.multiple_of(start + jb * tk, 8)
        kb = k_ref[pl.ds(r0, tk), :]
        if has_k2:
            kb = jnp.concatenate([kb, k2_ref[pl.ds(r0, tk), :]], axis=1)
        s = lax.dot_general(Q, kb.astype(BF16), (((1,), (1,)), ((), ())),
                            preferred_element_type=F32) * scale
        if has_bias:
            s = s + bias_ref[0, 0]
        if masked:
            krow = r0 + lax.broadcasted_iota(I32, (1, tk), 1)
            kc = (kbase + krow) >> CHUNK_SHIFT
            s = jnp.where(kc <= qc, s, NEG_INF)
            if n_prev is not None:
                s = jnp.where(kc >= qc - n_prev, s, NEG_INF)
            if sk_valid < sk:
                s = jnp.where(krow < sk_valid, s, NEG_INF)
        m_old = m_ref[...]
        m_new = jnp.maximum(m_old, jnp.max(s, axis=-1, keepdims=True))
        p = jnp.exp(s - m_new)
        alpha = jnp.exp(m_old - m_new)
        l_ref[...] = alpha * l_ref[...] + jnp.sum(p, axis=-1, keepdims=True)
        acc_ref[...] = alpha * acc_ref[...] + jnp.dot(
            p.astype(BF16), v_ref[pl.ds(r0, tk), :].astype(BF16), preferred_element_type=F32)
        m_ref[...] = m_new
        return carry

    lax.fori_loop(0, nb, body, 0)

    m = m_ref[...]
    l = l_ref[...]
    acc = acc_ref[...]
    if has_sinks:
        sk_rows = [jnp.full((tq, 1), sinks_ref[g * ns + s], F32) for s in range(ns)]
        sink = sk_rows[0] if ns == 1 else jnp.concatenate(sk_rows, axis=0)
        m_t = jnp.maximum(m, sink)
        a = jnp.exp(m - m_t)
        l = l * a + jnp.exp(sink - m_t)
        acc = acc * a
    o = acc / l
    for t, spec in enumerate(outs):
        if spec[0] == "slot":
            ot = o[spec[1] * tq:(spec[1] + 1) * tq]
        else:
            ot = jnp.where(lane < 64, o[spec[1] * tq:(spec[1] + 1) * tq],
                           o[spec[2] * tq:(spec[2] + 1) * tq])
        o_ref[:, t * LANES:(t + 1) * LANES] = ot.astype(o_ref.dtype)


def attention(q, k, v, *, B, Sq, Sk, sk_valid, tq, tk, G, qw, slots, outs, scale, window=None,
              n_prev=None, causal=False, masked=True, qbase=0, kbase=0, q2=None, k2=None, bias=None,
              sinks=None, name="attn"):
    nq = Sq // tq
    ns = len(slots)
    R = ns * tq
    ow = len(outs)
    in_specs = [pl.BlockSpec((tq, qw * LANES), lambda b, g, i: (b * nq + i, g)),
                pl.BlockSpec((Sk, LANES), lambda b, g, i: (b, g)),
                pl.BlockSpec((Sk, LANES), lambda b, g, i: (b, g))]
    args = [q, k, v]
    blk = tq * qw * LANES * 4 + 2 * Sk * LANES * 4 + tq * ow * LANES * 2
    if q2 is not None:
        in_specs += [pl.BlockSpec((tq, LANES), lambda b, g, i: (b * nq + i, g // 2)),
                     pl.BlockSpec((Sk, LANES), lambda b, g, i: (b, 0))]
        args += [q2, k2]
        blk += tq * LANES * 4 + Sk * LANES * 4
    if bias is not None:
        nv = bias.shape[0]
        assert window == tk
        in_specs.append(pl.BlockSpec((1, 1, tq, window),
                                     lambda b, g, i, nv=nv: (jnp.minimum(i, nv - 1), g, 0, 0)))
        args.append(bias)
        blk += tq * window * 4
    if sinks is not None:
        in_specs.append(pl.BlockSpec(memory_space=pltpu.SMEM))
        args.append(sinks.astype(F32))
    kern = functools.partial(
        _attn_kernel, tq=tq, tk=tk, nq=nq, sk_valid=sk_valid, window=window, n_prev=n_prev,
        causal=causal, masked=masked, qbase=qbase, kbase=kbase, slots=tuple(slots), outs=tuple(outs),
        scale=scale, has_k2=q2 is not None, has_bias=bias is not None, has_sinks=sinks is not None,
        sk=Sk)
    return pl.pallas_call(
        kern,
        grid=(B, G, nq),
        in_specs=in_specs,
        out_specs=pl.BlockSpec((tq, ow * LANES), lambda b, g, i: (b * nq + i, g)),
        out_shape=jax.ShapeDtypeStruct((B * Sq, G * ow * LANES), BF16),
        scratch_shapes=[pltpu.VMEM((R, 1), F32), pltpu.VMEM((R, 1), F32), pltpu.VMEM((R, LANES), F32)],
        compiler_params=pltpu.CompilerParams(
            dimension_semantics=("parallel", "parallel", "arbitrary"),
            vmem_limit_bytes=_vmem_limit(blk, 6 * R * tk * 4)),
        name=name,
    )(*args)


def _unit(slots, k_tile, outs, q2=None, bias=None, sinks=()):
    return (tuple(slots), k_tile, tuple(outs), q2, bias, tuple(sinks))


def _attn2_kernel(*refs, tq, nq, units, mode, win, sk, sk_valid, n_prev, masked, qbase, kbase, scale,
                  has_q2, has_bias, has_sinks, sinks_per_group):
    q_ref, k_ref, v_ref = refs[0], refs[1], refs[2]
    pos = 3
    q2_ref = k2_ref = bias_ref = sinks_ref = None
    if has_q2:
        q2_ref, k2_ref = refs[pos], refs[pos + 1]
        pos += 2
    if has_bias:
        bias_ref = refs[pos]
        pos += 1
    if has_sinks:
        sinks_ref = refs[pos]
        pos += 1
    o_ref = refs[pos]

    g = pl.program_id(1)
    i = pl.program_id(2)
    q0 = i * tq
    lane = lax.broadcasted_iota(I32, (tq, LANES), 1)
    half_mask = {"lo": jnp.where(lane < 64, 1.0, 0.0).astype(BF16),
                 "hi": jnp.where(lane >= 64, 1.0, 0.0).astype(BF16)}
    qc1 = (qbase + q0 + lax.broadcasted_iota(I32, (tq, 1), 0)) >> CHUNK_SHIFT

    def compute(r0, L):
        if masked:
            krow = r0 + lax.broadcasted_iota(I32, (1, L), 1)
            kc = (kbase + krow) >> CHUNK_SHIFT
        rows = pl.ds(r0, L) if not isinstance(r0, int) else slice(r0, r0 + L)
        for u, (slots, k_tile, outs, q2, bias_idx, sink_idx) in enumerate(units):
            ns = len(slots)
            qs = []
            for (t, mk) in slots:
                qt = q_ref[:, t * LANES:(t + 1) * LANES]
                if mk is not None:
                    qt = qt * half_mask[mk]
                if q2 is not None:
                    q2t = q2_ref[:, q2[0] * LANES:(q2[0] + 1) * LANES] * half_mask[q2[1]]
                    qt = jnp.concatenate([qt, q2t], axis=1)
                qs.append(qt)
            Q = qs[0] if ns == 1 else jnp.concatenate(qs, axis=0)
            kb = k_ref[rows, k_tile * LANES:(k_tile + 1) * LANES]
            if q2 is not None:
                kb = jnp.concatenate([kb, k2_ref[rows, :]], axis=1)
            vb = v_ref[rows, k_tile * LANES:(k_tile + 1) * LANES]
            log2 = bias_idx is None
            expf = jnp.exp2 if log2 else jnp.exp
            s = lax.dot_general(Q, kb, (((1,), (1,)), ((), ())), preferred_element_type=F32) * (
                scale * LOG2E if log2 else scale)
            if bias_idx is not None:
                s = s + bias_ref[0, bias_idx]
            if masked:
                qc = qc1 if ns == 1 else jnp.concatenate([qc1] * ns, axis=0)
                s = jnp.where(kc <= qc, s, NEG_INF)
                if n_prev is not None:
                    s = jnp.where(kc >= qc - n_prev, s, NEG_INF)
                if sk_valid < sk:
                    s = jnp.where(krow < sk_valid, s, NEG_INF)
            m = jnp.max(s, axis=-1, keepdims=True)
            if sink_idx:
                sk_rows = [jnp.full((tq, 1), sinks_ref[g * sinks_per_group + si], F32) for si in sink_idx]
                sink = sk_rows[0] if ns == 1 else jnp.concatenate(sk_rows, axis=0)
                sink = sink * LOG2E if log2 else sink
                m = jnp.maximum(m, sink)
            p = expf(s - m)
            l = jnp.sum(p, axis=-1, keepdims=True)
            if sink_idx:
                l = l + expf(sink - m)
            o = jnp.dot(p.astype(BF16), vb, preferred_element_type=F32) / l
            for spec in outs:
                t = spec[0]
                if spec[1] == "slot":
                    ot = o[spec[2] * tq:(spec[2] + 1) * tq]
                else:
                    ot = jnp.where(lane < 64, o[spec[2] * tq:(spec[2] + 1) * tq],
                                   o[spec[3] * tq:(spec[3] + 1) * tq])
                o_ref[:, t * LANES:(t + 1) * LANES] = ot.astype(o_ref.dtype)

    if mode == "window":
        lo_row = ((qbase + q0) >> CHUNK_SHIFT) * CHUNK - n_prev * CHUNK - kbase
        compute(pl.multiple_of(jnp.clip(lo_row, 0, sk - win), 16), win)
    elif mode == "full":
        compute(0, sk)
    else:
        for c in range(nq):
            pl.when(i == c)(functools.partial(compute, 0, min(sk, (c + 1) * tq)))


def attention2(q, k, v, *, B, Sq, Sk, sk_valid, tq, G, qw, kw, ow, units, scale, mode, win=None,
               n_prev=None, masked=True, qbase=0, kbase=0, q2=None, q2w=0, k2=None, bias=None,
               sinks=None, sinks_per_group=0, name="attn"):
    nq = Sq // tq
    assert mode != "causal" or (qbase == 0 and kbase == 0 and tq % CHUNK == 0)
    in_specs = [pl.BlockSpec((tq, qw * LANES), lambda b, g, i: (b * nq + i, g)),
                pl.BlockSpec((Sk, kw * LANES), lambda b, g, i: (b, g)),
                pl.BlockSpec((Sk, kw * LANES), lambda b, g, i: (b, g))]
    args = [q, k, v]
    blk = tq * qw * LANES * 2 + 2 * Sk * kw * LANES * 2 + tq * ow * LANES * 2
    if q2 is not None:
        in_specs += [pl.BlockSpec((tq, q2w * LANES), lambda b, g, i: (b * nq + i, g)),
                     pl.BlockSpec((Sk, LANES), lambda b, g, i: (b, 0))]
        args += [q2, k2]
        blk += tq * q2w * LANES * 2 + Sk * LANES * 2
    if bias is not None:
        nv, nbh = bias.shape[0], bias.shape[1] // G
        in_specs.append(pl.BlockSpec((1, nbh, tq, win),
                                     lambda b, g, i, nv=nv: (jnp.minimum(i, nv - 1), g, 0, 0)))
        args.append(bias)
        blk += nbh * tq * win * 4
    if sinks is not None:
        in_specs.append(pl.BlockSpec(memory_space=pltpu.SMEM))
        args.append(sinks.astype(F32))
    L = win if mode == "window" else Sk
    temp = sum(len(u[0]) for u in units) * tq * L * 12
    kern = functools.partial(
        _attn2_kernel, tq=tq, nq=nq, units=tuple(units), mode=mode, win=win, sk=Sk, sk_valid=sk_valid,
        n_prev=n_prev, masked=masked, qbase=qbase, kbase=kbase, scale=scale, has_q2=q2 is not None,
        has_bias=bias is not None, has_sinks=sinks is not None, sinks_per_group=sinks_per_group)
    return pl.pallas_call(
        kern,
        grid=(B, G, nq),
        in_specs=in_specs,
        out_specs=pl.BlockSpec((tq, ow * LANES), lambda b, g, i: (b * nq + i, g)),
        out_shape=jax.ShapeDtypeStruct((B * Sq, G * ow * LANES), BF16),
        compiler_params=pltpu.CompilerParams(
            dimension_semantics=("parallel", "parallel", "arbitrary"),
            vmem_limit_bytes=_vmem_limit(blk, temp)),
        name=name,
    )(*args)


def _toeplitz_kernel(e_ref, o_ref, *, tq, win):
    wp = e_ref.shape[-1]
    x = jnp.broadcast_to(e_ref[0, 0, 0:1, :], (tq, wp))
    y = pltpu.roll(x, 0, 1, stride=1, stride_axis=0)
    o_ref[0, 0] = y[:, :win]


def band_bias(table, tq, win, offs):
    H = table.shape[1]
    wp = _round_up(win + tq - 1, LANES)
    m = np.arange(wp)
    m = np.where(m < win, m, m - wp)
    idx = np.stack([np.clip(off - m, -MAX_REL, MAX_REL) + MAX_REL for off in offs])
    e = jnp.transpose(table.astype(F32)[idx], (0, 2, 1))
    e = jnp.broadcast_to(e[:, :, None, :], (len(offs), H, 8, wp))
    return pl.pallas_call(
        functools.partial(_toeplitz_kernel, tq=tq, win=win),
        grid=(len(offs), H),
        in_specs=[pl.BlockSpec((1, 1, 8, wp), lambda v, h: (v, h, 0, 0))],
        out_specs=pl.BlockSpec((1, 1, tq, win), lambda v, h: (v, h, 0, 0)),
        out_shape=jax.ShapeDtypeStruct((len(offs), H, tq, win), F32),
        compiler_params=pltpu.CompilerParams(dimension_semantics=("parallel", "parallel")),
        name="band_bias",
    )(e)


def _dsa_kernel(q_ref, k_ref, v_ref, iq_ref, ik_ref, iw_ref, o_ref, key_ref, mb_ref, m_ref, l_ref,
                acc_ref, *, tq, tk, sk, sk_valid, qbase, n_sel, scale, idx_scale):
    i = pl.program_id(1)
    q0 = i * tq
    qpos = qbase + q0 + lax.broadcasted_iota(I32, (tq, 1), 0)
    qc = qpos >> CHUNK_SHIFT
    hi_row = (((qbase + q0 + tq - 1) >> CHUNK_SHIFT) + 1) * CHUNK
    nb = (jnp.minimum(hi_row, sk_valid) + tk - 1) // tk
    lane = lax.broadcasted_iota(I32, (tq, LANES), 1)
    n_idx_tiles = iq_ref.shape[1] // LANES

    def col_of(jb):
        return jb * tk + lax.broadcasted_iota(I32, (1, tk), 1)

    def mask_invisible(x, col):
        x = jnp.where((col >> CHUNK_SHIFT) <= qc, x, NEG_INF)
        if sk_valid < sk:
            x = jnp.where(col < sk_valid, x, NEG_INF)
        return x

    def score_body(jb, carry):
        r0 = pl.multiple_of(jb * tk, 8)
        ikb = ik_ref[pl.ds(r0, tk), :].astype(BF16)
        acc = jnp.zeros((tq, tk), F32)
        for p in range(n_idx_tiles):
            t = iq_ref[:, p * LANES:(p + 1) * LANES]
            for hf in range(2):
                qm = jnp.where((lane < 64) if hf == 0 else (lane >= 64), t, 0.0).astype(BF16)
                lg = lax.dot_general(qm, ikb, (((1,), (1,)), ((), ())),
                                     preferred_element_type=F32) * idx_scale
                h = 2 * p + hf
                acc = acc + iw_ref[:, h:h + 1] * jnp.maximum(lg, 0.0)
        sc = mask_invisible(acc, col_of(jb))
        bits = lax.bitcast_convert_type(sc, I32)
        key_ref[jb] = bits ^ ((bits >> 31) & 0x7FFFFFFF)
        return carry

    lax.fori_loop(0, nb, score_body, 0)

    def count(pred):
        def cb(jb, c):
            return c + jnp.sum(jnp.where(pred(key_ref[jb], jb), 1.0, 0.0), axis=1, keepdims=True)
        return lax.fori_loop(0, nb, cb, jnp.zeros((tq, 1), F32))

    def count_eq_before(jtrial):
        def cb(jb, c):
            hit = jnp.where(key_ref[jb] == thr, jnp.where(col_of(jb) < jtrial, 1.0, 0.0), 0.0)
            return c + jnp.sum(hit, axis=1, keepdims=True)
        return lax.fori_loop(0, nb, cb, jnp.zeros((tq, 1), F32))

    def bit_body(it, cand):
        trial = cand | (jnp.int32(1) << (31 - it))
        trial_s = trial ^ INT_MIN
        cnt = count(lambda kk, jb: kk >= trial_s)
        return jnp.where(cnt >= n_sel, trial, cand)

    cand = lax.fori_loop(0, 32, bit_body, jnp.zeros((tq, 1), I32))
    thr = cand ^ INT_MIN

    cnt_gt = count(lambda kk, jb: kk > thr)
    n_eq = count(lambda kk, jb: kk == thr)
    need = n_sel - cnt_gt
    neg_bits = int(np.float32(NEG_INF).view(np.int32))
    neg_key = neg_bits ^ ((neg_bits >> 31) & 0x7FFFFFFF)
    tie = jnp.where(thr > neg_key, jnp.where(n_eq > need, 1.0, 0.0), 0.0)
    any_tie = jnp.max(tie) > 0.0

    def tie_cols():
        def jbit(it, cj):
            trial = cj | (jnp.int32(1) << (30 - it))
            c = count_eq_before(trial)
            return jnp.where(c < need, trial, cj)
        return lax.fori_loop(0, 31, jbit, jnp.zeros((tq, 1), I32))

    jmax = lax.cond(any_tie, tie_cols, lambda: jnp.full((tq, 1), BIG_COL, I32))

    def mask_body(jb, carry):
        kk = key_ref[jb]
        col = col_of(jb)
        sel = jnp.where(kk > thr, 0.0, jnp.where(kk == thr, jnp.where(col <= jmax, 0.0, NEG_INF), NEG_INF))
        mb_ref[jb] = mask_invisible(sel, col)
        return carry

    lax.fori_loop(0, nb, mask_body, 0)

    n_kv = k_ref.shape[1] // LANES
    grp = (q_ref.shape[1] // LANES) // n_kv
    R = grp * tq
    for kvh in range(n_kv):
        Q = jnp.concatenate(
            [q_ref[:, (kvh * grp + gi) * LANES:(kvh * grp + gi + 1) * LANES] for gi in range(grp)],
            axis=0).astype(BF16)
        m_ref[...] = jnp.full((R, 1), M_INIT, F32)
        l_ref[...] = jnp.zeros((R, 1), F32)
        acc_ref[...] = jnp.zeros((R, LANES), F32)

        def body(jb, carry, kvh=kvh, Q=Q):
            r0 = pl.multiple_of(jb * tk, 8)
            kb = k_ref[pl.ds(r0, tk), kvh * LANES:(kvh + 1) * LANES].astype(BF16)
            vb = v_ref[pl.ds(r0, tk), kvh * LANES:(kvh + 1) * LANES].astype(BF16)
            s = lax.dot_general(Q, kb, (((1,), (1,)), ((), ())), preferred_element_type=F32) * scale
            mb = mb_ref[jb]
            s = jnp.where(jnp.concatenate([mb] * grp, axis=0) < 0.0, NEG_INF, s)
            m_old = m_ref[...]
            m_new = jnp.maximum(m_old, jnp.max(s, axis=-1, keepdims=True))
            p = jnp.exp(s - m_new)
            alpha = jnp.exp(m_old - m_new)
            l_ref[...] = alpha * l_ref[...] + jnp.sum(p, axis=-1, keepdims=True)
            acc_ref[...] = alpha * acc_ref[...] + jnp.dot(p.astype(BF16), vb,
                                                          preferred_element_type=F32)
            m_ref[...] = m_new
            return carry

        lax.fori_loop(0, nb, body, 0)
        o = acc_ref[...] / l_ref[...]
        for gi in range(grp):
            hh = kvh * grp + gi
            o_ref[:, hh * LANES:(hh + 1) * LANES] = o[gi * tq:(gi + 1) * tq].astype(o_ref.dtype)


def dsa_attention(q, k, v, iq, ik2, iw, *, B, Sq, Sk, sk_valid, tq, tk, qbase, n_sel):
    nq = Sq // tq
    nkb = Sk // tk
    Hq = q.shape[1]
    Hk = k.shape[1]
    grp = Hq // Hk
    R = grp * tq
    kern = functools.partial(_dsa_kernel, tq=tq, tk=tk, sk=Sk, sk_valid=sk_valid, qbase=qbase, n_sel=n_sel,
                             scale=DSA_HEAD_DIM ** -0.5, idx_scale=IDX_DIM ** -0.5)
    blk = tq * (Hq + iq.shape[1] + LANES) * 4 + Sk * (2 * Hk + LANES) * 4 + tq * Hq * 2
    return pl.pallas_call(
        kern,
        grid=(B, nq),
        in_specs=[pl.BlockSpec((tq, Hq), lambda b, i: (b * nq + i, 0)),
                  pl.BlockSpec((Sk, Hk), lambda b, i: (b, 0)),
                  pl.BlockSpec((Sk, Hk), lambda b, i: (b, 0)),
                  pl.BlockSpec((tq, iq.shape[1]), lambda b, i: (b * nq + i, 0)),
                  pl.BlockSpec((Sk, LANES), lambda b, i: (b, 0)),
                  pl.BlockSpec((tq, LANES), lambda b, i: (b * nq + i, 0))],
        out_specs=pl.BlockSpec((tq, Hq), lambda b, i: (b * nq + i, 0)),
        out_shape=jax.ShapeDtypeStruct((B * Sq, Hq), BF16),
        scratch_shapes=[pltpu.VMEM((nkb, tq, tk), I32), pltpu.VMEM((nkb, tq, tk), F32),
                        pltpu.VMEM((R, 1), F32), pltpu.VMEM((R, 1), F32), pltpu.VMEM((R, LANES), F32)],
        compiler_params=pltpu.CompilerParams(
            dimension_semantics=("parallel", "arbitrary"),
            vmem_limit_bytes=_vmem_limit(blk, 2 * tq * Sk * 4 + 6 * R * tk * 4)),
        name="dsa_attention",
    )(q, k, v, iq, ik2, iw)


def _dsa2_kernel(q_ref, k_ref, v_ref, iq_ref, ik_ref, iw_ref, o_ref, key_ref, mb_ref, m_ref, l_ref,
                 acc_ref, alpha_ref, s_ref, p_ref, *, tq, tk, sk, sk_valid, qbase, n_sel, scale):
    i = pl.program_id(1)
    q0 = i * tq
    qc = (qbase + q0 + lax.broadcasted_iota(I32, (tq, 1), 0)) >> CHUNK_SHIFT
    hi_row = (((qbase + q0 + tq - 1) >> CHUNK_SHIFT) + 1) * CHUNK
    nb = (jnp.minimum(hi_row, sk_valid) + tk - 1) // tk
    lane = lax.broadcasted_iota(I32, (tq, LANES), 1)
    half_mask = (jnp.where(lane < 64, 1.0, 0.0).astype(BF16), jnp.where(lane >= 64, 1.0, 0.0).astype(BF16))
    n_idx_tiles = iq_ref.shape[1] // LANES
    n_lane_tiles = tk // LANES

    def col_of(jb):
        return jb * tk + lax.broadcasted_iota(I32, (1, tk), 1)

    def mask_invisible(x, col):
        x = jnp.where((col >> CHUNK_SHIFT) <= qc, x, NEG_INF)
        if sk_valid < sk:
            x = jnp.where(col < sk_valid, x, NEG_INF)
        return x

    def lane_fold(x):
        acc = x[:, 0:LANES]
        for t in range(1, n_lane_tiles):
            acc = acc + x[:, t * LANES:(t + 1) * LANES]
        return acc

    def score_body(jb, carry):
        r0 = pl.multiple_of(jb * tk, 16)
        ikb = ik_ref[pl.ds(r0, tk), :]
        acc = jnp.zeros((tq, tk), F32)
        for p in range(n_idx_tiles):
            t = iq_ref[:, p * LANES:(p + 1) * LANES]
            for hf in range(2):
                lg = lax.dot_general(t * half_mask[hf], ikb, (((1,), (1,)), ((), ())),
                                     preferred_element_type=F32)
                h = 2 * p + hf
                acc = acc + iw_ref[:, h:h + 1] * jnp.maximum(lg, 0.0)
        sc = mask_invisible(acc, col_of(jb))
        bits = lax.bitcast_convert_type(sc, I32)
        key_ref[jb] = bits ^ ((bits >> 31) & 0x7FFFFFFF)
        return carry

    lax.fori_loop(0, nb, score_body, 0)

    def count(hit):
        def cb(jb, c):
            return c + lane_fold(hit(key_ref[jb], jb))
        part = lax.fori_loop(0, nb, cb, jnp.zeros((tq, LANES), F32))
        return jnp.sum(part, axis=1, keepdims=True)

    def bit_body(it, cand):
        trial = cand | (jnp.int32(1) << (31 - it))
        trial_s = trial ^ INT_MIN
        cnt = count(lambda kk, jb: jnp.where(kk >= trial_s, 1.0, 0.0))
        return jnp.where(cnt >= n_sel, trial, cand)

    cand = lax.fori_loop(0, 32, bit_body, jnp.zeros((tq, 1), I32))
    thr = cand ^ INT_MIN

    cnt_gt = count(lambda kk, jb: jnp.where(kk > thr, 1.0, 0.0))
    n_eq = count(lambda kk, jb: jnp.where(kk == thr, 1.0, 0.0))
    need = n_sel - cnt_gt
    neg_bits = int(np.float32(NEG_INF).view(np.int32))
    neg_key = neg_bits ^ ((neg_bits >> 31) & 0x7FFFFFFF)
    tie = jnp.where(thr > neg_key, jnp.where(n_eq > need, 1.0, 0.0), 0.0)
    any_tie = jnp.max(tie) > 0.0

    def tie_cols():
        def jbit(it, cj):
            trial = cj | (jnp.int32(1) << (30 - it))
            c = count(lambda kk, jb: jnp.where(kk == thr, jnp.where(col_of(jb) < trial, 1.0, 0.0), 0.0))
            return jnp.where(c < need, trial, cj)
        return lax.fori_loop(0, 31, jbit, jnp.zeros((tq, 1), I32))

    jmax = lax.cond(any_tie, tie_cols, lambda: jnp.full((tq, 1), BIG_COL, I32))

    def mask_body(jb, carry):
        kk = key_ref[jb]
        col = col_of(jb)
        sel = jnp.where(kk > thr, 0.0, jnp.where(kk == thr, jnp.where(col <= jmax, 0.0, NEG_INF), NEG_INF))
        mb_ref[jb] = mask_invisible(sel, col)
        return carry

    lax.fori_loop(0, nb, mask_body, 0)

    n_kv = k_ref.shape[1] // LANES
    grp = (q_ref.shape[1] // LANES) // n_kv
    R = grp * tq
    m_ref[...] = jnp.full(m_ref.shape, M_INIT, F32)
    l_ref[...] = jnp.zeros(l_ref.shape, F32)
    acc_ref[...] = jnp.zeros(acc_ref.shape, F32)

    ch = min(SOFTMAX_ROWS, tq)
    n_ch = R // ch

    def body(jb, carry):
        r0 = pl.multiple_of(jb * tk, 16)
        for kvh in range(n_kv):
            Q = jnp.concatenate(
                [q_ref[:, (kvh * grp + gi) * LANES:(kvh * grp + gi + 1) * LANES] for gi in range(grp)], axis=0)
            kb = k_ref[pl.ds(r0, tk), kvh * LANES:(kvh + 1) * LANES]
            vb = v_ref[pl.ds(r0, tk), kvh * LANES:(kvh + 1) * LANES]
            s_ref[kvh] = lax.dot_general(Q, kb, (((1,), (1,)), ((), ())), preferred_element_type=F32)

            def rows_body(c, carry2, kvh=kvh):
                rr = pl.multiple_of(c * ch, ch)
                qr = pl.multiple_of((c * ch) % tq, ch)
                s = s_ref[kvh, pl.ds(rr, ch), :] * (scale * LOG2E) + mb_ref[jb, pl.ds(qr, ch), :]
                m_old = m_ref[kvh, pl.ds(rr, ch), :]
                m_new = jnp.maximum(m_old, jnp.max(s, axis=-1, keepdims=True))
                p = jnp.exp2(s - jnp.concatenate([m_new] * n_lane_tiles, axis=1))
                alpha = jnp.exp2(m_old - m_new)
                l_ref[kvh, pl.ds(rr, ch), :] = alpha * l_ref[kvh, pl.ds(rr, ch), :] + jnp.sum(
                    p, axis=-1, keepdims=True)
                m_ref[kvh, pl.ds(rr, ch), :] = m_new
                alpha_ref[kvh, pl.ds(rr, ch), :] = alpha
                p_ref[kvh, pl.ds(rr, ch), :] = p.astype(BF16)
                return carry2

            lax.fori_loop(0, n_ch, rows_body, 0, unroll=min(4, n_ch))
            acc_ref[kvh] = alpha_ref[kvh] * acc_ref[kvh] + jnp.dot(p_ref[kvh], vb,
                                                                   preferred_element_type=F32)
        return carry

    lax.fori_loop(0, nb, body, 0)
    for kvh in range(n_kv):
        o = acc_ref[kvh] / l_ref[kvh]
        for gi in range(grp):
            hh = kvh * grp + gi
            o_ref[:, hh * LANES:(hh + 1) * LANES] = o[gi * tq:(gi + 1) * tq].astype(o_ref.dtype)


def dsa_attention2(q, k, v, iq, ik2, iw, *, B, Sq, Sk, sk_valid, tq, tk, qbase, n_sel):
    nq = Sq // tq
    nkb = Sk // tk
    Hq = q.shape[1]
    Hk = k.shape[1]
    n_kv = Hk // LANES
    R = (Hq // Hk) * tq
    kern = functools.partial(_dsa2_kernel, tq=tq, tk=tk, sk=Sk, sk_valid=sk_valid, qbase=qbase,
                             n_sel=n_sel, scale=DSA_HEAD_DIM ** -0.5)
    blk = tq * (Hq + iq.shape[1]) * 2 + tq * LANES * 4 + Sk * (2 * Hk + LANES) * 2 + tq * Hq * 2
    scratch = 2 * nkb * tq * tk * 4 + 3 * n_kv * R * LANES * 4
    return pl.pallas_call(
        kern,
        grid=(B, nq),
        in_specs=[pl.BlockSpec((tq, Hq), lambda b, i: (b * nq + i, 0)),
                  pl.BlockSpec((Sk, Hk), lambda b, i: (b, 0)),
                  pl.BlockSpec((Sk, Hk), lambda b, i: (b, 0)),
                  pl.BlockSpec((tq, iq.shape[1]), lambda b, i: (b * nq + i, 0)),
                  pl.BlockSpec((Sk, LANES), lambda b, i: (b, 0)),
                  pl.BlockSpec((tq, LANES), lambda b, i: (b * nq + i, 0))],
        out_specs=pl.BlockSpec((tq, Hq), lambda b, i: (b * nq + i, 0)),
        out_shape=jax.ShapeDtypeStruct((B * Sq, Hq), BF16),
        scratch_shapes=[pltpu.VMEM((nkb, tq, tk), I32), pltpu.VMEM((nkb, tq, tk), F32),
                        pltpu.VMEM((n_kv, R, LANES), F32), pltpu.VMEM((n_kv, R, LANES), F32),
                        pltpu.VMEM((n_kv, R, LANES), F32), pltpu.VMEM((n_kv, R, LANES), F32),
                        pltpu.VMEM((n_kv, R, tk), F32), pltpu.VMEM((n_kv, R, tk), BF16)],
        compiler_params=pltpu.CompilerParams(
            dimension_semantics=("parallel", "arbitrary"),
            vmem_limit_bytes=_vmem_limit(blk, scratch + n_kv * R * (LANES * 4 + tk * 6) + (8 << 20))),
        name="dsa_attention",
    )(q, k, v, iq, ik2, iw)


def _rope_tables(pos, rot, group):
    width = 2 * LANES
    half = rot // 2
    inv = ROPE_THETA ** (-jnp.arange(half, dtype=F32) / half)
    ang = pos.astype(F32)[:, None] * inv[None, :]
    c, s = jnp.cos(ang), jnp.sin(ang)
    lane = np.arange(width) % group
    is_lo = lane < half
    is_hi = (lane >= half) & (lane < rot)
    idx = np.where(is_lo, lane, np.where(is_hi, lane - half, 0))
    cg, sg = c[:, idx], s[:, idx]
    C = jnp.where(is_lo | is_hi, cg, 1.0)
    S = jnp.where(is_lo, -sg, jnp.where(is_hi, sg, 0.0))
    perm = np.zeros((width, width), np.float32)
    dst = np.arange(width)
    src = np.where(is_lo, dst + half, dst - half)
    perm[src[is_lo | is_hi], dst[is_lo | is_hi]] = 1.0
    return half, C, S, jnp.asarray(perm, BF16)


def _pad_cols(w, n):
    return jnp.pad(w, ((0, 0), (0, n - w.shape[1])))


def _tile_gain(g, n):
    return jnp.tile(g.astype(F32), n // g.shape[0])


def _band_bias(table, tq, window, offs):
    qi = np.arange(tq)[:, None]
    kj = np.arange(window)[None, :]
    idx = np.stack([np.clip(qi - kj + off, -MAX_REL, MAX_REL) + MAX_REL for off in offs])
    return jnp.transpose(table.astype(F32)[idx], (0, 3, 1, 2))


def _pad_rows(x, B, rows, rows_pad):
    if rows == rows_pad:
        return x
    x = x.reshape(B, rows, -1)
    x = jnp.pad(x, ((0, 0), (0, rows_pad - rows), (0, 0)))
    return x.reshape(B * rows_pad, -1)


def _with_cache(cache, new, B, rows_pad, dtype=F32):
    n = new.shape[0] // B
    P = cache.shape[1]
    parts = [cache.reshape(B, P, -1).astype(dtype), new.reshape(B, n, -1).astype(dtype)]
    if rows_pad > P + n:
        parts.append(jnp.zeros((B, rows_pad - P - n, new.shape[1]), dtype))
    return jnp.concatenate(parts, axis=1).reshape(B * rows_pad, -1)


def _round_up(x, m):
    return (x + m - 1) // m * m


_SLOT1 = ((0, None),)
_OUT1 = (("slot", 0),)


def _mla_weights(w_in, q_lora_norm, kv_lora_norm, w_uq, w_ukv, q_norm, k_norm):
    H = MLA_HEADS
    wq3 = w_uq.reshape(MLA_Q_LORA, H, MLA_NOPE + MLA_ROPE)
    wkv3 = w_ukv.reshape(MLA_KV_LORA, H, MLA_NOPE + MLA_V)
    return dict(
        w_cq=w_in[:, :MLA_Q_LORA].astype(BF16),
        w_ckv=w_in[:, MLA_Q_LORA:MLA_Q_LORA + MLA_KV_LORA].astype(BF16),
        w_kr=_pad_cols(w_in[:, MLA_Q_LORA + MLA_KV_LORA:], LANES).astype(BF16),
        g_cq=q_lora_norm, g_ckv=kv_lora_norm,
        g_kr=_tile_gain(k_norm[MLA_NOPE:], LANES),
        w_qn=wq3[:, :, :MLA_NOPE].reshape(MLA_Q_LORA, H * MLA_NOPE).astype(BF16),
        w_qr=wq3[:, :, MLA_NOPE:].reshape(MLA_Q_LORA, H * MLA_ROPE).astype(BF16),
        g_qn=_tile_gain(q_norm[:MLA_NOPE], H * MLA_NOPE),
        g_qr=_tile_gain(q_norm[MLA_NOPE:], H * MLA_ROPE),
        w_kn=wkv3[:, :, :MLA_NOPE].reshape(MLA_KV_LORA, H * MLA_NOPE).astype(BF16),
        w_v=wkv3[:, :, MLA_NOPE:].reshape(MLA_KV_LORA, H * MLA_V).astype(BF16),
        g_kn=_tile_gain(k_norm[:MLA_NOPE], H * MLA_NOPE),
    )


def _mla_project(h, W, rope64):
    cq = mm(h, W["w_cq"], gain=W["g_cq"], gs=MLA_Q_LORA, norm=True, out_dtype=BF16, name="mla_cq")
    ckv, ckv_bf = mm(h, W["w_ckv"], gain=W["g_ckv"], gs=MLA_KV_LORA, norm=True, out_dtype=(F32, BF16),
                     name="mla_ckv")
    kr = mm(h, W["w_kr"], gain=W["g_kr"], gs=64, norm=True, rope=rope64, name="mla_kr")
    qn = mm(cq, W["w_qn"], gain=W["g_qn"], gs=LANES, norm=True, out_dtype=BF16, name="mla_qn")
    qr = mm(cq, W["w_qr"], gain=W["g_qr"], gs=64, norm=True, rope=rope64, out_dtype=BF16, name="mla_qr")
    return qn, qr, ckv, ckv_bf, kr


def _mla_expand(ckv_bf, W):
    kn = mm(ckv_bf, W["w_kn"], gain=W["g_kn"], gs=LANES, norm=True, out_dtype=BF16, name="mla_kn")
    v = mm(ckv_bf, W["w_v"], out_dtype=BF16, name="mla_v")
    return kn, v


def _mla_attend(qn, qr, kn, kr, v, *, B, Sq, Sk, sk_valid, tq, hps, mode, qbase):
    kr2 = jnp.concatenate([kr[:, :MLA_ROPE], kr[:, :MLA_ROPE]], axis=1).astype(BF16)
    units = tuple(_unit(((h, None),), h, ((h, "slot", 0),), q2=(h // 2, "lo" if h % 2 == 0 else "hi"))
                  for h in range(hps))
    return attention2(qn, kn, v, B=B, Sq=Sq, Sk=Sk, sk_valid=sk_valid, tq=tq, G=MLA_HEADS // hps,
                      qw=hps, kw=hps, ow=hps, units=units, scale=(MLA_NOPE + MLA_ROPE) ** -0.5,
                      mode=mode, qbase=qbase, q2=qr, q2w=hps // 2, k2=kr2, name="mla_attn")


def kernel(x_prompt, x_sample, mem_prompt, cache_a_ckv, cache_a_krope, cache_b_k, cache_b_v, cache_c_k, cache_c_v, cache_c_idxk, cache_d_k, cache_d_v, cache_mem_k, cache_mem_v, state_ffn_conv, norm_mix, norm_xattn, norm_mem, norm_ffn, a_w_in, a_q_lora_norm, a_kv_lora_norm, a_w_uq, a_w_ukv, a_q_norm, a_k_norm, a_w_out, b_w_in, b_q_norm, b_k_norm, b_rel_bias, b_w_out, c_w_in, c_q_norm, c_k_norm, c_idx_k_norm, c_w_out, d_w_in, d_q_norm, d_k_norm, d_sinks, d_w_out, xa_w_q, xa_w_kv, xa_q_norm, xa_k_norm, xa_w_o, ffn_w_up, ffn_conv_w, ffn_conv_b, ffn_w_down):
    B, S, D = x_prompt.shape
    Bd, n, _ = x_sample.shape
    P = cache_a_ckv.shape[2]
    depth = norm_mix.shape[0]
    Fd = ffn_conv_b.shape[1]
    Mp, Ms = B * S, Bd * n
    pos_p = jnp.arange(S, dtype=I32)
    pos_s = jnp.tile(P + jnp.arange(n, dtype=I32), Bd)
    names = ('a_ckv_p', 'a_krope_p', 'b_k_p', 'b_v_p', 'c_k_p', 'c_v_p', 'c_idxk_p', 'd_k_p', 'd_v_p',
             'mem_k_p', 'mem_v_p', 'conv_p', 'a_ckv_s', 'a_krope_s', 'b_k_s', 'b_v_s', 'c_k_s', 'c_v_s',
             'c_idxk_s', 'd_k_s', 'd_v_s', 'conv_s')
    st = {name: [] for name in names}
    xp = x_prompt.reshape(Mp, D)
    xs = x_sample.reshape(Ms, D)
    tq_p = 128

    def rope_pair(rot, group):
        return _rope_tables(pos_p, rot, group), _rope_tables(pos_s, rot, group)

    for i in range(depth):
        m, j = i % 4, i // 4
        hp = rmsnorm_cast(xp, norm_mix[i])
        hs = rmsnorm_cast(xs, norm_mix[i])
        if m == 0:
            W = _mla_weights(a_w_in[j], a_q_lora_norm[j], a_kv_lora_norm[j], a_w_uq[j], a_w_ukv[j],
                             a_q_norm[j], a_k_norm[j])
            rp, rs = rope_pair(MLA_ROPE, 64)
            w_out = a_w_out[j].astype(BF16)
            qn, qr, ckv, ckv_bf, kr = _mla_project(hp, W, rp)
            kn, v = _mla_expand(ckv_bf, W)
            op = _mla_attend(qn, qr, kn, kr, v, B=B, Sq=S, Sk=S, sk_valid=S, tq=_pick(S, (256, 128)),
                             hps=2, mode="causal", qbase=0)
            xp = mm(op, w_out, res=xp, name="mix_out")
            st['a_ckv_p'].append(ckv.reshape(B, S, MLA_KV_LORA))
            st['a_krope_p'].append(kr[:, :MLA_ROPE].reshape(B, S, MLA_ROPE))
            qn, qr, ckv, ckv_bf, kr = _mla_project(hs, W, rs)
            sk_valid = P + n
            Skp = _round_up(sk_valid, LANES)
            ckv_all = _with_cache(cache_a_ckv[j], ckv_bf, Bd, Skp, BF16)
            kr_all = _with_cache(cache_a_krope[j], kr[:, :MLA_ROPE], Bd, Skp)
            kn, v = _mla_expand(ckv_all, W)
            os_ = _mla_attend(qn, qr, kn, kr_all, v, B=Bd, Sq=n, Sk=Skp, sk_valid=sk_valid, tq=n,
                              hps=MLA_HEADS, mode="full", qbase=P)
            xs = mm(os_, w_out, res=xs, name="mix_out")
            st['a_ckv_s'].append(ckv.reshape(Bd, n, MLA_KV_LORA))
            st['a_krope_s'].append(kr[:, :MLA_ROPE].reshape(Bd, n, MLA_ROPE))
        elif m == 1:
            H, dh = BAND_HEADS, D // BAND_HEADS
            w_in = b_w_in[j]
            wq, wk, wv = (w_in[:, t * H * dh:(t + 1) * H * dh].astype(BF16) for t in range(3))
            gq, gk = _tile_gain(b_q_norm[j], H * dh), _tile_gain(b_k_norm[j], H * dh)
            w_out = b_w_out[j].astype(BF16)
            npv = BAND_PREV_CHUNKS
            def band_units(nh):
                return tuple(_unit(((h, None),), h, ((h, "slot", 0),), bias=h) for h in range(nh))

            q = mm(hp, wq, gain=gq, gs=dh, norm=True, out_dtype=BF16, name="band_q")
            k, k_bf = mm(hp, wk, gain=gk, gs=dh, norm=True, out_dtype=(F32, BF16), name="band_k")
            v, v_bf = mm(hp, wv, out_dtype=(F32, BF16), name="band_v")
            win = npv * CHUNK + tq_p
            offs = [t * tq_p for t in range(npv * CHUNK // tq_p)] + [npv * CHUNK]
            bias = band_bias(b_rel_bias[j], tq_p, win, offs)
            hps = 8
            op = attention2(q, k_bf, v_bf, B=B, Sq=S, Sk=S, sk_valid=S, tq=tq_p, G=H // hps, qw=hps,
                            kw=hps, ow=hps, units=band_units(hps), scale=dh ** -0.5, mode="window",
                            win=win, n_prev=npv, bias=bias, name="band_attn")
            xp = mm(op, w_out, res=xp, name="mix_out")
            r = min(npv * CHUNK, S)
            st['b_k_p'].append(k.reshape(B, S, H, dh)[:, S - r:])
            st['b_v_p'].append(v.reshape(B, S, H, dh)[:, S - r:])
            q = mm(hs, wq, gain=gq, gs=dh, norm=True, out_dtype=BF16, name="band_q")
            k = mm(hs, wk, gain=gk, gs=dh, norm=True, name="band_k")
            v = mm(hs, wv, name="band_v")
            rows = cache_b_k.shape[2]
            sk_valid = rows + n
            Skp = _round_up(sk_valid, LANES)
            k_all = _with_cache(cache_b_k[j].reshape(Bd, rows, H * dh), k, Bd, Skp, BF16)
            v_all = _with_cache(cache_b_v[j].reshape(Bd, rows, H * dh), v, Bd, Skp, BF16)
            bias = band_bias(b_rel_bias[j], n, Skp, [rows])
            os_ = attention2(q, k_all, v_all, B=Bd, Sq=n, Sk=Skp, sk_valid=sk_valid, tq=n, G=1, qw=H,
                             kw=H, ow=H, units=band_units(H), scale=dh ** -0.5, mode="window", win=Skp,
                             n_prev=npv, qbase=P, kbase=P - rows, bias=bias, name="band_attn")
            xs = mm(os_, w_out, res=xs, name="mix_out")
            st['b_k_s'].append(k.reshape(Bd, n, H, dh))
            st['b_v_s'].append(v.reshape(Bd, n, H, dh))
        elif m == 2:
            Hq, Hk, dh = DSA_HEADS, DSA_KV_HEADS, DSA_HEAD_DIM
            w_in = c_w_in[j]
            oq, ok_, ov = Hq * dh, Hq * dh + Hk * dh, Hq * dh + 2 * Hk * dh
            oiq = ov + IDX_HEADS * IDX_DIM
            oik = oiq + IDX_DIM
            wq = w_in[:, :oq].astype(BF16)
            wk = w_in[:, oq:ok_].astype(BF16)
            wv = w_in[:, ok_:ov].astype(BF16)
            wiq = w_in[:, ov:oiq].astype(BF16)
            wik = _pad_cols(w_in[:, oiq:oik], LANES).astype(BF16)
            wiw = _pad_cols(w_in[:, oik:], LANES).astype(BF16)
            gq, gk = _tile_gain(c_q_norm[j], oq), _tile_gain(c_k_norm[j], Hk * dh)
            gik = _tile_gain(c_idx_k_norm[j], LANES)
            giw = jnp.full((LANES,), IDX_HEADS ** -0.5 * IDX_DIM ** -0.5, F32)
            w_out = c_w_out[j].astype(BF16)
            rp128, rs128 = rope_pair(dh // 4, LANES)
            rp64, rs64 = rope_pair(IDX_DIM // 4, 64)

            def project(h, r128, r64):
                q = mm(h, wq, gain=gq, gs=dh, norm=True, rope=r128, out_dtype=BF16, name="dsa_q")
                k, k_bf = mm(h, wk, gain=gk, gs=dh, norm=True, rope=r128, out_dtype=(F32, BF16), name="dsa_k")
                v, v_bf = mm(h, wv, out_dtype=(F32, BF16), name="dsa_v")
                iq = mm(h, wiq, rope=r64, out_dtype=BF16, name="dsa_iq")
                ik = mm(h, wik, gain=gik, gs=64, norm=True, rope=r64, name="dsa_ik")
                iw = mm(h, wiw, gain=giw, name="dsa_iw")
                return q, k, v, k_bf, v_bf, iq, ik, iw

            q, k, v, k_bf, v_bf, iq, ik, iw = project(hp, rp128, rp64)
            ik2 = jnp.concatenate([ik[:, :IDX_DIM], ik[:, :IDX_DIM]], axis=1).astype(BF16)
            op = dsa_attention2(q, k_bf, v_bf, iq, ik2, iw, B=B, Sq=S, Sk=S, sk_valid=S,
                                tq=_pick(S, (256, 128)), tk=_pick(S, (512, 256, 128)), qbase=0,
                                n_sel=min(DSA_TOPK, S // 4))
            xp = mm(op, w_out, res=xp, name="mix_out")
            st['c_k_p'].append(k.reshape(B, S, Hk, dh))
            st['c_v_p'].append(v.reshape(B, S, Hk, dh))
            st['c_idxk_p'].append(ik[:, :IDX_DIM].reshape(B, S, IDX_DIM))
            q, k, v, _, _, iq, ik, iw = project(hs, rs128, rs64)
            sk_valid = P + n
            Skp = _round_up(sk_valid, LANES)
            k_all = _with_cache(cache_c_k[j].reshape(Bd, P, Hk * dh), k, Bd, Skp, BF16)
            v_all = _with_cache(cache_c_v[j].reshape(Bd, P, Hk * dh), v, Bd, Skp, BF16)
            ik_all = _with_cache(cache_c_idxk[j], ik[:, :IDX_DIM], Bd, Skp, BF16)
            ik2 = jnp.concatenate([ik_all, ik_all], axis=1)
            os_ = dsa_attention2(q, k_all, v_all, iq, ik2, iw, B=Bd, Sq=n, Sk=Skp, sk_valid=sk_valid,
                                 tq=n, tk=_pick(Skp, (512, 384, 256, 128)), qbase=P,
                                 n_sel=min(DSA_TOPK, sk_valid // 4))
            xs = mm(os_, w_out, res=xs, name="mix_out")
            st['c_k_s'].append(k.reshape(Bd, n, Hk, dh))
            st['c_v_s'].append(v.reshape(Bd, n, Hk, dh))
            st['c_idxk_s'].append(ik[:, :IDX_DIM].reshape(Bd, n, IDX_DIM))
        else:
            Hq, Hk, dh = SWA_HEADS, SWA_KV_HEADS, SWA_HEAD_DIM
            grp = Hq // Hk
            npairs = Hk // 2
            perm = np.array([[[2 * p * grp + t, (2 * p + 1) * grp + t] for t in range(grp)]
                             for p in range(npairs)]).reshape(-1)
            w_in = d_w_in[j]
            nqc, nkc = Hq * dh, Hk * dh
            wq = w_in[:, :nqc].reshape(D, Hq, dh)[:, perm].reshape(D, nqc).astype(BF16)
            wk = w_in[:, nqc:nqc + nkc].astype(BF16)
            wv = w_in[:, nqc + nkc:].astype(BF16)
            gq, gk = _tile_gain(d_q_norm[j], nqc), _tile_gain(d_k_norm[j], nkc)
            w_out = d_w_out[j].reshape(Hq, dh, D)[perm].reshape(nqc, D).astype(BF16)
            sinks = d_sinks[j][perm]
            rp, rs = rope_pair(dh // 4, 64)
            npv = SWA_PREV_CHUNKS
            def swa_units(npr):
                us = []
                for p in range(npr):
                    slots = tuple((p * grp + t, hf) for t in range(grp) for hf in ("lo", "hi"))
                    outs = tuple((p * grp + t, "pair", 2 * t, 2 * t + 1) for t in range(grp))
                    us.append(_unit(slots, p, outs, sinks=range(2 * grp * p, 2 * grp * (p + 1))))
                return tuple(us)

            def project(h, r):
                q = mm(h, wq, gain=gq, gs=64, norm=True, rope=r, out_dtype=BF16, name="swa_q")
                k, k_bf = mm(h, wk, gain=gk, gs=64, norm=True, rope=r, out_dtype=(F32, BF16), name="swa_k")
                v, v_bf = mm(h, wv, out_dtype=(F32, BF16), name="swa_v")
                return q, k, v, k_bf, v_bf

            q, k, v, k_bf, v_bf = project(hp, rp)
            win = npv * CHUNK + tq_p
            pps = 4
            op = attention2(q, k_bf, v_bf, B=B, Sq=S, Sk=S, sk_valid=S, tq=tq_p, G=npairs // pps,
                            qw=pps * grp, kw=pps, ow=pps * grp, units=swa_units(pps), scale=dh ** -0.5,
                            mode="window", win=win, n_prev=npv, sinks=sinks,
                            sinks_per_group=2 * grp * pps, name="swa_attn")
            xp = mm(op, w_out, res=xp, name="mix_out")
            r = min(npv * CHUNK, S)
            st['d_k_p'].append(k.reshape(B, S, Hk, dh)[:, S - r:])
            st['d_v_p'].append(v.reshape(B, S, Hk, dh)[:, S - r:])
            q, k, v, _, _ = project(hs, rs)
            rows = cache_d_k.shape[2]
            sk_valid = rows + n
            Skp = _round_up(sk_valid, LANES)
            k_all = _with_cache(cache_d_k[j].reshape(Bd, rows, nkc), k, Bd, Skp, BF16)
            v_all = _with_cache(cache_d_v[j].reshape(Bd, rows, nkc), v, Bd, Skp, BF16)
            os_ = attention2(q, k_all, v_all, B=Bd, Sq=n, Sk=Skp, sk_valid=sk_valid, tq=n, G=1,
                             qw=npairs * grp, kw=npairs, ow=npairs * grp, units=swa_units(npairs),
                             scale=dh ** -0.5, mode="window", win=Skp, n_prev=npv, qbase=P,
                             kbase=P - rows, sinks=sinks, sinks_per_group=2 * grp * npairs,
                             name="swa_attn")
            xs = mm(os_, w_out, res=xs, name="mix_out")
            st['d_k_s'].append(k.reshape(Bd, n, Hk, dh))
            st['d_v_s'].append(v.reshape(Bd, n, Hk, dh))

        Hx, dx = XA_HEADS, XA_HEAD_DIM
        Mm = mem_prompt.shape[1]
        hm = rmsnorm_cast(mem_prompt.reshape(B * Mm, D), norm_mem[i])
        w_kv = xa_w_kv[i]
        mk, mk_bf = mm(hm, w_kv[:, :Hx * dx].astype(BF16), gain=_tile_gain(xa_k_norm[i], Hx * dx), gs=dx,
                       norm=True, out_dtype=(F32, BF16), name="xa_k")
        mv, mv_bf = mm(hm, w_kv[:, Hx * dx:].astype(BF16), out_dtype=(F32, BF16), name="xa_v")
        st['mem_k_p'].append(mk.reshape(B, Mm, Hx, dx))
        st['mem_v_p'].append(mv.reshape(B, Mm, Hx, dx))
        wxq = xa_w_q[i].astype(BF16)
        gxq = _tile_gain(xa_q_norm[i], Hx * dx)
        wxo = xa_w_o[i].astype(BF16)
        xa_units = tuple(_unit(((h, None),), h, ((h, "slot", 0),)) for h in range(Hx))

        def cross(x, mk2, mv2, Bx, Sx, tq):
            qx = mm(x, wxq, in_gain=norm_xattn[i], gain=gxq, gs=dx, norm=True, out_dtype=BF16, name="xa_q")
            ox = attention2(qx, mk2, mv2, B=Bx, Sq=Sx, Sk=Mm, sk_valid=Mm, tq=tq, G=1, qw=Hx, kw=Hx,
                            ow=Hx, units=xa_units, scale=dx ** -0.5, mode="full", masked=False,
                            name="xattn")
            return mm(ox, wxo, res=x, name="xa_out")

        xp = cross(xp, mk_bf, mv_bf, B, S, _pick(S, (512, 256, 128)))
        xs = cross(xs, cache_mem_k[i].reshape(Bd * Mm, Hx * dx).astype(BF16),
                   cache_mem_v[i].reshape(Bd * Mm, Hx * dx).astype(BF16), Bd, n, n)

        w_up = ffn_w_up[i]
        wg, wv_ = w_up[:, :Fd].astype(BF16), w_up[:, Fd:].astype(BF16)
        w_dn = ffn_w_down[i].astype(BF16)
        act, conv_p = ffn_up(xp, norm_ffn[i], wg, wv_, ffn_conv_w[i], ffn_conv_b[i],
                             jnp.zeros((B, CONV_W - 1, Fd), F32), n_seq=B, seq_rows=S)
        xp = mm(act, w_dn, res=xp, name="ffn_down")
        act, conv_s = ffn_up(xs, norm_ffn[i], wg, wv_, ffn_conv_w[i], ffn_conv_b[i], state_ffn_conv[i],
                             n_seq=Bd, seq_rows=n)
        xs = mm(act, w_dn, res=xs, name="ffn_down")
        st['conv_p'].append(conv_p)
        st['conv_s'].append(conv_s)

    order = ('a_ckv_p', 'a_krope_p', 'b_k_p', 'b_v_p', 'c_k_p', 'c_v_p', 'c_idxk_p', 'd_k_p', 'd_v_p',
             'mem_k_p', 'mem_v_p', 'conv_p', 'a_ckv_s', 'a_krope_s', 'b_k_s', 'b_v_s', 'c_k_s', 'c_v_s',
             'c_idxk_s', 'd_k_s', 'd_v_s', 'conv_s')
    return (xp.reshape(B, S, D), xs.reshape(Bd, n, D)) + tuple(jnp.stack(st[nm]) for nm in order)
```

```python
import functools

import numpy as np
import jax
import jax.numpy as jnp
from jax import lax
from jax.experimental import pallas as pl
from jax.experimental.pallas import tpu as pltpu

F32, BF16, I32 = jnp.float32, jnp.bfloat16, jnp.int32

CHUNK = 64
CHUNK_SHIFT = 6
ROPE_THETA = 500000.0
EPS = 1e-6
NEG_INF = -1e30
MLA_HEADS, MLA_NOPE, MLA_ROPE, MLA_V = 16, 128, 64, 128
MLA_Q_LORA, MLA_KV_LORA = 512, 512
BAND_HEADS, BAND_PREV_CHUNKS, MAX_REL = 16, 8, 128
DSA_HEADS, DSA_KV_HEADS, DSA_HEAD_DIM = 16, 4, 128
IDX_HEADS, IDX_DIM, DSA_TOPK = 16, 64, 256
SWA_HEADS, SWA_KV_HEADS, SWA_HEAD_DIM, SWA_PREV_CHUNKS = 32, 8, 64, 2
XA_HEADS, XA_HEAD_DIM = 4, 128
CONV_W = 3

LANES = 128
VMEM_LIMIT_MAX = 56 * 1024 * 1024
VMEM_LIMIT_MIN = 32 * 1024 * 1024

M_INIT = -3.0e38
INT_MIN = -2 ** 31
BIG_COL = 2 ** 30
LOG2E = 1.4426950408889634
SOFTMAX_ROWS = 64


def _vmem_limit(block_bytes, temp_bytes=0):
    need = 2 * block_bytes + temp_bytes + (4 << 20)
    return int(min(max(need, VMEM_LIMIT_MIN), VMEM_LIMIT_MAX))


def _pick(n, cands):
    for c in cands:
        if n % c == 0:
            return c
    raise ValueError(f"no tile for {n}")


def _rmsnorm_kernel(x_ref, g_ref, o_ref):
    x = x_ref[...]
    ms = jnp.mean(x * x, axis=-1, keepdims=True)
    o_ref[...] = (x * lax.rsqrt(ms + EPS) * g_ref[...]).astype(o_ref.dtype)


def rmsnorm_cast(x, g):
    M, K = x.shape
    tm = _pick(M, (512, 256, 128, 64, 16, 8))
    return pl.pallas_call(
        _rmsnorm_kernel,
        grid=(M // tm,),
        in_specs=[pl.BlockSpec((tm, K), lambda i: (i, 0)), pl.BlockSpec((1, K), lambda i: (0, 0))],
        out_specs=pl.BlockSpec((tm, K), lambda i: (i, 0)),
        out_shape=jax.ShapeDtypeStruct((M, K), BF16),
        compiler_params=pltpu.CompilerParams(
            dimension_semantics=("parallel",), vmem_limit_bytes=_vmem_limit(tm * K * 6, tm * K * 8)),
        name="rmsnorm_cast",
    )(x, g.reshape(1, K).astype(F32))


def _group_norm_rope(y, gain_ref, rope_refs, gs, norm, half):
    tm, tn = y.shape
    if norm and gs == tn:
        ss = jnp.sum(y * y, axis=-1, keepdims=True)
        return y * lax.rsqrt(ss * (1.0 / gs) + EPS) * gain_ref[...]
    outs = []
    for c in range(tn // LANES):
        yc = y[:, c * LANES:(c + 1) * LANES]
        if norm:
            yy = yc * yc
            if gs == LANES:
                ss = jnp.sum(yy, axis=-1, keepdims=True)
            else:
                lo = lax.broadcasted_iota(I32, yc.shape, 1) < gs
                s_lo = jnp.sum(jnp.where(lo, yy, 0.0), axis=-1, keepdims=True)
                s_hi = jnp.sum(jnp.where(lo, 0.0, yy), axis=-1, keepdims=True)
                ss = jnp.where(lo, s_lo, s_hi)
            yc = yc * lax.rsqrt(ss * (1.0 / gs) + EPS) * gain_ref[:, c * LANES:(c + 1) * LANES]
        elif gain_ref is not None:
            yc = yc * gain_ref[:, c * LANES:(c + 1) * LANES]
        outs.append(yc)
    y = outs[0] if len(outs) == 1 else jnp.concatenate(outs, axis=1)
    if not half:
        return y
    c_ref, s_ref, p_ref = rope_refs
    pw = p_ref.shape[0]
    outs = []
    for b in range(tn // pw):
        yb = y[:, b * pw:(b + 1) * pw]
        hi = yb.astype(BF16)
        lo = (yb - hi.astype(F32)).astype(BF16)
        partner = (jnp.dot(hi, p_ref[...], preferred_element_type=F32)
                   + jnp.dot(lo, p_ref[...], preferred_element_type=F32))
        outs.append(yb * c_ref[...] + partner * s_ref[...])
    return outs[0] if len(outs) == 1 else jnp.concatenate(outs, axis=1)


def _normed_rows(x_ref, g_ref, h_ref):
    @pl.when(pl.program_id(1) == 0)
    def _():
        x = x_ref[...]
        ms = jnp.mean(x * x, axis=-1, keepdims=True)
        h_ref[...] = (x * lax.rsqrt(ms + EPS) * g_ref[...]).astype(h_ref.dtype)
    return h_ref[...]


def _mm_kernel(*refs, has_gain, gs, norm, half, has_res, has_in_norm, n_out):
    a_ref, w_ref = refs[0], refs[1]
    pos = 2
    in_gain_ref = gain_ref = rope_refs = res_ref = None
    if has_in_norm:
        in_gain_ref = refs[pos]
        pos += 1
    if has_gain:
        gain_ref = refs[pos]
        pos += 1
    if half:
        rope_refs = refs[pos:pos + 3]
        pos += 3
    if has_res:
        res_ref = refs[pos]
        pos += 1
    a = _normed_rows(a_ref, in_gain_ref, refs[pos + n_out]) if has_in_norm else a_ref[...]
    y = jnp.dot(a, w_ref[...], preferred_element_type=F32)
    if has_gain or half:
        y = _group_norm_rope(y, gain_ref, rope_refs, gs, norm, half)
    if has_res:
        y = res_ref[...] + y
    for o_ref in refs[pos:pos + n_out]:
        o_ref[...] = y.astype(o_ref.dtype)


def mm(a, w, *, gain=None, gs=LANES, norm=False, rope=None, res=None, out_dtype=F32, tn=None,
       in_gain=None, name="mm"):
    M, K = a.shape
    N = w.shape[1]
    assert a.dtype == (BF16 if in_gain is None else F32) and w.dtype == BF16 and N % LANES == 0
    tm_c = (1024, 512, 256, 128, 64, 16, 8)
    if tn is None:
        if norm and gs > LANES:
            tn = gs
        elif K <= 512:
            tn = _pick(N, (2048, 1024, 512, 256, 128))
            if res is not None:
                tm_c = tm_c[1:]
        else:
            tn = _pick(N, (512, 256, 128))
    rows_period = M if rope is None else rope[1].shape[0]
    tm = _pick(np.gcd(M, rows_period), tm_c)
    half = 0
    in_specs = [pl.BlockSpec((tm, K), lambda i, j: (i, 0)), pl.BlockSpec((K, tn), lambda i, j: (0, j))]
    args = [a, w]
    blk = tm * K * 2 + K * tn * 2 + tm * tn * 4
    scratch = []
    if in_gain is not None:
        in_specs.append(pl.BlockSpec((1, K), lambda i, j: (0, 0)))
        args.append(in_gain.reshape(1, K).astype(F32))
        scratch.append(pltpu.VMEM((tm, K), BF16))
        blk += tm * K * 4
    if gain is not None:
        in_specs.append(pl.BlockSpec((1, tn), lambda i, j: (0, j)))
        args.append(gain.reshape(1, N).astype(F32))
    if rope is not None:
        half, tab_c, tab_s, perm = rope
        if tn % perm.shape[0] != 0:
            tab_c, tab_s, perm = tab_c[:, :LANES], tab_s[:, :LANES], perm[:LANES, :LANES]
        pw = perm.shape[0]
        assert tn % pw == 0
        nrb = rows_period // tm
        for t in (tab_c, tab_s):
            in_specs.append(pl.BlockSpec((tm, pw), lambda i, j, nrb=nrb: (i % nrb, 0)))
            args.append(t)
        in_specs.append(pl.BlockSpec((pw, pw), lambda i, j: (0, 0)))
        args.append(perm)
        blk += 2 * tm * pw * 4 + pw * pw * 2
    if res is not None:
        in_specs.append(pl.BlockSpec((tm, tn), lambda i, j: (i, j)))
        args.append(res)
        blk += tm * tn * 4
    dtypes = out_dtype if isinstance(out_dtype, tuple) else (out_dtype,)
    kern = functools.partial(_mm_kernel, has_gain=gain is not None, gs=gs, norm=norm, half=half,
                             has_res=res is not None, has_in_norm=in_gain is not None, n_out=len(dtypes))
    outs = pl.pallas_call(
        kern,
        grid=(M // tm, N // tn),
        in_specs=in_specs,
        out_specs=[pl.BlockSpec((tm, tn), lambda i, j: (i, j)) for _ in dtypes],
        out_shape=[jax.ShapeDtypeStruct((M, N), dt) for dt in dtypes],
        scratch_shapes=scratch,
        compiler_params=pltpu.CompilerParams(
            dimension_semantics=("parallel", "arbitrary"),
            vmem_limit_bytes=_vmem_limit(blk + (len(dtypes) - 1) * tm * tn * 4, 3 * tm * tn * 4)),
        name=name,
    )(*args)
    return tuple(outs) if isinstance(out_dtype, tuple) else outs[0]


def _ffn_up_kernel(x_ref, gx_ref, wg_ref, wv_ref, cw_ref, cb_ref, prev_ref, act_ref, last_ref, carry_ref,
                   h_ref, *, tiles_per_seq, seq_rows):
    i = pl.program_id(0)
    h = _normed_rows(x_ref, gx_ref, h_ref)
    g = jnp.dot(h, wg_ref[...], preferred_element_type=F32)
    val = jnp.dot(h, wv_ref[...], preferred_element_type=F32)
    tm, tn = g.shape
    row = lax.broadcasted_iota(I32, (tm, tn), 0)
    g1 = pltpu.roll(g, 1, 0)
    g2 = pltpu.roll(g, 2, 0)
    if tiles_per_seq >= 1:
        first = (i % tiles_per_seq) == 0
        j = pl.program_id(1)
        p_prev = prev_ref[0]
        p_carry = carry_ref[j]
        p2 = jnp.where(first, p_prev[0:1, :], p_carry[6:7, :])
        p1 = jnp.where(first, p_prev[1:2, :], p_carry[7:8, :])
        g1 = jnp.where(row == 0, p1, g1)
        g2 = jnp.where(row == 0, p2, jnp.where(row == 1, p1, g2))
        carry_ref[j] = g[tm - 8:tm, :]
        last_ref[0] = g[tm - 8:tm, :]
    else:
        rmod = row % seq_rows
        g1 = jnp.where(rmod == 0, prev_ref[0], g1)
        g2 = jnp.where(rmod < 2, prev_ref[1], g2)
        last_ref[...] = g
    c = cw_ref[0:1, :] * g2 + cw_ref[1:2, :] * g1 + cw_ref[2:3, :] * g + cb_ref[...]
    act_ref[...] = (c * (1.0 / (1.0 + jnp.exp(-c))) * val).astype(act_ref.dtype)


def ffn_up(x, gx, wg, wv, conv_w, conv_b, prev, *, n_seq, seq_rows):
    M, D = x.shape
    Fd = wg.shape[1]
    tn = _pick(Fd, (512, 256, 128))
    nj = Fd // tn
    if seq_rows >= 256:
        tm = _pick(seq_rows, (1024, 512, 256))
        tps = seq_rows // tm
        prev_arg = prev
        prev_spec = pl.BlockSpec((1, 2, tn), lambda i, j: (i // tps, 0, j))
        last_shape = jax.ShapeDtypeStruct((M // tm, 8, Fd), F32)
        last_spec = pl.BlockSpec((1, 8, tn), lambda i, j: (i, 0, j))
    else:
        tm = M
        tps = 0
        z = jnp.zeros((n_seq, seq_rows - 1, Fd), F32)
        inj1 = jnp.concatenate([prev[:, 1:2], z], axis=1).reshape(M, Fd)
        inj2 = jnp.concatenate([prev[:, 0:2], z[:, 1:]], axis=1).reshape(M, Fd)
        prev_arg = jnp.stack([inj1, inj2])
        prev_spec = pl.BlockSpec((2, tm, tn), lambda i, j: (0, i, j))
        last_shape = jax.ShapeDtypeStruct((M, Fd), F32)
        last_spec = pl.BlockSpec((tm, tn), lambda i, j: (i, j))
    kern = functools.partial(_ffn_up_kernel, tiles_per_seq=tps, seq_rows=seq_rows)
    blk = tm * D * 5 + 2 * D * tn * 2 + tm * tn * 2 + 4 * tn * 4 + 2 * tm * tn * 4
    act, last = pl.pallas_call(
        kern,
        grid=(M // tm, nj),
        in_specs=[pl.BlockSpec((tm, D), lambda i, j: (i, 0)),
                  pl.BlockSpec((1, D), lambda i, j: (0, 0)),
                  pl.BlockSpec((D, tn), lambda i, j: (0, j)),
                  pl.BlockSpec((D, tn), lambda i, j: (0, j)),
                  pl.BlockSpec((CONV_W, tn), lambda i, j: (0, j)),
                  pl.BlockSpec((1, tn), lambda i, j: (0, j)),
                  prev_spec],
        out_specs=[pl.BlockSpec((tm, tn), lambda i, j: (i, j)), last_spec],
        out_shape=[jax.ShapeDtypeStruct((M, Fd), BF16), last_shape],
        scratch_shapes=[pltpu.VMEM((nj, 8, tn), F32), pltpu.VMEM((tm, D), BF16)],
        compiler_params=pltpu.CompilerParams(
            dimension_semantics=("arbitrary", "arbitrary"),
            vmem_limit_bytes=_vmem_limit(blk, 6 * tm * tn * 4)),
        name="ffn_up",
    )(x, gx.reshape(1, D).astype(F32), wg, wv, conv_w.astype(F32), conv_b.reshape(1, Fd).astype(F32), prev_arg)
    if tps == 0:
        last = last.reshape(n_seq, seq_rows, Fd)[:, seq_rows - 2:]
    else:
        last = last[tps - 1::tps, 6:8]
    return act, last


def _attn_kernel(*refs, tq, tk, nq, sk_valid, window, n_prev, causal, masked, qbase, kbase, slots,
                 outs, scale, has_k2, has_bias, has_sinks, sk):
    q_ref, k_ref, v_ref = refs[0], refs[1], refs[2]
    pos = 3
    q2_ref = k2_ref = bias_ref = sinks_ref = None
    if has_k2:
        q2_ref, k2_ref = refs[pos], refs[pos + 1]
        pos += 2
    if has_bias:
        bias_ref = refs[pos]
        pos += 1
    if has_sinks:
        sinks_ref = refs[pos]
        pos += 1
    o_ref = refs[pos]
    m_ref, l_ref, acc_ref = refs[pos + 1], refs[pos + 2], refs[pos + 3]

    g = pl.program_id(1)
    i = pl.program_id(2)
    q0 = i * tq
    ns = len(slots)
    R = ns * tq
    lane = lax.broadcasted_iota(I32, (tq, LANES), 1)

    qs = []
    for (t, mode) in slots:
        qt = q_ref[:, t * LANES:(t + 1) * LANES]
        if mode == "lo":
            qt = jnp.where(lane < 64, qt, 0.0)
        elif mode == "hi":
            qt = jnp.where(lane >= 64, qt, 0.0)
        if has_k2:
            q2 = jnp.where((lane >> 6) == (g % 2), q2_ref[...], 0.0)
            qt = jnp.concatenate([qt, q2], axis=1)
        qs.append(qt.astype(BF16))
    Q = qs[0] if ns == 1 else jnp.concatenate(qs, axis=0)

    qpos1 = qbase + q0 + lax.broadcasted_iota(I32, (tq, 1), 0)
    qc1 = qpos1 >> CHUNK_SHIFT
    qc = qc1 if ns == 1 else jnp.concatenate([qc1] * ns, axis=0)

    if window is not None:
        lo_row = ((qbase + q0) >> CHUNK_SHIFT) * CHUNK - n_prev * CHUNK - kbase
        start = jnp.clip(lo_row, 0, sk - window)
        nb = window // tk
    elif causal:
        start = 0
        hi_row = (((qbase + q0 + tq - 1) >> CHUNK_SHIFT) + 1) * CHUNK - kbase
        nb = (jnp.minimum(hi_row, sk_valid) + tk - 1) // tk
    else:
        start = 0
        nb = sk // tk

    m_ref[...] = jnp.full((R, 1), M_INIT, F32)
    l_ref[...] = jnp.zeros((R, 1), F32)
    acc_ref[...] = jnp.zeros((R, LANES), F32)

    def body(jb, carry):
        r0 = pl.multiple_of(start + jb * tk, 8)
        kb = k_ref[pl.ds(r0, tk), :]
        if has_k2:
            kb = jnp.concatenate([kb, k2_ref[pl.ds(r0, tk), :]], axis=1)
        s = lax.dot_general(Q, kb.astype(BF16), (((1,), (1,)), ((), ())),
                            preferred_element_type=F32) * scale
        if has_bias:
            s = s + bias_ref[0, 0]
        if masked:
            krow = r0 + lax.broadcasted_iota(I32, (1, tk), 1)
            kc = (kbase + krow) >> CHUNK_SHIFT
            s = jnp.where(kc <= qc, s, NEG_INF)
            if n_prev is not None:
                s = jnp.where(kc >= qc - n_prev, s, NEG_INF)
            if sk_valid < sk:
                s = jnp.where(krow < sk_valid, s, NEG_INF)
        m_old = m_ref[...]
        m_new = jnp.maximum(m_old, jnp.max(s, axis=-1, keepdims=True))
        p = jnp.exp(s - m_new)
        alpha = jnp.exp(m_old - m_new)
        l_ref[...] = alpha * l_ref[...] + jnp.sum(p, axis=-1, keepdims=True)
        acc_ref[...] = alpha * acc_ref[...] + jnp.dot(
            p.astype(BF16), v_ref[pl.ds(r0, tk), :].astype(BF16), preferred_element_type=F32)
        m_ref[...] = m_new
        return carry

    lax.fori_loop(0, nb, body, 0)

    m = m_ref[...]
    l = l_ref[...]
    acc = acc_ref[...]
    if has_sinks:
        sk_rows = [jnp.full((tq, 1), sinks_ref[g * ns + s], F32) for s in range(ns)]
        sink = sk_rows[0] if ns == 1 else jnp.concatenate(sk_rows, axis=0)
        m_t = jnp.maximum(m, sink)
        a = jnp.exp(m - m_t)
        l = l * a + jnp.exp(sink - m_t)
        acc = acc * a
    o = acc / l
    for t, spec in enumerate(outs):
        if spec[0] == "slot":
            ot = o[spec[1] * tq:(spec[1] + 1) * tq]
        else:
            ot = jnp.where(lane < 64, o[spec[1] * tq:(spec[1] + 1) * tq],
                           o[spec[2] * tq:(spec[2] + 1) * tq])
        o_ref[:, t * LANES:(t + 1) * LANES] = ot.astype(o_ref.dtype)


def attention(q, k, v, *, B, Sq, Sk, sk_valid, tq, tk, G, qw, slots, outs, scale, window=None,
              n_prev=None, causal=False, masked=True, qbase=0, kbase=0, q2=None, k2=None, bias=None,
              sinks=None, name="attn"):
    nq = Sq // tq
    ns = len(slots)
    R = ns * tq
    ow = len(outs)
    in_specs = [pl.BlockSpec((tq, qw * LANES), lambda b, g, i: (b * nq + i, g)),
                pl.BlockSpec((Sk, LANES), lambda b, g, i: (b, g)),
                pl.BlockSpec((Sk, LANES), lambda b, g, i: (b, g))]
    args = [q, k, v]
    blk = tq * qw * LANES * 4 + 2 * Sk * LANES * 4 + tq * ow * LANES * 2
    if q2 is not None:
        in_specs += [pl.BlockSpec((tq, LANES), lambda b, g, i: (b * nq + i, g // 2)),
                     pl.BlockSpec((Sk, LANES), lambda b, g, i: (b, 0))]
        args += [q2, k2]
        blk += tq * LANES * 4 + Sk * LANES * 4
    if bias is not None:
        nv = bias.shape[0]
        assert window == tk
        in_specs.append(pl.BlockSpec((1, 1, tq, window),
                                     lambda b, g, i, nv=nv: (jnp.minimum(i, nv - 1), g, 0, 0)))
        args.append(bias)
        blk += tq * window * 4
    if sinks is not None:
        in_specs.append(pl.BlockSpec(memory_space=pltpu.SMEM))
        args.append(sinks.astype(F32))
    kern = functools.partial(
        _attn_kernel, tq=tq, tk=tk, nq=nq, sk_valid=sk_valid, window=window, n_prev=n_prev,
        causal=causal, masked=masked, qbase=qbase, kbase=kbase, slots=tuple(slots), outs=tuple(outs),
        scale=scale, has_k2=q2 is not None, has_bias=bias is not None, has_sinks=sinks is not None,
        sk=Sk)
    return pl.pallas_call(
        kern,
        grid=(B, G, nq),
        in_specs=in_specs,
        out_specs=pl.BlockSpec((tq, ow * LANES), lambda b, g, i: (b * nq + i, g)),
        out_shape=jax.ShapeDtypeStruct((B * Sq, G * ow * LANES), BF16),
        scratch_shapes=[pltpu.VMEM((R, 1), F32), pltpu.VMEM((R, 1), F32), pltpu.VMEM((R, LANES), F32)],
        compiler_params=pltpu.CompilerParams(
            dimension_semantics=("parallel", "parallel", "arbitrary"),
            vmem_limit_bytes=_vmem_limit(blk, 6 * R * tk * 4)),
        name=name,
    )(*args)


def _unit(slots, k_tile, outs, q2=None, bias=None, sinks=()):
    return (tuple(slots), k_tile, tuple(outs), q2, bias, tuple(sinks))


def _attn2_kernel(*refs, tq, nq, units, mode, win, sk, sk_valid, n_prev, masked, qbase, kbase, scale,
                  has_q2, has_bias, has_sinks, sinks_per_group):
    q_ref, k_ref, v_ref = refs[0], refs[1], refs[2]
    pos = 3
    q2_ref = k2_ref = bias_ref = sinks_ref = None
    if has_q2:
        q2_ref, k2_ref = refs[pos], refs[pos + 1]
        pos += 2
    if has_bias:
        bias_ref = refs[pos]
        pos += 1
    if has_sinks:
        sinks_ref = refs[pos]
        pos += 1
    o_ref = refs[pos]

    g = pl.program_id(1)
    i = pl.program_id(2)
    q0 = i * tq
    lane = lax.broadcasted_iota(I32, (tq, LANES), 1)
    half_mask = {"lo": jnp.where(lane < 64, 1.0, 0.0).astype(BF16),
                 "hi": jnp.where(lane >= 64, 1.0, 0.0).astype(BF16)}
    qc1 = (qbase + q0 + lax.broadcasted_iota(I32, (tq, 1), 0)) >> CHUNK_SHIFT

    def compute(r0, L, mask_from=0):
        if masked:
            krow = r0 + mask_from + lax.broadcasted_iota(I32, (1, L - mask_from), 1)
            kc = (kbase + krow) >> CHUNK_SHIFT
        rows = pl.ds(r0, L) if not isinstance(r0, int) else slice(r0, r0 + L)
        for u, (slots, k_tile, outs, q2, bias_idx, sink_idx) in enumerate(units):
            ns = len(slots)
            qs = []
            for (t, mk) in slots:
                qt = q_ref[:, t * LANES:(t + 1) * LANES]
                if mk is not None:
                    qt = qt * half_mask[mk]
                if q2 is not None:
                    q2t = q2_ref[:, q2[0] * LANES:(q2[0] + 1) * LANES] * half_mask[q2[1]]
                    qt = jnp.concatenate([qt, q2t], axis=1)
                qs.append(qt)
            Q = qs[0] if ns == 1 else jnp.concatenate(qs, axis=0)
            kb = k_ref[rows, k_tile * LANES:(k_tile + 1) * LANES]
            if q2 is not None:
                kb = jnp.concatenate([kb, k2_ref[rows, :]], axis=1)
            vb = v_ref[rows, k_tile * LANES:(k_tile + 1) * LANES]
            log2 = bias_idx is None
            expf = jnp.exp2 if log2 else jnp.exp
            s = lax.dot_general(Q, kb, (((1,), (1,)), ((), ())), preferred_element_type=F32) * (
                scale * LOG2E if log2 else scale)
            if bias_idx is not None:
                s = s + bias_ref[0, bias_idx]
            if masked:
                qc = qc1 if ns == 1 else jnp.concatenate([qc1] * ns, axis=0)
                sm = s[:, mask_from:]
                sm = jnp.where(kc <= qc, sm, NEG_INF)
                if n_prev is not None:
                    sm = jnp.where(kc >= qc - n_prev, sm, NEG_INF)
                if sk_valid < sk:
                    sm = jnp.where(krow < sk_valid, sm, NEG_INF)
                s = sm if mask_from == 0 else jnp.concatenate([s[:, :mask_from], sm], axis=1)
            m = jnp.max(s, axis=-1, keepdims=True)
            if sink_idx:
                sk_rows = [jnp.full((tq, 1), sinks_ref[g * sinks_per_group + si], F32) for si in sink_idx]
                sink = sk_rows[0] if ns == 1 else jnp.concatenate(sk_rows, axis=0)
                sink = sink * LOG2E if log2 else sink
                m = jnp.maximum(m, sink)
            p = expf(s - m)
            l = jnp.sum(p, axis=-1, keepdims=True)
            if sink_idx:
                l = l + expf(sink - m)
            o = jnp.dot(p.astype(BF16), vb, preferred_element_type=F32) / l
            for spec in outs:
                t = spec[0]
                if spec[1] == "slot":
                    ot = o[spec[2] * tq:(spec[2] + 1) * tq]
                else:
                    ot = jnp.where(lane < 64, o[spec[2] * tq:(spec[2] + 1) * tq],
                                   o[spec[3] * tq:(spec[3] + 1) * tq])
                o_ref[:, t * LANES:(t + 1) * LANES] = ot.astype(o_ref.dtype)

    if mode == "window":
        lo_row = ((qbase + q0) >> CHUNK_SHIFT) * CHUNK - n_prev * CHUNK - kbase
        compute(pl.multiple_of(jnp.clip(lo_row, 0, sk - win), 16), win)
    elif mode == "full":
        compute(0, sk)
    else:
        for c in range(nq):
            pl.when(i == c)(functools.partial(compute, 0, min(sk, (c + 1) * tq), c * tq))


def attention2(q, k, v, *, B, Sq, Sk, sk_valid, tq, G, qw, kw, ow, units, scale, mode, win=None,
               n_prev=None, masked=True, qbase=0, kbase=0, q2=None, q2w=0, k2=None, bias=None,
               sinks=None, sinks_per_group=0, name="attn"):
    nq = Sq // tq
    assert mode != "causal" or (qbase == 0 and kbase == 0 and tq % CHUNK == 0)
    in_specs = [pl.BlockSpec((tq, qw * LANES), lambda b, g, i: (b * nq + i, g)),
                pl.BlockSpec((Sk, kw * LANES), lambda b, g, i: (b, g)),
                pl.BlockSpec((Sk, kw * LANES), lambda b, g, i: (b, g))]
    args = [q, k, v]
    blk = tq * qw * LANES * 2 + 2 * Sk * kw * LANES * 2 + tq * ow * LANES * 2
    if q2 is not None:
        in_specs += [pl.BlockSpec((tq, q2w * LANES), lambda b, g, i: (b * nq + i, g)),
                     pl.BlockSpec((Sk, LANES), lambda b, g, i: (b, 0))]
        args += [q2, k2]
        blk += tq * q2w * LANES * 2 + Sk * LANES * 2
    if bias is not None:
        nv, nbh = bias.shape[0], bias.shape[1] // G
        in_specs.append(pl.BlockSpec((1, nbh, tq, win),
                                     lambda b, g, i, nv=nv: (jnp.minimum(i, nv - 1), g, 0, 0)))
        args.append(bias)
        blk += nbh * tq * win * 4
    if sinks is not None:
        in_specs.append(pl.BlockSpec(memory_space=pltpu.SMEM))
        args.append(sinks.astype(F32))
    L = win if mode == "window" else Sk
    temp = sum(len(u[0]) for u in units) * tq * L * 12
    kern = functools.partial(
        _attn2_kernel, tq=tq, nq=nq, units=tuple(units), mode=mode, win=win, sk=Sk, sk_valid=sk_valid,
        n_prev=n_prev, masked=masked, qbase=qbase, kbase=kbase, scale=scale, has_q2=q2 is not None,
        has_bias=bias is not None, has_sinks=sinks is not None, sinks_per_group=sinks_per_group)
    return pl.pallas_call(
        kern,
        grid=(B, G, nq),
        in_specs=in_specs,
        out_specs=pl.BlockSpec((tq, ow * LANES), lambda b, g, i: (b * nq + i, g)),
        out_shape=jax.ShapeDtypeStruct((B * Sq, G * ow * LANES), BF16),
        compiler_params=pltpu.CompilerParams(
            dimension_semantics=("parallel", "parallel", "arbitrary"),
            vmem_limit_bytes=_vmem_limit(blk, temp)),
        name=name,
    )(*args)


def _toeplitz_kernel(e_ref, o_ref, *, tq, win):
    wp = e_ref.shape[-1]
    x = jnp.broadcast_to(e_ref[0, 0, 0:1, :], (tq, wp))
    y = pltpu.roll(x, 0, 1, stride=1, stride_axis=0)
    o_ref[0, 0] = y[:, :win]


def band_bias(table, tq, win, offs):
    H = table.shape[1]
    wp = _round_up(win + tq - 1, LANES)
    m = np.arange(wp)
    m = np.where(m < win, m, m - wp)
    idx = np.stack([np.clip(off - m, -MAX_REL, MAX_REL) + MAX_REL for off in offs])
    e = jnp.transpose(table.astype(F32)[idx], (0, 2, 1))
    e = jnp.broadcast_to(e[:, :, None, :], (len(offs), H, 8, wp))
    return pl.pallas_call(
        functools.partial(_toeplitz_kernel, tq=tq, win=win),
        grid=(len(offs), H),
        in_specs=[pl.BlockSpec((1, 1, 8, wp), lambda v, h: (v, h, 0, 0))],
        out_specs=pl.BlockSpec((1, 1, tq, win), lambda v, h: (v, h, 0, 0)),
        out_shape=jax.ShapeDtypeStruct((len(offs), H, tq, win), F32),
        compiler_params=pltpu.CompilerParams(dimension_semantics=("parallel", "parallel")),
        name="band_bias",
    )(e)


def _dsa_kernel(q_ref, k_ref, v_ref, iq_ref, ik_ref, iw_ref, o_ref, key_ref, mb_ref, m_ref, l_ref,
                acc_ref, *, tq, tk, sk, sk_valid, qbase, n_sel, scale, idx_scale):
    i = pl.program_id(1)
    q0 = i * tq
    qpos = qbase + q0 + lax.broadcasted_iota(I32, (tq, 1), 0)
    qc = qpos >> CHUNK_SHIFT
    hi_row = (((qbase + q0 + tq - 1) >> CHUNK_SHIFT) + 1) * CHUNK
    nb = (jnp.minimum(hi_row, sk_valid) + tk - 1) // tk
    lane = lax.broadcasted_iota(I32, (tq, LANES), 1)
    n_idx_tiles = iq_ref.shape[1] // LANES

    def col_of(jb):
        return jb * tk + lax.broadcasted_iota(I32, (1, tk), 1)

    def mask_invisible(x, col):
        x = jnp.where((col >> CHUNK_SHIFT) <= qc, x, NEG_INF)
        if sk_valid < sk:
            x = jnp.where(col < sk_valid, x, NEG_INF)
        return x

    def score_body(jb, carry):
        r0 = pl.multiple_of(jb * tk, 8)
        ikb = ik_ref[pl.ds(r0, tk), :].astype(BF16)
        acc = jnp.zeros((tq, tk), F32)
        for p in range(n_idx_tiles):
            t = iq_ref[:, p * LANES:(p + 1) * LANES]
            for hf in range(2):
                qm = jnp.where((lane < 64) if hf == 0 else (lane >= 64), t, 0.0).astype(BF16)
                lg = lax.dot_general(qm, ikb, (((1,), (1,)), ((), ())),
                                     preferred_element_type=F32) * idx_scale
                h = 2 * p + hf
                acc = acc + iw_ref[:, h:h + 1] * jnp.maximum(lg, 0.0)
        sc = mask_invisible(acc, col_of(jb))
        bits = lax.bitcast_convert_type(sc, I32)
        key_ref[jb] = bits ^ ((bits >> 31) & 0x7FFFFFFF)
        return carry

    lax.fori_loop(0, nb, score_body, 0)

    def count(pred):
        def cb(jb, c):
            return c + jnp.sum(jnp.where(pred(key_ref[jb], jb), 1.0, 0.0), axis=1, keepdims=True)
        return lax.fori_loop(0, nb, cb, jnp.zeros((tq, 1), F32))

    def count_eq_before(jtrial):
        def cb(jb, c):
            hit = jnp.where(key_ref[jb] == thr, jnp.where(col_of(jb) < jtrial, 1.0, 0.0), 0.0)
            return c + jnp.sum(hit, axis=1, keepdims=True)
        return lax.fori_loop(0, nb, cb, jnp.zeros((tq, 1), F32))

    def bit_body(it, cand):
        trial = cand | (jnp.int32(1) << (31 - it))
        trial_s = trial ^ INT_MIN
        cnt = count(lambda kk, jb: kk >= trial_s)
        return jnp.where(cnt >= n_sel, trial, cand)

    cand = lax.fori_loop(0, 32, bit_body, jnp.zeros((tq, 1), I32))
    thr = cand ^ INT_MIN

    cnt_gt = count(lambda kk, jb: kk > thr)
    n_eq = count(lambda kk, jb: kk == thr)
    need = n_sel - cnt_gt
    neg_bits = int(np.float32(NEG_INF).view(np.int32))
    neg_key = neg_bits ^ ((neg_bits >> 31) & 0x7FFFFFFF)
    tie = jnp.where(thr > neg_key, jnp.where(n_eq > need, 1.0, 0.0), 0.0)
    any_tie = jnp.max(tie) > 0.0

    def tie_cols():
        def jbit(it, cj):
            trial = cj | (jnp.int32(1) << (30 - it))
            c = count_eq_before(trial)
            return jnp.where(c < need, trial, cj)
        return lax.fori_loop(0, 31, jbit, jnp.zeros((tq, 1), I32))

    jmax = lax.cond(any_tie, tie_cols, lambda: jnp.full((tq, 1), BIG_COL, I32))

    def mask_body(jb, carry):
        kk = key_ref[jb]
        col = col_of(jb)
        sel = jnp.where(kk > thr, 0.0, jnp.where(kk == thr, jnp.where(col <= jmax, 0.0, NEG_INF), NEG_INF))
        mb_ref[jb] = mask_invisible(sel, col)
        return carry

    lax.fori_loop(0, nb, mask_body, 0)

    n_kv = k_ref.shape[1] // LANES
    grp = (q_ref.shape[1] // LANES) // n_kv
    R = grp * tq
    for kvh in range(n_kv):
        Q = jnp.concatenate(
            [q_ref[:, (kvh * grp + gi) * LANES:(kvh * grp + gi + 1) * LANES] for gi in range(grp)],
            axis=0).astype(BF16)
        m_ref[...] = jnp.full((R, 1), M_INIT, F32)
        l_ref[...] = jnp.zeros((R, 1), F32)
        acc_ref[...] = jnp.zeros((R, LANES), F32)

        def body(jb, carry, kvh=kvh, Q=Q):
            r0 = pl.multiple_of(jb * tk, 8)
            kb = k_ref[pl.ds(r0, tk), kvh * LANES:(kvh + 1) * LANES].astype(BF16)
            vb = v_ref[pl.ds(r0, tk), kvh * LANES:(kvh + 1) * LANES].astype(BF16)
            s = lax.dot_general(Q, kb, (((1,), (1,)), ((), ())), preferred_element_type=F32) * scale
            mb = mb_ref[jb]
            s = jnp.where(jnp.concatenate([mb] * grp, axis=0) < 0.0, NEG_INF, s)
            m_old = m_ref[...]
            m_new = jnp.maximum(m_old, jnp.max(s, axis=-1, keepdims=True))
            p = jnp.exp(s - m_new)
            alpha = jnp.exp(m_old - m_new)
            l_ref[...] = alpha * l_ref[...] + jnp.sum(p, axis=-1, keepdims=True)
            acc_ref[...] = alpha * acc_ref[...] + jnp.dot(p.astype(BF16), vb,
                                                          preferred_element_type=F32)
            m_ref[...] = m_new
            return carry

        lax.fori_loop(0, nb, body, 0)
        o = acc_ref[...] / l_ref[...]
        for gi in range(grp):
            hh = kvh * grp + gi
            o_ref[:, hh * LANES:(hh + 1) * LANES] = o[gi * tq:(gi + 1) * tq].astype(o_ref.dtype)


def dsa_attention(q, k, v, iq, ik2, iw, *, B, Sq, Sk, sk_valid, tq, tk, qbase, n_sel):
    nq = Sq // tq
    nkb = Sk // tk
    Hq = q.shape[1]
    Hk = k.shape[1]
    grp = Hq // Hk
    R = grp * tq
    kern = functools.partial(_dsa_kernel, tq=tq, tk=tk, sk=Sk, sk_valid=sk_valid, qbase=qbase, n_sel=n_sel,
                             scale=DSA_HEAD_DIM ** -0.5, idx_scale=IDX_DIM ** -0.5)
    blk = tq * (Hq + iq.shape[1] + LANES) * 4 + Sk * (2 * Hk + LANES) * 4 + tq * Hq * 2
    return pl.pallas_call(
        kern,
        grid=(B, nq),
        in_specs=[pl.BlockSpec((tq, Hq), lambda b, i: (b * nq + i, 0)),
                  pl.BlockSpec((Sk, Hk), lambda b, i: (b, 0)),
                  pl.BlockSpec((Sk, Hk), lambda b, i: (b, 0)),
                  pl.BlockSpec((tq, iq.shape[1]), lambda b, i: (b * nq + i, 0)),
                  pl.BlockSpec((Sk, LANES), lambda b, i: (b, 0)),
                  pl.BlockSpec((tq, LANES), lambda b, i: (b * nq + i, 0))],
        out_specs=pl.BlockSpec((tq, Hq), lambda b, i: (b * nq + i, 0)),
        out_shape=jax.ShapeDtypeStruct((B * Sq, Hq), BF16),
        scratch_shapes=[pltpu.VMEM((nkb, tq, tk), I32), pltpu.VMEM((nkb, tq, tk), F32),
                        pltpu.VMEM((R, 1), F32), pltpu.VMEM((R, 1), F32), pltpu.VMEM((R, LANES), F32)],
        compiler_params=pltpu.CompilerParams(
            dimension_semantics=("parallel", "arbitrary"),
            vmem_limit_bytes=_vmem_limit(blk, 2 * tq * Sk * 4 + 6 * R * tk * 4)),
        name="dsa_attention",
    )(q, k, v, iq, ik2, iw)


def _dsa2_kernel(q_ref, k_ref, v_ref, iq_ref, ik_ref, iw_ref, o_ref, key_ref, mb_ref, m_ref, l_ref,
                 acc_ref, alpha_ref, s_ref, p_ref, *, tq, tk, sk, sk_valid, qbase, n_sel, scale):
    i = pl.program_id(1)
    q0 = i * tq
    qc = (qbase + q0 + lax.broadcasted_iota(I32, (tq, 1), 0)) >> CHUNK_SHIFT
    hi_row = (((qbase + q0 + tq - 1) >> CHUNK_SHIFT) + 1) * CHUNK
    nb = (jnp.minimum(hi_row, sk_valid) + tk - 1) // tk
    lane = lax.broadcasted_iota(I32, (tq, LANES), 1)
    half_mask = (jnp.where(lane < 64, 1.0, 0.0).astype(BF16), jnp.where(lane >= 64, 1.0, 0.0).astype(BF16))
    n_idx_tiles = iq_ref.shape[1] // LANES
    n_lane_tiles = tk // LANES

    def col_of(jb):
        return jb * tk + lax.broadcasted_iota(I32, (1, tk), 1)

    def mask_invisible(x, col):
        x = jnp.where((col >> CHUNK_SHIFT) <= qc, x, NEG_INF)
        if sk_valid < sk:
            x = jnp.where(col < sk_valid, x, NEG_INF)
        return x

    def lane_fold(x):
        acc = x[:, 0:LANES]
        for t in range(1, n_lane_tiles):
            acc = acc + x[:, t * LANES:(t + 1) * LANES]
        return acc

    def score_body(jb, carry):
        r0 = pl.multiple_of(jb * tk, 16)
        ikb = ik_ref[pl.ds(r0, tk), :]
        acc = jnp.zeros((tq, tk), F32)
        for p in range(n_idx_tiles):
            t = iq_ref[:, p * LANES:(p + 1) * LANES]
            for hf in range(2):
                lg = lax.dot_general(t * half_mask[hf], ikb, (((1,), (1,)), ((), ())),
                                     preferred_element_type=F32)
                h = 2 * p + hf
                acc = acc + iw_ref[:, h:h + 1] * jnp.maximum(lg, 0.0)
        sc = mask_invisible(acc, col_of(jb))
        bits = lax.bitcast_convert_type(sc, I32)
        key_ref[jb] = bits ^ ((bits >> 31) & 0x7FFFFFFF)
        return carry

    lax.fori_loop(0, nb, score_body, 0)

    def count(hit):
        def cb(jb, c):
            return c + lane_fold(hit(key_ref[jb], jb))
        part = lax.fori_loop(0, nb, cb, jnp.zeros((tq, LANES), F32))
        return jnp.sum(part, axis=1, keepdims=True)

    def bit_body(it, cand):
        trial = cand | (jnp.int32(1) << (31 - it))
        trial_s = trial ^ INT_MIN
        cnt = count(lambda kk, jb: jnp.where(kk >= trial_s, 1.0, 0.0))
        return jnp.where(cnt >= n_sel, trial, cand)

    cand = lax.fori_loop(0, 32, bit_body, jnp.zeros((tq, 1), I32))
    thr = cand ^ INT_MIN

    cnt_gt = count(lambda kk, jb: jnp.where(kk > thr, 1.0, 0.0))
    n_eq = count(lambda kk, jb: jnp.where(kk == thr, 1.0, 0.0))
    need = n_sel - cnt_gt
    neg_bits = int(np.float32(NEG_INF).view(np.int32))
    neg_key = neg_bits ^ ((neg_bits >> 31) & 0x7FFFFFFF)
    tie = jnp.where(thr > neg_key, jnp.where(n_eq > need, 1.0, 0.0), 0.0)
    any_tie = jnp.max(tie) > 0.0

    def tie_cols():
        def jbit(it, cj):
            trial = cj | (jnp.int32(1) << (30 - it))
            c = count(lambda kk, jb: jnp.where(kk == thr, jnp.where(col_of(jb) < trial, 1.0, 0.0), 0.0))
            return jnp.where(c < need, trial, cj)
        return lax.fori_loop(0, 31, jbit, jnp.zeros((tq, 1), I32))

    jmax = lax.cond(any_tie, tie_cols, lambda: jnp.full((tq, 1), BIG_COL, I32))

    def mask_body(jb, carry):
        kk = key_ref[jb]
        col = col_of(jb)
        sel = jnp.where(kk > thr, 0.0, jnp.where(kk == thr, jnp.where(col <= jmax, 0.0, NEG_INF), NEG_INF))
        mb_ref[jb] = mask_invisible(sel, col)
        return carry

    lax.fori_loop(0, nb, mask_body, 0)

    n_kv = k_ref.shape[1] // LANES
    grp = (q_ref.shape[1] // LANES) // n_kv
    R = grp * tq
    m_ref[...] = jnp.full(m_ref.shape, M_INIT, F32)
    l_ref[...] = jnp.zeros(l_ref.shape, F32)
    acc_ref[...] = jnp.zeros(acc_ref.shape, F32)

    ch = min(SOFTMAX_ROWS, tq)
    n_ch = R // ch

    def body(jb, carry):
        r0 = pl.multiple_of(jb * tk, 16)
        for kvh in range(n_kv):
            Q = jnp.concatenate(
                [q_ref[:, (kvh * grp + gi) * LANES:(kvh * grp + gi + 1) * LANES] for gi in range(grp)], axis=0)
            kb = k_ref[pl.ds(r0, tk), kvh * LANES:(kvh + 1) * LANES]
            vb = v_ref[pl.ds(r0, tk), kvh * LANES:(kvh + 1) * LANES]
            s_ref[kvh] = lax.dot_general(Q, kb, (((1,), (1,)), ((), ())), preferred_element_type=F32)

            def rows_body(c, carry2, kvh=kvh):
                rr = pl.multiple_of(c * ch, ch)
                qr = pl.multiple_of((c * ch) % tq, ch)
                s = s_ref[kvh, pl.ds(rr, ch), :] * (scale * LOG2E) + mb_ref[jb, pl.ds(qr, ch), :]
                m_old = m_ref[kvh, pl.ds(rr, ch), :]
                m_new = jnp.maximum(m_old, jnp.max(s, axis=-1, keepdims=True))
                p = jnp.exp2(s - jnp.concatenate([m_new] * n_lane_tiles, axis=1))
                alpha = jnp.exp2(m_old - m_new)
                l_ref[kvh, pl.ds(rr, ch), :] = alpha * l_ref[kvh, pl.ds(rr, ch), :] + jnp.sum(
                    p, axis=-1, keepdims=True)
                m_ref[kvh, pl.ds(rr, ch), :] = m_new
                alpha_ref[kvh, pl.ds(rr, ch), :] = alpha
                p_ref[kvh, pl.ds(rr, ch), :] = p.astype(BF16)
                return carry2

            lax.fori_loop(0, n_ch, rows_body, 0, unroll=min(4, n_ch))
            acc_ref[kvh] = alpha_ref[kvh] * acc_ref[kvh] + jnp.dot(p_ref[kvh], vb,
                                                                   preferred_element_type=F32)
        return carry

    lax.fori_loop(0, nb, body, 0)
    for kvh in range(n_kv):
        o = acc_ref[kvh] / l_ref[kvh]
        for gi in range(grp):
            hh = kvh * grp + gi
            o_ref[:, hh * LANES:(hh + 1) * LANES] = o[gi * tq:(gi + 1) * tq].astype(o_ref.dtype)


def dsa_attention2(q, k, v, iq, ik2, iw, *, B, Sq, Sk, sk_valid, tq, tk, qbase, n_sel):
    nq = Sq // tq
    nkb = Sk // tk
    Hq = q.shape[1]
    Hk = k.shape[1]
    n_kv = Hk // LANES
    R = (Hq // Hk) * tq
    kern = functools.partial(_dsa2_kernel, tq=tq, tk=tk, sk=Sk, sk_valid=sk_valid, qbase=qbase,
                             n_sel=n_sel, scale=DSA_HEAD_DIM ** -0.5)
    blk = tq * (Hq + iq.shape[1]) * 2 + tq * LANES * 4 + Sk * (2 * Hk + LANES) * 2 + tq * Hq * 2
    scratch = 2 * nkb * tq * tk * 4 + 3 * n_kv * R * LANES * 4
    return pl.pallas_call(
        kern,
        grid=(B, nq),
        in_specs=[pl.BlockSpec((tq, Hq), lambda b, i: (b * nq + i, 0)),
                  pl.BlockSpec((Sk, Hk), lambda b, i: (b, 0)),
                  pl.BlockSpec((Sk, Hk), lambda b, i: (b, 0)),
                  pl.BlockSpec((tq, iq.shape[1]), lambda b, i: (b * nq + i, 0)),
                  pl.BlockSpec((Sk, LANES), lambda b, i: (b, 0)),
                  pl.BlockSpec((tq, LANES), lambda b, i: (b * nq + i, 0))],
        out_specs=pl.BlockSpec((tq, Hq), lambda b, i: (b * nq + i, 0)),
        out_shape=jax.ShapeDtypeStruct((B * Sq, Hq), BF16),
        scratch_shapes=[pltpu.VMEM((nkb, tq, tk), I32), pltpu.VMEM((nkb, tq, tk), F32),
                        pltpu.VMEM((n_kv, R, LANES), F32), pltpu.VMEM((n_kv, R, LANES), F32),
                        pltpu.VMEM((n_kv, R, LANES), F32), pltpu.VMEM((n_kv, R, LANES), F32),
                        pltpu.VMEM((n_kv, R, tk), F32), pltpu.VMEM((n_kv, R, tk), BF16)],
        compiler_params=pltpu.CompilerParams(
            dimension_semantics=("parallel", "arbitrary"),
            vmem_limit_bytes=_vmem_limit(blk, scratch + n_kv * R * (LANES * 4 + tk * 6) + (8 << 20))),
        name="dsa_attention",
    )(q, k, v, iq, ik2, iw)


def _pack_cache_kernel(*refs, rows, n_new, n_heads):
    cache_ref, o_ref = refs[0], refs[-1]
    dh = cache_ref.shape[3]
    for h in range(n_heads):
        o_ref[0:rows, h * dh:(h + 1) * dh] = cache_ref[0, :, h, :].astype(o_ref.dtype)
    if n_new:
        o_ref[rows:rows + n_new, :] = refs[1][...].astype(o_ref.dtype)
    n_pad = o_ref.shape[0] - rows - n_new
    if n_pad:
        o_ref[rows + n_new:, :] = jnp.zeros((n_pad, o_ref.shape[1]), o_ref.dtype)


def pack_cache(cache, new, rows_pad):
    B, rows, H, dh = cache.shape
    n_new = 0 if new is None else new.shape[0] // B
    in_specs = [pl.BlockSpec((1, rows, H, dh), lambda b: (b, 0, 0, 0))]
    args = [cache]
    if n_new:
        in_specs.append(pl.BlockSpec((n_new, H * dh), lambda b: (b, 0)))
        args.append(new)
    return pl.pallas_call(
        functools.partial(_pack_cache_kernel, rows=rows, n_new=n_new, n_heads=H),
        grid=(B,),
        in_specs=in_specs,
        out_specs=pl.BlockSpec((rows_pad, H * dh), lambda b: (b, 0)),
        out_shape=jax.ShapeDtypeStruct((B * rows_pad, H * dh), BF16),
        compiler_params=pltpu.CompilerParams(
            dimension_semantics=("parallel",),
            vmem_limit_bytes=_vmem_limit(rows * H * dh * 4 + rows_pad * H * dh * 2, rows * H * dh * 4)),
        name="pack_cache",
    )(*args)


def _rope_tables(pos, rot, group):
    width = 2 * LANES
    half = rot // 2
    inv = ROPE_THETA ** (-jnp.arange(half, dtype=F32) / half)
    ang = pos.astype(F32)[:, None] * inv[None, :]
    c, s = jnp.cos(ang), jnp.sin(ang)
    lane = np.arange(width) % group
    is_lo = lane < half
    is_hi = (lane >= half) & (lane < rot)
    idx = np.where(is_lo, lane, np.where(is_hi, lane - half, 0))
    cg, sg = c[:, idx], s[:, idx]
    C = jnp.where(is_lo | is_hi, cg, 1.0)
    S = jnp.where(is_lo, -sg, jnp.where(is_hi, sg, 0.0))
    perm = np.zeros((width, width), np.float32)
    dst = np.arange(width)
    src = np.where(is_lo, dst + half, dst - half)
    perm[src[is_lo | is_hi], dst[is_lo | is_hi]] = 1.0
    return half, C, S, jnp.asarray(perm, BF16)


def _pad_cols(w, n):
    return jnp.pad(w, ((0, 0), (0, n - w.shape[1])))


def _tile_gain(g, n):
    return jnp.tile(g.astype(F32), n // g.shape[0])


def _band_bias(table, tq, window, offs):
    qi = np.arange(tq)[:, None]
    kj = np.arange(window)[None, :]
    idx = np.stack([np.clip(qi - kj + off, -MAX_REL, MAX_REL) + MAX_REL for off in offs])
    return jnp.transpose(table.astype(F32)[idx], (0, 3, 1, 2))


def _pad_rows(x, B, rows, rows_pad):
    if rows == rows_pad:
        return x
    x = x.reshape(B, rows, -1)
    x = jnp.pad(x, ((0, 0), (0, rows_pad - rows), (0, 0)))
    return x.reshape(B * rows_pad, -1)


def _with_cache(cache, new, B, rows_pad, dtype=F32):
    n = new.shape[0] // B
    P = cache.shape[1]
    parts = [cache.reshape(B, P, -1).astype(dtype), new.reshape(B, n, -1).astype(dtype)]
    if rows_pad > P + n:
        parts.append(jnp.zeros((B, rows_pad - P - n, new.shape[1]), dtype))
    return jnp.concatenate(parts, axis=1).reshape(B * rows_pad, -1)


def _round_up(x, m):
    return (x + m - 1) // m * m


_SLOT1 = ((0, None),)
_OUT1 = (("slot", 0),)


def _mla_weights(w_in, q_lora_norm, kv_lora_norm, w_uq, w_ukv, q_norm, k_norm):
    H = MLA_HEADS
    wq3 = w_uq.reshape(MLA_Q_LORA, H, MLA_NOPE + MLA_ROPE)
    wkv3 = w_ukv.reshape(MLA_KV_LORA, H, MLA_NOPE + MLA_V)
    return dict(
        w_cq=w_in[:, :MLA_Q_LORA].astype(BF16),
        w_ckv=w_in[:, MLA_Q_LORA:MLA_Q_LORA + MLA_KV_LORA].astype(BF16),
        w_kr=_pad_cols(w_in[:, MLA_Q_LORA + MLA_KV_LORA:], LANES).astype(BF16),
        g_cq=q_lora_norm, g_ckv=kv_lora_norm,
        g_kr=_tile_gain(k_norm[MLA_NOPE:], LANES),
        w_qn=wq3[:, :, :MLA_NOPE].reshape(MLA_Q_LORA, H * MLA_NOPE).astype(BF16),
        w_qr=wq3[:, :, MLA_NOPE:].reshape(MLA_Q_LORA, H * MLA_ROPE).astype(BF16),
        g_qn=_tile_gain(q_norm[:MLA_NOPE], H * MLA_NOPE),
        g_qr=_tile_gain(q_norm[MLA_NOPE:], H * MLA_ROPE),
        w_kn=wkv3[:, :, :MLA_NOPE].reshape(MLA_KV_LORA, H * MLA_NOPE).astype(BF16),
        w_v=wkv3[:, :, MLA_NOPE:].reshape(MLA_KV_LORA, H * MLA_V).astype(BF16),
        g_kn=_tile_gain(k_norm[:MLA_NOPE], H * MLA_NOPE),
    )


def _mla_project(h, W, rope64):
    cq = mm(h, W["w_cq"], gain=W["g_cq"], gs=MLA_Q_LORA, norm=True, out_dtype=BF16, name="mla_cq")
    ckv, ckv_bf = mm(h, W["w_ckv"], gain=W["g_ckv"], gs=MLA_KV_LORA, norm=True, out_dtype=(F32, BF16),
                     name="mla_ckv")
    kr = mm(h, W["w_kr"], gain=W["g_kr"], gs=64, norm=True, rope=rope64, name="mla_kr")
    qn = mm(cq, W["w_qn"], gain=W["g_qn"], gs=LANES, norm=True, out_dtype=BF16, name="mla_qn")
    qr = mm(cq, W["w_qr"], gain=W["g_qr"], gs=64, norm=True, rope=rope64, out_dtype=BF16, name="mla_qr")
    return qn, qr, ckv, ckv_bf, kr


def _mla_expand(ckv_bf, W):
    kn = mm(ckv_bf, W["w_kn"], gain=W["g_kn"], gs=LANES, norm=True, out_dtype=BF16, name="mla_kn")
    v = mm(ckv_bf, W["w_v"], out_dtype=BF16, name="mla_v")
    return kn, v


def _mla_attend(qn, qr, kn, kr, v, *, B, Sq, Sk, sk_valid, tq, hps, mode, qbase):
    kr2 = jnp.concatenate([kr[:, :MLA_ROPE], kr[:, :MLA_ROPE]], axis=1).astype(BF16)
    units = tuple(_unit(((h, None),), h, ((h, "slot", 0),), q2=(h // 2, "lo" if h % 2 == 0 else "hi"))
                  for h in range(hps))
    return attention2(qn, kn, v, B=B, Sq=Sq, Sk=Sk, sk_valid=sk_valid, tq=tq, G=MLA_HEADS // hps,
                      qw=hps, kw=hps, ow=hps, units=units, scale=(MLA_NOPE + MLA_ROPE) ** -0.5,
                      mode=mode, qbase=qbase, q2=qr, q2w=hps // 2, k2=kr2, name="mla_attn")


def kernel(x_prompt, x_sample, mem_prompt, cache_a_ckv, cache_a_krope, cache_b_k, cache_b_v, cache_c_k, cache_c_v, cache_c_idxk, cache_d_k, cache_d_v, cache_mem_k, cache_mem_v, state_ffn_conv, norm_mix, norm_xattn, norm_mem, norm_ffn, a_w_in, a_q_lora_norm, a_kv_lora_norm, a_w_uq, a_w_ukv, a_q_norm, a_k_norm, a_w_out, b_w_in, b_q_norm, b_k_norm, b_rel_bias, b_w_out, c_w_in, c_q_norm, c_k_norm, c_idx_k_norm, c_w_out, d_w_in, d_q_norm, d_k_norm, d_sinks, d_w_out, xa_w_q, xa_w_kv, xa_q_norm, xa_k_norm, xa_w_o, ffn_w_up, ffn_conv_w, ffn_conv_b, ffn_w_down):
    B, S, D = x_prompt.shape
    Bd, n, _ = x_sample.shape
    P = cache_a_ckv.shape[2]
    depth = norm_mix.shape[0]
    Fd = ffn_conv_b.shape[1]
    Mp, Ms = B * S, Bd * n
    pos_p = jnp.arange(S, dtype=I32)
    pos_s = jnp.tile(P + jnp.arange(n, dtype=I32), Bd)
    names = ('a_ckv_p', 'a_krope_p', 'b_k_p', 'b_v_p', 'c_k_p', 'c_v_p', 'c_idxk_p', 'd_k_p', 'd_v_p',
             'mem_k_p', 'mem_v_p', 'conv_p', 'a_ckv_s', 'a_krope_s', 'b_k_s', 'b_v_s', 'c_k_s', 'c_v_s',
             'c_idxk_s', 'd_k_s', 'd_v_s', 'conv_s')
    st = {name: [] for name in names}
    xp = x_prompt.reshape(Mp, D)
    xs = x_sample.reshape(Ms, D)
    tq_p = 128

    def rope_pair(rot, group):
        return _rope_tables(pos_p, rot, group), _rope_tables(pos_s, rot, group)

    for i in range(depth):
        m, j = i % 4, i // 4
        hp = rmsnorm_cast(xp, norm_mix[i])
        hs = rmsnorm_cast(xs, norm_mix[i])
        if m == 0:
            W = _mla_weights(a_w_in[j], a_q_lora_norm[j], a_kv_lora_norm[j], a_w_uq[j], a_w_ukv[j],
                             a_q_norm[j], a_k_norm[j])
            rp, rs = rope_pair(MLA_ROPE, 64)
            w_out = a_w_out[j].astype(BF16)
            qn, qr, ckv, ckv_bf, kr = _mla_project(hp, W, rp)
            kn, v = _mla_expand(ckv_bf, W)
            op = _mla_attend(qn, qr, kn, kr, v, B=B, Sq=S, Sk=S, sk_valid=S, tq=_pick(S, (256, 128)),
                             hps=2, mode="causal", qbase=0)
            xp = mm(op, w_out, res=xp, name="mix_out")
            st['a_ckv_p'].append(ckv.reshape(B, S, MLA_KV_LORA))
            st['a_krope_p'].append(kr[:, :MLA_ROPE].reshape(B, S, MLA_ROPE))
            qn, qr, ckv, ckv_bf, kr = _mla_project(hs, W, rs)
            sk_valid = P + n
            Skp = _round_up(sk_valid, LANES)
            ckv_all = _with_cache(cache_a_ckv[j], ckv_bf, Bd, Skp, BF16)
            kr_all = _with_cache(cache_a_krope[j], kr[:, :MLA_ROPE], Bd, Skp)
            kn, v = _mla_expand(ckv_all, W)
            os_ = _mla_attend(qn, qr, kn, kr_all, v, B=Bd, Sq=n, Sk=Skp, sk_valid=sk_valid, tq=n,
                              hps=MLA_HEADS, mode="full", qbase=P)
            xs = mm(os_, w_out, res=xs, name="mix_out")
            st['a_ckv_s'].append(ckv.reshape(Bd, n, MLA_KV_LORA))
            st['a_krope_s'].append(kr[:, :MLA_ROPE].reshape(Bd, n, MLA_ROPE))
        elif m == 1:
            H, dh = BAND_HEADS, D // BAND_HEADS
            w_in = b_w_in[j]
            wq, wk, wv = (w_in[:, t * H * dh:(t + 1) * H * dh].astype(BF16) for t in range(3))
            gq, gk = _tile_gain(b_q_norm[j], H * dh), _tile_gain(b_k_norm[j], H * dh)
            w_out = b_w_out[j].astype(BF16)
            npv = BAND_PREV_CHUNKS
            def band_units(nh):
                return tuple(_unit(((h, None),), h, ((h, "slot", 0),), bias=h) for h in range(nh))

            q = mm(hp, wq, gain=gq, gs=dh, norm=True, out_dtype=BF16, name="band_q")
            k, k_bf = mm(hp, wk, gain=gk, gs=dh, norm=True, out_dtype=(F32, BF16), name="band_k")
            v, v_bf = mm(hp, wv, out_dtype=(F32, BF16), name="band_v")
            win = npv * CHUNK + tq_p
            offs = [t * tq_p for t in range(npv * CHUNK // tq_p)] + [npv * CHUNK]
            bias = band_bias(b_rel_bias[j], tq_p, win, offs)
            hps = 8
            op = attention2(q, k_bf, v_bf, B=B, Sq=S, Sk=S, sk_valid=S, tq=tq_p, G=H // hps, qw=hps,
                            kw=hps, ow=hps, units=band_units(hps), scale=dh ** -0.5, mode="window",
                            win=win, n_prev=npv, bias=bias, name="band_attn")
            xp = mm(op, w_out, res=xp, name="mix_out")
            r = min(npv * CHUNK, S)
            st['b_k_p'].append(k.reshape(B, S, H, dh)[:, S - r:])
            st['b_v_p'].append(v.reshape(B, S, H, dh)[:, S - r:])
            q = mm(hs, wq, gain=gq, gs=dh, norm=True, out_dtype=BF16, name="band_q")
            k = mm(hs, wk, gain=gk, gs=dh, norm=True, name="band_k")
            v = mm(hs, wv, name="band_v")
            rows = cache_b_k.shape[2]
            sk_valid = rows + n
            Skp = _round_up(sk_valid, LANES)
            k_all = pack_cache(cache_b_k[j], k, Skp)
            v_all = pack_cache(cache_b_v[j], v, Skp)
            bias = band_bias(b_rel_bias[j], n, Skp, [rows])
            os_ = attention2(q, k_all, v_all, B=Bd, Sq=n, Sk=Skp, sk_valid=sk_valid, tq=n, G=1, qw=H,
                             kw=H, ow=H, units=band_units(H), scale=dh ** -0.5, mode="window", win=Skp,
                             n_prev=npv, qbase=P, kbase=P - rows, bias=bias, name="band_attn")
            xs = mm(os_, w_out, res=xs, name="mix_out")
            st['b_k_s'].append(k.reshape(Bd, n, H, dh))
            st['b_v_s'].append(v.reshape(Bd, n, H, dh))
        elif m == 2:
            Hq, Hk, dh = DSA_HEADS, DSA_KV_HEADS, DSA_HEAD_DIM
            w_in = c_w_in[j]
            oq, ok_, ov = Hq * dh, Hq * dh + Hk * dh, Hq * dh + 2 * Hk * dh
            oiq = ov + IDX_HEADS * IDX_DIM
            oik = oiq + IDX_DIM
            wq = w_in[:, :oq].astype(BF16)
            wk = w_in[:, oq:ok_].astype(BF16)
            wv = w_in[:, ok_:ov].astype(BF16)
            wiq = w_in[:, ov:oiq].astype(BF16)
            wik = _pad_cols(w_in[:, oiq:oik], LANES).astype(BF16)
            wiw = _pad_cols(w_in[:, oik:], LANES).astype(BF16)
            gq, gk = _tile_gain(c_q_norm[j], oq), _tile_gain(c_k_norm[j], Hk * dh)
            gik = _tile_gain(c_idx_k_norm[j], LANES)
            giw = jnp.full((LANES,), IDX_HEADS ** -0.5 * IDX_DIM ** -0.5, F32)
            w_out = c_w_out[j].astype(BF16)
            rp128, rs128 = rope_pair(dh // 4, LANES)
            rp64, rs64 = rope_pair(IDX_DIM // 4, 64)

            def project(h, r128, r64):
                q = mm(h, wq, gain=gq, gs=dh, norm=True, rope=r128, out_dtype=BF16, name="dsa_q")
                k, k_bf = mm(h, wk, gain=gk, gs=dh, norm=True, rope=r128, out_dtype=(F32, BF16), name="dsa_k")
                v, v_bf = mm(h, wv, out_dtype=(F32, BF16), name="dsa_v")
                iq = mm(h, wiq, rope=r64, out_dtype=BF16, name="dsa_iq")
                ik = mm(h, wik, gain=gik, gs=64, norm=True, rope=r64, name="dsa_ik")
                iw = mm(h, wiw, gain=giw, name="dsa_iw")
                return q, k, v, k_bf, v_bf, iq, ik, iw

            q, k, v, k_bf, v_bf, iq, ik, iw = project(hp, rp128, rp64)
            ik2 = jnp.concatenate([ik[:, :IDX_DIM], ik[:, :IDX_DIM]], axis=1).astype(BF16)
            op = dsa_attention2(q, k_bf, v_bf, iq, ik2, iw, B=B, Sq=S, Sk=S, sk_valid=S,
                                tq=_pick(S, (256, 128)), tk=_pick(S, (512, 256, 128)), qbase=0,
                                n_sel=min(DSA_TOPK, S // 4))
            xp = mm(op, w_out, res=xp, name="mix_out")
            st['c_k_p'].append(k.reshape(B, S, Hk, dh))
            st['c_v_p'].append(v.reshape(B, S, Hk, dh))
            st['c_idxk_p'].append(ik[:, :IDX_DIM].reshape(B, S, IDX_DIM))
            q, k, v, _, _, iq, ik, iw = project(hs, rs128, rs64)
            sk_valid = P + n
            Skp = _round_up(sk_valid, LANES)
            k_all = pack_cache(cache_c_k[j], k, Skp)
            v_all = pack_cache(cache_c_v[j], v, Skp)
            ik_all = _with_cache(cache_c_idxk[j], ik[:, :IDX_DIM], Bd, Skp, BF16)
            ik2 = jnp.concatenate([ik_all, ik_all], axis=1)
            os_ = dsa_attention2(q, k_all, v_all, iq, ik2, iw, B=Bd, Sq=n, Sk=Skp, sk_valid=sk_valid,
                                 tq=n, tk=_pick(Skp, (512, 384, 256, 128)), qbase=P,
                                 n_sel=min(DSA_TOPK, sk_valid // 4))
            xs = mm(os_, w_out, res=xs, name="mix_out")
            st['c_k_s'].append(k.reshape(Bd, n, Hk, dh))
            st['c_v_s'].append(v.reshape(Bd, n, Hk, dh))
            st['c_idxk_s'].append(ik[:, :IDX_DIM].reshape(Bd, n, IDX_DIM))
        else:
            Hq, Hk, dh = SWA_HEADS, SWA_KV_HEADS, SWA_HEAD_DIM
            grp = Hq // Hk
            npairs = Hk // 2
            perm = np.array([[[2 * p * grp + t, (2 * p + 1) * grp + t] for t in range(grp)]
                             for p in range(npairs)]).reshape(-1)
            w_in = d_w_in[j]
            nqc, nkc = Hq * dh, Hk * dh
            wq = w_in[:, :nqc].reshape(D, Hq, dh)[:, perm].reshape(D, nqc).astype(BF16)
            wk = w_in[:, nqc:nqc + nkc].astype(BF16)
            wv = w_in[:, nqc + nkc:].astype(BF16)
            gq, gk = _tile_gain(d_q_norm[j], nqc), _tile_gain(d_k_norm[j], nkc)
            w_out = d_w_out[j].reshape(Hq, dh, D)[perm].reshape(nqc, D).astype(BF16)
            sinks = d_sinks[j][perm]
            rp, rs = rope_pair(dh // 4, 64)
            npv = SWA_PREV_CHUNKS
            def swa_units(npr):
                us = []
                for p in range(npr):
                    slots = tuple((p * grp + t, hf) for t in range(grp) for hf in ("lo", "hi"))
                    outs = tuple((p * grp + t, "pair", 2 * t, 2 * t + 1) for t in range(grp))
                    us.append(_unit(slots, p, outs, sinks=range(2 * grp * p, 2 * grp * (p + 1))))
                return tuple(us)

            def project(h, r):
                q = mm(h, wq, gain=gq, gs=64, norm=True, rope=r, out_dtype=BF16, name="swa_q")
                k, k_bf = mm(h, wk, gain=gk, gs=64, norm=True, rope=r, out_dtype=(F32, BF16), name="swa_k")
                v, v_bf = mm(h, wv, out_dtype=(F32, BF16), name="swa_v")
                return q, k, v, k_bf, v_bf

            q, k, v, k_bf, v_bf = project(hp, rp)
            win = npv * CHUNK + tq_p
            pps = 4
            op = attention2(q, k_bf, v_bf, B=B, Sq=S, Sk=S, sk_valid=S, tq=tq_p, G=npairs // pps,
                            qw=pps * grp, kw=pps, ow=pps * grp, units=swa_units(pps), scale=dh ** -0.5,
                            mode="window", win=win, n_prev=npv, sinks=sinks,
                            sinks_per_group=2 * grp * pps, name="swa_attn")
            xp = mm(op, w_out, res=xp, name="mix_out")
            r = min(npv * CHUNK, S)
            st['d_k_p'].append(k.reshape(B, S, Hk, dh)[:, S - r:])
            st['d_v_p'].append(v.reshape(B, S, Hk, dh)[:, S - r:])
            q, k, v, _, _ = project(hs, rs)
            rows = cache_d_k.shape[2]
            sk_valid = rows + n
            Skp = _round_up(sk_valid, LANES)
            k_all = _with_cache(cache_d_k[j].reshape(Bd, rows, nkc), k, Bd, Skp, BF16)
            v_all = _with_cache(cache_d_v[j].reshape(Bd, rows, nkc), v, Bd, Skp, BF16)
            os_ = attention2(q, k_all, v_all, B=Bd, Sq=n, Sk=Skp, sk_valid=sk_valid, tq=n, G=1,
                             qw=npairs * grp, kw=npairs, ow=npairs * grp, units=swa_units(npairs),
                             scale=dh ** -0.5, mode="window", win=Skp, n_prev=npv, qbase=P,
                             kbase=P - rows, sinks=sinks, sinks_per_group=2 * grp * npairs,
                             name="swa_attn")
            xs = mm(os_, w_out, res=xs, name="mix_out")
            st['d_k_s'].append(k.reshape(Bd, n, Hk, dh))
            st['d_v_s'].append(v.reshape(Bd, n, Hk, dh))

        Hx, dx = XA_HEADS, XA_HEAD_DIM
        Mm = mem_prompt.shape[1]
        hm = rmsnorm_cast(mem_prompt.reshape(B * Mm, D), norm_mem[i])
        w_kv = xa_w_kv[i]
        mk, mk_bf = mm(hm, w_kv[:, :Hx * dx].astype(BF16), gain=_tile_gain(xa_k_norm[i], Hx * dx), gs=dx,
                       norm=True, out_dtype=(F32, BF16), name="xa_k")
        mv, mv_bf = mm(hm, w_kv[:, Hx * dx:].astype(BF16), out_dtype=(F32, BF16), name="xa_v")
        st['mem_k_p'].append(mk.reshape(B, Mm, Hx, dx))
        st['mem_v_p'].append(mv.reshape(B, Mm, Hx, dx))
        wxq = xa_w_q[i].astype(BF16)
        gxq = _tile_gain(xa_q_norm[i], Hx * dx)
        wxo = xa_w_o[i].astype(BF16)
        xa_units = tuple(_unit(((h, None),), h, ((h, "slot", 0),)) for h in range(Hx))

        def cross(x, mk2, mv2, Bx, Sx, tq):
            qx = mm(x, wxq, in_gain=norm_xattn[i], gain=gxq, gs=dx, norm=True, out_dtype=BF16, name="xa_q")
            ox = attention2(qx, mk2, mv2, B=Bx, Sq=Sx, Sk=Mm, sk_valid=Mm, tq=tq, G=1, qw=Hx, kw=Hx,
                            ow=Hx, units=xa_units, scale=dx ** -0.5, mode="full", masked=False,
                            name="xattn")
            return mm(ox, wxo, res=x, name="xa_out")

        xp = cross(xp, mk_bf, mv_bf, B, S, _pick(S, (512, 256, 128)))
        xs = cross(xs, pack_cache(cache_mem_k[i], None, Mm), pack_cache(cache_mem_v[i], None, Mm), Bd, n, n)

        w_up = ffn_w_up[i]
        wg, wv_ = w_up[:, :Fd].astype(BF16), w_up[:, Fd:].astype(BF16)
        w_dn = ffn_w_down[i].astype(BF16)
        act, conv_p = ffn_up(xp, norm_ffn[i], wg, wv_, ffn_conv_w[i], ffn_conv_b[i],
                             jnp.zeros((B, CONV_W - 1, Fd), F32), n_seq=B, seq_rows=S)
        xp = mm(act, w_dn, res=xp, name="ffn_down")
        act, conv_s = ffn_up(xs, norm_ffn[i], wg, wv_, ffn_conv_w[i], ffn_conv_b[i], state_ffn_conv[i],
                             n_seq=Bd, seq_rows=n)
        xs = mm(act, w_dn, res=xs, name="ffn_down")
        st['conv_p'].append(conv_p)
        st['conv_s'].append(conv_s)

    order = ('a_ckv_p', 'a_krope_p', 'b_k_p', 'b_v_p', 'c_k_p', 'c_v_p', 'c_idxk_p', 'd_k_p', 'd_v_p',
             'mem_k_p', 'mem_v_p', 'conv_p', 'a_ckv_s', 'a_krope_s', 'b_k_s', 'b_v_s', 'c_k_s', 'c_v_s',
             'c_idxk_s', 'd_k_s', 'd_v_s', 'conv_s')
    return (xp.reshape(B, S, D), xs.reshape(Bd, n, D)) + tuple(jnp.stack(st[nm]) for nm in order)
```

```python
import functools

import numpy as np
import jax
import jax.numpy as jnp
from jax import lax
from jax.experimental import pallas as pl
from jax.experimental.pallas import tpu as pltpu

F32, BF16, I32 = jnp.float32, jnp.bfloat16, jnp.int32

CHUNK = 64
CHUNK_SHIFT = 6
ROPE_THETA = 500000.0
EPS = 1e-6
NEG_INF = -1e30
MLA_HEADS, MLA_NOPE, MLA_ROPE, MLA_V = 16, 128, 64, 128
MLA_Q_LORA, MLA_KV_LORA = 512, 512
BAND_HEADS, BAND_PREV_CHUNKS, MAX_REL = 16, 8, 128
DSA_HEADS, DSA_KV_HEADS, DSA_HEAD_DIM = 16, 4, 128
IDX_HEADS, IDX_DIM, DSA_TOPK = 16, 64, 256
SWA_HEADS, SWA_KV_HEADS, SWA_HEAD_DIM, SWA_PREV_CHUNKS = 32, 8, 64, 2
XA_HEADS, XA_HEAD_DIM = 4, 128
CONV_W = 3

LANES = 128
VMEM_LIMIT_MAX = 56 * 1024 * 1024
VMEM_LIMIT_MIN = 32 * 1024 * 1024

M_INIT = -3.0e38
INT_MIN = -2 ** 31
BIG_COL = 2 ** 30
LOG2E = 1.4426950408889634
SOFTMAX_ROWS = 64


def _vmem_limit(block_bytes, temp_bytes=0):
    need = 2 * block_bytes + temp_bytes + (4 << 20)
    return int(min(max(need, VMEM_LIMIT_MIN), VMEM_LIMIT_MAX))


def _pick(n, cands):
    for c in cands:
        if n % c == 0:
            return c
    raise ValueError(f"no tile for {n}")


def _rmsnorm_kernel(x_ref, g_ref, o_ref):
    x = x_ref[...]
    ms = jnp.mean(x * x, axis=-1, keepdims=True)
    o_ref[...] = (x * lax.rsqrt(ms + EPS) * g_ref[...]).astype(o_ref.dtype)


def rmsnorm_cast(x, g):
    M, K = x.shape
    tm = _pick(M, (512, 256, 128, 64, 16, 8))
    return pl.pallas_call(
        _rmsnorm_kernel,
        grid=(M // tm,),
        in_specs=[pl.BlockSpec((tm, K), lambda i: (i, 0)), pl.BlockSpec((1, K), lambda i: (0, 0))],
        out_specs=pl.BlockSpec((tm, K), lambda i: (i, 0)),
        out_shape=jax.ShapeDtypeStruct((M, K), BF16),
        compiler_params=pltpu.CompilerParams(
            dimension_semantics=("parallel",), vmem_limit_bytes=_vmem_limit(tm * K * 6, tm * K * 8)),
        name="rmsnorm_cast",
    )(x, g.reshape(1, K).astype(F32))


def _group_norm_rope(y, gain_ref, rope_refs, gs, norm, half):
    tm, tn = y.shape
    if norm and gs == tn:
        ss = jnp.sum(y * y, axis=-1, keepdims=True)
        return y * lax.rsqrt(ss * (1.0 / gs) + EPS) * gain_ref[...]
    outs = []
    for c in range(tn // LANES):
        yc = y[:, c * LANES:(c + 1) * LANES]
        if norm:
            yy = yc * yc
            if gs == LANES:
                ss = jnp.sum(yy, axis=-1, keepdims=True)
            else:
                lo = lax.broadcasted_iota(I32, yc.shape, 1) < gs
                s_lo = jnp.sum(jnp.where(lo, yy, 0.0), axis=-1, keepdims=True)
                s_hi = jnp.sum(jnp.where(lo, 0.0, yy), axis=-1, keepdims=True)
                ss = jnp.where(lo, s_lo, s_hi)
            yc = yc * lax.rsqrt(ss * (1.0 / gs) + EPS) * gain_ref[:, c * LANES:(c + 1) * LANES]
        elif gain_ref is not None:
            yc = yc * gain_ref[:, c * LANES:(c + 1) * LANES]
        outs.append(yc)
    y = outs[0] if len(outs) == 1 else jnp.concatenate(outs, axis=1)
    if not half:
        return y
    c_ref, s_ref, p_ref = rope_refs
    pw = p_ref.shape[0]
    outs = []
    for b in range(tn // pw):
        yb = y[:, b * pw:(b + 1) * pw]
        hi = yb.astype(BF16)
        lo = (yb - hi.astype(F32)).astype(BF16)
        partner = (jnp.dot(hi, p_ref[...], preferred_element_type=F32)
                   + jnp.dot(lo, p_ref[...], preferred_element_type=F32))
        outs.append(yb * c_ref[...] + partner * s_ref[...])
    return outs[0] if len(outs) == 1 else jnp.concatenate(outs, axis=1)


def _normed_rows(x_ref, g_ref, h_ref):
    @pl.when(pl.program_id(1) == 0)
    def _():
        x = x_ref[...]
        ms = jnp.mean(x * x, axis=-1, keepdims=True)
        h_ref[...] = (x * lax.rsqrt(ms + EPS) * g_ref[...]).astype(h_ref.dtype)
    return h_ref[...]


def _mm_kernel(*refs, has_gain, gs, norm, half, has_res, has_in_norm, n_out):
    a_ref, w_ref = refs[0], refs[1]
    pos = 2
    in_gain_ref = gain_ref = rope_refs = res_ref = None
    if has_in_norm:
        in_gain_ref = refs[pos]
        pos += 1
    if has_gain:
        gain_ref = refs[pos]
        pos += 1
    if half:
        rope_refs = refs[pos:pos + 3]
        pos += 3
    if has_res:
        res_ref = refs[pos]
        pos += 1
    a = _normed_rows(a_ref, in_gain_ref, refs[pos + n_out]) if has_in_norm else a_ref[...]
    y = jnp.dot(a, w_ref[...], preferred_element_type=F32)
    if has_gain or half:
        y = _group_norm_rope(y, gain_ref, rope_refs, gs, norm, half)
    if has_res:
        y = res_ref[...] + y
    for o_ref in refs[pos:pos + n_out]:
        o_ref[...] = y.astype(o_ref.dtype)


def mm(a, w, *, gain=None, gs=LANES, norm=False, rope=None, res=None, out_dtype=F32, tn=None,
       in_gain=None, name="mm"):
    M, K = a.shape
    N = w.shape[1]
    assert a.dtype == (BF16 if in_gain is None else F32) and w.dtype == BF16 and N % LANES == 0
    tm_c = (1024, 512, 256, 128, 64, 16, 8)
    if tn is None:
        if norm and gs > LANES:
            tn = gs
        elif K <= 512:
            tn = _pick(N, (2048, 1024, 512, 256, 128))
            if res is not None:
                tm_c = tm_c[1:]
        elif K <= 2048:
            tn = _pick(N, (1024, 512, 256, 128))
        else:
            tn = _pick(N, (512, 256, 128))
    rows_period = M if rope is None else rope[1].shape[0]
    tm = _pick(np.gcd(M, rows_period), tm_c)
    half = 0
    in_specs = [pl.BlockSpec((tm, K), lambda i, j: (i, 0)), pl.BlockSpec((K, tn), lambda i, j: (0, j))]
    args = [a, w]
    blk = tm * K * 2 + K * tn * 2 + tm * tn * 4
    scratch = []
    if in_gain is not None:
        in_specs.append(pl.BlockSpec((1, K), lambda i, j: (0, 0)))
        args.append(in_gain.reshape(1, K).astype(F32))
        scratch.append(pltpu.VMEM((tm, K), BF16))
        blk += tm * K * 4
    if gain is not None:
        in_specs.append(pl.BlockSpec((1, tn), lambda i, j: (0, j)))
        args.append(gain.reshape(1, N).astype(F32))
    if rope is not None:
        half, tab_c, tab_s, perm = rope
        if tn % perm.shape[0] != 0:
            tab_c, tab_s, perm = tab_c[:, :LANES], tab_s[:, :LANES], perm[:LANES, :LANES]
        pw = perm.shape[0]
        assert tn % pw == 0
        nrb = rows_period // tm
        for t in (tab_c, tab_s):
            in_specs.append(pl.BlockSpec((tm, pw), lambda i, j, nrb=nrb: (i % nrb, 0)))
            args.append(t)
        in_specs.append(pl.BlockSpec((pw, pw), lambda i, j: (0, 0)))
        args.append(perm)
        blk += 2 * tm * pw * 4 + pw * pw * 2
    if res is not None:
        in_specs.append(pl.BlockSpec((tm, tn), lambda i, j: (i, j)))
        args.append(res)
        blk += tm * tn * 4
    dtypes = out_dtype if isinstance(out_dtype, tuple) else (out_dtype,)
    kern = functools.partial(_mm_kernel, has_gain=gain is not None, gs=gs, norm=norm, half=half,
                             has_res=res is not None, has_in_norm=in_gain is not None, n_out=len(dtypes))
    outs = pl.pallas_call(
        kern,
        grid=(M // tm, N // tn),
        in_specs=in_specs,
        out_specs=[pl.BlockSpec((tm, tn), lambda i, j: (i, j)) for _ in dtypes],
        out_shape=[jax.ShapeDtypeStruct((M, N), dt) for dt in dtypes],
        scratch_shapes=scratch,
        compiler_params=pltpu.CompilerParams(
            dimension_semantics=("parallel", "arbitrary"),
            vmem_limit_bytes=_vmem_limit(blk + (len(dtypes) - 1) * tm * tn * 4, 3 * tm * tn * 4)),
        name=name,
    )(*args)
    return tuple(outs) if isinstance(out_dtype, tuple) else outs[0]


def _ffn_up_kernel(x_ref, gx_ref, wg_ref, wv_ref, cw_ref, cb_ref, prev_ref, act_ref, last_ref, carry_ref,
                   h_ref, *, tiles_per_seq, seq_rows):
    i = pl.program_id(0)
    h = _normed_rows(x_ref, gx_ref, h_ref)
    g = jnp.dot(h, wg_ref[...], preferred_element_type=F32)
    val = jnp.dot(h, wv_ref[...], preferred_element_type=F32)
    tm, tn = g.shape
    row = lax.broadcasted_iota(I32, (tm, tn), 0)
    g1 = pltpu.roll(g, 1, 0)
    g2 = pltpu.roll(g, 2, 0)
    if tiles_per_seq >= 1:
        first = (i % tiles_per_seq) == 0
        j = pl.program_id(1)
        p_prev = prev_ref[0]
        p_carry = carry_ref[j]
        p2 = jnp.where(first, p_prev[0:1, :], p_carry[6:7, :])
        p1 = jnp.where(first, p_prev[1:2, :], p_carry[7:8, :])
        g1 = jnp.where(row == 0, p1, g1)
        g2 = jnp.where(row == 0, p2, jnp.where(row == 1, p1, g2))
        carry_ref[j] = g[tm - 8:tm, :]
        last_ref[0] = g[tm - 8:tm, :]
    else:
        rmod = row % seq_rows
        g1 = jnp.where(rmod == 0, prev_ref[0], g1)
        g2 = jnp.where(rmod < 2, prev_ref[1], g2)
        last_ref[...] = g
    c = cw_ref[0:1, :] * g2 + cw_ref[1:2, :] * g1 + cw_ref[2:3, :] * g + cb_ref[...]
    act_ref[...] = (c * (1.0 / (1.0 + jnp.exp(-c))) * val).astype(act_ref.dtype)


def ffn_up(x, gx, wg, wv, conv_w, conv_b, prev, *, n_seq, seq_rows):
    M, D = x.shape
    Fd = wg.shape[1]
    tn = _pick(Fd, (512, 256, 128))
    nj = Fd // tn
    if seq_rows >= 256:
        tm = _pick(seq_rows, (1024, 512, 256))
        tps = seq_rows // tm
        prev_arg = prev
        prev_spec = pl.BlockSpec((1, 2, tn), lambda i, j: (i // tps, 0, j))
        last_shape = jax.ShapeDtypeStruct((M // tm, 8, Fd), F32)
        last_spec = pl.BlockSpec((1, 8, tn), lambda i, j: (i, 0, j))
    else:
        tm = M
        tps = 0
        z = jnp.zeros((n_seq, seq_rows - 1, Fd), F32)
        inj1 = jnp.concatenate([prev[:, 1:2], z], axis=1).reshape(M, Fd)
        inj2 = jnp.concatenate([prev[:, 0:2], z[:, 1:]], axis=1).reshape(M, Fd)
        prev_arg = jnp.stack([inj1, inj2])
        prev_spec = pl.BlockSpec((2, tm, tn), lambda i, j: (0, i, j))
        last_shape = jax.ShapeDtypeStruct((M, Fd), F32)
        last_spec = pl.BlockSpec((tm, tn), lambda i, j: (i, j))
    kern = functools.partial(_ffn_up_kernel, tiles_per_seq=tps, seq_rows=seq_rows)
    blk = tm * D * 5 + 2 * D * tn * 2 + tm * tn * 2 + 4 * tn * 4 + 2 * tm * tn * 4
    act, last = pl.pallas_call(
        kern,
        grid=(M // tm, nj),
        in_specs=[pl.BlockSpec((tm, D), lambda i, j: (i, 0)),
                  pl.BlockSpec((1, D), lambda i, j: (0, 0)),
                  pl.BlockSpec((D, tn), lambda i, j: (0, j)),
                  pl.BlockSpec((D, tn), lambda i, j: (0, j)),
                  pl.BlockSpec((CONV_W, tn), lambda i, j: (0, j)),
                  pl.BlockSpec((1, tn), lambda i, j: (0, j)),
                  prev_spec],
        out_specs=[pl.BlockSpec((tm, tn), lambda i, j: (i, j)), last_spec],
        out_shape=[jax.ShapeDtypeStruct((M, Fd), BF16), last_shape],
        scratch_shapes=[pltpu.VMEM((nj, 8, tn), F32), pltpu.VMEM((tm, D), BF16)],
        compiler_params=pltpu.CompilerParams(
            dimension_semantics=("arbitrary", "arbitrary"),
            vmem_limit_bytes=_vmem_limit(blk, 6 * tm * tn * 4)),
        name="ffn_up",
    )(x, gx.reshape(1, D).astype(F32), wg, wv, conv_w.astype(F32), conv_b.reshape(1, Fd).astype(F32), prev_arg)
    if tps == 0:
        last = last.reshape(n_seq, seq_rows, Fd)[:, seq_rows - 2:]
    else:
        last = last[tps - 1::tps, 6:8]
    return act, last


def _attn_kernel(*refs, tq, tk, nq, sk_valid, window, n_prev, causal, masked, qbase, kbase, slots,
                 outs, scale, has_k2, has_bias, has_sinks, sk):
    q_ref, k_ref, v_ref = refs[0], refs[1], refs[2]
    pos = 3
    q2_ref = k2_ref = bias_ref = sinks_ref = None
    if has_k2:
        q2_ref, k2_ref = refs[pos], refs[pos + 1]
        pos += 2
    if has_bias:
        bias_ref = refs[pos]
        pos += 1
    if has_sinks:
        sinks_ref = refs[pos]
        pos += 1
    o_ref = refs[pos]
    m_ref, l_ref, acc_ref = refs[pos + 1], refs[pos + 2], refs[pos + 3]

    g = pl.program_id(1)
    i = pl.program_id(2)
    q0 = i * tq
    ns = len(slots)
    R = ns * tq
    lane = lax.broadcasted_iota(I32, (tq, LANES), 1)

    qs = []
    for (t, mode) in slots:
        qt = q_ref[:, t * LANES:(t + 1) * LANES]
        if mode == "lo":
            qt = jnp.where(lane < 64, qt, 0.0)
        elif mode == "hi":
            qt = jnp.where(lane >= 64, qt, 0.0)
        if has_k2:
            q2 = jnp.where((lane >> 6) == (g % 2), q2_ref[...], 0.0)
            qt = jnp.concatenate([qt, q2], axis=1)
        qs.append(qt.astype(BF16))
    Q = qs[0] if ns == 1 else jnp.concatenate(qs, axis=0)

    qpos1 = qbase + q0 + lax.broadcasted_iota(I32, (tq, 1), 0)
    qc1 = qpos1 >> CHUNK_SHIFT
    qc = qc1 if ns == 1 else jnp.concatenate([qc1] * ns, axis=0)

    if window is not None:
        lo_row = ((qbase + q0) >> CHUNK_SHIFT) * CHUNK - n_prev * CHUNK - kbase
        start = jnp.clip(lo_row, 0, sk - window)
        nb = window // tk
    elif causal:
        start = 0
        hi_row = (((qbase + q0 + tq - 1) >> CHUNK_SHIFT) + 1) * CHUNK - kbase
        nb = (jnp.minimum(hi_row, sk_valid) + tk - 1) // tk
    else:
        start = 0
        nb = sk // tk

    m_ref[...] = jnp.full((R, 1), M_INIT, F32)
    l_ref[...] = jnp.zeros((R, 1), F32)
    acc_ref[...] = jnp.zeros((R, LANES), F32)

    def body(jb, carry):
        r0 = pl.multiple_of(start + jb * tk, 8)
        kb = k_ref[pl.ds(r0, tk), :]
        if has_k2:
            kb = jnp.concatenate([kb, k2_ref[pl.ds(r0, tk), :]], axis=1)
        s = lax.dot_general(Q, kb.astype(BF16), (((1,), (1,)), ((), ())),
                            preferred_element_type=F32) * scale
        if has_bias:
            s = s + bias_ref[0, 0]
        if masked:
            krow = r0 + lax.broadcasted_iota(I32, (1, tk), 1)
            kc = (kbase + krow) >> CHUNK_SHIFT
            s = jnp.where(kc <= qc, s, NEG_INF)
            if n_prev is not None:
                s = jnp.where(kc >= qc - n_prev, s, NEG_INF)
            if sk_valid < sk:
                s = jnp.where(krow < sk_valid, s, NEG_INF)
        m_old = m_ref[...]
        m_new = jnp.maximum(m_old, jnp.max(s, axis=-1, keepdims=True))
        p = jnp.exp(s - m_new)
        alpha = jnp.exp(m_old - m_new)
        l_ref[...] = alpha * l_ref[...] + jnp.sum(p, axis=-1, keepdims=True)
        acc_ref[...] = alpha * acc_ref[...] + jnp.dot(
            p.astype(BF16), v_ref[pl.ds(r0, tk), :].astype(BF16), preferred_element_type=F32)
        m_ref[...] = m_new
        return carry

    lax.fori_loop(0, nb, body, 0)

    m = m_ref[...]
    l = l_ref[...]
    acc = acc_ref[...]
    if has_sinks:
        sk_rows = [jnp.full((tq, 1), sinks_ref[g * ns + s], F32) for s in range(ns)]
        sink = sk_rows[0] if ns == 1 else jnp.concatenate(sk_rows, axis=0)
        m_t = jnp.maximum(m, sink)
        a = jnp.exp(m - m_t)
        l = l * a + jnp.exp(sink - m_t)
        acc = acc * a
    o = acc / l
    for t, spec in enumerate(outs):
        if spec[0] == "slot":
            ot = o[spec[1] * tq:(spec[1] + 1) * tq]
        else:
            ot = jnp.where(lane < 64, o[spec[1] * tq:(spec[1] + 1) * tq],
                           o[spec[2] * tq:(spec[2] + 1) * tq])
        o_ref[:, t * LANES:(t + 1) * LANES] = ot.astype(o_ref.dtype)


def attention(q, k, v, *, B, Sq, Sk, sk_valid, tq, tk, G, qw, slots, outs, scale, window=None,
              n_prev=None, causal=False, masked=True, qbase=0, kbase=0, q2=None, k2=None, bias=None,
              sinks=None, name="attn"):
    nq = Sq // tq
    ns = len(slots)
    R = ns * tq
    ow = len(outs)
    in_specs = [pl.BlockSpec((tq, qw * LANES), lambda b, g, i: (b * nq + i, g)),
                pl.BlockSpec((Sk, LANES), lambda b, g, i: (b, g)),
                pl.BlockSpec((Sk, LANES), lambda b, g, i: (b, g))]
    args = [q, k, v]
    blk = tq * qw * LANES * 4 + 2 * Sk * LANES * 4 + tq * ow * LANES * 2
    if q2 is not None:
        in_specs += [pl.BlockSpec((tq, LANES), lambda b, g, i: (b * nq + i, g // 2)),
                     pl.BlockSpec((Sk, LANES), lambda b, g, i: (b, 0))]
        args += [q2, k2]
        blk += tq * LANES * 4 + Sk * LANES * 4
    if bias is not None:
        nv = bias.shape[0]
        assert window == tk
        in_specs.append(pl.BlockSpec((1, 1, tq, window),
                                     lambda b, g, i, nv=nv: (jnp.minimum(i, nv - 1), g, 0, 0)))
        args.append(bias)
        blk += tq * window * 4
    if sinks is not None:
        in_specs.append(pl.BlockSpec(memory_space=pltpu.SMEM))
        args.append(sinks.astype(F32))
    kern = functools.partial(
        _attn_kernel, tq=tq, tk=tk, nq=nq, sk_valid=sk_valid, window=window, n_prev=n_prev,
        causal=causal, masked=masked, qbase=qbase, kbase=kbase, slots=tuple(slots), outs=tuple(outs),
        scale=scale, has_k2=q2 is not None, has_bias=bias is not None, has_sinks=sinks is not None,
        sk=Sk)
    return pl.pallas_call(
        kern,
        grid=(B, G, nq),
        in_specs=in_specs,
        out_specs=pl.BlockSpec((tq, ow * LANES), lambda b, g, i: (b * nq + i, g)),
        out_shape=jax.ShapeDtypeStruct((B * Sq, G * ow * LANES), BF16),
        scratch_shapes=[pltpu.VMEM((R, 1), F32), pltpu.VMEM((R, 1), F32), pltpu.VMEM((R, LANES), F32)],
        compiler_params=pltpu.CompilerParams(
            dimension_semantics=("parallel", "parallel", "arbitrary"),
            vmem_limit_bytes=_vmem_limit(blk, 6 * R * tk * 4)),
        name=name,
    )(*args)


def _unit(slots, k_tile, outs, q2=None, bias=None, sinks=()):
    return (tuple(slots), k_tile, tuple(outs), q2, bias, tuple(sinks))


def _attn2_kernel(*refs, tq, nq, units, mode, win, sk, sk_valid, n_prev, masked, qbase, kbase, scale,
                  has_q2, has_bias, has_sinks, sinks_per_group):
    q_ref, k_ref, v_ref = refs[0], refs[1], refs[2]
    pos = 3
    q2_ref = k2_ref = bias_ref = sinks_ref = None
    if has_q2:
        q2_ref, k2_ref = refs[pos], refs[pos + 1]
        pos += 2
    if has_bias:
        bias_ref = refs[pos]
        pos += 1
    if has_sinks:
        sinks_ref = refs[pos]
        pos += 1
    o_ref = refs[pos]

    g = pl.program_id(1)
    i = pl.program_id(2)
    q0 = i * tq
    lane = lax.broadcasted_iota(I32, (tq, LANES), 1)
    half_mask = {"lo": jnp.where(lane < 64, 1.0, 0.0).astype(BF16),
                 "hi": jnp.where(lane >= 64, 1.0, 0.0).astype(BF16)}
    qc1 = (qbase + q0 + lax.broadcasted_iota(I32, (tq, 1), 0)) >> CHUNK_SHIFT

    def compute(r0, L, mask_from=0):
        if masked:
            krow = r0 + mask_from + lax.broadcasted_iota(I32, (1, L - mask_from), 1)
            kc = (kbase + krow) >> CHUNK_SHIFT
        rows = pl.ds(r0, L) if not isinstance(r0, int) else slice(r0, r0 + L)
        for u, (slots, k_tile, outs, q2, bias_idx, sink_idx) in enumerate(units):
            ns = len(slots)
            qs = []
            for (t, mk) in slots:
                qt = q_ref[:, t * LANES:(t + 1) * LANES]
                if mk is not None:
                    qt = qt * half_mask[mk]
                if q2 is not None:
                    q2t = q2_ref[:, q2[0] * LANES:(q2[0] + 1) * LANES] * half_mask[q2[1]]
                    qt = jnp.concatenate([qt, q2t], axis=1)
                qs.append(qt)
            Q = qs[0] if ns == 1 else jnp.concatenate(qs, axis=0)
            kb = k_ref[rows, k_tile * LANES:(k_tile + 1) * LANES]
            if q2 is not None:
                kb = jnp.concatenate([kb, k2_ref[rows, :]], axis=1)
            vb = v_ref[rows, k_tile * LANES:(k_tile + 1) * LANES]
            log2 = bias_idx is None
            expf = jnp.exp2 if log2 else jnp.exp
            s = lax.dot_general(Q, kb, (((1,), (1,)), ((), ())), preferred_element_type=F32) * (
                scale * LOG2E if log2 else scale)
            if bias_idx is not None:
                s = s + bias_ref[0, bias_idx]
            if masked:
                qc = qc1 if ns == 1 else jnp.concatenate([qc1] * ns, axis=0)
                sm = s[:, mask_from:]
                sm = jnp.where(kc <= qc, sm, NEG_INF)
                if n_prev is not None:
                    sm = jnp.where(kc >= qc - n_prev, sm, NEG_INF)
                if sk_valid < sk:
                    sm = jnp.where(krow < sk_valid, sm, NEG_INF)
                s = sm if mask_from == 0 else jnp.concatenate([s[:, :mask_from], sm], axis=1)
            m = jnp.max(s, axis=-1, keepdims=True)
            if sink_idx:
                sk_rows = [jnp.full((tq, 1), sinks_ref[g * sinks_per_group + si], F32) for si in sink_idx]
                sink = sk_rows[0] if ns == 1 else jnp.concatenate(sk_rows, axis=0)
                sink = sink * LOG2E if log2 else sink
                m = jnp.maximum(m, sink)
            p = expf(s - m)
            l = jnp.sum(p, axis=-1, keepdims=True)
            if sink_idx:
                l = l + expf(sink - m)
            o = jnp.dot(p.astype(BF16), vb, preferred_element_type=F32) / l
            for spec in outs:
                t = spec[0]
                if spec[1] == "slot":
                    ot = o[spec[2] * tq:(spec[2] + 1) * tq]
                else:
                    ot = jnp.where(lane < 64, o[spec[2] * tq:(spec[2] + 1) * tq],
                                   o[spec[3] * tq:(spec[3] + 1) * tq])
                o_ref[:, t * LANES:(t + 1) * LANES] = ot.astype(o_ref.dtype)

    if mode == "window":
        lo_row = ((qbase + q0) >> CHUNK_SHIFT) * CHUNK - n_prev * CHUNK - kbase
        compute(pl.multiple_of(jnp.clip(lo_row, 0, sk - win), 16), win)
    elif mode == "full":
        compute(0, sk)
    else:
        for c in range(nq):
            pl.when(i == c)(functools.partial(compute, 0, min(sk, (c + 1) * tq), c * tq))


def attention2(q, k, v, *, B, Sq, Sk, sk_valid, tq, G, qw, kw, ow, units, scale, mode, win=None,
               n_prev=None, masked=True, qbase=0, kbase=0, q2=None, q2w=0, k2=None, bias=None,
               sinks=None, sinks_per_group=0, name="attn"):
    nq = Sq // tq
    assert mode != "causal" or (qbase == 0 and kbase == 0 and tq % CHUNK == 0)
    in_specs = [pl.BlockSpec((tq, qw * LANES), lambda b, g, i: (b * nq + i, g)),
                pl.BlockSpec((Sk, kw * LANES), lambda b, g, i: (b, g)),
                pl.BlockSpec((Sk, kw * LANES), lambda b, g, i: (b, g))]
    args = [q, k, v]
    blk = tq * qw * LANES * 2 + 2 * Sk * kw * LANES * 2 + tq * ow * LANES * 2
    if q2 is not None:
        in_specs += [pl.BlockSpec((tq, q2w * LANES), lambda b, g, i: (b * nq + i, g)),
                     pl.BlockSpec((Sk, LANES), lambda b, g, i: (b, 0))]
        args += [q2, k2]
        blk += tq * q2w * LANES * 2 + Sk * LANES * 2
    if bias is not None:
        nv, nbh = bias.shape[0], bias.shape[1] // G
        in_specs.append(pl.BlockSpec((1, nbh, tq, win),
                                     lambda b, g, i, nv=nv: (jnp.minimum(i, nv - 1), g, 0, 0)))
        args.append(bias)
        blk += nbh * tq * win * 4
    if sinks is not None:
        in_specs.append(pl.BlockSpec(memory_space=pltpu.SMEM))
        args.append(sinks.astype(F32))
    L = win if mode == "window" else Sk
    temp = sum(len(u[0]) for u in units) * tq * L * 12
    kern = functools.partial(
        _attn2_kernel, tq=tq, nq=nq, units=tuple(units), mode=mode, win=win, sk=Sk, sk_valid=sk_valid,
        n_prev=n_prev, masked=masked, qbase=qbase, kbase=kbase, scale=scale, has_q2=q2 is not None,
        has_bias=bias is not None, has_sinks=sinks is not None, sinks_per_group=sinks_per_group)
    return pl.pallas_call(
        kern,
        grid=(B, G, nq),
        in_specs=in_specs,
        out_specs=pl.BlockSpec((tq, ow * LANES), lambda b, g, i: (b * nq + i, g)),
        out_shape=jax.ShapeDtypeStruct((B * Sq, G * ow * LANES), BF16),
        compiler_params=pltpu.CompilerParams(
            dimension_semantics=("parallel", "parallel", "arbitrary"),
            vmem_limit_bytes=_vmem_limit(blk, temp)),
        name=name,
    )(*args)


def _toeplitz_kernel(e_ref, o_ref, *, tq, win):
    wp = e_ref.shape[-1]
    x = jnp.broadcast_to(e_ref[0, 0, 0:1, :], (tq, wp))
    y = pltpu.roll(x, 0, 1, stride=1, stride_axis=0)
    o_ref[0, 0] = y[:, :win]


def band_bias(table, tq, win, offs):
    H = table.shape[1]
    wp = _round_up(win + tq - 1, LANES)
    m = np.arange(wp)
    m = np.where(m < win, m, m - wp)
    idx = np.stack([np.clip(off - m, -MAX_REL, MAX_REL) + MAX_REL for off in offs])
    e = jnp.transpose(table.astype(F32)[idx], (0, 2, 1))
    e = jnp.broadcast_to(e[:, :, None, :], (len(offs), H, 8, wp))
    return pl.pallas_call(
        functools.partial(_toeplitz_kernel, tq=tq, win=win),
        grid=(len(offs), H),
        in_specs=[pl.BlockSpec((1, 1, 8, wp), lambda v, h: (v, h, 0, 0))],
        out_specs=pl.BlockSpec((1, 1, tq, win), lambda v, h: (v, h, 0, 0)),
        out_shape=jax.ShapeDtypeStruct((len(offs), H, tq, win), F32),
        compiler_params=pltpu.CompilerParams(dimension_semantics=("parallel", "parallel")),
        name="band_bias",
    )(e)


def _dsa_kernel(q_ref, k_ref, v_ref, iq_ref, ik_ref, iw_ref, o_ref, key_ref, mb_ref, m_ref, l_ref,
                acc_ref, *, tq, tk, sk, sk_valid, qbase, n_sel, scale, idx_scale):
    i = pl.program_id(1)
    q0 = i * tq
    qpos = qbase + q0 + lax.broadcasted_iota(I32, (tq, 1), 0)
    qc = qpos >> CHUNK_SHIFT
    hi_row = (((qbase + q0 + tq - 1) >> CHUNK_SHIFT) + 1) * CHUNK
    nb = (jnp.minimum(hi_row, sk_valid) + tk - 1) // tk
    lane = lax.broadcasted_iota(I32, (tq, LANES), 1)
    n_idx_tiles = iq_ref.shape[1] // LANES

    def col_of(jb):
        return jb * tk + lax.broadcasted_iota(I32, (1, tk), 1)

    def mask_invisible(x, col):
        x = jnp.where((col >> CHUNK_SHIFT) <= qc, x, NEG_INF)
        if sk_valid < sk:
            x = jnp.where(col < sk_valid, x, NEG_INF)
        return x

    def score_body(jb, carry):
        r0 = pl.multiple_of(jb * tk, 8)
        ikb = ik_ref[pl.ds(r0, tk), :].astype(BF16)
        acc = jnp.zeros((tq, tk), F32)
        for p in range(n_idx_tiles):
            t = iq_ref[:, p * LANES:(p + 1) * LANES]
            for hf in range(2):
                qm = jnp.where((lane < 64) if hf == 0 else (lane >= 64), t, 0.0).astype(BF16)
                lg = lax.dot_general(qm, ikb, (((1,), (1,)), ((), ())),
                                     preferred_element_type=F32) * idx_scale
                h = 2 * p + hf
                acc = acc + iw_ref[:, h:h + 1] * jnp.maximum(lg, 0.0)
        sc = mask_invisible(acc, col_of(jb))
        bits = lax.bitcast_convert_type(sc, I32)
        key_ref[jb] = bits ^ ((bits >> 31) & 0x7FFFFFFF)
        return carry

    lax.fori_loop(0, nb, score_body, 0)

    def count(pred):
        def cb(jb, c):
            return c + jnp.sum(jnp.where(pred(key_ref[jb], jb), 1.0, 0.0), axis=1, keepdims=True)
        return lax.fori_loop(0, nb, cb, jnp.zeros((tq, 1), F32))

    def count_eq_before(jtrial):
        def cb(jb, c):
            hit = jnp.where(key_ref[jb] == thr, jnp.where(col_of(jb) < jtrial, 1.0, 0.0), 0.0)
            return c + jnp.sum(hit, axis=1, keepdims=True)
        return lax.fori_loop(0, nb, cb, jnp.zeros((tq, 1), F32))

    def bit_body(it, cand):
        trial = cand | (jnp.int32(1) << (31 - it))
        trial_s = trial ^ INT_MIN
        cnt = count(lambda kk, jb: kk >= trial_s)
        return jnp.where(cnt >= n_sel, trial, cand)

    cand = lax.fori_loop(0, 32, bit_body, jnp.zeros((tq, 1), I32))
    thr = cand ^ INT_MIN

    cnt_gt = count(lambda kk, jb: kk > thr)
    n_eq = count(lambda kk, jb: kk == thr)
    need = n_sel - cnt_gt
    neg_bits = int(np.float32(NEG_INF).view(np.int32))
    neg_key = neg_bits ^ ((neg_bits >> 31) & 0x7FFFFFFF)
    tie = jnp.where(thr > neg_key, jnp.where(n_eq > need, 1.0, 0.0), 0.0)
    any_tie = jnp.max(tie) > 0.0

    def tie_cols():
        def jbit(it, cj):
            trial = cj | (jnp.int32(1) << (30 - it))
            c = count_eq_before(trial)
            return jnp.where(c < need, trial, cj)
        return lax.fori_loop(0, 31, jbit, jnp.zeros((tq, 1), I32))

    jmax = lax.cond(any_tie, tie_cols, lambda: jnp.full((tq, 1), BIG_COL, I32))

    def mask_body(jb, carry):
        kk = key_ref[jb]
        col = col_of(jb)
        sel = jnp.where(kk > thr, 0.0, jnp.where(kk == thr, jnp.where(col <= jmax, 0.0, NEG_INF), NEG_INF))
        mb_ref[jb] = mask_invisible(sel, col)
        return carry

    lax.fori_loop(0, nb, mask_body, 0)

    n_kv = k_ref.shape[1] // LANES
    grp = (q_ref.shape[1] // LANES) // n_kv
    R = grp * tq
    for kvh in range(n_kv):
        Q = jnp.concatenate(
            [q_ref[:, (kvh * grp + gi) * LANES:(kvh * grp + gi + 1) * LANES] for gi in range(grp)],
            axis=0).astype(BF16)
        m_ref[...] = jnp.full((R, 1), M_INIT, F32)
        l_ref[...] = jnp.zeros((R, 1), F32)
        acc_ref[...] = jnp.zeros((R, LANES), F32)

        def body(jb, carry, kvh=kvh, Q=Q):
            r0 = pl.multiple_of(jb * tk, 8)
            kb = k_ref[pl.ds(r0, tk), kvh * LANES:(kvh + 1) * LANES].astype(BF16)
            vb = v_ref[pl.ds(r0, tk), kvh * LANES:(kvh + 1) * LANES].astype(BF16)
            s = lax.dot_general(Q, kb, (((1,), (1,)), ((), ())), preferred_element_type=F32) * scale
            mb = mb_ref[jb]
            s = jnp.where(jnp.concatenate([mb] * grp, axis=0) < 0.0, NEG_INF, s)
            m_old = m_ref[...]
            m_new = jnp.maximum(m_old, jnp.max(s, axis=-1, keepdims=True))
            p = jnp.exp(s - m_new)
            alpha = jnp.exp(m_old - m_new)
            l_ref[...] = alpha * l_ref[...] + jnp.sum(p, axis=-1, keepdims=True)
            acc_ref[...] = alpha * acc_ref[...] + jnp.dot(p.astype(BF16), vb,
                                                          preferred_element_type=F32)
            m_ref[...] = m_new
            return carry

        lax.fori_loop(0, nb, body, 0)
        o = acc_ref[...] / l_ref[...]
        for gi in range(grp):
            hh = kvh * grp + gi
            o_ref[:, hh * LANES:(hh + 1) * LANES] = o[gi * tq:(gi + 1) * tq].astype(o_ref.dtype)


def dsa_attention(q, k, v, iq, ik2, iw, *, B, Sq, Sk, sk_valid, tq, tk, qbase, n_sel):
    nq = Sq // tq
    nkb = Sk // tk
    Hq = q.shape[1]
    Hk = k.shape[1]
    grp = Hq // Hk
    R = grp * tq
    kern = functools.partial(_dsa_kernel, tq=tq, tk=tk, sk=Sk, sk_valid=sk_valid, qbase=qbase, n_sel=n_sel,
                             scale=DSA_HEAD_DIM ** -0.5, idx_scale=IDX_DIM ** -0.5)
    blk = tq * (Hq + iq.shape[1] + LANES) * 4 + Sk * (2 * Hk + LANES) * 4 + tq * Hq * 2
    return pl.pallas_call(
        kern,
        grid=(B, nq),
        in_specs=[pl.BlockSpec((tq, Hq), lambda b, i: (b * nq + i, 0)),
                  pl.BlockSpec((Sk, Hk), lambda b, i: (b, 0)),
                  pl.BlockSpec((Sk, Hk), lambda b, i: (b, 0)),
                  pl.BlockSpec((tq, iq.shape[1]), lambda b, i: (b * nq + i, 0)),
                  pl.BlockSpec((Sk, LANES), lambda b, i: (b, 0)),
                  pl.BlockSpec((tq, LANES), lambda b, i: (b * nq + i, 0))],
        out_specs=pl.BlockSpec((tq, Hq), lambda b, i: (b * nq + i, 0)),
        out_shape=jax.ShapeDtypeStruct((B * Sq, Hq), BF16),
        scratch_shapes=[pltpu.VMEM((nkb, tq, tk), I32), pltpu.VMEM((nkb, tq, tk), F32),
                        pltpu.VMEM((R, 1), F32), pltpu.VMEM((R, 1), F32), pltpu.VMEM((R, LANES), F32)],
        compiler_params=pltpu.CompilerParams(
            dimension_semantics=("parallel", "arbitrary"),
            vmem_limit_bytes=_vmem_limit(blk, 2 * tq * Sk * 4 + 6 * R * tk * 4)),
        name="dsa_attention",
    )(q, k, v, iq, ik2, iw)


def _dsa2_kernel(q_ref, k_ref, v_ref, iq_ref, ik_ref, iw_ref, o_ref, key_ref, mb_ref, m_ref, l_ref,
                 acc_ref, alpha_ref, s_ref, p_ref, *, tq, tk, sk, sk_valid, qbase, n_sel, scale):
    i = pl.program_id(1)
    q0 = i * tq
    qc = (qbase + q0 + lax.broadcasted_iota(I32, (tq, 1), 0)) >> CHUNK_SHIFT
    hi_row = (((qbase + q0 + tq - 1) >> CHUNK_SHIFT) + 1) * CHUNK
    nb = (jnp.minimum(hi_row, sk_valid) + tk - 1) // tk
    lane = lax.broadcasted_iota(I32, (tq, LANES), 1)
    half_mask = (jnp.where(lane < 64, 1.0, 0.0).astype(BF16), jnp.where(lane >= 64, 1.0, 0.0).astype(BF16))
    n_idx_tiles = iq_ref.shape[1] // LANES
    n_lane_tiles = tk // LANES

    def col_of(jb):
        return jb * tk + lax.broadcasted_iota(I32, (1, tk), 1)

    def mask_invisible(x, col):
        x = jnp.where((col >> CHUNK_SHIFT) <= qc, x, NEG_INF)
        if sk_valid < sk:
            x = jnp.where(col < sk_valid, x, NEG_INF)
        return x

    def lane_fold(x):
        acc = x[:, 0:LANES]
        for t in range(1, n_lane_tiles):
            acc = acc + x[:, t * LANES:(t + 1) * LANES]
        return acc

    def score_body(jb, carry):
        r0 = pl.multiple_of(jb * tk, 16)
        ikb = ik_ref[pl.ds(r0, tk), :]
        acc = jnp.zeros((tq, tk), F32)
        for p in range(n_idx_tiles):
            t = iq_ref[:, p * LANES:(p + 1) * LANES]
            for hf in range(2):
                lg = lax.dot_general(t * half_mask[hf], ikb, (((1,), (1,)), ((), ())),
                                     preferred_element_type=F32)
                h = 2 * p + hf
                acc = acc + iw_ref[:, h:h + 1] * jnp.maximum(lg, 0.0)
        sc = mask_invisible(acc, col_of(jb))
        bits = lax.bitcast_convert_type(sc, I32)
        key_ref[jb] = bits ^ ((bits >> 31) & 0x7FFFFFFF)
        return carry

    lax.fori_loop(0, nb, score_body, 0)

    def count(hit):
        def cb(jb, c):
            return c + lane_fold(hit(key_ref[jb], jb))
        part = lax.fori_loop(0, nb, cb, jnp.zeros((tq, LANES), F32))
        return jnp.sum(part, axis=1, keepdims=True)

    def bit_body(it, cand):
        trial = cand | (jnp.int32(1) << (31 - it))
        trial_s = trial ^ INT_MIN
        cnt = count(lambda kk, jb: jnp.where(kk >= trial_s, 1.0, 0.0))
        return jnp.where(cnt >= n_sel, trial, cand)

    cand = lax.fori_loop(0, 32, bit_body, jnp.zeros((tq, 1), I32))
    thr = cand ^ INT_MIN

    cnt_gt = count(lambda kk, jb: jnp.where(kk > thr, 1.0, 0.0))
    n_eq = count(lambda kk, jb: jnp.where(kk == thr, 1.0, 0.0))
    need = n_sel - cnt_gt
    neg_bits = int(np.float32(NEG_INF).view(np.int32))
    neg_key = neg_bits ^ ((neg_bits >> 31) & 0x7FFFFFFF)
    tie = jnp.where(thr > neg_key, jnp.where(n_eq > need, 1.0, 0.0), 0.0)
    any_tie = jnp.max(tie) > 0.0

    def tie_cols():
        def jbit(it, cj):
            trial = cj | (jnp.int32(1) << (30 - it))
            c = count(lambda kk, jb: jnp.where(kk == thr, jnp.where(col_of(jb) < trial, 1.0, 0.0), 0.0))
            return jnp.where(c < need, trial, cj)
        return lax.fori_loop(0, 31, jbit, jnp.zeros((tq, 1), I32))

    jmax = lax.cond(any_tie, tie_cols, lambda: jnp.full((tq, 1), BIG_COL, I32))

    def mask_body(jb, carry):
        kk = key_ref[jb]
        col = col_of(jb)
        sel = jnp.where(kk > thr, 0.0, jnp.where(kk == thr, jnp.where(col <= jmax, 0.0, NEG_INF), NEG_INF))
        mb_ref[jb] = mask_invisible(sel, col)
        return carry

    lax.fori_loop(0, nb, mask_body, 0)

    n_kv = k_ref.shape[1] // LANES
    grp = (q_ref.shape[1] // LANES) // n_kv
    R = grp * tq
    m_ref[...] = jnp.full(m_ref.shape, M_INIT, F32)
    l_ref[...] = jnp.zeros(l_ref.shape, F32)
    acc_ref[...] = jnp.zeros(acc_ref.shape, F32)

    ch = min(SOFTMAX_ROWS, tq)
    n_ch = R // ch

    def body(jb, carry):
        r0 = pl.multiple_of(jb * tk, 16)
        for kvh in range(n_kv):
            Q = jnp.concatenate(
                [q_ref[:, (kvh * grp + gi) * LANES:(kvh * grp + gi + 1) * LANES] for gi in range(grp)], axis=0)
            kb = k_ref[pl.ds(r0, tk), kvh * LANES:(kvh + 1) * LANES]
            vb = v_ref[pl.ds(r0, tk), kvh * LANES:(kvh + 1) * LANES]
            s_ref[kvh] = lax.dot_general(Q, kb, (((1,), (1,)), ((), ())), preferred_element_type=F32)

            def rows_body(c, carry2, kvh=kvh):
                rr = pl.multiple_of(c * ch, ch)
                qr = pl.multiple_of((c * ch) % tq, ch)
                s = s_ref[kvh, pl.ds(rr, ch), :] * (scale * LOG2E) + mb_ref[jb, pl.ds(qr, ch), :]
                m_old = m_ref[kvh, pl.ds(rr, ch), :]
                m_new = jnp.maximum(m_old, jnp.max(s, axis=-1, keepdims=True))
                p = jnp.exp2(s - jnp.concatenate([m_new] * n_lane_tiles, axis=1))
                alpha = jnp.exp2(m_old - m_new)
                l_ref[kvh, pl.ds(rr, ch), :] = alpha * l_ref[kvh, pl.ds(rr, ch), :] + jnp.sum(
                    p, axis=-1, keepdims=True)
                m_ref[kvh, pl.ds(rr, ch), :] = m_new
                alpha_ref[kvh, pl.ds(rr, ch), :] = alpha
                p_ref[kvh, pl.ds(rr, ch), :] = p.astype(BF16)
                return carry2

            lax.fori_loop(0, n_ch, rows_body, 0, unroll=min(4, n_ch))
            acc_ref[kvh] = alpha_ref[kvh] * acc_ref[kvh] + jnp.dot(p_ref[kvh], vb,
                                                                   preferred_element_type=F32)
        return carry

    lax.fori_loop(0, nb, body, 0)
    for kvh in range(n_kv):
        o = acc_ref[kvh] / l_ref[kvh]
        for gi in range(grp):
            hh = kvh * grp + gi
            o_ref[:, hh * LANES:(hh + 1) * LANES] = o[gi * tq:(gi + 1) * tq].astype(o_ref.dtype)


def dsa_attention2(q, k, v, iq, ik2, iw, *, B, Sq, Sk, sk_valid, tq, tk, qbase, n_sel):
    nq = Sq // tq
    nkb = Sk // tk
    Hq = q.shape[1]
    Hk = k.shape[1]
    n_kv = Hk // LANES
    R = (Hq // Hk) * tq
    kern = functools.partial(_dsa2_kernel, tq=tq, tk=tk, sk=Sk, sk_valid=sk_valid, qbase=qbase,
                             n_sel=n_sel, scale=DSA_HEAD_DIM ** -0.5)
    blk = tq * (Hq + iq.shape[1]) * 2 + tq * LANES * 4 + Sk * (2 * Hk + LANES) * 2 + tq * Hq * 2
    scratch = 2 * nkb * tq * tk * 4 + 3 * n_kv * R * LANES * 4
    return pl.pallas_call(
        kern,
        grid=(B, nq),
        in_specs=[pl.BlockSpec((tq, Hq), lambda b, i: (b * nq + i, 0)),
                  pl.BlockSpec((Sk, Hk), lambda b, i: (b, 0)),
                  pl.BlockSpec((Sk, Hk), lambda b, i: (b, 0)),
                  pl.BlockSpec((tq, iq.shape[1]), lambda b, i: (b * nq + i, 0)),
                  pl.BlockSpec((Sk, LANES), lambda b, i: (b, 0)),
                  pl.BlockSpec((tq, LANES), lambda b, i: (b * nq + i, 0))],
        out_specs=pl.BlockSpec((tq, Hq), lambda b, i: (b * nq + i, 0)),
        out_shape=jax.ShapeDtypeStruct((B * Sq, Hq), BF16),
        scratch_shapes=[pltpu.VMEM((nkb, tq, tk), I32), pltpu.VMEM((nkb, tq, tk), F32),
                        pltpu.VMEM((n_kv, R, LANES), F32), pltpu.VMEM((n_kv, R, LANES), F32),
                        pltpu.VMEM((n_kv, R, LANES), F32), pltpu.VMEM((n_kv, R, LANES), F32),
                        pltpu.VMEM((n_kv, R, tk), F32), pltpu.VMEM((n_kv, R, tk), BF16)],
        compiler_params=pltpu.CompilerParams(
            dimension_semantics=("parallel", "arbitrary"),
            vmem_limit_bytes=_vmem_limit(blk, scratch + n_kv * R * (LANES * 4 + tk * 6) + (8 << 20))),
        name="dsa_attention",
    )(q, k, v, iq, ik2, iw)


def _pack_cache_kernel(*refs, rows, n_new, n_heads):
    cache_ref, o_ref = refs[0], refs[-1]
    dh = cache_ref.shape[3]
    for h in range(n_heads):
        o_ref[0:rows, h * dh:(h + 1) * dh] = cache_ref[0, :, h, :].astype(o_ref.dtype)
    if n_new:
        o_ref[rows:rows + n_new, :] = refs[1][...].astype(o_ref.dtype)
    n_pad = o_ref.shape[0] - rows - n_new
    if n_pad:
        o_ref[rows + n_new:, :] = jnp.zeros((n_pad, o_ref.shape[1]), o_ref.dtype)


def pack_cache(cache, new, rows_pad):
    B, rows, H, dh = cache.shape
    n_new = 0 if new is None else new.shape[0] // B
    in_specs = [pl.BlockSpec((1, rows, H, dh), lambda b: (b, 0, 0, 0))]
    args = [cache]
    if n_new:
        in_specs.append(pl.BlockSpec((n_new, H * dh), lambda b: (b, 0)))
        args.append(new)
    return pl.pallas_call(
        functools.partial(_pack_cache_kernel, rows=rows, n_new=n_new, n_heads=H),
        grid=(B,),
        in_specs=in_specs,
        out_specs=pl.BlockSpec((rows_pad, H * dh), lambda b: (b, 0)),
        out_shape=jax.ShapeDtypeStruct((B * rows_pad, H * dh), BF16),
        compiler_params=pltpu.CompilerParams(
            dimension_semantics=("parallel",),
            vmem_limit_bytes=_vmem_limit(rows * H * dh * 4 + rows_pad * H * dh * 2, rows * H * dh * 4)),
        name="pack_cache",
    )(*args)


def _rope_tables(pos, rot, group):
    width = 2 * LANES
    half = rot // 2
    inv = ROPE_THETA ** (-jnp.arange(half, dtype=F32) / half)
    ang = pos.astype(F32)[:, None] * inv[None, :]
    c, s = jnp.cos(ang), jnp.sin(ang)
    lane = np.arange(width) % group
    is_lo = lane < half
    is_hi = (lane >= half) & (lane < rot)
    idx = np.where(is_lo, lane, np.where(is_hi, lane - half, 0))
    cg, sg = c[:, idx], s[:, idx]
    C = jnp.where(is_lo | is_hi, cg, 1.0)
    S = jnp.where(is_lo, -sg, jnp.where(is_hi, sg, 0.0))
    perm = np.zeros((width, width), np.float32)
    dst = np.arange(width)
    src = np.where(is_lo, dst + half, dst - half)
    perm[src[is_lo | is_hi], dst[is_lo | is_hi]] = 1.0
    return half, C, S, jnp.asarray(perm, BF16)


def _pad_cols(w, n):
    return jnp.pad(w, ((0, 0), (0, n - w.shape[1])))


def _tile_gain(g, n):
    return jnp.tile(g.astype(F32), n // g.shape[0])


def _band_bias(table, tq, window, offs):
    qi = np.arange(tq)[:, None]
    kj = np.arange(window)[None, :]
    idx = np.stack([np.clip(qi - kj + off, -MAX_REL, MAX_REL) + MAX_REL for off in offs])
    return jnp.transpose(table.astype(F32)[idx], (0, 3, 1, 2))


def _pad_rows(x, B, rows, rows_pad):
    if rows == rows_pad:
        return x
    x = x.reshape(B, rows, -1)
    x = jnp.pad(x, ((0, 0), (0, rows_pad - rows), (0, 0)))
    return x.reshape(B * rows_pad, -1)


def _with_cache(cache, new, B, rows_pad, dtype=F32):
    n = new.shape[0] // B
    P = cache.shape[1]
    parts = [cache.reshape(B, P, -1).astype(dtype), new.reshape(B, n, -1).astype(dtype)]
    if rows_pad > P + n:
        parts.append(jnp.zeros((B, rows_pad - P - n, new.shape[1]), dtype))
    return jnp.concatenate(parts, axis=1).reshape(B * rows_pad, -1)


def _round_up(x, m):
    return (x + m - 1) // m * m


_SLOT1 = ((0, None),)
_OUT1 = (("slot", 0),)


def _mla_weights(w_in, q_lora_norm, kv_lora_norm, w_uq, w_ukv, q_norm, k_norm):
    H = MLA_HEADS
    wq3 = w_uq.reshape(MLA_Q_LORA, H, MLA_NOPE + MLA_ROPE)
    wkv3 = w_ukv.reshape(MLA_KV_LORA, H, MLA_NOPE + MLA_V)
    return dict(
        w_cq=w_in[:, :MLA_Q_LORA].astype(BF16),
        w_ckv=w_in[:, MLA_Q_LORA:MLA_Q_LORA + MLA_KV_LORA].astype(BF16),
        w_kr=_pad_cols(w_in[:, MLA_Q_LORA + MLA_KV_LORA:], LANES).astype(BF16),
        g_cq=q_lora_norm, g_ckv=kv_lora_norm,
        g_kr=_tile_gain(k_norm[MLA_NOPE:], LANES),
        w_qn=wq3[:, :, :MLA_NOPE].reshape(MLA_Q_LORA, H * MLA_NOPE).astype(BF16),
        w_qr=wq3[:, :, MLA_NOPE:].reshape(MLA_Q_LORA, H * MLA_ROPE).astype(BF16),
        g_qn=_tile_gain(q_norm[:MLA_NOPE], H * MLA_NOPE),
        g_qr=_tile_gain(q_norm[MLA_NOPE:], H * MLA_ROPE),
        w_kn=wkv3[:, :, :MLA_NOPE].reshape(MLA_KV_LORA, H * MLA_NOPE).astype(BF16),
        w_v=wkv3[:, :, MLA_NOPE:].reshape(MLA_KV_LORA, H * MLA_V).astype(BF16),
        g_kn=_tile_gain(k_norm[:MLA_NOPE], H * MLA_NOPE),
    )


def _mla_project(h, W, rope64):
    cq = mm(h, W["w_cq"], gain=W["g_cq"], gs=MLA_Q_LORA, norm=True, out_dtype=BF16, name="mla_cq")
    ckv, ckv_bf = mm(h, W["w_ckv"], gain=W["g_ckv"], gs=MLA_KV_LORA, norm=True, out_dtype=(F32, BF16),
                     name="mla_ckv")
    kr = mm(h, W["w_kr"], gain=W["g_kr"], gs=64, norm=True, rope=rope64, name="mla_kr")
    qn = mm(cq, W["w_qn"], gain=W["g_qn"], gs=LANES, norm=True, out_dtype=BF16, name="mla_qn")
    qr = mm(cq, W["w_qr"], gain=W["g_qr"], gs=64, norm=True, rope=rope64, out_dtype=BF16, name="mla_qr")
    return qn, qr, ckv, ckv_bf, kr


def _mla_expand(ckv_bf, W):
    kn = mm(ckv_bf, W["w_kn"], gain=W["g_kn"], gs=LANES, norm=True, out_dtype=BF16, name="mla_kn")
    v = mm(ckv_bf, W["w_v"], out_dtype=BF16, name="mla_v")
    return kn, v


def _mla_attend(qn, qr, kn, kr, v, *, B, Sq, Sk, sk_valid, tq, hps, mode, qbase):
    kr2 = jnp.concatenate([kr[:, :MLA_ROPE], kr[:, :MLA_ROPE]], axis=1).astype(BF16)
    units = tuple(_unit(((h, None),), h, ((h, "slot", 0),), q2=(h // 2, "lo" if h % 2 == 0 else "hi"))
                  for h in range(hps))
    return attention2(qn, kn, v, B=B, Sq=Sq, Sk=Sk, sk_valid=sk_valid, tq=tq, G=MLA_HEADS // hps,
                      qw=hps, kw=hps, ow=hps, units=units, scale=(MLA_NOPE + MLA_ROPE) ** -0.5,
                      mode=mode, qbase=qbase, q2=qr, q2w=hps // 2, k2=kr2, name="mla_attn")


def kernel(x_prompt, x_sample, mem_prompt, cache_a_ckv, cache_a_krope, cache_b_k, cache_b_v, cache_c_k, cache_c_v, cache_c_idxk, cache_d_k, cache_d_v, cache_mem_k, cache_mem_v, state_ffn_conv, norm_mix, norm_xattn, norm_mem, norm_ffn, a_w_in, a_q_lora_norm, a_kv_lora_norm, a_w_uq, a_w_ukv, a_q_norm, a_k_norm, a_w_out, b_w_in, b_q_norm, b_k_norm, b_rel_bias, b_w_out, c_w_in, c_q_norm, c_k_norm, c_idx_k_norm, c_w_out, d_w_in, d_q_norm, d_k_norm, d_sinks, d_w_out, xa_w_q, xa_w_kv, xa_q_norm, xa_k_norm, xa_w_o, ffn_w_up, ffn_conv_w, ffn_conv_b, ffn_w_down):
    B, S, D = x_prompt.shape
    Bd, n, _ = x_sample.shape
    P = cache_a_ckv.shape[2]
    depth = norm_mix.shape[0]
    Fd = ffn_conv_b.shape[1]
    Mp, Ms = B * S, Bd * n
    pos_p = jnp.arange(S, dtype=I32)
    pos_s = jnp.tile(P + jnp.arange(n, dtype=I32), Bd)
    names = ('a_ckv_p', 'a_krope_p', 'b_k_p', 'b_v_p', 'c_k_p', 'c_v_p', 'c_idxk_p', 'd_k_p', 'd_v_p',
             'mem_k_p', 'mem_v_p', 'conv_p', 'a_ckv_s', 'a_krope_s', 'b_k_s', 'b_v_s', 'c_k_s', 'c_v_s',
             'c_idxk_s', 'd_k_s', 'd_v_s', 'conv_s')
    st = {name: [] for name in names}
    xp = x_prompt.reshape(Mp, D)
    xs = x_sample.reshape(Ms, D)
    tq_p = 128

    def rope_pair(rot, group):
        return _rope_tables(pos_p, rot, group), _rope_tables(pos_s, rot, group)

    for i in range(depth):
        m, j = i % 4, i // 4
        hp = rmsnorm_cast(xp, norm_mix[i])
        hs = rmsnorm_cast(xs, norm_mix[i])
        if m == 0:
            W = _mla_weights(a_w_in[j], a_q_lora_norm[j], a_kv_lora_norm[j], a_w_uq[j], a_w_ukv[j],
                             a_q_norm[j], a_k_norm[j])
            rp, rs = rope_pair(MLA_ROPE, 64)
            w_out = a_w_out[j].astype(BF16)
            qn, qr, ckv, ckv_bf, kr = _mla_project(hp, W, rp)
            kn, v = _mla_expand(ckv_bf, W)
            op = _mla_attend(qn, qr, kn, kr, v, B=B, Sq=S, Sk=S, sk_valid=S, tq=_pick(S, (256, 128)),
                             hps=4, mode="causal", qbase=0)
            xp = mm(op, w_out, res=xp, name="mix_out")
            st['a_ckv_p'].append(ckv.reshape(B, S, MLA_KV_LORA))
            st['a_krope_p'].append(kr[:, :MLA_ROPE].reshape(B, S, MLA_ROPE))
            qn, qr, ckv, ckv_bf, kr = _mla_project(hs, W, rs)
            sk_valid = P + n
            Skp = _round_up(sk_valid, LANES)
            ckv_all = _with_cache(cache_a_ckv[j], ckv_bf, Bd, Skp, BF16)
            kr_all = _with_cache(cache_a_krope[j], kr[:, :MLA_ROPE], Bd, Skp)
            kn, v = _mla_expand(ckv_all, W)
            os_ = _mla_attend(qn, qr, kn, kr_all, v, B=Bd, Sq=n, Sk=Skp, sk_valid=sk_valid, tq=n,
                              hps=MLA_HEADS, mode="full", qbase=P)
            xs = mm(os_, w_out, res=xs, name="mix_out")
            st['a_ckv_s'].append(ckv.reshape(Bd, n, MLA_KV_LORA))
            st['a_krope_s'].append(kr[:, :MLA_ROPE].reshape(Bd, n, MLA_ROPE))
        elif m == 1:
            H, dh = BAND_HEADS, D // BAND_HEADS
            w_in = b_w_in[j]
            wq, wk, wv = (w_in[:, t * H * dh:(t + 1) * H * dh].astype(BF16) for t in range(3))
            gq, gk = _tile_gain(b_q_norm[j], H * dh), _tile_gain(b_k_norm[j], H * dh)
            w_out = b_w_out[j].astype(BF16)
            npv = BAND_PREV_CHUNKS
            def band_units(nh):
                return tuple(_unit(((h, None),), h, ((h, "slot", 0),), bias=h) for h in range(nh))

            q = mm(hp, wq, gain=gq, gs=dh, norm=True, out_dtype=BF16, name="band_q")
            k, k_bf = mm(hp, wk, gain=gk, gs=dh, norm=True, out_dtype=(F32, BF16), name="band_k")
            v, v_bf = mm(hp, wv, out_dtype=(F32, BF16), name="band_v")
            win = npv * CHUNK + tq_p
            offs = [t * tq_p for t in range(npv * CHUNK // tq_p)] + [npv * CHUNK]
            bias = band_bias(b_rel_bias[j], tq_p, win, offs)
            hps = 8
            op = attention2(q, k_bf, v_bf, B=B, Sq=S, Sk=S, sk_valid=S, tq=tq_p, G=H // hps, qw=hps,
                            kw=hps, ow=hps, units=band_units(hps), scale=dh ** -0.5, mode="window",
                            win=win, n_prev=npv, bias=bias, name="band_attn")
            xp = mm(op, w_out, res=xp, name="mix_out")
            r = min(npv * CHUNK, S)
            st['b_k_p'].append(k.reshape(B, S, H, dh)[:, S - r:])
            st['b_v_p'].append(v.reshape(B, S, H, dh)[:, S - r:])
            q = mm(hs, wq, gain=gq, gs=dh, norm=True, out_dtype=BF16, name="band_q")
            k = mm(hs, wk, gain=gk, gs=dh, norm=True, name="band_k")
            v = mm(hs, wv, name="band_v")
            rows = cache_b_k.shape[2]
            sk_valid = rows + n
            Skp = _round_up(sk_valid, LANES)
            k_all = pack_cache(cache_b_k[j], k, Skp)
            v_all = pack_cache(cache_b_v[j], v, Skp)
            bias = band_bias(b_rel_bias[j], n, Skp, [rows])
            os_ = attention2(q, k_all, v_all, B=Bd, Sq=n, Sk=Skp, sk_valid=sk_valid, tq=n, G=1, qw=H,
                             kw=H, ow=H, units=band_units(H), scale=dh ** -0.5, mode="window", win=Skp,
                             n_prev=npv, qbase=P, kbase=P - rows, bias=bias, name="band_attn")
            xs = mm(os_, w_out, res=xs, name="mix_out")
            st['b_k_s'].append(k.reshape(Bd, n, H, dh))
            st['b_v_s'].append(v.reshape(Bd, n, H, dh))
        elif m == 2:
            Hq, Hk, dh = DSA_HEADS, DSA_KV_HEADS, DSA_HEAD_DIM
            w_in = c_w_in[j]
            oq, ok_, ov = Hq * dh, Hq * dh + Hk * dh, Hq * dh + 2 * Hk * dh
            oiq = ov + IDX_HEADS * IDX_DIM
            oik = oiq + IDX_DIM
            wq = w_in[:, :oq].astype(BF16)
            wk = w_in[:, oq:ok_].astype(BF16)
            wv = w_in[:, ok_:ov].astype(BF16)
            wiq = w_in[:, ov:oiq].astype(BF16)
            wik = _pad_cols(w_in[:, oiq:oik], LANES).astype(BF16)
            wiw = _pad_cols(w_in[:, oik:], LANES).astype(BF16)
            gq, gk = _tile_gain(c_q_norm[j], oq), _tile_gain(c_k_norm[j], Hk * dh)
            gik = _tile_gain(c_idx_k_norm[j], LANES)
            giw = jnp.full((LANES,), IDX_HEADS ** -0.5 * IDX_DIM ** -0.5, F32)
            w_out = c_w_out[j].astype(BF16)
            rp128, rs128 = rope_pair(dh // 4, LANES)
            rp64, rs64 = rope_pair(IDX_DIM // 4, 64)

            def project(h, r128, r64):
                q = mm(h, wq, gain=gq, gs=dh, norm=True, rope=r128, out_dtype=BF16, name="dsa_q")
                k, k_bf = mm(h, wk, gain=gk, gs=dh, norm=True, rope=r128, out_dtype=(F32, BF16), name="dsa_k")
                v, v_bf = mm(h, wv, out_dtype=(F32, BF16), name="dsa_v")
                iq = mm(h, wiq, rope=r64, out_dtype=BF16, name="dsa_iq")
                ik = mm(h, wik, gain=gik, gs=64, norm=True, rope=r64, name="dsa_ik")
                iw = mm(h, wiw, gain=giw, name="dsa_iw")
                return q, k, v, k_bf, v_bf, iq, ik, iw

            q, k, v, k_bf, v_bf, iq, ik, iw = project(hp, rp128, rp64)
            ik2 = jnp.concatenate([ik[:, :IDX_DIM], ik[:, :IDX_DIM]], axis=1).astype(BF16)
            op = dsa_attention2(q, k_bf, v_bf, iq, ik2, iw, B=B, Sq=S, Sk=S, sk_valid=S,
                                tq=_pick(S, (256, 128)), tk=_pick(S, (512, 256, 128)), qbase=0,
                                n_sel=min(DSA_TOPK, S // 4))
            xp = mm(op, w_out, res=xp, name="mix_out")
            st['c_k_p'].append(k.reshape(B, S, Hk, dh))
            st['c_v_p'].append(v.reshape(B, S, Hk, dh))
            st['c_idxk_p'].append(ik[:, :IDX_DIM].reshape(B, S, IDX_DIM))
            q, k, v, _, _, iq, ik, iw = project(hs, rs128, rs64)
            sk_valid = P + n
            Skp = _round_up(sk_valid, LANES)
            k_all = pack_cache(cache_c_k[j], k, Skp)
            v_all = pack_cache(cache_c_v[j], v, Skp)
            ik_all = _with_cache(cache_c_idxk[j], ik[:, :IDX_DIM], Bd, Skp, BF16)
            ik2 = jnp.concatenate([ik_all, ik_all], axis=1)
            os_ = dsa_attention2(q, k_all, v_all, iq, ik2, iw, B=Bd, Sq=n, Sk=Skp, sk_valid=sk_valid,
                                 tq=n, tk=_pick(Skp, (512, 384, 256, 128)), qbase=P,
                                 n_sel=min(DSA_TOPK, sk_valid // 4))
            xs = mm(os_, w_out, res=xs, name="mix_out")
            st['c_k_s'].append(k.reshape(Bd, n, Hk, dh))
            st['c_v_s'].append(v.reshape(Bd, n, Hk, dh))
            st['c_idxk_s'].append(ik[:, :IDX_DIM].reshape(Bd, n, IDX_DIM))
        else:
            Hq, Hk, dh = SWA_HEADS, SWA_KV_HEADS, SWA_HEAD_DIM
            grp = Hq // Hk
            npairs = Hk // 2
            perm = np.array([[[2 * p * grp + t, (2 * p + 1) * grp + t] for t in range(grp)]
                             for p in range(npairs)]).reshape(-1)
            w_in = d_w_in[j]
            nqc, nkc = Hq * dh, Hk * dh
            wq = w_in[:, :nqc].reshape(D, Hq, dh)[:, perm].reshape(D, nqc).astype(BF16)
            wk = w_in[:, nqc:nqc + nkc].astype(BF16)
            wv = w_in[:, nqc + nkc:].astype(BF16)
            gq, gk = _tile_gain(d_q_norm[j], nqc), _tile_gain(d_k_norm[j], nkc)
            w_out = d_w_out[j].reshape(Hq, dh, D)[perm].reshape(nqc, D).astype(BF16)
            sinks = d_sinks[j][perm]
            rp, rs = rope_pair(dh // 4, 64)
            npv = SWA_PREV_CHUNKS
            def swa_units(npr):
                us = []
                for p in range(npr):
                    slots = tuple((p * grp + t, hf) for t in range(grp) for hf in ("lo", "hi"))
                    outs = tuple((p * grp + t, "pair", 2 * t, 2 * t + 1) for t in range(grp))
                    us.append(_unit(slots, p, outs, sinks=range(2 * grp * p, 2 * grp * (p + 1))))
                return tuple(us)

            def project(h, r):
                q = mm(h, wq, gain=gq, gs=64, norm=True, rope=r, out_dtype=BF16, name="swa_q")
                k, k_bf = mm(h, wk, gain=gk, gs=64, norm=True, rope=r, out_dtype=(F32, BF16), name="swa_k")
                v, v_bf = mm(h, wv, out_dtype=(F32, BF16), name="swa_v")
                return q, k, v, k_bf, v_bf

            q, k, v, k_bf, v_bf = project(hp, rp)
            win = npv * CHUNK + tq_p
            pps = 4
            op = attention2(q, k_bf, v_bf, B=B, Sq=S, Sk=S, sk_valid=S, tq=tq_p, G=npairs // pps,
                            qw=pps * grp, kw=pps, ow=pps * grp, units=swa_units(pps), scale=dh ** -0.5,
                            mode="window", win=win, n_prev=npv, sinks=sinks,
                            sinks_per_group=2 * grp * pps, name="swa_attn")
            xp = mm(op, w_out, res=xp, name="mix_out")
            r = min(npv * CHUNK, S)
            st['d_k_p'].append(k.reshape(B, S, Hk, dh)[:, S - r:])
            st['d_v_p'].append(v.reshape(B, S, Hk, dh)[:, S - r:])
            q, k, v, _, _ = project(hs, rs)
            rows = cache_d_k.shape[2]
            sk_valid = rows + n
            Skp = _round_up(sk_valid, LANES)
            k_all = _with_cache(cache_d_k[j].reshape(Bd, rows, nkc), k, Bd, Skp, BF16)
            v_all = _with_cache(cache_d_v[j].reshape(Bd, rows, nkc), v, Bd, Skp, BF16)
            os_ = attention2(q, k_all, v_all, B=Bd, Sq=n, Sk=Skp, sk_valid=sk_valid, tq=n, G=1,
                             qw=npairs * grp, kw=npairs, ow=npairs * grp, units=swa_units(npairs),
                             scale=dh ** -0.5, mode="window", win=Skp, n_prev=npv, qbase=P,
                             kbase=P - rows, sinks=sinks, sinks_per_group=2 * grp * npairs,
                             name="swa_attn")
            xs = mm(os_, w_out, res=xs, name="mix_out")
            st['d_k_s'].append(k.reshape(Bd, n, Hk, dh))
            st['d_v_s'].append(v.reshape(Bd, n, Hk, dh))

        Hx, dx = XA_HEADS, XA_HEAD_DIM
        Mm = mem_prompt.shape[1]
        hm = rmsnorm_cast(mem_prompt.reshape(B * Mm, D), norm_mem[i])
        w_kv = xa_w_kv[i]
        mk, mk_bf = mm(hm, w_kv[:, :Hx * dx].astype(BF16), gain=_tile_gain(xa_k_norm[i], Hx * dx), gs=dx,
                       norm=True, out_dtype=(F32, BF16), name="xa_k")
        mv, mv_bf = mm(hm, w_kv[:, Hx * dx:].astype(BF16), out_dtype=(F32, BF16), name="xa_v")
        st['mem_k_p'].append(mk.reshape(B, Mm, Hx, dx))
        st['mem_v_p'].append(mv.reshape(B, Mm, Hx, dx))
        wxq = xa_w_q[i].astype(BF16)
        gxq = _tile_gain(xa_q_norm[i], Hx * dx)
        wxo = xa_w_o[i].astype(BF16)
        xa_units = tuple(_unit(((h, None),), h, ((h, "slot", 0),)) for h in range(Hx))

        def cross(x, mk2, mv2, Bx, Sx, tq):
            qx = mm(x, wxq, in_gain=norm_xattn[i], gain=gxq, gs=dx, norm=True, out_dtype=BF16, name="xa_q")
            ox = attention2(qx, mk2, mv2, B=Bx, Sq=Sx, Sk=Mm, sk_valid=Mm, tq=tq, G=1, qw=Hx, kw=Hx,
                            ow=Hx, units=xa_units, scale=dx ** -0.5, mode="full", masked=False,
                            name="xattn")
            return mm(ox, wxo, res=x, name="xa_out")

        xp = cross(xp, mk_bf, mv_bf, B, S, _pick(S, (512, 256, 128)))
        xs = cross(xs, pack_cache(cache_mem_k[i], None, Mm), pack_cache(cache_mem_v[i], None, Mm), Bd, n, n)

        w_up = ffn_w_up[i]
        wg, wv_ = w_up[:, :Fd].astype(BF16), w_up[:, Fd:].astype(BF16)
        w_dn = ffn_w_down[i].astype(BF16)
        act, conv_p = ffn_up(xp, norm_ffn[i], wg, wv_, ffn_conv_w[i], ffn_conv_b[i],
                             jnp.zeros((B, CONV_W - 1, Fd), F32), n_seq=B, seq_rows=S)
        xp = mm(act, w_dn, res=xp, name="ffn_down")
        act, conv_s = ffn_up(xs, norm_ffn[i], wg, wv_, ffn_conv_w[i], ffn_conv_b[i], state_ffn_conv[i],
                             n_seq=Bd, seq_rows=n)
        xs = mm(act, w_dn, res=xs, name="ffn_down")
        st['conv_p'].append(conv_p)
        st['conv_s'].append(conv_s)

    order = ('a_ckv_p', 'a_krope_p', 'b_k_p', 'b_v_p', 'c_k_p', 'c_v_p', 'c_idxk_p', 'd_k_p', 'd_v_p',
             'mem_k_p', 'mem_v_p', 'conv_p', 'a_ckv_s', 'a_krope_s', 'b_k_s', 'b_v_s', 'c_k_s', 'c_v_s',
             'c_idxk_s', 'd_k_s', 'd_v_s', 'conv_s')
    return (xp.reshape(B, S, D), xs.reshape(Bd, n, D)) + tuple(jnp.stack(st[nm]) for nm in order)
```

```python
import functools

import numpy as np
import jax
import jax.numpy as jnp
from jax import lax
from jax.experimental import pallas as pl
from jax.experimental.pallas import tpu as pltpu

F32, BF16, I32 = jnp.float32, jnp.bfloat16, jnp.int32

CHUNK = 64
CHUNK_SHIFT = 6
ROPE_THETA = 500000.0
EPS = 1e-6
NEG_INF = -1e30
MLA_HEADS, MLA_NOPE, MLA_ROPE, MLA_V = 16, 128, 64, 128
MLA_Q_LORA, MLA_KV_LORA = 512, 512
BAND_HEADS, BAND_PREV_CHUNKS, MAX_REL = 16, 8, 128
DSA_HEADS, DSA_KV_HEADS, DSA_HEAD_DIM = 16, 4, 128
IDX_HEADS, IDX_DIM, DSA_TOPK = 16, 64, 256
SWA_HEADS, SWA_KV_HEADS, SWA_HEAD_DIM, SWA_PREV_CHUNKS = 32, 8, 64, 2
XA_HEADS, XA_HEAD_DIM = 4, 128
CONV_W = 3

LANES = 128
VMEM_LIMIT_MAX = 56 * 1024 * 1024
VMEM_LIMIT_MIN = 32 * 1024 * 1024

M_INIT = -3.0e38
INT_MIN = -2 ** 31
BIG_COL = 2 ** 30
LOG2E = 1.4426950408889634
SOFTMAX_ROWS = 64


def _vmem_limit(block_bytes, temp_bytes=0):
    need = 2 * block_bytes + temp_bytes + (4 << 20)
    return int(min(max(need, VMEM_LIMIT_MIN), VMEM_LIMIT_MAX))


def _pick(n, cands):
    for c in cands:
        if n % c == 0:
            return c
    raise ValueError(f"no tile for {n}")


def _rmsnorm_kernel(x_ref, g_ref, o_ref):
    x = x_ref[...]
    ms = jnp.mean(x * x, axis=-1, keepdims=True)
    o_ref[...] = (x * lax.rsqrt(ms + EPS) * g_ref[...]).astype(o_ref.dtype)


def rmsnorm_cast(x, g):
    M, K = x.shape
    tm = _pick(M, (512, 256, 128, 64, 16, 8))
    return pl.pallas_call(
        _rmsnorm_kernel,
        grid=(M // tm,),
        in_specs=[pl.BlockSpec((tm, K), lambda i: (i, 0)), pl.BlockSpec((1, K), lambda i: (0, 0))],
        out_specs=pl.BlockSpec((tm, K), lambda i: (i, 0)),
        out_shape=jax.ShapeDtypeStruct((M, K), BF16),
        compiler_params=pltpu.CompilerParams(
            dimension_semantics=("parallel",), vmem_limit_bytes=_vmem_limit(tm * K * 6, tm * K * 8)),
        name="rmsnorm_cast",
    )(x, g.reshape(1, K).astype(F32))


def _group_norm_rope(y, gain_ref, rope_refs, gs, norm, half):
    tm, tn = y.shape
    if norm and gs == tn:
        ss = jnp.sum(y * y, axis=-1, keepdims=True)
        return y * lax.rsqrt(ss * (1.0 / gs) + EPS) * gain_ref[...]
    outs = []
    for c in range(tn // LANES):
        yc = y[:, c * LANES:(c + 1) * LANES]
        if norm:
            yy = yc * yc
            if gs == LANES:
                ss = jnp.sum(yy, axis=-1, keepdims=True)
            else:
                lo = lax.broadcasted_iota(I32, yc.shape, 1) < gs
                s_lo = jnp.sum(jnp.where(lo, yy, 0.0), axis=-1, keepdims=True)
                s_hi = jnp.sum(jnp.where(lo, 0.0, yy), axis=-1, keepdims=True)
                ss = jnp.where(lo, s_lo, s_hi)
            yc = yc * lax.rsqrt(ss * (1.0 / gs) + EPS) * gain_ref[:, c * LANES:(c + 1) * LANES]
        elif gain_ref is not None:
            yc = yc * gain_ref[:, c * LANES:(c + 1) * LANES]
        outs.append(yc)
    y = outs[0] if len(outs) == 1 else jnp.concatenate(outs, axis=1)
    if not half:
        return y
    c_ref, s_ref, p_ref = rope_refs
    pw = p_ref.shape[0]
    outs = []
    for b in range(tn // pw):
        yb = y[:, b * pw:(b + 1) * pw]
        hi = yb.astype(BF16)
        lo = (yb - hi.astype(F32)).astype(BF16)
        partner = (jnp.dot(hi, p_ref[...], preferred_element_type=F32)
                   + jnp.dot(lo, p_ref[...], preferred_element_type=F32))
        outs.append(yb * c_ref[...] + partner * s_ref[...])
    return outs[0] if len(outs) == 1 else jnp.concatenate(outs, axis=1)


def _normed_rows(x_ref, g_ref, h_ref):
    @pl.when(pl.program_id(1) == 0)
    def _():
        x = x_ref[...]
        ms = jnp.mean(x * x, axis=-1, keepdims=True)
        h_ref[...] = (x * lax.rsqrt(ms + EPS) * g_ref[...]).astype(h_ref.dtype)
    return h_ref[...]


def _mm_kernel(*refs, has_gain, gs, norm, half, has_res, has_in_norm, n_out):
    a_ref, w_ref = refs[0], refs[1]
    pos = 2
    in_gain_ref = gain_ref = rope_refs = res_ref = None
    if has_in_norm:
        in_gain_ref = refs[pos]
        pos += 1
    if has_gain:
        gain_ref = refs[pos]
        pos += 1
    if half:
        rope_refs = refs[pos:pos + 3]
        pos += 3
    if has_res:
        res_ref = refs[pos]
        pos += 1
    a = _normed_rows(a_ref, in_gain_ref, refs[pos + n_out]) if has_in_norm else a_ref[...]
    y = jnp.dot(a, w_ref[...], preferred_element_type=F32)
    if has_gain or half:
        y = _group_norm_rope(y, gain_ref, rope_refs, gs, norm, half)
    if has_res:
        y = res_ref[...] + y
    for o_ref in refs[pos:pos + n_out]:
        o_ref[...] = y.astype(o_ref.dtype)


def mm(a, w, *, gain=None, gs=LANES, norm=False, rope=None, res=None, out_dtype=F32, tn=None,
       in_gain=None, name="mm"):
    M, K = a.shape
    N = w.shape[1]
    assert a.dtype == (BF16 if in_gain is None else F32) and w.dtype == BF16 and N % LANES == 0
    tm_c = (1024, 512, 256, 128, 64, 16, 8)
    if tn is None:
        if norm and gs > LANES:
            tn = gs
        elif K <= 512:
            tn = _pick(N, (2048, 1024, 512, 256, 128))
            if res is not None:
                tm_c = tm_c[1:]
        elif K <= 2048:
            tn = _pick(N, (1024, 512, 256, 128))
        else:
            tn = _pick(N, (512, 256, 128))
    rows_period = M if rope is None else rope[1].shape[0]
    tm = _pick(np.gcd(M, rows_period), tm_c)
    half = 0
    in_specs = [pl.BlockSpec((tm, K), lambda i, j: (i, 0)), pl.BlockSpec((K, tn), lambda i, j: (0, j))]
    args = [a, w]
    blk = tm * K * 2 + K * tn * 2 + tm * tn * 4
    scratch = []
    if in_gain is not None:
        in_specs.append(pl.BlockSpec((1, K), lambda i, j: (0, 0)))
        args.append(in_gain.reshape(1, K).astype(F32))
        scratch.append(pltpu.VMEM((tm, K), BF16))
        blk += tm * K * 4
    if gain is not None:
        in_specs.append(pl.BlockSpec((1, tn), lambda i, j: (0, j)))
        args.append(gain.reshape(1, N).astype(F32))
    if rope is not None:
        half, tab_c, tab_s, perm = rope
        if tn % perm.shape[0] != 0:
            tab_c, tab_s, perm = tab_c[:, :LANES], tab_s[:, :LANES], perm[:LANES, :LANES]
        pw = perm.shape[0]
        assert tn % pw == 0
        nrb = rows_period // tm
        for t in (tab_c, tab_s):
            in_specs.append(pl.BlockSpec((tm, pw), lambda i, j, nrb=nrb: (i % nrb, 0)))
            args.append(t)
        in_specs.append(pl.BlockSpec((pw, pw), lambda i, j: (0, 0)))
        args.append(perm)
        blk += 2 * tm * pw * 4 + pw * pw * 2
    if res is not None:
        in_specs.append(pl.BlockSpec((tm, tn), lambda i, j: (i, j)))
        args.append(res)
        blk += tm * tn * 4
    dtypes = out_dtype if isinstance(out_dtype, tuple) else (out_dtype,)
    kern = functools.partial(_mm_kernel, has_gain=gain is not None, gs=gs, norm=norm, half=half,
                             has_res=res is not None, has_in_norm=in_gain is not None, n_out=len(dtypes))
    outs = pl.pallas_call(
        kern,
        grid=(M // tm, N // tn),
        in_specs=in_specs,
        out_specs=[pl.BlockSpec((tm, tn), lambda i, j: (i, j)) for _ in dtypes],
        out_shape=[jax.ShapeDtypeStruct((M, N), dt) for dt in dtypes],
        scratch_shapes=scratch,
        compiler_params=pltpu.CompilerParams(
            dimension_semantics=("parallel", "arbitrary"),
            vmem_limit_bytes=_vmem_limit(blk + (len(dtypes) - 1) * tm * tn * 4, 3 * tm * tn * 4)),
        name=name,
    )(*args)
    return tuple(outs) if isinstance(out_dtype, tuple) else outs[0]


def _ffn_up_kernel(x_ref, gx_ref, wg_ref, wv_ref, cw_ref, cb_ref, prev_ref, act_ref, last_ref, carry_ref,
                   h_ref, *, tiles_per_seq, seq_rows):
    i = pl.program_id(0)
    h = _normed_rows(x_ref, gx_ref, h_ref)
    g = jnp.dot(h, wg_ref[...], preferred_element_type=F32)
    val = jnp.dot(h, wv_ref[...], preferred_element_type=F32)
    tm, tn = g.shape
    row = lax.broadcasted_iota(I32, (tm, tn), 0)
    g1 = pltpu.roll(g, 1, 0)
    g2 = pltpu.roll(g, 2, 0)
    if tiles_per_seq >= 1:
        first = (i % tiles_per_seq) == 0
        j = pl.program_id(1)
        p_prev = prev_ref[0]
        p_carry = carry_ref[j]
        p2 = jnp.where(first, p_prev[0:1, :], p_carry[6:7, :])
        p1 = jnp.where(first, p_prev[1:2, :], p_carry[7:8, :])
        g1 = jnp.where(row == 0, p1, g1)
        g2 = jnp.where(row == 0, p2, jnp.where(row == 1, p1, g2))
        carry_ref[j] = g[tm - 8:tm, :]
        last_ref[0] = g[tm - 8:tm, :]
    else:
        rmod = row % seq_rows
        g1 = jnp.where(rmod == 0, prev_ref[0], g1)
        g2 = jnp.where(rmod < 2, prev_ref[1], g2)
        last_ref[...] = g
    c = cw_ref[0:1, :] * g2 + cw_ref[1:2, :] * g1 + cw_ref[2:3, :] * g + cb_ref[...]
    act_ref[...] = (c * (1.0 / (1.0 + jnp.exp(-c))) * val).astype(act_ref.dtype)


def ffn_up(x, gx, wg, wv, conv_w, conv_b, prev, *, n_seq, seq_rows):
    M, D = x.shape
    Fd = wg.shape[1]
    tn = _pick(Fd, (512, 256, 128))
    nj = Fd // tn
    if seq_rows >= 256:
        tm = _pick(seq_rows, (1024, 512, 256))
        tps = seq_rows // tm
        prev_arg = prev
        prev_spec = pl.BlockSpec((1, 2, tn), lambda i, j: (i // tps, 0, j))
        last_shape = jax.ShapeDtypeStruct((M // tm, 8, Fd), F32)
        last_spec = pl.BlockSpec((1, 8, tn), lambda i, j: (i, 0, j))
    else:
        tm = M
        tps = 0
        z = jnp.zeros((n_seq, seq_rows - 1, Fd), F32)
        inj1 = jnp.concatenate([prev[:, 1:2], z], axis=1).reshape(M, Fd)
        inj2 = jnp.concatenate([prev[:, 0:2], z[:, 1:]], axis=1).reshape(M, Fd)
        prev_arg = jnp.stack([inj1, inj2])
        prev_spec = pl.BlockSpec((2, tm, tn), lambda i, j: (0, i, j))
        last_shape = jax.ShapeDtypeStruct((M, Fd), F32)
        last_spec = pl.BlockSpec((tm, tn), lambda i, j: (i, j))
    kern = functools.partial(_ffn_up_kernel, tiles_per_seq=tps, seq_rows=seq_rows)
    blk = tm * D * 5 + 2 * D * tn * 2 + tm * tn * 2 + 4 * tn * 4 + 2 * tm * tn * 4
    act, last = pl.pallas_call(
        kern,
        grid=(M // tm, nj),
        in_specs=[pl.BlockSpec((tm, D), lambda i, j: (i, 0)),
                  pl.BlockSpec((1, D), lambda i, j: (0, 0)),
                  pl.BlockSpec((D, tn), lambda i, j: (0, j)),
                  pl.BlockSpec((D, tn), lambda i, j: (0, j)),
                  pl.BlockSpec((CONV_W, tn), lambda i, j: (0, j)),
                  pl.BlockSpec((1, tn), lambda i, j: (0, j)),
                  prev_spec],
        out_specs=[pl.BlockSpec((tm, tn), lambda i, j: (i, j)), last_spec],
        out_shape=[jax.ShapeDtypeStruct((M, Fd), BF16), last_shape],
        scratch_shapes=[pltpu.VMEM((nj, 8, tn), F32), pltpu.VMEM((tm, D), BF16)],
        compiler_params=pltpu.CompilerParams(
            dimension_semantics=("arbitrary", "arbitrary"),
            vmem_limit_bytes=_vmem_limit(blk, 6 * tm * tn * 4)),
        name="ffn_up",
    )(x, gx.reshape(1, D).astype(F32), wg, wv, conv_w.astype(F32), conv_b.reshape(1, Fd).astype(F32), prev_arg)
    if tps == 0:
        last = last.reshape(n_seq, seq_rows, Fd)[:, seq_rows - 2:]
    else:
        last = last[tps - 1::tps, 6:8]
    return act, last


def _attn_kernel(*refs, tq, tk, nq, sk_valid, window, n_prev, causal, masked, qbase, kbase, slots,
                 outs, scale, has_k2, has_bias, has_sinks, sk):
    q_ref, k_ref, v_ref = refs[0], refs[1], refs[2]
    pos = 3
    q2_ref = k2_ref = bias_ref = sinks_ref = None
    if has_k2:
        q2_ref, k2_ref = refs[pos], refs[pos + 1]
        pos += 2
    if has_bias:
        bias_ref = refs[pos]
        pos += 1
    if has_sinks:
        sinks_ref = refs[pos]
        pos += 1
    o_ref = refs[pos]
    m_ref, l_ref, acc_ref = refs[pos + 1], refs[pos + 2], refs[pos + 3]

    g = pl.program_id(1)
    i = pl.program_id(2)
    q0 = i * tq
    ns = len(slots)
    R = ns * tq
    lane = lax.broadcasted_iota(I32, (tq, LANES), 1)

    qs = []
    for (t, mode) in slots:
        qt = q_ref[:, t * LANES:(t + 1) * LANES]
        if mode == "lo":
            qt = jnp.where(lane < 64, qt, 0.0)
        elif mode == "hi":
            qt = jnp.where(lane >= 64, qt, 0.0)
        if has_k2:
            q2 = jnp.where((lane >> 6) == (g % 2), q2_ref[...], 0.0)
            qt = jnp.concatenate([qt, q2], axis=1)
        qs.append(qt.astype(BF16))
    Q = qs[0] if ns == 1 else jnp.concatenate(qs, axis=0)

    qpos1 = qbase + q0 + lax.broadcasted_iota(I32, (tq, 1), 0)
    qc1 = qpos1 >> CHUNK_SHIFT
    qc = qc1 if ns == 1 else jnp.concatenate([qc1] * ns, axis=0)

    if window is not None:
        lo_row = ((qbase + q0) >> CHUNK_SHIFT) * CHUNK - n_prev * CHUNK - kbase
        start = jnp.clip(lo_row, 0, sk - window)
        nb = window // tk
    elif causal:
        start = 0
        hi_row = (((qbase + q0 + tq - 1) >> CHUNK_SHIFT) + 1) * CHUNK - kbase
        nb = (jnp.minimum(hi_row, sk_valid) + tk - 1) // tk
    else:
        start = 0
        nb = sk // tk

    m_ref[...] = jnp.full((R, 1), M_INIT, F32)
    l_ref[...] = jnp.zeros((R, 1), F32)
    acc_ref[...] = jnp.zeros((R, LANES), F32)

    def body(jb, carry):
        r0 = pl.multiple_of(start + jb * tk, 8)
        kb = k_ref[pl.ds(r0, tk), :]
        if has_k2:
            kb = jnp.concatenate([kb, k2_ref[pl.ds(r0, tk), :]], axis=1)
        s = lax.dot_general(Q, kb.astype(BF16), (((1,), (1,)), ((), ())),
                            preferred_element_type=F32) * scale
        if has_bias:
            s = s + bias_ref[0, 0]
        if masked:
            krow = r0 + lax.broadcasted_iota(I32, (1, tk), 1)
            kc = (kbase + krow) >> CHUNK_SHIFT
            s = jnp.where(kc <= qc, s, NEG_INF)
            if n_prev is not None:
                s = jnp.where(kc >= qc - n_prev, s, NEG_INF)
            if sk_valid < sk:
                s = jnp.where(krow < sk_valid, s, NEG_INF)
        m_old = m_ref[...]
        m_new = jnp.maximum(m_old, jnp.max(s, axis=-1, keepdims=True))
        p = jnp.exp(s - m_new)
        alpha = jnp.exp(m_old - m_new)
        l_ref[...] = alpha * l_ref[...] + jnp.sum(p, axis=-1, keepdims=True)
        acc_ref[...] = alpha * acc_ref[...] + jnp.dot(
            p.astype(BF16), v_ref[pl.ds(r0, tk), :].astype(BF16), preferred_element_type=F32)
        m_ref[...] = m_new
        return carry

    lax.fori_loop(0, nb, body, 0)

    m = m_ref[...]
    l = l_ref[...]
    acc = acc_ref[...]
    if has_sinks:
        sk_rows = [jnp.full((tq, 1), sinks_ref[g * ns + s], F32) for s in range(ns)]
        sink = sk_rows[0] if ns == 1 else jnp.concatenate(sk_rows, axis=0)
        m_t = jnp.maximum(m, sink)
        a = jnp.exp(m - m_t)
        l = l * a + jnp.exp(sink - m_t)
        acc = acc * a
    o = acc / l
    for t, spec in enumerate(outs):
        if spec[0] == "slot":
            ot = o[spec[1] * tq:(spec[1] + 1) * tq]
        else:
            ot = jnp.where(lane < 64, o[spec[1] * tq:(spec[1] + 1) * tq],
                           o[spec[2] * tq:(spec[2] + 1) * tq])
        o_ref[:, t * LANES:(t + 1) * LANES] = ot.astype(o_ref.dtype)


def attention(q, k, v, *, B, Sq, Sk, sk_valid, tq, tk, G, qw, slots, outs, scale, window=None,
              n_prev=None, causal=False, masked=True, qbase=0, kbase=0, q2=None, k2=None, bias=None,
              sinks=None, name="attn"):
    nq = Sq // tq
    ns = len(slots)
    R = ns * tq
    ow = len(outs)
    in_specs = [pl.BlockSpec((tq, qw * LANES), lambda b, g, i: (b * nq + i, g)),
                pl.BlockSpec((Sk, LANES), lambda b, g, i: (b, g)),
                pl.BlockSpec((Sk, LANES), lambda b, g, i: (b, g))]
    args = [q, k, v]
    blk = tq * qw * LANES * 4 + 2 * Sk * LANES * 4 + tq * ow * LANES * 2
    if q2 is not None:
        in_specs += [pl.BlockSpec((tq, LANES), lambda b, g, i: (b * nq + i, g // 2)),
                     pl.BlockSpec((Sk, LANES), lambda b, g, i: (b, 0))]
        args += [q2, k2]
        blk += tq * LANES * 4 + Sk * LANES * 4
    if bias is not None:
        nv = bias.shape[0]
        assert window == tk
        in_specs.append(pl.BlockSpec((1, 1, tq, window),
                                     lambda b, g, i, nv=nv: (jnp.minimum(i, nv - 1), g, 0, 0)))
        args.append(bias)
        blk += tq * window * 4
    if sinks is not None:
        in_specs.append(pl.BlockSpec(memory_space=pltpu.SMEM))
        args.append(sinks.astype(F32))
    kern = functools.partial(
        _attn_kernel, tq=tq, tk=tk, nq=nq, sk_valid=sk_valid, window=window, n_prev=n_prev,
        causal=causal, masked=masked, qbase=qbase, kbase=kbase, slots=tuple(slots), outs=tuple(outs),
        scale=scale, has_k2=q2 is not None, has_bias=bias is not None, has_sinks=sinks is not None,
        sk=Sk)
    return pl.pallas_call(
        kern,
        grid=(B, G, nq),
        in_specs=in_specs,
        out_specs=pl.BlockSpec((tq, ow * LANES), lambda b, g, i: (b * nq + i, g)),
        out_shape=jax.ShapeDtypeStruct((B * Sq, G * ow * LANES), BF16),
        scratch_shapes=[pltpu.VMEM((R, 1), F32), pltpu.VMEM((R, 1), F32), pltpu.VMEM((R, LANES), F32)],
        compiler_params=pltpu.CompilerParams(
            dimension_semantics=("parallel", "parallel", "arbitrary"),
            vmem_limit_bytes=_vmem_limit(blk, 6 * R * tk * 4)),
        name=name,
    )(*args)


def _unit(slots, k_tile, outs, q2=None, bias=None, sinks=()):
    return (tuple(slots), k_tile, tuple(outs), q2, bias, tuple(sinks))


def _attn2_kernel(*refs, tq, nq, units, mode, win, sk, sk_valid, n_prev, masked, qbase, kbase, scale,
                  has_q2, has_bias, has_sinks, sinks_per_group, bias_is_mask):
    q_ref, k_ref, v_ref = refs[0], refs[1], refs[2]
    pos = 3
    q2_ref = k2_ref = bias_ref = sinks_ref = None
    if has_q2:
        q2_ref, k2_ref = refs[pos], refs[pos + 1]
        pos += 2
    if has_bias:
        bias_ref = refs[pos]
        pos += 1
    if has_sinks:
        sinks_ref = refs[pos]
        pos += 1
    o_ref = refs[pos]

    g = pl.program_id(1)
    i = pl.program_id(2)
    q0 = i * tq
    lane = lax.broadcasted_iota(I32, (tq, LANES), 1)
    half_mask = {"lo": jnp.where(lane < 64, 1.0, 0.0).astype(BF16),
                 "hi": jnp.where(lane >= 64, 1.0, 0.0).astype(BF16)}
    qc1 = (qbase + q0 + lax.broadcasted_iota(I32, (tq, 1), 0)) >> CHUNK_SHIFT

    def compute(r0, L, mask_from=0):
        if masked:
            krow = r0 + mask_from + lax.broadcasted_iota(I32, (1, L - mask_from), 1)
            kc = (kbase + krow) >> CHUNK_SHIFT
        rows = pl.ds(r0, L) if not isinstance(r0, int) else slice(r0, r0 + L)
        for u, (slots, k_tile, outs, q2, bias_idx, sink_idx) in enumerate(units):
            ns = len(slots)
            qs = []
            for (t, mk) in slots:
                qt = q_ref[:, t * LANES:(t + 1) * LANES]
                if mk is not None:
                    qt = qt * half_mask[mk]
                if q2 is not None:
                    q2t = q2_ref[:, q2[0] * LANES:(q2[0] + 1) * LANES] * half_mask[q2[1]]
                    qt = jnp.concatenate([qt, q2t], axis=1)
                qs.append(qt)
            Q = qs[0] if ns == 1 else jnp.concatenate(qs, axis=0)
            kb = k_ref[rows, k_tile * LANES:(k_tile + 1) * LANES]
            if q2 is not None:
                kb = jnp.concatenate([kb, k2_ref[rows, :]], axis=1)
            vb = v_ref[rows, k_tile * LANES:(k_tile + 1) * LANES]
            log2 = bias_idx is None or bias_is_mask
            expf = jnp.exp2 if log2 else jnp.exp
            s = lax.dot_general(Q, kb, (((1,), (1,)), ((), ())), preferred_element_type=F32) * (
                scale * LOG2E if log2 else scale)
            if bias_idx is not None:
                bt = bias_ref[0, bias_idx]
                s = s + (bt if ns == 1 else jnp.concatenate([bt] * ns, axis=0))
            if masked:
                qc = qc1 if ns == 1 else jnp.concatenate([qc1] * ns, axis=0)
                sm = s[:, mask_from:]
                sm = jnp.where(kc <= qc, sm, NEG_INF)
                if n_prev is not None:
                    sm = jnp.where(kc >= qc - n_prev, sm, NEG_INF)
                if sk_valid < sk:
                    sm = jnp.where(krow < sk_valid, sm, NEG_INF)
                s = sm if mask_from == 0 else jnp.concatenate([s[:, :mask_from], sm], axis=1)
            m = jnp.max(s, axis=-1, keepdims=True)
            if sink_idx:
                sk_rows = [jnp.full((tq, 1), sinks_ref[g * sinks_per_group + si], F32) for si in sink_idx]
                sink = sk_rows[0] if ns == 1 else jnp.concatenate(sk_rows, axis=0)
                sink = sink * LOG2E if log2 else sink
                m = jnp.maximum(m, sink)
            p = expf(s - m)
            l = jnp.sum(p, axis=-1, keepdims=True)
            if sink_idx:
                l = l + expf(sink - m)
            o = jnp.dot(p.astype(BF16), vb, preferred_element_type=F32) / l
            for spec in outs:
                t = spec[0]
                if spec[1] == "slot":
                    ot = o[spec[2] * tq:(spec[2] + 1) * tq]
                else:
                    ot = jnp.where(lane < 64, o[spec[2] * tq:(spec[2] + 1) * tq],
                                   o[spec[3] * tq:(spec[3] + 1) * tq])
                o_ref[:, t * LANES:(t + 1) * LANES] = ot.astype(o_ref.dtype)

    if mode == "window":
        lo_row = ((qbase + q0) >> CHUNK_SHIFT) * CHUNK - n_prev * CHUNK - kbase
        compute(pl.multiple_of(jnp.clip(lo_row, 0, sk - win), 16), win)
    elif mode == "full":
        compute(0, sk)
    else:
        for c in range(nq):
            pl.when(i == c)(functools.partial(compute, 0, min(sk, (c + 1) * tq), c * tq))


def attention2(q, k, v, *, B, Sq, Sk, sk_valid, tq, G, qw, kw, ow, units, scale, mode, win=None,
               n_prev=None, masked=True, qbase=0, kbase=0, q2=None, q2w=0, k2=None, bias=None,
               bias_is_mask=False, sinks=None, sinks_per_group=0, name="attn"):
    nq = Sq // tq
    assert mode != "causal" or (qbase == 0 and kbase == 0 and tq % CHUNK == 0)
    in_specs = [pl.BlockSpec((tq, qw * LANES), lambda b, g, i: (b * nq + i, g)),
                pl.BlockSpec((Sk, kw * LANES), lambda b, g, i: (b, g)),
                pl.BlockSpec((Sk, kw * LANES), lambda b, g, i: (b, g))]
    args = [q, k, v]
    blk = tq * qw * LANES * 2 + 2 * Sk * kw * LANES * 2 + tq * ow * LANES * 2
    if q2 is not None:
        in_specs += [pl.BlockSpec((tq, q2w * LANES), lambda b, g, i: (b * nq + i, g)),
                     pl.BlockSpec((Sk, LANES), lambda b, g, i: (b, 0))]
        args += [q2, k2]
        blk += tq * q2w * LANES * 2 + Sk * LANES * 2
    if bias is not None:
        nv, nbh = bias.shape[0], bias.shape[1] // G
        in_specs.append(pl.BlockSpec((1, nbh, tq, win),
                                     lambda b, g, i, nv=nv: (jnp.minimum(i, nv - 1), g, 0, 0)))
        args.append(bias)
        blk += nbh * tq * win * 4
    if sinks is not None:
        in_specs.append(pl.BlockSpec(memory_space=pltpu.SMEM))
        args.append(sinks.astype(F32))
    L = win if mode == "window" else Sk
    temp = sum(len(u[0]) for u in units) * tq * L * 12
    kern = functools.partial(
        _attn2_kernel, tq=tq, nq=nq, units=tuple(units), mode=mode, win=win, sk=Sk, sk_valid=sk_valid,
        n_prev=n_prev, masked=masked, qbase=qbase, kbase=kbase, scale=scale, has_q2=q2 is not None,
        has_bias=bias is not None, has_sinks=sinks is not None, sinks_per_group=sinks_per_group,
        bias_is_mask=bias_is_mask)
    return pl.pallas_call(
        kern,
        grid=(B, G, nq),
        in_specs=in_specs,
        out_specs=pl.BlockSpec((tq, ow * LANES), lambda b, g, i: (b * nq + i, g)),
        out_shape=jax.ShapeDtypeStruct((B * Sq, G * ow * LANES), BF16),
        compiler_params=pltpu.CompilerParams(
            dimension_semantics=("parallel", "parallel", "arbitrary"),
            vmem_limit_bytes=_vmem_limit(blk, temp)),
        name=name,
    )(*args)


def _toeplitz_kernel(ab_ref, e_ref, o_ref, *, tq, win, n_prev, n_valid, has_table):
    v = pl.program_id(0)
    if has_table:
        wp = e_ref.shape[-1]
        x = jnp.broadcast_to(e_ref[0, 0, 0:1, :], (tq, wp))
        y = pltpu.roll(x, 0, 1, stride=1, stride_axis=0)[:, :win]
    else:
        y = jnp.zeros((tq, win), F32)
    qc = (ab_ref[v, 0] + lax.broadcasted_iota(I32, (tq, win), 0)) >> CHUNK_SHIFT
    kj = lax.broadcasted_iota(I32, (tq, win), 1)
    kc = (ab_ref[v, 1] + kj) >> CHUNK_SHIFT
    y = jnp.where(kc <= qc, y, NEG_INF)
    y = jnp.where(kc >= qc - n_prev, y, NEG_INF)
    if n_valid < win:
        y = jnp.where(kj < n_valid, y, NEG_INF)
    o_ref[0, 0] = y


def band_bias(table, tq, win, pos, n_prev, n_valid):
    nv = len(pos)
    ab = np.array([[q - (k // CHUNK) * CHUNK, k - (k // CHUNK) * CHUNK] for q, k in pos], np.int32)
    has_table = table is not None
    H = table.shape[1] if has_table else 1
    wp = _round_up(win + tq - 1, LANES)
    if has_table:
        m = np.arange(wp)
        m = np.where(m < win, m, m - wp)
        idx = np.stack([np.clip(q - k - m, -MAX_REL, MAX_REL) + MAX_REL for q, k in pos])
        e = jnp.transpose(table.astype(F32)[idx], (0, 2, 1))
        e = jnp.broadcast_to(e[:, :, None, :], (nv, H, 8, wp))
    else:
        e = jnp.zeros((nv, H, 8, wp), F32)
    return pl.pallas_call(
        functools.partial(_toeplitz_kernel, tq=tq, win=win, n_prev=n_prev, n_valid=n_valid,
                          has_table=has_table),
        grid=(nv, H),
        in_specs=[pl.BlockSpec(memory_space=pltpu.SMEM),
                  pl.BlockSpec((1, 1, 8, wp), lambda v, h: (v, h, 0, 0))],
        out_specs=pl.BlockSpec((1, 1, tq, win), lambda v, h: (v, h, 0, 0)),
        out_shape=jax.ShapeDtypeStruct((nv, H, tq, win), F32),
        compiler_params=pltpu.CompilerParams(dimension_semantics=("parallel", "parallel")),
        name="band_bias",
    )(jnp.asarray(ab), e)


def _dsa_kernel(q_ref, k_ref, v_ref, iq_ref, ik_ref, iw_ref, o_ref, key_ref, mb_ref, m_ref, l_ref,
                acc_ref, *, tq, tk, sk, sk_valid, qbase, n_sel, scale, idx_scale):
    i = pl.program_id(1)
    q0 = i * tq
    qpos = qbase + q0 + lax.broadcasted_iota(I32, (tq, 1), 0)
    qc = qpos >> CHUNK_SHIFT
    hi_row = (((qbase + q0 + tq - 1) >> CHUNK_SHIFT) + 1) * CHUNK
    nb = (jnp.minimum(hi_row, sk_valid) + tk - 1) // tk
    lane = lax.broadcasted_iota(I32, (tq, LANES), 1)
    n_idx_tiles = iq_ref.shape[1] // LANES

    def col_of(jb):
        return jb * tk + lax.broadcasted_iota(I32, (1, tk), 1)

    def mask_invisible(x, col):
        x = jnp.where((col >> CHUNK_SHIFT) <= qc, x, NEG_INF)
        if sk_valid < sk:
            x = jnp.where(col < sk_valid, x, NEG_INF)
        return x

    def score_body(jb, carry):
        r0 = pl.multiple_of(jb * tk, 8)
        ikb = ik_ref[pl.ds(r0, tk), :].astype(BF16)
        acc = jnp.zeros((tq, tk), F32)
        for p in range(n_idx_tiles):
            t = iq_ref[:, p * LANES:(p + 1) * LANES]
            for hf in range(2):
                qm = jnp.where((lane < 64) if hf == 0 else (lane >= 64), t, 0.0).astype(BF16)
                lg = lax.dot_general(qm, ikb, (((1,), (1,)), ((), ())),
                                     preferred_element_type=F32) * idx_scale
                h = 2 * p + hf
                acc = acc + iw_ref[:, h:h + 1] * jnp.maximum(lg, 0.0)
        sc = mask_invisible(acc, col_of(jb))
        bits = lax.bitcast_convert_type(sc, I32)
        key_ref[jb] = bits ^ ((bits >> 31) & 0x7FFFFFFF)
        return carry

    lax.fori_loop(0, nb, score_body, 0)

    def count(pred):
        def cb(jb, c):
            return c + jnp.sum(jnp.where(pred(key_ref[jb], jb), 1.0, 0.0), axis=1, keepdims=True)
        return lax.fori_loop(0, nb, cb, jnp.zeros((tq, 1), F32))

    def count_eq_before(jtrial):
        def cb(jb, c):
            hit = jnp.where(key_ref[jb] == thr, jnp.where(col_of(jb) < jtrial, 1.0, 0.0), 0.0)
            return c + jnp.sum(hit, axis=1, keepdims=True)
        return lax.fori_loop(0, nb, cb, jnp.zeros((tq, 1), F32))

    def bit_body(it, cand):
        trial = cand | (jnp.int32(1) << (31 - it))
        trial_s = trial ^ INT_MIN
        cnt = count(lambda kk, jb: kk >= trial_s)
        return jnp.where(cnt >= n_sel, trial, cand)

    cand = lax.fori_loop(0, 32, bit_body, jnp.zeros((tq, 1), I32))
    thr = cand ^ INT_MIN

    cnt_gt = count(lambda kk, jb: kk > thr)
    n_eq = count(lambda kk, jb: kk == thr)
    need = n_sel - cnt_gt
    neg_bits = int(np.float32(NEG_INF).view(np.int32))
    neg_key = neg_bits ^ ((neg_bits >> 31) & 0x7FFFFFFF)
    tie = jnp.where(thr > neg_key, jnp.where(n_eq > need, 1.0, 0.0), 0.0)
    any_tie = jnp.max(tie) > 0.0

    def tie_cols():
        def jbit(it, cj):
            trial = cj | (jnp.int32(1) << (30 - it))
            c = count_eq_before(trial)
            return jnp.where(c < need, trial, cj)
        return lax.fori_loop(0, 31, jbit, jnp.zeros((tq, 1), I32))

    jmax = lax.cond(any_tie, tie_cols, lambda: jnp.full((tq, 1), BIG_COL, I32))

    def mask_body(jb, carry):
        kk = key_ref[jb]
        col = col_of(jb)
        sel = jnp.where(kk > thr, 0.0, jnp.where(kk == thr, jnp.where(col <= jmax, 0.0, NEG_INF), NEG_INF))
        mb_ref[jb] = mask_invisible(sel, col)
        return carry

    lax.fori_loop(0, nb, mask_body, 0)

    n_kv = k_ref.shape[1] // LANES
    grp = (q_ref.shape[1] // LANES) // n_kv
    R = grp * tq
    for kvh in range(n_kv):
        Q = jnp.concatenate(
            [q_ref[:, (kvh * grp + gi) * LANES:(kvh * grp + gi + 1) * LANES] for gi in range(grp)],
            axis=0).astype(BF16)
        m_ref[...] = jnp.full((R, 1), M_INIT, F32)
        l_ref[...] = jnp.zeros((R, 1), F32)
        acc_ref[...] = jnp.zeros((R, LANES), F32)

        def body(jb, carry, kvh=kvh, Q=Q):
            r0 = pl.multiple_of(jb * tk, 8)
            kb = k_ref[pl.ds(r0, tk), kvh * LANES:(kvh + 1) * LANES].astype(BF16)
            vb = v_ref[pl.ds(r0, tk), kvh * LANES:(kvh + 1) * LANES].astype(BF16)
            s = lax.dot_general(Q, kb, (((1,), (1,)), ((), ())), preferred_element_type=F32) * scale
            mb = mb_ref[jb]
            s = jnp.where(jnp.concatenate([mb] * grp, axis=0) < 0.0, NEG_INF, s)
            m_old = m_ref[...]
            m_new = jnp.maximum(m_old, jnp.max(s, axis=-1, keepdims=True))
            p = jnp.exp(s - m_new)
            alpha = jnp.exp(m_old - m_new)
            l_ref[...] = alpha * l_ref[...] + jnp.sum(p, axis=-1, keepdims=True)
            acc_ref[...] = alpha * acc_ref[...] + jnp.dot(p.astype(BF16), vb,
                                                          preferred_element_type=F32)
            m_ref[...] = m_new
            return carry

        lax.fori_loop(0, nb, body, 0)
        o = acc_ref[...] / l_ref[...]
        for gi in range(grp):
            hh = kvh * grp + gi
            o_ref[:, hh * LANES:(hh + 1) * LANES] = o[gi * tq:(gi + 1) * tq].astype(o_ref.dtype)


def dsa_attention(q, k, v, iq, ik2, iw, *, B, Sq, Sk, sk_valid, tq, tk, qbase, n_sel):
    nq = Sq // tq
    nkb = Sk // tk
    Hq = q.shape[1]
    Hk = k.shape[1]
    grp = Hq // Hk
    R = grp * tq
    kern = functools.partial(_dsa_kernel, tq=tq, tk=tk, sk=Sk, sk_valid=sk_valid, qbase=qbase, n_sel=n_sel,
                             scale=DSA_HEAD_DIM ** -0.5, idx_scale=IDX_DIM ** -0.5)
    blk = tq * (Hq + iq.shape[1] + LANES) * 4 + Sk * (2 * Hk + LANES) * 4 + tq * Hq * 2
    return pl.pallas_call(
        kern,
        grid=(B, nq),
        in_specs=[pl.BlockSpec((tq, Hq), lambda b, i: (b * nq + i, 0)),
                  pl.BlockSpec((Sk, Hk), lambda b, i: (b, 0)),
                  pl.BlockSpec((Sk, Hk), lambda b, i: (b, 0)),
                  pl.BlockSpec((tq, iq.shape[1]), lambda b, i: (b * nq + i, 0)),
                  pl.BlockSpec((Sk, LANES), lambda b, i: (b, 0)),
                  pl.BlockSpec((tq, LANES), lambda b, i: (b * nq + i, 0))],
        out_specs=pl.BlockSpec((tq, Hq), lambda b, i: (b * nq + i, 0)),
        out_shape=jax.ShapeDtypeStruct((B * Sq, Hq), BF16),
        scratch_shapes=[pltpu.VMEM((nkb, tq, tk), I32), pltpu.VMEM((nkb, tq, tk), F32),
                        pltpu.VMEM((R, 1), F32), pltpu.VMEM((R, 1), F32), pltpu.VMEM((R, LANES), F32)],
        compiler_params=pltpu.CompilerParams(
            dimension_semantics=("parallel", "arbitrary"),
            vmem_limit_bytes=_vmem_limit(blk, 2 * tq * Sk * 4 + 6 * R * tk * 4)),
        name="dsa_attention",
    )(q, k, v, iq, ik2, iw)


def _dsa2_kernel(q_ref, k_ref, v_ref, iq_ref, ik_ref, iw_ref, o_ref, key_ref, mb_ref, m_ref, l_ref,
                 acc_ref, alpha_ref, s_ref, p_ref, *, tq, tk, sk, sk_valid, qbase, n_sel, scale):
    i = pl.program_id(1)
    q0 = i * tq
    qc = (qbase + q0 + lax.broadcasted_iota(I32, (tq, 1), 0)) >> CHUNK_SHIFT
    hi_row = (((qbase + q0 + tq - 1) >> CHUNK_SHIFT) + 1) * CHUNK
    nb = (jnp.minimum(hi_row, sk_valid) + tk - 1) // tk
    lane = lax.broadcasted_iota(I32, (tq, LANES), 1)
    half_mask = (jnp.where(lane < 64, 1.0, 0.0).astype(BF16), jnp.where(lane >= 64, 1.0, 0.0).astype(BF16))
    n_idx_tiles = iq_ref.shape[1] // LANES
    n_lane_tiles = tk // LANES

    def col_of(jb):
        return jb * tk + lax.broadcasted_iota(I32, (1, tk), 1)

    def mask_invisible(x, col):
        x = jnp.where((col >> CHUNK_SHIFT) <= qc, x, NEG_INF)
        if sk_valid < sk:
            x = jnp.where(col < sk_valid, x, NEG_INF)
        return x

    def lane_fold(x):
        acc = x[:, 0:LANES]
        for t in range(1, n_lane_tiles):
            acc = acc + x[:, t * LANES:(t + 1) * LANES]
        return acc

    def score_body(jb, carry):
        r0 = pl.multiple_of(jb * tk, 16)
        ikb = ik_ref[pl.ds(r0, tk), :]
        acc = jnp.zeros((tq, tk), F32)
        for p in range(n_idx_tiles):
            t = iq_ref[:, p * LANES:(p + 1) * LANES]
            for hf in range(2):
                lg = lax.dot_general(t * half_mask[hf], ikb, (((1,), (1,)), ((), ())),
                                     preferred_element_type=F32)
                h = 2 * p + hf
                acc = acc + iw_ref[:, h:h + 1] * jnp.maximum(lg, 0.0)
        sc = mask_invisible(acc, col_of(jb))
        bits = lax.bitcast_convert_type(sc, I32)
        key_ref[jb] = bits ^ ((bits >> 31) & 0x7FFFFFFF)
        return carry

    lax.fori_loop(0, nb, score_body, 0)

    def count(hit):
        def cb(jb, c):
            return c + lane_fold(hit(key_ref[jb], jb))
        part = lax.fori_loop(0, nb, cb, jnp.zeros((tq, LANES), F32))
        return jnp.sum(part, axis=1, keepdims=True)

    def bit_body(it, cand):
        trial = cand | (jnp.int32(1) << (31 - it))
        trial_s = trial ^ INT_MIN
        cnt = count(lambda kk, jb: jnp.where(kk >= trial_s, 1.0, 0.0))
        return jnp.where(cnt >= n_sel, trial, cand)

    cand = lax.fori_loop(0, 32, bit_body, jnp.zeros((tq, 1), I32))
    thr = cand ^ INT_MIN

    cnt_gt = count(lambda kk, jb: jnp.where(kk > thr, 1.0, 0.0))
    n_eq = count(lambda kk, jb: jnp.where(kk == thr, 1.0, 0.0))
    need = n_sel - cnt_gt
    neg_bits = int(np.float32(NEG_INF).view(np.int32))
    neg_key = neg_bits ^ ((neg_bits >> 31) & 0x7FFFFFFF)
    tie = jnp.where(thr > neg_key, jnp.where(n_eq > need, 1.0, 0.0), 0.0)
    any_tie = jnp.max(tie) > 0.0

    def tie_cols():
        def jbit(it, cj):
            trial = cj | (jnp.int32(1) << (30 - it))
            c = count(lambda kk, jb: jnp.where(kk == thr, jnp.where(col_of(jb) < trial, 1.0, 0.0), 0.0))
            return jnp.where(c < need, trial, cj)
        return lax.fori_loop(0, 31, jbit, jnp.zeros((tq, 1), I32))

    jmax = lax.cond(any_tie, tie_cols, lambda: jnp.full((tq, 1), BIG_COL, I32))

    def mask_body(jb, carry):
        kk = key_ref[jb]
        col = col_of(jb)
        sel = jnp.where(kk > thr, 0.0, jnp.where(kk == thr, jnp.where(col <= jmax, 0.0, NEG_INF), NEG_INF))
        mb_ref[jb] = mask_invisible(sel, col)
        return carry

    lax.fori_loop(0, nb, mask_body, 0)

    n_kv = k_ref.shape[1] // LANES
    grp = (q_ref.shape[1] // LANES) // n_kv
    R = grp * tq
    m_ref[...] = jnp.full(m_ref.shape, M_INIT, F32)
    l_ref[...] = jnp.zeros(l_ref.shape, F32)
    acc_ref[...] = jnp.zeros(acc_ref.shape, F32)

    ch = min(SOFTMAX_ROWS, tq)
    n_ch = R // ch

    def body(jb, carry):
        r0 = pl.multiple_of(jb * tk, 16)
        for kvh in range(n_kv):
            Q = jnp.concatenate(
                [q_ref[:, (kvh * grp + gi) * LANES:(kvh * grp + gi + 1) * LANES] for gi in range(grp)], axis=0)
            kb = k_ref[pl.ds(r0, tk), kvh * LANES:(kvh + 1) * LANES]
            vb = v_ref[pl.ds(r0, tk), kvh * LANES:(kvh + 1) * LANES]
            s_ref[kvh] = lax.dot_general(Q, kb, (((1,), (1,)), ((), ())), preferred_element_type=F32)

            def rows_body(c, carry2, kvh=kvh):
                rr = pl.multiple_of(c * ch, ch)
                qr = pl.multiple_of((c * ch) % tq, ch)
                s = s_ref[kvh, pl.ds(rr, ch), :] * (scale * LOG2E) + mb_ref[jb, pl.ds(qr, ch), :]
                m_old = m_ref[kvh, pl.ds(rr, ch), :]
                m_new = jnp.maximum(m_old, jnp.max(s, axis=-1, keepdims=True))
                p = jnp.exp2(s - jnp.concatenate([m_new] * n_lane_tiles, axis=1))
                alpha = jnp.exp2(m_old - m_new)
                l_ref[kvh, pl.ds(rr, ch), :] = alpha * l_ref[kvh, pl.ds(rr, ch), :] + jnp.sum(
                    p, axis=-1, keepdims=True)
                m_ref[kvh, pl.ds(rr, ch), :] = m_new
                alpha_ref[kvh, pl.ds(rr, ch), :] = alpha
                p_ref[kvh, pl.ds(rr, ch), :] = p.astype(BF16)
                return carry2

            lax.fori_loop(0, n_ch, rows_body, 0, unroll=n_ch)
            acc_ref[kvh] = alpha_ref[kvh] * acc_ref[kvh] + jnp.dot(p_ref[kvh], vb,
                                                                   preferred_element_type=F32)
        return carry

    lax.fori_loop(0, nb, body, 0)
    for kvh in range(n_kv):
        o = acc_ref[kvh] / l_ref[kvh]
        for gi in range(grp):
            hh = kvh * grp + gi
            o_ref[:, hh * LANES:(hh + 1) * LANES] = o[gi * tq:(gi + 1) * tq].astype(o_ref.dtype)


def dsa_attention2(q, k, v, iq, ik2, iw, *, B, Sq, Sk, sk_valid, tq, tk, qbase, n_sel):
    nq = Sq // tq
    nkb = Sk // tk
    Hq = q.shape[1]
    Hk = k.shape[1]
    n_kv = Hk // LANES
    R = (Hq // Hk) * tq
    kern = functools.partial(_dsa2_kernel, tq=tq, tk=tk, sk=Sk, sk_valid=sk_valid, qbase=qbase,
                             n_sel=n_sel, scale=DSA_HEAD_DIM ** -0.5)
    blk = tq * (Hq + iq.shape[1]) * 2 + tq * LANES * 4 + Sk * (2 * Hk + LANES) * 2 + tq * Hq * 2
    scratch = 2 * nkb * tq * tk * 4 + 3 * n_kv * R * LANES * 4
    return pl.pallas_call(
        kern,
        grid=(B, nq),
        in_specs=[pl.BlockSpec((tq, Hq), lambda b, i: (b * nq + i, 0)),
                  pl.BlockSpec((Sk, Hk), lambda b, i: (b, 0)),
                  pl.BlockSpec((Sk, Hk), lambda b, i: (b, 0)),
                  pl.BlockSpec((tq, iq.shape[1]), lambda b, i: (b * nq + i, 0)),
                  pl.BlockSpec((Sk, LANES), lambda b, i: (b, 0)),
                  pl.BlockSpec((tq, LANES), lambda b, i: (b * nq + i, 0))],
        out_specs=pl.BlockSpec((tq, Hq), lambda b, i: (b * nq + i, 0)),
        out_shape=jax.ShapeDtypeStruct((B * Sq, Hq), BF16),
        scratch_shapes=[pltpu.VMEM((nkb, tq, tk), I32), pltpu.VMEM((nkb, tq, tk), F32),
                        pltpu.VMEM((n_kv, R, LANES), F32), pltpu.VMEM((n_kv, R, LANES), F32),
                        pltpu.VMEM((n_kv, R, LANES), F32), pltpu.VMEM((n_kv, R, LANES), F32),
                        pltpu.VMEM((n_kv, R, tk), F32), pltpu.VMEM((n_kv, R, tk), BF16)],
        compiler_params=pltpu.CompilerParams(
            dimension_semantics=("parallel", "arbitrary"),
            vmem_limit_bytes=_vmem_limit(blk, scratch + n_kv * R * (LANES * 4 + tk * 6) + (8 << 20))),
        name="dsa_attention",
    )(q, k, v, iq, ik2, iw)


def _pack_cache_kernel(*refs, rows, n_new, n_heads):
    cache_ref, o_ref = refs[0], refs[-1]
    dh = cache_ref.shape[3]
    for h in range(n_heads):
        o_ref[0:rows, h * dh:(h + 1) * dh] = cache_ref[0, :, h, :].astype(o_ref.dtype)
    if n_new:
        o_ref[rows:rows + n_new, :] = refs[1][...].astype(o_ref.dtype)
    n_pad = o_ref.shape[0] - rows - n_new
    if n_pad:
        o_ref[rows + n_new:, :] = jnp.zeros((n_pad, o_ref.shape[1]), o_ref.dtype)


def pack_cache(cache, new, rows_pad):
    B, rows, H, dh = cache.shape
    n_new = 0 if new is None else new.shape[0] // B
    in_specs = [pl.BlockSpec((1, rows, H, dh), lambda b: (b, 0, 0, 0))]
    args = [cache]
    if n_new:
        in_specs.append(pl.BlockSpec((n_new, H * dh), lambda b: (b, 0)))
        args.append(new)
    return pl.pallas_call(
        functools.partial(_pack_cache_kernel, rows=rows, n_new=n_new, n_heads=H),
        grid=(B,),
        in_specs=in_specs,
        out_specs=pl.BlockSpec((rows_pad, H * dh), lambda b: (b, 0)),
        out_shape=jax.ShapeDtypeStruct((B * rows_pad, H * dh), BF16),
        compiler_params=pltpu.CompilerParams(
            dimension_semantics=("parallel",),
            vmem_limit_bytes=_vmem_limit(rows * H * dh * 4 + rows_pad * H * dh * 2, rows * H * dh * 4)),
        name="pack_cache",
    )(*args)


def _rope_tables(pos, rot, group):
    width = 2 * LANES
    half = rot // 2
    inv = ROPE_THETA ** (-jnp.arange(half, dtype=F32) / half)
    ang = pos.astype(F32)[:, None] * inv[None, :]
    c, s = jnp.cos(ang), jnp.sin(ang)
    lane = np.arange(width) % group
    is_lo = lane < half
    is_hi = (lane >= half) & (lane < rot)
    idx = np.where(is_lo, lane, np.where(is_hi, lane - half, 0))
    cg, sg = c[:, idx], s[:, idx]
    C = jnp.where(is_lo | is_hi, cg, 1.0)
    S = jnp.where(is_lo, -sg, jnp.where(is_hi, sg, 0.0))
    perm = np.zeros((width, width), np.float32)
    dst = np.arange(width)
    src = np.where(is_lo, dst + half, dst - half)
    perm[src[is_lo | is_hi], dst[is_lo | is_hi]] = 1.0
    return half, C, S, jnp.asarray(perm, BF16)


def _pad_cols(w, n):
    return jnp.pad(w, ((0, 0), (0, n - w.shape[1])))


def _tile_gain(g, n):
    return jnp.tile(g.astype(F32), n // g.shape[0])


def _band_bias(table, tq, window, offs):
    qi = np.arange(tq)[:, None]
    kj = np.arange(window)[None, :]
    idx = np.stack([np.clip(qi - kj + off, -MAX_REL, MAX_REL) + MAX_REL for off in offs])
    return jnp.transpose(table.astype(F32)[idx], (0, 3, 1, 2))


def _pad_rows(x, B, rows, rows_pad):
    if rows == rows_pad:
        return x
    x = x.reshape(B, rows, -1)
    x = jnp.pad(x, ((0, 0), (0, rows_pad - rows), (0, 0)))
    return x.reshape(B * rows_pad, -1)


def _with_cache(cache, new, B, rows_pad, dtype=F32):
    n = new.shape[0] // B
    P = cache.shape[1]
    parts = [cache.reshape(B, P, -1).astype(dtype), new.reshape(B, n, -1).astype(dtype)]
    if rows_pad > P + n:
        parts.append(jnp.zeros((B, rows_pad - P - n, new.shape[1]), dtype))
    return jnp.concatenate(parts, axis=1).reshape(B * rows_pad, -1)


def _round_up(x, m):
    return (x + m - 1) // m * m


def _window_variants(tq, n_prev):
    return [(t * tq, 0) for t in range(n_prev * CHUNK // tq)] + [(n_prev * CHUNK, 0)]


_SLOT1 = ((0, None),)
_OUT1 = (("slot", 0),)


def _mla_weights(w_in, q_lora_norm, kv_lora_norm, w_uq, w_ukv, q_norm, k_norm):
    H = MLA_HEADS
    wq3 = w_uq.reshape(MLA_Q_LORA, H, MLA_NOPE + MLA_ROPE)
    wkv3 = w_ukv.reshape(MLA_KV_LORA, H, MLA_NOPE + MLA_V)
    return dict(
        w_cq=w_in[:, :MLA_Q_LORA].astype(BF16),
        w_ckv=w_in[:, MLA_Q_LORA:MLA_Q_LORA + MLA_KV_LORA].astype(BF16),
        w_kr=_pad_cols(w_in[:, MLA_Q_LORA + MLA_KV_LORA:], LANES).astype(BF16),
        g_cq=q_lora_norm, g_ckv=kv_lora_norm,
        g_kr=_tile_gain(k_norm[MLA_NOPE:], LANES),
        w_qn=wq3[:, :, :MLA_NOPE].reshape(MLA_Q_LORA, H * MLA_NOPE).astype(BF16),
        w_qr=wq3[:, :, MLA_NOPE:].reshape(MLA_Q_LORA, H * MLA_ROPE).astype(BF16),
        g_qn=_tile_gain(q_norm[:MLA_NOPE], H * MLA_NOPE),
        g_qr=_tile_gain(q_norm[MLA_NOPE:], H * MLA_ROPE),
        w_kn=wkv3[:, :, :MLA_NOPE].reshape(MLA_KV_LORA, H * MLA_NOPE).astype(BF16),
        w_v=wkv3[:, :, MLA_NOPE:].reshape(MLA_KV_LORA, H * MLA_V).astype(BF16),
        g_kn=_tile_gain(k_norm[:MLA_NOPE], H * MLA_NOPE),
    )


def _mla_project(h, W, rope64):
    cq = mm(h, W["w_cq"], gain=W["g_cq"], gs=MLA_Q_LORA, norm=True, out_dtype=BF16, name="mla_cq")
    ckv, ckv_bf = mm(h, W["w_ckv"], gain=W["g_ckv"], gs=MLA_KV_LORA, norm=True, out_dtype=(F32, BF16),
                     name="mla_ckv")
    kr = mm(h, W["w_kr"], gain=W["g_kr"], gs=64, norm=True, rope=rope64, name="mla_kr")
    qn = mm(cq, W["w_qn"], gain=W["g_qn"], gs=LANES, norm=True, out_dtype=BF16, name="mla_qn")
    qr = mm(cq, W["w_qr"], gain=W["g_qr"], gs=64, norm=True, rope=rope64, out_dtype=BF16, name="mla_qr")
    return qn, qr, ckv, ckv_bf, kr


def _mla_expand(ckv_bf, W):
    kn = mm(ckv_bf, W["w_kn"], gain=W["g_kn"], gs=LANES, norm=True, out_dtype=BF16, name="mla_kn")
    v = mm(ckv_bf, W["w_v"], out_dtype=BF16, name="mla_v")
    return kn, v


def _mla_attend(qn, qr, kn, kr, v, *, B, Sq, Sk, sk_valid, tq, hps, mode, qbase):
    kr2 = jnp.concatenate([kr[:, :MLA_ROPE], kr[:, :MLA_ROPE]], axis=1).astype(BF16)
    units = tuple(_unit(((h, None),), h, ((h, "slot", 0),), q2=(h // 2, "lo" if h % 2 == 0 else "hi"))
                  for h in range(hps))
    return attention2(qn, kn, v, B=B, Sq=Sq, Sk=Sk, sk_valid=sk_valid, tq=tq, G=MLA_HEADS // hps,
                      qw=hps, kw=hps, ow=hps, units=units, scale=(MLA_NOPE + MLA_ROPE) ** -0.5,
                      mode=mode, qbase=qbase, q2=qr, q2w=hps // 2, k2=kr2, name="mla_attn")


def kernel(x_prompt, x_sample, mem_prompt, cache_a_ckv, cache_a_krope, cache_b_k, cache_b_v, cache_c_k, cache_c_v, cache_c_idxk, cache_d_k, cache_d_v, cache_mem_k, cache_mem_v, state_ffn_conv, norm_mix, norm_xattn, norm_mem, norm_ffn, a_w_in, a_q_lora_norm, a_kv_lora_norm, a_w_uq, a_w_ukv, a_q_norm, a_k_norm, a_w_out, b_w_in, b_q_norm, b_k_norm, b_rel_bias, b_w_out, c_w_in, c_q_norm, c_k_norm, c_idx_k_norm, c_w_out, d_w_in, d_q_norm, d_k_norm, d_sinks, d_w_out, xa_w_q, xa_w_kv, xa_q_norm, xa_k_norm, xa_w_o, ffn_w_up, ffn_conv_w, ffn_conv_b, ffn_w_down):
    B, S, D = x_prompt.shape
    Bd, n, _ = x_sample.shape
    P = cache_a_ckv.shape[2]
    depth = norm_mix.shape[0]
    Fd = ffn_conv_b.shape[1]
    Mp, Ms = B * S, Bd * n
    pos_p = jnp.arange(S, dtype=I32)
    pos_s = jnp.tile(P + jnp.arange(n, dtype=I32), Bd)
    names = ('a_ckv_p', 'a_krope_p', 'b_k_p', 'b_v_p', 'c_k_p', 'c_v_p', 'c_idxk_p', 'd_k_p', 'd_v_p',
             'mem_k_p', 'mem_v_p', 'conv_p', 'a_ckv_s', 'a_krope_s', 'b_k_s', 'b_v_s', 'c_k_s', 'c_v_s',
             'c_idxk_s', 'd_k_s', 'd_v_s', 'conv_s')
    st = {name: [] for name in names}
    xp = x_prompt.reshape(Mp, D)
    xs = x_sample.reshape(Ms, D)
    tq_p = 128

    def rope_pair(rot, group):
        return _rope_tables(pos_p, rot, group), _rope_tables(pos_s, rot, group)

    for i in range(depth):
        m, j = i % 4, i // 4
        hp = rmsnorm_cast(xp, norm_mix[i])
        hs = rmsnorm_cast(xs, norm_mix[i])
        if m == 0:
            W = _mla_weights(a_w_in[j], a_q_lora_norm[j], a_kv_lora_norm[j], a_w_uq[j], a_w_ukv[j],
                             a_q_norm[j], a_k_norm[j])
            rp, rs = rope_pair(MLA_ROPE, 64)
            w_out = a_w_out[j].astype(BF16)
            qn, qr, ckv, ckv_bf, kr = _mla_project(hp, W, rp)
            kn, v = _mla_expand(ckv_bf, W)
            op = _mla_attend(qn, qr, kn, kr, v, B=B, Sq=S, Sk=S, sk_valid=S, tq=_pick(S, (256, 128)),
                             hps=4, mode="causal", qbase=0)
            xp = mm(op, w_out, res=xp, name="mix_out")
            st['a_ckv_p'].append(ckv.reshape(B, S, MLA_KV_LORA))
            st['a_krope_p'].append(kr[:, :MLA_ROPE].reshape(B, S, MLA_ROPE))
            qn, qr, ckv, ckv_bf, kr = _mla_project(hs, W, rs)
            sk_valid = P + n
            Skp = _round_up(sk_valid, LANES)
            ckv_all = _with_cache(cache_a_ckv[j], ckv_bf, Bd, Skp, BF16)
            kr_all = _with_cache(cache_a_krope[j], kr[:, :MLA_ROPE], Bd, Skp)
            kn, v = _mla_expand(ckv_all, W)
            os_ = _mla_attend(qn, qr, kn, kr_all, v, B=Bd, Sq=n, Sk=Skp, sk_valid=sk_valid, tq=n,
                              hps=MLA_HEADS, mode="full", qbase=P)
            xs = mm(os_, w_out, res=xs, name="mix_out")
            st['a_ckv_s'].append(ckv.reshape(Bd, n, MLA_KV_LORA))
            st['a_krope_s'].append(kr[:, :MLA_ROPE].reshape(Bd, n, MLA_ROPE))
        elif m == 1:
            H, dh = BAND_HEADS, D // BAND_HEADS
            w_in = b_w_in[j]
            wq, wk, wv = (w_in[:, t * H * dh:(t + 1) * H * dh].astype(BF16) for t in range(3))
            gq, gk = _tile_gain(b_q_norm[j], H * dh), _tile_gain(b_k_norm[j], H * dh)
            w_out = b_w_out[j].astype(BF16)
            npv = BAND_PREV_CHUNKS
            def band_units(nh):
                return tuple(_unit(((h, None),), h, ((h, "slot", 0),), bias=h) for h in range(nh))

            q = mm(hp, wq, gain=gq, gs=dh, norm=True, out_dtype=BF16, name="band_q")
            k, k_bf = mm(hp, wk, gain=gk, gs=dh, norm=True, out_dtype=(F32, BF16), name="band_k")
            v, v_bf = mm(hp, wv, out_dtype=(F32, BF16), name="band_v")
            win = npv * CHUNK + tq_p
            bias = band_bias(b_rel_bias[j], tq_p, win, _window_variants(tq_p, npv), npv, win)
            hps = 8
            op = attention2(q, k_bf, v_bf, B=B, Sq=S, Sk=S, sk_valid=S, tq=tq_p, G=H // hps, qw=hps,
                            kw=hps, ow=hps, units=band_units(hps), scale=dh ** -0.5, mode="window",
                            win=win, n_prev=npv, masked=False, bias=bias, name="band_attn")
            xp = mm(op, w_out, res=xp, name="mix_out")
            r = min(npv * CHUNK, S)
            st['b_k_p'].append(k.reshape(B, S, H, dh)[:, S - r:])
            st['b_v_p'].append(v.reshape(B, S, H, dh)[:, S - r:])
            q = mm(hs, wq, gain=gq, gs=dh, norm=True, out_dtype=BF16, name="band_q")
            k = mm(hs, wk, gain=gk, gs=dh, norm=True, name="band_k")
            v = mm(hs, wv, name="band_v")
            rows = cache_b_k.shape[2]
            sk_valid = rows + n
            Skp = _round_up(sk_valid, LANES)
            k_all = pack_cache(cache_b_k[j], k, Skp)
            v_all = pack_cache(cache_b_v[j], v, Skp)
            bias = band_bias(b_rel_bias[j], n, Skp, [(P, P - rows)], npv, sk_valid)
            os_ = attention2(q, k_all, v_all, B=Bd, Sq=n, Sk=Skp, sk_valid=sk_valid, tq=n, G=1, qw=H,
                             kw=H, ow=H, units=band_units(H), scale=dh ** -0.5, mode="window", win=Skp,
                             n_prev=npv, masked=False, qbase=P, kbase=P - rows, bias=bias,
                             name="band_attn")
            xs = mm(os_, w_out, res=xs, name="mix_out")
            st['b_k_s'].append(k.reshape(Bd, n, H, dh))
            st['b_v_s'].append(v.reshape(Bd, n, H, dh))
        elif m == 2:
            Hq, Hk, dh = DSA_HEADS, DSA_KV_HEADS, DSA_HEAD_DIM
            w_in = c_w_in[j]
            oq, ok_, ov = Hq * dh, Hq * dh + Hk * dh, Hq * dh + 2 * Hk * dh
            oiq = ov + IDX_HEADS * IDX_DIM
            oik = oiq + IDX_DIM
            wq = w_in[:, :oq].astype(BF16)
            wk = w_in[:, oq:ok_].astype(BF16)
            wv = w_in[:, ok_:ov].astype(BF16)
            wiq = w_in[:, ov:oiq].astype(BF16)
            wik = _pad_cols(w_in[:, oiq:oik], LANES).astype(BF16)
            wiw = _pad_cols(w_in[:, oik:], LANES).astype(BF16)
            gq, gk = _tile_gain(c_q_norm[j], oq), _tile_gain(c_k_norm[j], Hk * dh)
            gik = _tile_gain(c_idx_k_norm[j], LANES)
            giw = jnp.full((LANES,), IDX_HEADS ** -0.5 * IDX_DIM ** -0.5, F32)
            w_out = c_w_out[j].astype(BF16)
            rp128, rs128 = rope_pair(dh // 4, LANES)
            rp64, rs64 = rope_pair(IDX_DIM // 4, 64)

            def project(h, r128, r64):
                q = mm(h, wq, gain=gq, gs=dh, norm=True, rope=r128, out_dtype=BF16, name="dsa_q")
                k, k_bf = mm(h, wk, gain=gk, gs=dh, norm=True, rope=r128, out_dtype=(F32, BF16), name="dsa_k")
                v, v_bf = mm(h, wv, out_dtype=(F32, BF16), name="dsa_v")
                iq = mm(h, wiq, rope=r64, out_dtype=BF16, name="dsa_iq")
                ik = mm(h, wik, gain=gik, gs=64, norm=True, rope=r64, name="dsa_ik")
                iw = mm(h, wiw, gain=giw, name="dsa_iw")
                return q, k, v, k_bf, v_bf, iq, ik, iw

            q, k, v, k_bf, v_bf, iq, ik, iw = project(hp, rp128, rp64)
            ik2 = jnp.concatenate([ik[:, :IDX_DIM], ik[:, :IDX_DIM]], axis=1).astype(BF16)
            op = dsa_attention2(q, k_bf, v_bf, iq, ik2, iw, B=B, Sq=S, Sk=S, sk_valid=S,
                                tq=_pick(S, (256, 128)), tk=_pick(S, (512, 256, 128)), qbase=0,
                                n_sel=min(DSA_TOPK, S // 4))
            xp = mm(op, w_out, res=xp, name="mix_out")
            st['c_k_p'].append(k.reshape(B, S, Hk, dh))
            st['c_v_p'].append(v.reshape(B, S, Hk, dh))
            st['c_idxk_p'].append(ik[:, :IDX_DIM].reshape(B, S, IDX_DIM))
            q, k, v, _, _, iq, ik, iw = project(hs, rs128, rs64)
            sk_valid = P + n
            Skp = _round_up(sk_valid, LANES)
            k_all = pack_cache(cache_c_k[j], k, Skp)
            v_all = pack_cache(cache_c_v[j], v, Skp)
            ik_all = _with_cache(cache_c_idxk[j], ik[:, :IDX_DIM], Bd, Skp, BF16)
            ik2 = jnp.concatenate([ik_all, ik_all], axis=1)
            os_ = dsa_attention2(q, k_all, v_all, iq, ik2, iw, B=Bd, Sq=n, Sk=Skp, sk_valid=sk_valid,
                                 tq=n, tk=_pick(Skp, (512, 384, 256, 128)), qbase=P,
                                 n_sel=min(DSA_TOPK, sk_valid // 4))
            xs = mm(os_, w_out, res=xs, name="mix_out")
            st['c_k_s'].append(k.reshape(Bd, n, Hk, dh))
            st['c_v_s'].append(v.reshape(Bd, n, Hk, dh))
            st['c_idxk_s'].append(ik[:, :IDX_DIM].reshape(Bd, n, IDX_DIM))
        else:
            Hq, Hk, dh = SWA_HEADS, SWA_KV_HEADS, SWA_HEAD_DIM
            grp = Hq // Hk
            npairs = Hk // 2
            perm = np.array([[[2 * p * grp + t, (2 * p + 1) * grp + t] for t in range(grp)]
                             for p in range(npairs)]).reshape(-1)
            w_in = d_w_in[j]
            nqc, nkc = Hq * dh, Hk * dh
            wq = w_in[:, :nqc].reshape(D, Hq, dh)[:, perm].reshape(D, nqc).astype(BF16)
            wk = w_in[:, nqc:nqc + nkc].astype(BF16)
            wv = w_in[:, nqc + nkc:].astype(BF16)
            gq, gk = _tile_gain(d_q_norm[j], nqc), _tile_gain(d_k_norm[j], nkc)
            w_out = d_w_out[j].reshape(Hq, dh, D)[perm].reshape(nqc, D).astype(BF16)
            sinks = d_sinks[j][perm]
            rp, rs = rope_pair(dh // 4, 64)
            npv = SWA_PREV_CHUNKS
            def swa_units(npr):
                us = []
                for p in range(npr):
                    slots = tuple((p * grp + t, hf) for t in range(grp) for hf in ("lo", "hi"))
                    outs = tuple((p * grp + t, "pair", 2 * t, 2 * t + 1) for t in range(grp))
                    us.append(_unit(slots, p, outs, sinks=range(2 * grp * p, 2 * grp * (p + 1))))
                return tuple(us)

            def project(h, r):
                q = mm(h, wq, gain=gq, gs=64, norm=True, rope=r, out_dtype=BF16, name="swa_q")
                k, k_bf = mm(h, wk, gain=gk, gs=64, norm=True, rope=r, out_dtype=(F32, BF16), name="swa_k")
                v, v_bf = mm(h, wv, out_dtype=(F32, BF16), name="swa_v")
                return q, k, v, k_bf, v_bf

            q, k, v, k_bf, v_bf = project(hp, rp)
            win = npv * CHUNK + tq_p
            pps = npairs
            op = attention2(q, k_bf, v_bf, B=B, Sq=S, Sk=S, sk_valid=S, tq=tq_p, G=npairs // pps,
                            qw=pps * grp, kw=pps, ow=pps * grp, units=swa_units(pps), scale=dh ** -0.5,
                            mode="window", win=win, n_prev=npv, sinks=sinks,
                            sinks_per_group=2 * grp * pps, name="swa_attn")
            xp = mm(op, w_out, res=xp, name="mix_out")
            r = min(npv * CHUNK, S)
            st['d_k_p'].append(k.reshape(B, S, Hk, dh)[:, S - r:])
            st['d_v_p'].append(v.reshape(B, S, Hk, dh)[:, S - r:])
            q, k, v, _, _ = project(hs, rs)
            rows = cache_d_k.shape[2]
            sk_valid = rows + n
            Skp = _round_up(sk_valid, LANES)
            k_all = _with_cache(cache_d_k[j].reshape(Bd, rows, nkc), k, Bd, Skp, BF16)
            v_all = _with_cache(cache_d_v[j].reshape(Bd, rows, nkc), v, Bd, Skp, BF16)
            os_ = attention2(q, k_all, v_all, B=Bd, Sq=n, Sk=Skp, sk_valid=sk_valid, tq=n, G=1,
                             qw=npairs * grp, kw=npairs, ow=npairs * grp, units=swa_units(npairs),
                             scale=dh ** -0.5, mode="window", win=Skp, n_prev=npv, qbase=P,
                             kbase=P - rows, sinks=sinks, sinks_per_group=2 * grp * npairs,
                             name="swa_attn")
            xs = mm(os_, w_out, res=xs, name="mix_out")
            st['d_k_s'].append(k.reshape(Bd, n, Hk, dh))
            st['d_v_s'].append(v.reshape(Bd, n, Hk, dh))

        Hx, dx = XA_HEADS, XA_HEAD_DIM
        Mm = mem_prompt.shape[1]
        hm = rmsnorm_cast(mem_prompt.reshape(B * Mm, D), norm_mem[i])
        w_kv = xa_w_kv[i]
        mk, mk_bf = mm(hm, w_kv[:, :Hx * dx].astype(BF16), gain=_tile_gain(xa_k_norm[i], Hx * dx), gs=dx,
                       norm=True, out_dtype=(F32, BF16), name="xa_k")
        mv, mv_bf = mm(hm, w_kv[:, Hx * dx:].astype(BF16), out_dtype=(F32, BF16), name="xa_v")
        st['mem_k_p'].append(mk.reshape(B, Mm, Hx, dx))
        st['mem_v_p'].append(mv.reshape(B, Mm, Hx, dx))
        wxq = xa_w_q[i].astype(BF16)
        gxq = _tile_gain(xa_q_norm[i], Hx * dx)
        wxo = xa_w_o[i].astype(BF16)
        xa_units = tuple(_unit(((h, None),), h, ((h, "slot", 0),)) for h in range(Hx))

        def cross(x, mk2, mv2, Bx, Sx, tq):
            qx = mm(x, wxq, in_gain=norm_xattn[i], gain=gxq, gs=dx, norm=True, out_dtype=BF16, name="xa_q")
            ox = attention2(qx, mk2, mv2, B=Bx, Sq=Sx, Sk=Mm, sk_valid=Mm, tq=tq, G=1, qw=Hx, kw=Hx,
                            ow=Hx, units=xa_units, scale=dx ** -0.5, mode="full", masked=False,
                            name="xattn")
            return mm(ox, wxo, res=x, name="xa_out")

        xp = cross(xp, mk_bf, mv_bf, B, S, _pick(S, (512, 256, 128)))
        xs = cross(xs, pack_cache(cache_mem_k[i], None, Mm), pack_cache(cache_mem_v[i], None, Mm), Bd, n, n)

        w_up = ffn_w_up[i]
        wg, wv_ = w_up[:, :Fd].astype(BF16), w_up[:, Fd:].astype(BF16)
        w_dn = ffn_w_down[i].astype(BF16)
        act, conv_p = ffn_up(xp, norm_ffn[i], wg, wv_, ffn_conv_w[i], ffn_conv_b[i],
                             jnp.zeros((B, CONV_W - 1, Fd), F32), n_seq=B, seq_rows=S)
        xp = mm(act, w_dn, res=xp, name="ffn_down")
        act, conv_s = ffn_up(xs, norm_ffn[i], wg, wv_, ffn_conv_w[i], ffn_conv_b[i], state_ffn_conv[i],
                             n_seq=Bd, seq_rows=n)
        xs = mm(act, w_dn, res=xs, name="ffn_down")
        st['conv_p'].append(conv_p)
        st['conv_s'].append(conv_s)

    order = ('a_ckv_p', 'a_krope_p', 'b_k_p', 'b_v_p', 'c_k_p', 'c_v_p', 'c_idxk_p', 'd_k_p', 'd_v_p',
             'mem_k_p', 'mem_v_p', 'conv_p', 'a_ckv_s', 'a_krope_s', 'b_k_s', 'b_v_s', 'c_k_s', 'c_v_s',
             'c_idxk_s', 'd_k_s', 'd_v_s', 'conv_s')
    return (xp.reshape(B, S, D), xs.reshape(Bd, n, D)) + tuple(jnp.stack(st[nm]) for nm in order)
```

```python
import functools

import numpy as np
import jax
import jax.numpy as jnp
from jax import lax
from jax.experimental import pallas as pl
from jax.experimental.pallas import tpu as pltpu

F32, BF16, I32 = jnp.float32, jnp.bfloat16, jnp.int32

CHUNK = 64
CHUNK_SHIFT = 6
ROPE_THETA = 500000.0
EPS = 1e-6
NEG_INF = -1e30
MLA_HEADS, MLA_NOPE, MLA_ROPE, MLA_V = 16, 128, 64, 128
MLA_Q_LORA, MLA_KV_LORA = 512, 512
BAND_HEADS, BAND_PREV_CHUNKS, MAX_REL = 16, 8, 128
DSA_HEADS, DSA_KV_HEADS, DSA_HEAD_DIM = 16, 4, 128
IDX_HEADS, IDX_DIM, DSA_TOPK = 16, 64, 256
SWA_HEADS, SWA_KV_HEADS, SWA_HEAD_DIM, SWA_PREV_CHUNKS = 32, 8, 64, 2
XA_HEADS, XA_HEAD_DIM = 4, 128
CONV_W = 3

LANES = 128
VMEM_LIMIT_MAX = 56 * 1024 * 1024
VMEM_LIMIT_MIN = 32 * 1024 * 1024

M_INIT = -3.0e38
INT_MIN = -2 ** 31
BIG_COL = 2 ** 30
LOG2E = 1.4426950408889634
SOFTMAX_ROWS = 64


def _vmem_limit(block_bytes, temp_bytes=0):
    need = 2 * block_bytes + temp_bytes + (4 << 20)
    return int(min(max(need, VMEM_LIMIT_MIN), VMEM_LIMIT_MAX))


def _pick(n, cands):
    for c in cands:
        if n % c == 0:
            return c
    raise ValueError(f"no tile for {n}")


def _rmsnorm_kernel(x_ref, g_ref, o_ref):
    x = x_ref[...]
    ms = jnp.mean(x * x, axis=-1, keepdims=True)
    o_ref[...] = (x * lax.rsqrt(ms + EPS) * g_ref[...]).astype(o_ref.dtype)


def rmsnorm_cast(x, g):
    M, K = x.shape
    tm = _pick(M, (512, 256, 128, 64, 16, 8))
    return pl.pallas_call(
        _rmsnorm_kernel,
        grid=(M // tm,),
        in_specs=[pl.BlockSpec((tm, K), lambda i: (i, 0)), pl.BlockSpec((1, K), lambda i: (0, 0))],
        out_specs=pl.BlockSpec((tm, K), lambda i: (i, 0)),
        out_shape=jax.ShapeDtypeStruct((M, K), BF16),
        compiler_params=pltpu.CompilerParams(
            dimension_semantics=("parallel",), vmem_limit_bytes=_vmem_limit(tm * K * 6, tm * K * 8)),
        name="rmsnorm_cast",
    )(x, g.reshape(1, K).astype(F32))


def _group_norm_rope(y, gain_ref, rope_refs, gs, norm, half):
    tm, tn = y.shape
    if norm and gs == tn:
        ss = jnp.sum(y * y, axis=-1, keepdims=True)
        return y * lax.rsqrt(ss * (1.0 / gs) + EPS) * gain_ref[...]
    outs = []
    for c in range(tn // LANES):
        yc = y[:, c * LANES:(c + 1) * LANES]
        if norm:
            yy = yc * yc
            if gs == LANES:
                ss = jnp.sum(yy, axis=-1, keepdims=True)
            else:
                lo = lax.broadcasted_iota(I32, yc.shape, 1) < gs
                s_lo = jnp.sum(jnp.where(lo, yy, 0.0), axis=-1, keepdims=True)
                s_hi = jnp.sum(jnp.where(lo, 0.0, yy), axis=-1, keepdims=True)
                ss = jnp.where(lo, s_lo, s_hi)
            yc = yc * lax.rsqrt(ss * (1.0 / gs) + EPS) * gain_ref[:, c * LANES:(c + 1) * LANES]
        elif gain_ref is not None:
            yc = yc * gain_ref[:, c * LANES:(c + 1) * LANES]
        outs.append(yc)
    y = outs[0] if len(outs) == 1 else jnp.concatenate(outs, axis=1)
    if not half:
        return y
    c_ref, s_ref, p_ref = rope_refs
    pw = p_ref.shape[0]
    outs = []
    for b in range(tn // pw):
        yb = y[:, b * pw:(b + 1) * pw]
        hi = yb.astype(BF16)
        lo = (yb - hi.astype(F32)).astype(BF16)
        partner = (jnp.dot(hi, p_ref[...], preferred_element_type=F32)
                   + jnp.dot(lo, p_ref[...], preferred_element_type=F32))
        outs.append(yb * c_ref[...] + partner * s_ref[...])
    return outs[0] if len(outs) == 1 else jnp.concatenate(outs, axis=1)


def _normed_rows(x_ref, g_ref, h_ref):
    @pl.when(pl.program_id(1) == 0)
    def _():
        x = x_ref[...]
        ms = jnp.mean(x * x, axis=-1, keepdims=True)
        h_ref[...] = (x * lax.rsqrt(ms + EPS) * g_ref[...]).astype(h_ref.dtype)
    return h_ref[...]


def _mm_kernel(*refs, has_gain, gs, norm, half, has_res, has_in_norm, n_out):
    a_ref, w_ref = refs[0], refs[1]
    pos = 2
    in_gain_ref = gain_ref = rope_refs = res_ref = None
    if has_in_norm:
        in_gain_ref = refs[pos]
        pos += 1
    if has_gain:
        gain_ref = refs[pos]
        pos += 1
    if half:
        rope_refs = refs[pos:pos + 3]
        pos += 3
    if has_res:
        res_ref = refs[pos]
        pos += 1
    a = _normed_rows(a_ref, in_gain_ref, refs[pos + n_out]) if has_in_norm else a_ref[...]
    y = jnp.dot(a, w_ref[...], preferred_element_type=F32)
    if has_gain or half:
        y = _group_norm_rope(y, gain_ref, rope_refs, gs, norm, half)
    if has_res:
        y = res_ref[...] + y
    for o_ref in refs[pos:pos + n_out]:
        o_ref[...] = y.astype(o_ref.dtype)


def mm(a, w, *, gain=None, gs=LANES, norm=False, rope=None, res=None, out_dtype=F32, tn=None,
       in_gain=None, name="mm"):
    M, K = a.shape
    N = w.shape[1]
    assert a.dtype == (BF16 if in_gain is None else F32) and w.dtype == BF16 and N % LANES == 0
    tm_c = (1024, 512, 256, 128, 64, 16, 8)
    if tn is None:
        if norm and gs > LANES:
            tn = gs
        elif K <= 512:
            tn = _pick(N, (2048, 1024, 512, 256, 128))
            if res is not None:
                tm_c = tm_c[1:]
        elif K <= 2048:
            tn = _pick(N, (1024, 512, 256, 128))
        else:
            tn = _pick(N, (512, 256, 128))
    rows_period = M if rope is None else rope[1].shape[0]
    tm = _pick(np.gcd(M, rows_period), tm_c)
    half = 0
    in_specs = [pl.BlockSpec((tm, K), lambda i, j: (i, 0)), pl.BlockSpec((K, tn), lambda i, j: (0, j))]
    args = [a, w]
    blk = tm * K * 2 + K * tn * 2 + tm * tn * 4
    scratch = []
    if in_gain is not None:
        in_specs.append(pl.BlockSpec((1, K), lambda i, j: (0, 0)))
        args.append(in_gain.reshape(1, K).astype(F32))
        scratch.append(pltpu.VMEM((tm, K), BF16))
        blk += tm * K * 4
    if gain is not None:
        in_specs.append(pl.BlockSpec((1, tn), lambda i, j: (0, j)))
        args.append(gain.reshape(1, N).astype(F32))
    if rope is not None:
        half, tab_c, tab_s, perm = rope
        if tn % perm.shape[0] != 0:
            tab_c, tab_s, perm = tab_c[:, :LANES], tab_s[:, :LANES], perm[:LANES, :LANES]
        pw = perm.shape[0]
        assert tn % pw == 0
        nrb = rows_period // tm
        for t in (tab_c, tab_s):
            in_specs.append(pl.BlockSpec((tm, pw), lambda i, j, nrb=nrb: (i % nrb, 0)))
            args.append(t)
        in_specs.append(pl.BlockSpec((pw, pw), lambda i, j: (0, 0)))
        args.append(perm)
        blk += 2 * tm * pw * 4 + pw * pw * 2
    if res is not None:
        in_specs.append(pl.BlockSpec((tm, tn), lambda i, j: (i, j)))
        args.append(res)
        blk += tm * tn * 4
    dtypes = out_dtype if isinstance(out_dtype, tuple) else (out_dtype,)
    kern = functools.partial(_mm_kernel, has_gain=gain is not None, gs=gs, norm=norm, half=half,
                             has_res=res is not None, has_in_norm=in_gain is not None, n_out=len(dtypes))
    outs = pl.pallas_call(
        kern,
        grid=(M // tm, N // tn),
        in_specs=in_specs,
        out_specs=[pl.BlockSpec((tm, tn), lambda i, j: (i, j)) for _ in dtypes],
        out_shape=[jax.ShapeDtypeStruct((M, N), dt) for dt in dtypes],
        scratch_shapes=scratch,
        compiler_params=pltpu.CompilerParams(
            dimension_semantics=("parallel", "arbitrary"),
            vmem_limit_bytes=_vmem_limit(blk + (len(dtypes) - 1) * tm * tn * 4, 3 * tm * tn * 4)),
        name=name,
    )(*args)
    return tuple(outs) if isinstance(out_dtype, tuple) else outs[0]


def _ffn_up_kernel(x_ref, gx_ref, wg_ref, wv_ref, cw_ref, cb_ref, prev_ref, act_ref, last_ref, carry_ref,
                   h_ref, *, tiles_per_seq, seq_rows):
    i = pl.program_id(0)
    h = _normed_rows(x_ref, gx_ref, h_ref)
    g = jnp.dot(h, wg_ref[...], preferred_element_type=F32)
    val = jnp.dot(h, wv_ref[...], preferred_element_type=F32)
    tm, tn = g.shape
    row = lax.broadcasted_iota(I32, (tm, tn), 0)
    g1 = pltpu.roll(g, 1, 0)
    g2 = pltpu.roll(g, 2, 0)
    if tiles_per_seq >= 1:
        first = (i % tiles_per_seq) == 0
        j = pl.program_id(1)
        p_prev = prev_ref[0]
        p_carry = carry_ref[j]
        p2 = jnp.where(first, p_prev[0:1, :], p_carry[6:7, :])
        p1 = jnp.where(first, p_prev[1:2, :], p_carry[7:8, :])
        g1 = jnp.where(row == 0, p1, g1)
        g2 = jnp.where(row == 0, p2, jnp.where(row == 1, p1, g2))
        carry_ref[j] = g[tm - 8:tm, :]
        last_ref[0] = g[tm - 8:tm, :]
    else:
        rmod = row % seq_rows
        g1 = jnp.where(rmod == 0, prev_ref[0], g1)
        g2 = jnp.where(rmod < 2, prev_ref[1], g2)
        last_ref[...] = g
    c = cw_ref[0:1, :] * g2 + cw_ref[1:2, :] * g1 + cw_ref[2:3, :] * g + cb_ref[...]
    act_ref[...] = (c * (1.0 / (1.0 + jnp.exp(-c))) * val).astype(act_ref.dtype)


def ffn_up(x, gx, w_up, conv_w, conv_b, prev, *, n_seq, seq_rows):
    M, D = x.shape
    Fd = w_up.shape[1] // 2
    tn = _pick(Fd, (512, 256, 128))
    nj = Fd // tn
    if seq_rows >= 256:
        tm = _pick(seq_rows, (1024, 512, 256))
        tps = seq_rows // tm
        prev_arg = prev
        prev_spec = pl.BlockSpec((1, 2, tn), lambda i, j: (i // tps, 0, j))
        last_shape = jax.ShapeDtypeStruct((M // tm, 8, Fd), F32)
        last_spec = pl.BlockSpec((1, 8, tn), lambda i, j: (i, 0, j))
    else:
        tm = M
        tps = 0
        z = jnp.zeros((n_seq, seq_rows - 1, Fd), F32)
        inj1 = jnp.concatenate([prev[:, 1:2], z], axis=1).reshape(M, Fd)
        inj2 = jnp.concatenate([prev[:, 0:2], z[:, 1:]], axis=1).reshape(M, Fd)
        prev_arg = jnp.stack([inj1, inj2])
        prev_spec = pl.BlockSpec((2, tm, tn), lambda i, j: (0, i, j))
        last_shape = jax.ShapeDtypeStruct((M, Fd), F32)
        last_spec = pl.BlockSpec((tm, tn), lambda i, j: (i, j))
    kern = functools.partial(_ffn_up_kernel, tiles_per_seq=tps, seq_rows=seq_rows)
    blk = tm * D * 5 + 2 * D * tn * 2 + tm * tn * 2 + 4 * tn * 4 + 2 * tm * tn * 4
    act, last = pl.pallas_call(
        kern,
        grid=(M // tm, nj),
        in_specs=[pl.BlockSpec((tm, D), lambda i, j: (i, 0)),
                  pl.BlockSpec((1, D), lambda i, j: (0, 0)),
                  pl.BlockSpec((D, tn), lambda i, j: (0, j)),
                  pl.BlockSpec((D, tn), lambda i, j, nj=nj: (0, j + nj)),
                  pl.BlockSpec((CONV_W, tn), lambda i, j: (0, j)),
                  pl.BlockSpec((1, tn), lambda i, j: (0, j)),
                  prev_spec],
        out_specs=[pl.BlockSpec((tm, tn), lambda i, j: (i, j)), last_spec],
        out_shape=[jax.ShapeDtypeStruct((M, Fd), BF16), last_shape],
        scratch_shapes=[pltpu.VMEM((nj, 8, tn), F32), pltpu.VMEM((tm, D), BF16)],
        compiler_params=pltpu.CompilerParams(
            dimension_semantics=("arbitrary", "arbitrary"),
            vmem_limit_bytes=_vmem_limit(blk, 6 * tm * tn * 4)),
        name="ffn_up",
    )(x, gx.reshape(1, D).astype(F32), w_up, w_up, conv_w.astype(F32), conv_b.reshape(1, Fd).astype(F32),
      prev_arg)
    if tps == 0:
        last = last.reshape(n_seq, seq_rows, Fd)[:, seq_rows - 2:]
    else:
        last = last[tps - 1::tps, 6:8]
    return act, last


def _unit(slots, k_tile, outs, q2=None, bias=None, sinks=()):
    return (tuple(slots), k_tile, tuple(outs), q2, bias, tuple(sinks))


def _attn2_kernel(*refs, tq, nq, units, mode, win, sk, sk_valid, n_prev, masked, qbase, kbase, scale,
                  has_q2, has_bias, has_sinks, sinks_per_group, bias_is_mask):
    q_ref, k_ref, v_ref = refs[0], refs[1], refs[2]
    pos = 3
    q2_ref = k2_ref = bias_ref = sinks_ref = None
    if has_q2:
        q2_ref, k2_ref = refs[pos], refs[pos + 1]
        pos += 2
    if has_bias:
        bias_ref = refs[pos]
        pos += 1
    if has_sinks:
        sinks_ref = refs[pos]
        pos += 1
    o_ref = refs[pos]

    g = pl.program_id(1)
    i = pl.program_id(2)
    q0 = i * tq
    lane = lax.broadcasted_iota(I32, (tq, LANES), 1)
    half_mask = {"lo": jnp.where(lane < 64, 1.0, 0.0).astype(BF16),
                 "hi": jnp.where(lane >= 64, 1.0, 0.0).astype(BF16)}
    qc1 = (qbase + q0 + lax.broadcasted_iota(I32, (tq, 1), 0)) >> CHUNK_SHIFT

    def compute(r0, L, mask_from=0):
        if masked:
            krow = r0 + mask_from + lax.broadcasted_iota(I32, (1, L - mask_from), 1)
            kc = (kbase + krow) >> CHUNK_SHIFT
        rows = pl.ds(r0, L) if not isinstance(r0, int) else slice(r0, r0 + L)
        for u, (slots, k_tile, outs, q2, bias_idx, sink_idx) in enumerate(units):
            ns = len(slots)
            qs = []
            for (t, mk) in slots:
                qt = q_ref[:, t * LANES:(t + 1) * LANES]
                if mk is not None:
                    qt = qt * half_mask[mk]
                if q2 is not None:
                    q2t = q2_ref[:, q2[0] * LANES:(q2[0] + 1) * LANES] * half_mask[q2[1]]
                    qt = jnp.concatenate([qt, q2t], axis=1)
                qs.append(qt)
            Q = qs[0] if ns == 1 else jnp.concatenate(qs, axis=0)
            kb = k_ref[rows, k_tile * LANES:(k_tile + 1) * LANES]
            if q2 is not None:
                kb = jnp.concatenate([kb, k2_ref[rows, :]], axis=1)
            vb = v_ref[rows, k_tile * LANES:(k_tile + 1) * LANES]
            log2 = bias_idx is None or bias_is_mask
            expf = jnp.exp2 if log2 else jnp.exp
            s = lax.dot_general(Q, kb, (((1,), (1,)), ((), ())), preferred_element_type=F32) * (
                scale * LOG2E if log2 else scale)
            if bias_idx is not None:
                bt = bias_ref[0, bias_idx]
                s = s + (bt if ns == 1 else jnp.concatenate([bt] * ns, axis=0))
            if masked:
                qc = qc1 if ns == 1 else jnp.concatenate([qc1] * ns, axis=0)
                sm = s[:, mask_from:]
                sm = jnp.where(kc <= qc, sm, NEG_INF)
                if n_prev is not None:
                    sm = jnp.where(kc >= qc - n_prev, sm, NEG_INF)
                if sk_valid < sk:
                    sm = jnp.where(krow < sk_valid, sm, NEG_INF)
                s = sm if mask_from == 0 else jnp.concatenate([s[:, :mask_from], sm], axis=1)
            m = jnp.max(s, axis=-1, keepdims=True)
            if sink_idx:
                sk_rows = [jnp.full((tq, 1), sinks_ref[g * sinks_per_group + si], F32) for si in sink_idx]
                sink = sk_rows[0] if ns == 1 else jnp.concatenate(sk_rows, axis=0)
                sink = sink * LOG2E if log2 else sink
                m = jnp.maximum(m, sink)
            p = expf(s - m)
            l = jnp.sum(p, axis=-1, keepdims=True)
            if sink_idx:
                l = l + expf(sink - m)
            o = jnp.dot(p.astype(BF16), vb, preferred_element_type=F32) / l
            for spec in outs:
                t = spec[0]
                if spec[1] == "slot":
                    ot = o[spec[2] * tq:(spec[2] + 1) * tq]
                else:
                    ot = jnp.where(lane < 64, o[spec[2] * tq:(spec[2] + 1) * tq],
                                   o[spec[3] * tq:(spec[3] + 1) * tq])
                o_ref[:, t * LANES:(t + 1) * LANES] = ot.astype(o_ref.dtype)

    if mode == "window":
        lo_row = ((qbase + q0) >> CHUNK_SHIFT) * CHUNK - n_prev * CHUNK - kbase
        compute(pl.multiple_of(jnp.clip(lo_row, 0, sk - win), 16), win)
    elif mode == "full":
        compute(0, sk)
    else:
        for c in range(nq):
            pl.when(i == c)(functools.partial(compute, 0, min(sk, (c + 1) * tq), c * tq))


def attention2(q, k, v, *, B, Sq, Sk, sk_valid, tq, G, qw, kw, ow, units, scale, mode, win=None,
               n_prev=None, masked=True, qbase=0, kbase=0, q2=None, q2w=0, k2=None, bias=None,
               bias_is_mask=False, sinks=None, sinks_per_group=0, name="attn"):
    nq = Sq // tq
    assert mode != "causal" or (qbase == 0 and kbase == 0 and tq % CHUNK == 0)
    in_specs = [pl.BlockSpec((tq, qw * LANES), lambda b, g, i: (b * nq + i, g)),
                pl.BlockSpec((Sk, kw * LANES), lambda b, g, i: (b, g)),
                pl.BlockSpec((Sk, kw * LANES), lambda b, g, i: (b, g))]
    args = [q, k, v]
    blk = tq * qw * LANES * 2 + 2 * Sk * kw * LANES * 2 + tq * ow * LANES * 2
    if q2 is not None:
        in_specs += [pl.BlockSpec((tq, q2w * LANES), lambda b, g, i: (b * nq + i, g)),
                     pl.BlockSpec((Sk, LANES), lambda b, g, i: (b, 0))]
        args += [q2, k2]
        blk += tq * q2w * LANES * 2 + Sk * LANES * 2
    if bias is not None:
        nv, nbh = bias.shape[0], bias.shape[1] // G
        in_specs.append(pl.BlockSpec((1, nbh, tq, win),
                                     lambda b, g, i, nv=nv: (jnp.minimum(i, nv - 1), g, 0, 0)))
        args.append(bias)
        blk += nbh * tq * win * 4
    if sinks is not None:
        in_specs.append(pl.BlockSpec(memory_space=pltpu.SMEM))
        args.append(sinks.astype(F32))
    L = win if mode == "window" else Sk
    temp = sum(len(u[0]) for u in units) * tq * L * 12
    kern = functools.partial(
        _attn2_kernel, tq=tq, nq=nq, units=tuple(units), mode=mode, win=win, sk=Sk, sk_valid=sk_valid,
        n_prev=n_prev, masked=masked, qbase=qbase, kbase=kbase, scale=scale, has_q2=q2 is not None,
        has_bias=bias is not None, has_sinks=sinks is not None, sinks_per_group=sinks_per_group,
        bias_is_mask=bias_is_mask)
    return pl.pallas_call(
        kern,
        grid=(B, G, nq),
        in_specs=in_specs,
        out_specs=pl.BlockSpec((tq, ow * LANES), lambda b, g, i: (b * nq + i, g)),
        out_shape=jax.ShapeDtypeStruct((B * Sq, G * ow * LANES), BF16),
        compiler_params=pltpu.CompilerParams(
            dimension_semantics=("parallel", "parallel", "arbitrary"),
            vmem_limit_bytes=_vmem_limit(blk, temp)),
        name=name,
    )(*args)


def _toeplitz_kernel(ab_ref, e_ref, o_ref, *, tq, win, n_prev, n_valid, has_table):
    v = pl.program_id(0)
    if has_table:
        wp = e_ref.shape[-1]
        x = jnp.broadcast_to(e_ref[0, 0, 0:1, :], (tq, wp))
        y = pltpu.roll(x, 0, 1, stride=1, stride_axis=0)[:, :win]
    else:
        y = jnp.zeros((tq, win), F32)
    qc = (ab_ref[v, 0] + lax.broadcasted_iota(I32, (tq, win), 0)) >> CHUNK_SHIFT
    kj = lax.broadcasted_iota(I32, (tq, win), 1)
    kc = (ab_ref[v, 1] + kj) >> CHUNK_SHIFT
    y = jnp.where(kc <= qc, y, NEG_INF)
    y = jnp.where(kc >= qc - n_prev, y, NEG_INF)
    if n_valid < win:
        y = jnp.where(kj < n_valid, y, NEG_INF)
    o_ref[0, 0] = y


def band_bias(table, tq, win, pos, n_prev, n_valid):
    nv = len(pos)
    ab = np.array([[q - (k // CHUNK) * CHUNK, k - (k // CHUNK) * CHUNK] for q, k in pos], np.int32)
    has_table = table is not None
    H = table.shape[1] if has_table else 1
    wp = _round_up(win + tq - 1, LANES)
    if has_table:
        m = np.arange(wp)
        m = np.where(m < win, m, m - wp)
        idx = np.stack([np.clip(q - k - m, -MAX_REL, MAX_REL) + MAX_REL for q, k in pos])
        e = jnp.transpose(table.astype(F32)[idx], (0, 2, 1))
        e = jnp.broadcast_to(e[:, :, None, :], (nv, H, 8, wp))
    else:
        e = jnp.zeros((nv, H, 8, wp), F32)
    return pl.pallas_call(
        functools.partial(_toeplitz_kernel, tq=tq, win=win, n_prev=n_prev, n_valid=n_valid,
                          has_table=has_table),
        grid=(nv, H),
        in_specs=[pl.BlockSpec(memory_space=pltpu.SMEM),
                  pl.BlockSpec((1, 1, 8, wp), lambda v, h: (v, h, 0, 0))],
        out_specs=pl.BlockSpec((1, 1, tq, win), lambda v, h: (v, h, 0, 0)),
        out_shape=jax.ShapeDtypeStruct((nv, H, tq, win), F32),
        compiler_params=pltpu.CompilerParams(dimension_semantics=("parallel", "parallel")),
        name="band_bias",
    )(jnp.asarray(ab), e)


def _dsa2_kernel(q_ref, k_ref, v_ref, iq_ref, ik_ref, iw_ref, o_ref, key_ref, mb_ref, m_ref, l_ref,
                 acc_ref, alpha_ref, s_ref, p_ref, *, tq, tk, sk, sk_valid, qbase, n_sel, scale):
    i = pl.program_id(1)
    q0 = i * tq
    qc = (qbase + q0 + lax.broadcasted_iota(I32, (tq, 1), 0)) >> CHUNK_SHIFT
    hi_row = (((qbase + q0 + tq - 1) >> CHUNK_SHIFT) + 1) * CHUNK
    nb = (jnp.minimum(hi_row, sk_valid) + tk - 1) // tk
    lane = lax.broadcasted_iota(I32, (tq, LANES), 1)
    half_mask = (jnp.where(lane < 64, 1.0, 0.0).astype(BF16), jnp.where(lane >= 64, 1.0, 0.0).astype(BF16))
    n_idx_tiles = iq_ref.shape[1] // LANES
    n_lane_tiles = tk // LANES

    def col_of(jb):
        return jb * tk + lax.broadcasted_iota(I32, (1, tk), 1)

    def mask_invisible(x, col):
        x = jnp.where((col >> CHUNK_SHIFT) <= qc, x, NEG_INF)
        if sk_valid < sk:
            x = jnp.where(col < sk_valid, x, NEG_INF)
        return x

    def lane_fold(x):
        acc = x[:, 0:LANES]
        for t in range(1, n_lane_tiles):
            acc = acc + x[:, t * LANES:(t + 1) * LANES]
        return acc

    def score_body(jb, carry):
        r0 = pl.multiple_of(jb * tk, 16)
        ikb = ik_ref[pl.ds(r0, tk), :]
        acc = jnp.zeros((tq, tk), F32)
        for p in range(n_idx_tiles):
            t = iq_ref[:, p * LANES:(p + 1) * LANES]
            for hf in range(2):
                lg = lax.dot_general(t * half_mask[hf], ikb, (((1,), (1,)), ((), ())),
                                     preferred_element_type=F32)
                h = 2 * p + hf
                acc = acc + iw_ref[:, h:h + 1] * jnp.maximum(lg, 0.0)
        sc = mask_invisible(acc, col_of(jb))
        bits = lax.bitcast_convert_type(sc, I32)
        key_ref[jb] = bits ^ ((bits >> 31) & 0x7FFFFFFF)
        return carry

    lax.fori_loop(0, nb, score_body, 0)

    def count(hit):
        def cb(jb, c):
            return c + lane_fold(hit(key_ref[jb], jb))
        part = lax.fori_loop(0, nb, cb, jnp.zeros((tq, LANES), F32))
        return jnp.sum(part, axis=1, keepdims=True)

    def bit_body(it, cand):
        trial = cand | (jnp.int32(1) << (31 - it))
        trial_s = trial ^ INT_MIN
        cnt = count(lambda kk, jb: jnp.where(kk >= trial_s, 1.0, 0.0))
        return jnp.where(cnt >= n_sel, trial, cand)

    cand = lax.fori_loop(0, 32, bit_body, jnp.zeros((tq, 1), I32))
    thr = cand ^ INT_MIN

    cnt_gt = count(lambda kk, jb: jnp.where(kk > thr, 1.0, 0.0))
    n_eq = count(lambda kk, jb: jnp.where(kk == thr, 1.0, 0.0))
    need = n_sel - cnt_gt
    neg_bits = int(np.float32(NEG_INF).view(np.int32))
    neg_key = neg_bits ^ ((neg_bits >> 31) & 0x7FFFFFFF)
    tie = jnp.where(thr > neg_key, jnp.where(n_eq > need, 1.0, 0.0), 0.0)
    any_tie = jnp.max(tie) > 0.0

    def tie_cols():
        def jbit(it, cj):
            trial = cj | (jnp.int32(1) << (30 - it))
            c = count(lambda kk, jb: jnp.where(kk == thr, jnp.where(col_of(jb) < trial, 1.0, 0.0), 0.0))
            return jnp.where(c < need, trial, cj)
        return lax.fori_loop(0, 31, jbit, jnp.zeros((tq, 1), I32))

    jmax = lax.cond(any_tie, tie_cols, lambda: jnp.full((tq, 1), BIG_COL, I32))

    def mask_body(jb, carry):
        kk = key_ref[jb]
        col = col_of(jb)
        sel = jnp.where(kk > thr, 0.0, jnp.where(kk == thr, jnp.where(col <= jmax, 0.0, NEG_INF), NEG_INF))
        mb_ref[jb] = mask_invisible(sel, col)
        return carry

    lax.fori_loop(0, nb, mask_body, 0)

    n_kv = k_ref.shape[1] // LANES
    grp = (q_ref.shape[1] // LANES) // n_kv
    R = grp * tq
    m_ref[...] = jnp.full(m_ref.shape, M_INIT, F32)
    l_ref[...] = jnp.zeros(l_ref.shape, F32)
    acc_ref[...] = jnp.zeros(acc_ref.shape, F32)

    ch = min(SOFTMAX_ROWS, tq)
    n_ch = R // ch

    def body(jb, carry):
        r0 = pl.multiple_of(jb * tk, 16)
        for kvh in range(n_kv):
            Q = jnp.concatenate(
                [q_ref[:, (kvh * grp + gi) * LANES:(kvh * grp + gi + 1) * LANES] for gi in range(grp)], axis=0)
            kb = k_ref[pl.ds(r0, tk), kvh * LANES:(kvh + 1) * LANES]
            vb = v_ref[pl.ds(r0, tk), kvh * LANES:(kvh + 1) * LANES]
            s_ref[kvh] = lax.dot_general(Q, kb, (((1,), (1,)), ((), ())), preferred_element_type=F32)

            def rows_body(c, carry2, kvh=kvh):
                rr = pl.multiple_of(c * ch, ch)
                qr = pl.multiple_of((c * ch) % tq, ch)
                s = s_ref[kvh, pl.ds(rr, ch), :] * (scale * LOG2E) + mb_ref[jb, pl.ds(qr, ch), :]
                m_old = m_ref[kvh, pl.ds(rr, ch), :]
                m_new = jnp.maximum(m_old, jnp.max(s, axis=-1, keepdims=True))
                p = jnp.exp2(s - jnp.concatenate([m_new] * n_lane_tiles, axis=1))
                alpha = jnp.exp2(m_old - m_new)
                l_ref[kvh, pl.ds(rr, ch), :] = alpha * l_ref[kvh, pl.ds(rr, ch), :] + jnp.sum(
                    p, axis=-1, keepdims=True)
                m_ref[kvh, pl.ds(rr, ch), :] = m_new
                alpha_ref[kvh, pl.ds(rr, ch), :] = alpha
                p_ref[kvh, pl.ds(rr, ch), :] = p.astype(BF16)
                return carry2

            lax.fori_loop(0, n_ch, rows_body, 0, unroll=n_ch)
            acc_ref[kvh] = alpha_ref[kvh] * acc_ref[kvh] + jnp.dot(p_ref[kvh], vb,
                                                                   preferred_element_type=F32)
        return carry

    lax.fori_loop(0, nb, body, 0)
    for kvh in range(n_kv):
        o = acc_ref[kvh] / l_ref[kvh]
        for gi in range(grp):
            hh = kvh * grp + gi
            o_ref[:, hh * LANES:(hh + 1) * LANES] = o[gi * tq:(gi + 1) * tq].astype(o_ref.dtype)


def dsa_attention2(q, k, v, iq, ik2, iw, *, B, Sq, Sk, sk_valid, tq, tk, qbase, n_sel):
    nq = Sq // tq
    nkb = Sk // tk
    Hq = q.shape[1]
    Hk = k.shape[1]
    n_kv = Hk // LANES
    R = (Hq // Hk) * tq
    kern = functools.partial(_dsa2_kernel, tq=tq, tk=tk, sk=Sk, sk_valid=sk_valid, qbase=qbase,
                             n_sel=n_sel, scale=DSA_HEAD_DIM ** -0.5)
    blk = tq * (Hq + iq.shape[1]) * 2 + tq * LANES * 4 + Sk * (2 * Hk + LANES) * 2 + tq * Hq * 2
    scratch = 2 * nkb * tq * tk * 4 + 3 * n_kv * R * LANES * 4
    return pl.pallas_call(
        kern,
        grid=(B, nq),
        in_specs=[pl.BlockSpec((tq, Hq), lambda b, i: (b * nq + i, 0)),
                  pl.BlockSpec((Sk, Hk), lambda b, i: (b, 0)),
                  pl.BlockSpec((Sk, Hk), lambda b, i: (b, 0)),
                  pl.BlockSpec((tq, iq.shape[1]), lambda b, i: (b * nq + i, 0)),
                  pl.BlockSpec((Sk, LANES), lambda b, i: (b, 0)),
                  pl.BlockSpec((tq, LANES), lambda b, i: (b * nq + i, 0))],
        out_specs=pl.BlockSpec((tq, Hq), lambda b, i: (b * nq + i, 0)),
        out_shape=jax.ShapeDtypeStruct((B * Sq, Hq), BF16),
        scratch_shapes=[pltpu.VMEM((nkb, tq, tk), I32), pltpu.VMEM((nkb, tq, tk), F32),
                        pltpu.VMEM((n_kv, R, LANES), F32), pltpu.VMEM((n_kv, R, LANES), F32),
                        pltpu.VMEM((n_kv, R, LANES), F32), pltpu.VMEM((n_kv, R, LANES), F32),
                        pltpu.VMEM((n_kv, R, tk), F32), pltpu.VMEM((n_kv, R, tk), BF16)],
        compiler_params=pltpu.CompilerParams(
            dimension_semantics=("parallel", "arbitrary"),
            vmem_limit_bytes=_vmem_limit(blk, scratch + n_kv * R * (LANES * 4 + tk * 6) + (8 << 20))),
        name="dsa_attention",
    )(q, k, v, iq, ik2, iw)


def _pack_cache_kernel(*refs, rows, n_new, n_heads):
    cache_ref, o_ref = refs[0], refs[-1]
    dh = cache_ref.shape[3]
    for h in range(n_heads):
        o_ref[0:rows, h * dh:(h + 1) * dh] = cache_ref[0, :, h, :].astype(o_ref.dtype)
    if n_new:
        o_ref[rows:rows + n_new, :] = refs[1][...].astype(o_ref.dtype)
    n_pad = o_ref.shape[0] - rows - n_new
    if n_pad:
        o_ref[rows + n_new:, :] = jnp.zeros((n_pad, o_ref.shape[1]), o_ref.dtype)


def pack_cache(cache, new, rows_pad, layer=None):
    B, rows, H, dh = cache.shape[-4:]
    n_new = 0 if new is None else new.shape[0] // B
    if layer is None:
        in_specs = [pl.BlockSpec((1, rows, H, dh), lambda b: (b, 0, 0, 0))]
    else:
        in_specs = [pl.BlockSpec((None, 1, rows, H, dh), lambda b: (layer, b, 0, 0, 0))]
    args = [cache]
    if n_new:
        in_specs.append(pl.BlockSpec((n_new, H * dh), lambda b: (b, 0)))
        args.append(new)
    return pl.pallas_call(
        functools.partial(_pack_cache_kernel, rows=rows, n_new=n_new, n_heads=H),
        grid=(B,),
        in_specs=in_specs,
        out_specs=pl.BlockSpec((rows_pad, H * dh), lambda b: (b, 0)),
        out_shape=jax.ShapeDtypeStruct((B * rows_pad, H * dh), BF16),
        compiler_params=pltpu.CompilerParams(
            dimension_semantics=("parallel",),
            vmem_limit_bytes=_vmem_limit(rows * H * dh * 4 + rows_pad * H * dh * 2, rows * H * dh * 4)),
        name="pack_cache",
    )(*args)


def _rope_tables(pos, rot, group):
    width = 2 * LANES
    half = rot // 2
    inv = ROPE_THETA ** (-jnp.arange(half, dtype=F32) / half)
    ang = pos.astype(F32)[:, None] * inv[None, :]
    c, s = jnp.cos(ang), jnp.sin(ang)
    lane = np.arange(width) % group
    is_lo = lane < half
    is_hi = (lane >= half) & (lane < rot)
    idx = np.where(is_lo, lane, np.where(is_hi, lane - half, 0))
    cg, sg = c[:, idx], s[:, idx]
    C = jnp.where(is_lo | is_hi, cg, 1.0)
    S = jnp.where(is_lo, -sg, jnp.where(is_hi, sg, 0.0))
    perm = np.zeros((width, width), np.float32)
    dst = np.arange(width)
    src = np.where(is_lo, dst + half, dst - half)
    perm[src[is_lo | is_hi], dst[is_lo | is_hi]] = 1.0
    return half, C, S, jnp.asarray(perm, BF16)


def _pad_cols(w, n):
    return jnp.pad(w, ((0, 0), (0, n - w.shape[1])))


def _tile_gain(g, n):
    return jnp.tile(g.astype(F32), n // g.shape[0])


def _with_cache(cache, new, B, rows_pad, dtype=F32):
    n = new.shape[0] // B
    P = cache.shape[1]
    parts = [cache.reshape(B, P, -1).astype(dtype), new.reshape(B, n, -1).astype(dtype)]
    if rows_pad > P + n:
        parts.append(jnp.zeros((B, rows_pad - P - n, new.shape[1]), dtype))
    return jnp.concatenate(parts, axis=1).reshape(B * rows_pad, -1)


def _round_up(x, m):
    return (x + m - 1) // m * m


def _window_variants(tq, n_prev):
    return [(t * tq, 0) for t in range(n_prev * CHUNK // tq)] + [(n_prev * CHUNK, 0)]


def _mla_weights(w_in, q_lora_norm, kv_lora_norm, w_uq, w_ukv, q_norm, k_norm):
    H = MLA_HEADS
    wq3 = w_uq.reshape(MLA_Q_LORA, H, MLA_NOPE + MLA_ROPE)
    wkv3 = w_ukv.reshape(MLA_KV_LORA, H, MLA_NOPE + MLA_V)
    return dict(
        w_cq=w_in[:, :MLA_Q_LORA].astype(BF16),
        w_ckv=w_in[:, MLA_Q_LORA:MLA_Q_LORA + MLA_KV_LORA].astype(BF16),
        w_kr=_pad_cols(w_in[:, MLA_Q_LORA + MLA_KV_LORA:], LANES).astype(BF16),
        g_cq=q_lora_norm, g_ckv=kv_lora_norm,
        g_kr=_tile_gain(k_norm[MLA_NOPE:], LANES),
        w_qn=wq3[:, :, :MLA_NOPE].reshape(MLA_Q_LORA, H * MLA_NOPE).astype(BF16),
        w_qr=wq3[:, :, MLA_NOPE:].reshape(MLA_Q_LORA, H * MLA_ROPE).astype(BF16),
        g_qn=_tile_gain(q_norm[:MLA_NOPE], H * MLA_NOPE),
        g_qr=_tile_gain(q_norm[MLA_NOPE:], H * MLA_ROPE),
        w_kn=wkv3[:, :, :MLA_NOPE].reshape(MLA_KV_LORA, H * MLA_NOPE).astype(BF16),
        w_v=wkv3[:, :, MLA_NOPE:].reshape(MLA_KV_LORA, H * MLA_V).astype(BF16),
        g_kn=_tile_gain(k_norm[:MLA_NOPE], H * MLA_NOPE),
    )


def _mla_project(h, W, rope64):
    cq = mm(h, W["w_cq"], gain=W["g_cq"], gs=MLA_Q_LORA, norm=True, out_dtype=BF16, name="mla_cq")
    ckv, ckv_bf = mm(h, W["w_ckv"], gain=W["g_ckv"], gs=MLA_KV_LORA, norm=True, out_dtype=(F32, BF16),
                     name="mla_ckv")
    kr = mm(h, W["w_kr"], gain=W["g_kr"], gs=64, norm=True, rope=rope64, name="mla_kr")
    qn = mm(cq, W["w_qn"], gain=W["g_qn"], gs=LANES, norm=True, out_dtype=BF16, name="mla_qn")
    qr = mm(cq, W["w_qr"], gain=W["g_qr"], gs=64, norm=True, rope=rope64, out_dtype=BF16, name="mla_qr")
    return qn, qr, ckv, ckv_bf, kr


def _mla_expand(ckv_bf, W):
    kn = mm(ckv_bf, W["w_kn"], gain=W["g_kn"], gs=LANES, norm=True, out_dtype=BF16, name="mla_kn")
    v = mm(ckv_bf, W["w_v"], out_dtype=BF16, name="mla_v")
    return kn, v


def _mla_attend(qn, qr, kn, kr, v, *, B, Sq, Sk, sk_valid, tq, hps, mode, qbase):
    kr2 = jnp.concatenate([kr[:, :MLA_ROPE], kr[:, :MLA_ROPE]], axis=1).astype(BF16)
    units = tuple(_unit(((h, None),), h, ((h, "slot", 0),), q2=(h // 2, "lo" if h % 2 == 0 else "hi"))
                  for h in range(hps))
    return attention2(qn, kn, v, B=B, Sq=Sq, Sk=Sk, sk_valid=sk_valid, tq=tq, G=MLA_HEADS // hps,
                      qw=hps, kw=hps, ow=hps, units=units, scale=(MLA_NOPE + MLA_ROPE) ** -0.5,
                      mode=mode, qbase=qbase, q2=qr, q2w=hps // 2, k2=kr2, name="mla_attn")


def kernel(x_prompt, x_sample, mem_prompt, cache_a_ckv, cache_a_krope, cache_b_k, cache_b_v, cache_c_k, cache_c_v, cache_c_idxk, cache_d_k, cache_d_v, cache_mem_k, cache_mem_v, state_ffn_conv, norm_mix, norm_xattn, norm_mem, norm_ffn, a_w_in, a_q_lora_norm, a_kv_lora_norm, a_w_uq, a_w_ukv, a_q_norm, a_k_norm, a_w_out, b_w_in, b_q_norm, b_k_norm, b_rel_bias, b_w_out, c_w_in, c_q_norm, c_k_norm, c_idx_k_norm, c_w_out, d_w_in, d_q_norm, d_k_norm, d_sinks, d_w_out, xa_w_q, xa_w_kv, xa_q_norm, xa_k_norm, xa_w_o, ffn_w_up, ffn_conv_w, ffn_conv_b, ffn_w_down):
    B, S, D = x_prompt.shape
    Bd, n, _ = x_sample.shape
    P = cache_a_ckv.shape[2]
    depth = norm_mix.shape[0]
    Fd = ffn_conv_b.shape[1]
    Mp, Ms = B * S, Bd * n
    pos_p = jnp.arange(S, dtype=I32)
    pos_s = jnp.tile(P + jnp.arange(n, dtype=I32), Bd)
    names = ('a_ckv_p', 'a_krope_p', 'b_k_p', 'b_v_p', 'c_k_p', 'c_v_p', 'c_idxk_p', 'd_k_p', 'd_v_p',
             'mem_k_p', 'mem_v_p', 'conv_p', 'a_ckv_s', 'a_krope_s', 'b_k_s', 'b_v_s', 'c_k_s', 'c_v_s',
             'c_idxk_s', 'd_k_s', 'd_v_s', 'conv_s')
    st = {name: [] for name in names}
    xp = x_prompt.reshape(Mp, D)
    xs = x_sample.reshape(Ms, D)
    tq_p = 128

    def rope_pair(rot, group):
        return _rope_tables(pos_p, rot, group), _rope_tables(pos_s, rot, group)

    for i in range(depth):
        m, j = i % 4, i // 4
        hp = rmsnorm_cast(xp, norm_mix[i])
        hs = rmsnorm_cast(xs, norm_mix[i])
        if m == 0:
            W = _mla_weights(a_w_in[j], a_q_lora_norm[j], a_kv_lora_norm[j], a_w_uq[j], a_w_ukv[j],
                             a_q_norm[j], a_k_norm[j])
            rp, rs = rope_pair(MLA_ROPE, 64)
            w_out = a_w_out[j].astype(BF16)
            qn, qr, ckv, ckv_bf, kr = _mla_project(hp, W, rp)
            kn, v = _mla_expand(ckv_bf, W)
            op = _mla_attend(qn, qr, kn, kr, v, B=B, Sq=S, Sk=S, sk_valid=S, tq=_pick(S, (256, 128)),
                             hps=4, mode="causal", qbase=0)
            xp = mm(op, w_out, res=xp, name="mix_out")
            st['a_ckv_p'].append(ckv.reshape(B, S, MLA_KV_LORA))
            st['a_krope_p'].append(kr[:, :MLA_ROPE].reshape(B, S, MLA_ROPE))
            qn, qr, ckv, ckv_bf, kr = _mla_project(hs, W, rs)
            sk_valid = P + n
            Skp = _round_up(sk_valid, LANES)
            ckv_all = _with_cache(cache_a_ckv[j], ckv_bf, Bd, Skp, BF16)
            kr_all = _with_cache(cache_a_krope[j], kr[:, :MLA_ROPE], Bd, Skp)
            kn, v = _mla_expand(ckv_all, W)
            os_ = _mla_attend(qn, qr, kn, kr_all, v, B=Bd, Sq=n, Sk=Skp, sk_valid=sk_valid, tq=n,
                              hps=MLA_HEADS, mode="full", qbase=P)
            xs = mm(os_, w_out, res=xs, name="mix_out")
            st['a_ckv_s'].append(ckv.reshape(Bd, n, MLA_KV_LORA))
            st['a_krope_s'].append(kr[:, :MLA_ROPE].reshape(Bd, n, MLA_ROPE))
        elif m == 1:
            H, dh = BAND_HEADS, D // BAND_HEADS
            w_in = b_w_in[j]
            wq, wk, wv = (w_in[:, t * H * dh:(t + 1) * H * dh].astype(BF16) for t in range(3))
            gq, gk = _tile_gain(b_q_norm[j], H * dh), _tile_gain(b_k_norm[j], H * dh)
            w_out = b_w_out[j].astype(BF16)
            npv = BAND_PREV_CHUNKS
            def band_units(nh):
                return tuple(_unit(((h, None),), h, ((h, "slot", 0),), bias=h) for h in range(nh))

            q = mm(hp, wq, gain=gq, gs=dh, norm=True, out_dtype=BF16, name="band_q")
            k, k_bf = mm(hp, wk, gain=gk, gs=dh, norm=True, out_dtype=(F32, BF16), name="band_k")
            v, v_bf = mm(hp, wv, out_dtype=(F32, BF16), name="band_v")
            win = npv * CHUNK + tq_p
            bias = band_bias(b_rel_bias[j], tq_p, win, _window_variants(tq_p, npv), npv, win)
            hps = 8
            op = attention2(q, k_bf, v_bf, B=B, Sq=S, Sk=S, sk_valid=S, tq=tq_p, G=H // hps, qw=hps,
                            kw=hps, ow=hps, units=band_units(hps), scale=dh ** -0.5, mode="window",
                            win=win, n_prev=npv, masked=False, bias=bias, name="band_attn")
            xp = mm(op, w_out, res=xp, name="mix_out")
            r = min(npv * CHUNK, S)
            st['b_k_p'].append(k.reshape(B, S, H, dh)[:, S - r:])
            st['b_v_p'].append(v.reshape(B, S, H, dh)[:, S - r:])
            q = mm(hs, wq, gain=gq, gs=dh, norm=True, out_dtype=BF16, name="band_q")
            k = mm(hs, wk, gain=gk, gs=dh, norm=True, name="band_k")
            v = mm(hs, wv, name="band_v")
            rows = cache_b_k.shape[2]
            sk_valid = rows + n
            Skp = _round_up(sk_valid, LANES)
            k_all = pack_cache(cache_b_k[j], k, Skp)
            v_all = pack_cache(cache_b_v[j], v, Skp)
            bias = band_bias(b_rel_bias[j], n, Skp, [(P, P - rows)], npv, sk_valid)
            os_ = attention2(q, k_all, v_all, B=Bd, Sq=n, Sk=Skp, sk_valid=sk_valid, tq=n, G=1, qw=H,
                             kw=H, ow=H, units=band_units(H), scale=dh ** -0.5, mode="window", win=Skp,
                             n_prev=npv, masked=False, qbase=P, kbase=P - rows, bias=bias,
                             name="band_attn")
            xs = mm(os_, w_out, res=xs, name="mix_out")
            st['b_k_s'].append(k.reshape(Bd, n, H, dh))
            st['b_v_s'].append(v.reshape(Bd, n, H, dh))
        elif m == 2:
            Hq, Hk, dh = DSA_HEADS, DSA_KV_HEADS, DSA_HEAD_DIM
            w_in = c_w_in[j]
            oq, ok_, ov = Hq * dh, Hq * dh + Hk * dh, Hq * dh + 2 * Hk * dh
            oiq = ov + IDX_HEADS * IDX_DIM
            oik = oiq + IDX_DIM
            wq = w_in[:, :oq].astype(BF16)
            wk = w_in[:, oq:ok_].astype(BF16)
            wv = w_in[:, ok_:ov].astype(BF16)
            wiq = w_in[:, ov:oiq].astype(BF16)
            wik = _pad_cols(w_in[:, oiq:oik], LANES).astype(BF16)
            wiw = _pad_cols(w_in[:, oik:], LANES).astype(BF16)
            gq, gk = _tile_gain(c_q_norm[j], oq), _tile_gain(c_k_norm[j], Hk * dh)
            gik = _tile_gain(c_idx_k_norm[j], LANES)
            giw = jnp.full((LANES,), IDX_HEADS ** -0.5 * IDX_DIM ** -0.5, F32)
            w_out = c_w_out[j].astype(BF16)
            rp128, rs128 = rope_pair(dh // 4, LANES)
            rp64, rs64 = rope_pair(IDX_DIM // 4, 64)

            def project(h, r128, r64):
                q = mm(h, wq, gain=gq, gs=dh, norm=True, rope=r128, out_dtype=BF16, name="dsa_q")
                k, k_bf = mm(h, wk, gain=gk, gs=dh, norm=True, rope=r128, out_dtype=(F32, BF16), name="dsa_k")
                v, v_bf = mm(h, wv, out_dtype=(F32, BF16), name="dsa_v")
                iq = mm(h, wiq, rope=r64, out_dtype=BF16, name="dsa_iq")
                ik = mm(h, wik, gain=gik, gs=64, norm=True, rope=r64, name="dsa_ik")
                iw = mm(h, wiw, gain=giw, name="dsa_iw")
                return q, k, v, k_bf, v_bf, iq, ik, iw

            q, k, v, k_bf, v_bf, iq, ik, iw = project(hp, rp128, rp64)
            ik2 = jnp.concatenate([ik[:, :IDX_DIM], ik[:, :IDX_DIM]], axis=1).astype(BF16)
            op = dsa_attention2(q, k_bf, v_bf, iq, ik2, iw, B=B, Sq=S, Sk=S, sk_valid=S,
                                tq=_pick(S, (256, 128)), tk=_pick(S, (512, 256, 128)), qbase=0,
                                n_sel=min(DSA_TOPK, S // 4))
            xp = mm(op, w_out, res=xp, name="mix_out")
            st['c_k_p'].append(k.reshape(B, S, Hk, dh))
            st['c_v_p'].append(v.reshape(B, S, Hk, dh))
            st['c_idxk_p'].append(ik[:, :IDX_DIM].reshape(B, S, IDX_DIM))
            q, k, v, _, _, iq, ik, iw = project(hs, rs128, rs64)
            sk_valid = P + n
            Skp = _round_up(sk_valid, LANES)
            k_all = pack_cache(cache_c_k[j], k, Skp)
            v_all = pack_cache(cache_c_v[j], v, Skp)
            ik_all = _with_cache(cache_c_idxk[j], ik[:, :IDX_DIM], Bd, Skp, BF16)
            ik2 = jnp.concatenate([ik_all, ik_all], axis=1)
            os_ = dsa_attention2(q, k_all, v_all, iq, ik2, iw, B=Bd, Sq=n, Sk=Skp, sk_valid=sk_valid,
                                 tq=n, tk=_pick(Skp, (512, 384, 256, 128)), qbase=P,
                                 n_sel=min(DSA_TOPK, sk_valid // 4))
            xs = mm(os_, w_out, res=xs, name="mix_out")
            st['c_k_s'].append(k.reshape(Bd, n, Hk, dh))
            st['c_v_s'].append(v.reshape(Bd, n, Hk, dh))
            st['c_idxk_s'].append(ik[:, :IDX_DIM].reshape(Bd, n, IDX_DIM))
        else:
            Hq, Hk, dh = SWA_HEADS, SWA_KV_HEADS, SWA_HEAD_DIM
            grp = Hq // Hk
            npairs = Hk // 2
            perm = np.array([[[2 * p * grp + t, (2 * p + 1) * grp + t] for t in range(grp)]
                             for p in range(npairs)]).reshape(-1)
            w_in = d_w_in[j]
            nqc, nkc = Hq * dh, Hk * dh
            wq = w_in[:, :nqc].reshape(D, Hq, dh)[:, perm].reshape(D, nqc).astype(BF16)
            wk = w_in[:, nqc:nqc + nkc].astype(BF16)
            wv = w_in[:, nqc + nkc:].astype(BF16)
            gq, gk = _tile_gain(d_q_norm[j], nqc), _tile_gain(d_k_norm[j], nkc)
            w_out = d_w_out[j].reshape(Hq, dh, D)[perm].reshape(nqc, D).astype(BF16)
            sinks = d_sinks[j][perm]
            rp, rs = rope_pair(dh // 4, 64)
            npv = SWA_PREV_CHUNKS
            def swa_units(npr):
                us = []
                for p in range(npr):
                    slots = tuple((p * grp + t, hf) for t in range(grp) for hf in ("lo", "hi"))
                    outs = tuple((p * grp + t, "pair", 2 * t, 2 * t + 1) for t in range(grp))
                    us.append(_unit(slots, p, outs, sinks=range(2 * grp * p, 2 * grp * (p + 1))))
                return tuple(us)

            def project(h, r):
                q = mm(h, wq, gain=gq, gs=64, norm=True, rope=r, out_dtype=BF16, name="swa_q")
                k, k_bf = mm(h, wk, gain=gk, gs=64, norm=True, rope=r, out_dtype=(F32, BF16), name="swa_k")
                v, v_bf = mm(h, wv, out_dtype=(F32, BF16), name="swa_v")
                return q, k, v, k_bf, v_bf

            q, k, v, k_bf, v_bf = project(hp, rp)
            win = npv * CHUNK + tq_p
            pps = npairs
            op = attention2(q, k_bf, v_bf, B=B, Sq=S, Sk=S, sk_valid=S, tq=tq_p, G=npairs // pps,
                            qw=pps * grp, kw=pps, ow=pps * grp, units=swa_units(pps), scale=dh ** -0.5,
                            mode="window", win=win, n_prev=npv, sinks=sinks,
                            sinks_per_group=2 * grp * pps, name="swa_attn")
            xp = mm(op, w_out, res=xp, name="mix_out")
            r = min(npv * CHUNK, S)
            st['d_k_p'].append(k.reshape(B, S, Hk, dh)[:, S - r:])
            st['d_v_p'].append(v.reshape(B, S, Hk, dh)[:, S - r:])
            q, k, v, _, _ = project(hs, rs)
            rows = cache_d_k.shape[2]
            sk_valid = rows + n
            Skp = _round_up(sk_valid, LANES)
            k_all = _with_cache(cache_d_k[j].reshape(Bd, rows, nkc), k, Bd, Skp, BF16)
            v_all = _with_cache(cache_d_v[j].reshape(Bd, rows, nkc), v, Bd, Skp, BF16)
            os_ = attention2(q, k_all, v_all, B=Bd, Sq=n, Sk=Skp, sk_valid=sk_valid, tq=n, G=1,
                             qw=npairs * grp, kw=npairs, ow=npairs * grp, units=swa_units(npairs),
                             scale=dh ** -0.5, mode="window", win=Skp, n_prev=npv, qbase=P,
                             kbase=P - rows, sinks=sinks, sinks_per_group=2 * grp * npairs,
                             name="swa_attn")
            xs = mm(os_, w_out, res=xs, name="mix_out")
            st['d_k_s'].append(k.reshape(Bd, n, Hk, dh))
            st['d_v_s'].append(v.reshape(Bd, n, Hk, dh))

        Hx, dx = XA_HEADS, XA_HEAD_DIM
        Mm = mem_prompt.shape[1]
        hm = rmsnorm_cast(mem_prompt.reshape(B * Mm, D), norm_mem[i])
        w_kv = xa_w_kv[i]
        mk, mk_bf = mm(hm, w_kv[:, :Hx * dx].astype(BF16), gain=_tile_gain(xa_k_norm[i], Hx * dx), gs=dx,
                       norm=True, out_dtype=(F32, BF16), name="xa_k")
        mv, mv_bf = mm(hm, w_kv[:, Hx * dx:].astype(BF16), out_dtype=(F32, BF16), name="xa_v")
        st['mem_k_p'].append(mk.reshape(B, Mm, Hx, dx))
        st['mem_v_p'].append(mv.reshape(B, Mm, Hx, dx))
        wxq = xa_w_q[i].astype(BF16)
        gxq = _tile_gain(xa_q_norm[i], Hx * dx)
        wxo = xa_w_o[i].astype(BF16)
        xa_units = tuple(_unit(((h, None),), h, ((h, "slot", 0),)) for h in range(Hx))

        def cross(x, mk2, mv2, Bx, Sx, tq):
            qx = mm(x, wxq, in_gain=norm_xattn[i], gain=gxq, gs=dx, norm=True, out_dtype=BF16, name="xa_q")
            ox = attention2(qx, mk2, mv2, B=Bx, Sq=Sx, Sk=Mm, sk_valid=Mm, tq=tq, G=1, qw=Hx, kw=Hx,
                            ow=Hx, units=xa_units, scale=dx ** -0.5, mode="full", masked=False,
                            name="xattn")
            return mm(ox, wxo, res=x, name="xa_out")

        xp = cross(xp, mk_bf, mv_bf, B, S, _pick(S, (512, 256, 128)))
        xs = cross(xs, pack_cache(cache_mem_k, None, Mm, layer=i), pack_cache(cache_mem_v, None, Mm, layer=i),
                   Bd, n, n)

        w_up = ffn_w_up[i].astype(BF16)
        w_dn = ffn_w_down[i].astype(BF16)
        act, conv_p = ffn_up(xp, norm_ffn[i], w_up, ffn_conv_w[i], ffn_conv_b[i],
                             jnp.zeros((B, CONV_W - 1, Fd), F32), n_seq=B, seq_rows=S)
        xp = mm(act, w_dn, res=xp, name="ffn_down")
        act, conv_s = ffn_up(xs, norm_ffn[i], w_up, ffn_conv_w[i], ffn_conv_b[i], state_ffn_conv[i],
                             n_seq=Bd, seq_rows=n)
        xs = mm(act, w_dn, res=xs, name="ffn_down")
        st['conv_p'].append(conv_p)
        st['conv_s'].append(conv_s)

    order = ('a_ckv_p', 'a_krope_p', 'b_k_p', 'b_v_p', 'c_k_p', 'c_v_p', 'c_idxk_p', 'd_k_p', 'd_v_p',
             'mem_k_p', 'mem_v_p', 'conv_p', 'a_ckv_s', 'a_krope_s', 'b_k_s', 'b_v_s', 'c_k_s', 'c_v_s',
             'c_idxk_s', 'd_k_s', 'd_v_s', 'conv_s')
    return (xp.reshape(B, S, D), xs.reshape(Bd, n, D)) + tuple(jnp.stack(st[nm]) for nm in order)
```

```python
import functools

import numpy as np
import jax
import jax.numpy as jnp
from jax import lax
from jax.experimental import pallas as pl
from jax.experimental.pallas import tpu as pltpu

F32, BF16, I32 = jnp.float32, jnp.bfloat16, jnp.int32

CHUNK = 64
CHUNK_SHIFT = 6
ROPE_THETA = 500000.0
EPS = 1e-6
NEG_INF = -1e30
MLA_HEADS, MLA_NOPE, MLA_ROPE, MLA_V = 16, 128, 64, 128
MLA_Q_LORA, MLA_KV_LORA = 512, 512
BAND_HEADS, BAND_PREV_CHUNKS, MAX_REL = 16, 8, 128
DSA_HEADS, DSA_KV_HEADS, DSA_HEAD_DIM = 16, 4, 128
IDX_HEADS, IDX_DIM, DSA_TOPK = 16, 64, 256
SWA_HEADS, SWA_KV_HEADS, SWA_HEAD_DIM, SWA_PREV_CHUNKS = 32, 8, 64, 2
XA_HEADS, XA_HEAD_DIM = 4, 128
CONV_W = 3

LANES = 128
VMEM_LIMIT_MAX = 56 * 1024 * 1024
VMEM_LIMIT_MIN = 32 * 1024 * 1024

M_INIT = -3.0e38
INT_MIN = -2 ** 31
BIG_COL = 2 ** 30
LOG2E = 1.4426950408889634
SOFTMAX_ROWS = 64


def _vmem_limit(block_bytes, temp_bytes=0):
    need = 2 * block_bytes + temp_bytes + (4 << 20)
    return int(min(max(need, VMEM_LIMIT_MIN), VMEM_LIMIT_MAX))


def _pick(n, cands):
    for c in cands:
        if n % c == 0:
            return c
    raise ValueError(f"no tile for {n}")


def _rmsnorm_kernel(x_ref, g_ref, o_ref):
    x = x_ref[...]
    ms = jnp.mean(x * x, axis=-1, keepdims=True)
    o_ref[...] = (x * lax.rsqrt(ms + EPS) * g_ref[...]).astype(o_ref.dtype)


def rmsnorm_cast(x, g):
    M, K = x.shape
    tm = _pick(M, (512, 256, 128, 64, 16, 8))
    return pl.pallas_call(
        _rmsnorm_kernel,
        grid=(M // tm,),
        in_specs=[pl.BlockSpec((tm, K), lambda i: (i, 0)), pl.BlockSpec((1, K), lambda i: (0, 0))],
        out_specs=pl.BlockSpec((tm, K), lambda i: (i, 0)),
        out_shape=jax.ShapeDtypeStruct((M, K), BF16),
        compiler_params=pltpu.CompilerParams(
            dimension_semantics=("parallel",), vmem_limit_bytes=_vmem_limit(tm * K * 6, tm * K * 8)),
        name="rmsnorm_cast",
    )(x, g.reshape(1, K).astype(F32))


def _group_norm_rope(y, gain_ref, rope_refs, gs, norm, half):
    tm, tn = y.shape
    if norm and gs == tn:
        ss = jnp.sum(y * y, axis=-1, keepdims=True)
        return y * lax.rsqrt(ss * (1.0 / gs) + EPS) * gain_ref[...]
    outs = []
    for c in range(tn // LANES):
        yc = y[:, c * LANES:(c + 1) * LANES]
        if norm:
            yy = yc * yc
            if gs == LANES:
                ss = jnp.sum(yy, axis=-1, keepdims=True)
            else:
                lo = lax.broadcasted_iota(I32, yc.shape, 1) < gs
                s_lo = jnp.sum(jnp.where(lo, yy, 0.0), axis=-1, keepdims=True)
                s_hi = jnp.sum(jnp.where(lo, 0.0, yy), axis=-1, keepdims=True)
                ss = jnp.where(lo, s_lo, s_hi)
            yc = yc * lax.rsqrt(ss * (1.0 / gs) + EPS) * gain_ref[:, c * LANES:(c + 1) * LANES]
        elif gain_ref is not None:
            yc = yc * gain_ref[:, c * LANES:(c + 1) * LANES]
        outs.append(yc)
    y = outs[0] if len(outs) == 1 else jnp.concatenate(outs, axis=1)
    if not half:
        return y
    c_ref, s_ref, p_ref = rope_refs
    pw = p_ref.shape[0]
    outs = []
    for b in range(tn // pw):
        yb = y[:, b * pw:(b + 1) * pw]
        hi = yb.astype(BF16)
        lo = (yb - hi.astype(F32)).astype(BF16)
        partner = (jnp.dot(hi, p_ref[...], preferred_element_type=F32)
                   + jnp.dot(lo, p_ref[...], preferred_element_type=F32))
        outs.append(yb * c_ref[...] + partner * s_ref[...])
    return outs[0] if len(outs) == 1 else jnp.concatenate(outs, axis=1)


def _normed_rows(x_ref, g_ref, h_ref):
    @pl.when(pl.program_id(1) == 0)
    def _():
        x = x_ref[...]
        ms = jnp.mean(x * x, axis=-1, keepdims=True)
        h_ref[...] = (x * lax.rsqrt(ms + EPS) * g_ref[...]).astype(h_ref.dtype)
    return h_ref[...]


def _mm_kernel(*refs, has_gain, gs, norm, half, has_res, has_in_norm, n_out):
    a_ref, w_ref = refs[0], refs[1]
    pos = 2
    in_gain_ref = gain_ref = rope_refs = res_ref = None
    if has_in_norm:
        in_gain_ref = refs[pos]
        pos += 1
    if has_gain:
        gain_ref = refs[pos]
        pos += 1
    if half:
        rope_refs = refs[pos:pos + 3]
        pos += 3
    if has_res:
        res_ref = refs[pos]
        pos += 1
    a = _normed_rows(a_ref, in_gain_ref, refs[pos + n_out]) if has_in_norm else a_ref[...]
    y = jnp.dot(a, w_ref[...], preferred_element_type=F32)
    if has_gain or half:
        y = _group_norm_rope(y, gain_ref, rope_refs, gs, norm, half)
    if has_res:
        y = res_ref[...] + y
    for o_ref in refs[pos:pos + n_out]:
        o_ref[...] = y.astype(o_ref.dtype)


def mm(a, w, *, gain=None, gs=LANES, norm=False, rope=None, res=None, out_dtype=F32, tn=None,
       in_gain=None, name="mm"):
    M, K = a.shape
    N = w.shape[1]
    assert a.dtype == (BF16 if in_gain is None else F32) and w.dtype == BF16 and N % LANES == 0
    tm_c = (1024, 512, 256, 128, 64, 16, 8)
    if tn is None:
        if norm and gs > LANES:
            tn = gs
        elif K <= 512:
            tn = _pick(N, (2048, 1024, 512, 256, 128))
            if res is not None:
                tm_c = tm_c[1:]
        elif K <= 2048:
            tn = _pick(N, (1024, 512, 256, 128))
        else:
            tn = _pick(N, (512, 256, 128))
    rows_period = M if rope is None else rope[1].shape[0]
    tm = _pick(np.gcd(M, rows_period), tm_c)
    half = 0
    in_specs = [pl.BlockSpec((tm, K), lambda i, j: (i, 0)), pl.BlockSpec((K, tn), lambda i, j: (0, j))]
    args = [a, w]
    blk = tm * K * 2 + K * tn * 2 + tm * tn * 4
    scratch = []
    if in_gain is not None:
        in_specs.append(pl.BlockSpec((1, K), lambda i, j: (0, 0)))
        args.append(in_gain.reshape(1, K).astype(F32))
        scratch.append(pltpu.VMEM((tm, K), BF16))
        blk += tm * K * 4
    if gain is not None:
        in_specs.append(pl.BlockSpec((1, tn), lambda i, j: (0, j)))
        args.append(gain.reshape(1, N).astype(F32))
    if rope is not None:
        half, tab_c, tab_s, perm = rope
        if tn % perm.shape[0] != 0:
            tab_c, tab_s, perm = tab_c[:, :LANES], tab_s[:, :LANES], perm[:LANES, :LANES]
        pw = perm.shape[0]
        assert tn % pw == 0
        nrb = rows_period // tm
        for t in (tab_c, tab_s):
            in_specs.append(pl.BlockSpec((tm, pw), lambda i, j, nrb=nrb: (i % nrb, 0)))
            args.append(t)
        in_specs.append(pl.BlockSpec((pw, pw), lambda i, j: (0, 0)))
        args.append(perm)
        blk += 2 * tm * pw * 4 + pw * pw * 2
    if res is not None:
        in_specs.append(pl.BlockSpec((tm, tn), lambda i, j: (i, j)))
        args.append(res)
        blk += tm * tn * 4
    dtypes = out_dtype if isinstance(out_dtype, tuple) else (out_dtype,)
    kern = functools.partial(_mm_kernel, has_gain=gain is not None, gs=gs, norm=norm, half=half,
                             has_res=res is not None, has_in_norm=in_gain is not None, n_out=len(dtypes))
    outs = pl.pallas_call(
        kern,
        grid=(M // tm, N // tn),
        in_specs=in_specs,
        out_specs=[pl.BlockSpec((tm, tn), lambda i, j: (i, j)) for _ in dtypes],
        out_shape=[jax.ShapeDtypeStruct((M, N), dt) for dt in dtypes],
        scratch_shapes=scratch,
        compiler_params=pltpu.CompilerParams(
            dimension_semantics=("parallel", "arbitrary"),
            vmem_limit_bytes=_vmem_limit(blk + (len(dtypes) - 1) * tm * tn * 4, 3 * tm * tn * 4)),
        name=name,
    )(*args)
    return tuple(outs) if isinstance(out_dtype, tuple) else outs[0]


def _ffn_up_kernel(x_ref, gx_ref, wg_ref, wv_ref, cw_ref, cb_ref, prev_ref, act_ref, last_ref, carry_ref,
                   h_ref, *, tiles_per_seq, seq_rows):
    i = pl.program_id(0)
    h = _normed_rows(x_ref, gx_ref, h_ref)
    g = jnp.dot(h, wg_ref[...], preferred_element_type=F32)
    val = jnp.dot(h, wv_ref[...], preferred_element_type=F32)
    tm, tn = g.shape
    row = lax.broadcasted_iota(I32, (tm, tn), 0)
    g1 = pltpu.roll(g, 1, 0)
    g2 = pltpu.roll(g, 2, 0)
    if tiles_per_seq >= 1:
        first = (i % tiles_per_seq) == 0
        j = pl.program_id(1)
        p_prev = prev_ref[0]
        p_carry = carry_ref[j]
        p2 = jnp.where(first, p_prev[0:1, :], p_carry[6:7, :])
        p1 = jnp.where(first, p_prev[1:2, :], p_carry[7:8, :])
        g1 = jnp.where(row == 0, p1, g1)
        g2 = jnp.where(row == 0, p2, jnp.where(row == 1, p1, g2))
        carry_ref[j] = g[tm - 8:tm, :]
        last_ref[0] = g[tm - 8:tm, :]
    else:
        rmod = row % seq_rows
        g1 = jnp.where(rmod == 0, prev_ref[0], g1)
        g2 = jnp.where(rmod < 2, prev_ref[1], g2)
        last_ref[...] = g
    c = cw_ref[0:1, :] * g2 + cw_ref[1:2, :] * g1 + cw_ref[2:3, :] * g + cb_ref[...]
    act_ref[...] = (c * (1.0 / (1.0 + jnp.exp(-c))) * val).astype(act_ref.dtype)


def ffn_up(x, gx, w_up, conv_w, conv_b, prev, *, n_seq, seq_rows):
    M, D = x.shape
    Fd = w_up.shape[1] // 2
    tn = _pick(Fd, (512, 256, 128))
    nj = Fd // tn
    if seq_rows >= 256:
        tm = _pick(seq_rows, (1024, 512, 256))
        tps = seq_rows // tm
        prev_arg = prev
        prev_spec = pl.BlockSpec((1, 2, tn), lambda i, j: (i // tps, 0, j))
        last_shape = jax.ShapeDtypeStruct((M // tm, 8, Fd), F32)
        last_spec = pl.BlockSpec((1, 8, tn), lambda i, j: (i, 0, j))
    else:
        tm = M
        tps = 0
        z = jnp.zeros((n_seq, seq_rows - 1, Fd), F32)
        inj1 = jnp.concatenate([prev[:, 1:2], z], axis=1).reshape(M, Fd)
        inj2 = jnp.concatenate([prev[:, 0:2], z[:, 1:]], axis=1).reshape(M, Fd)
        prev_arg = jnp.stack([inj1, inj2])
        prev_spec = pl.BlockSpec((2, tm, tn), lambda i, j: (0, i, j))
        last_shape = jax.ShapeDtypeStruct((M, Fd), F32)
        last_spec = pl.BlockSpec((tm, tn), lambda i, j: (i, j))
    kern = functools.partial(_ffn_up_kernel, tiles_per_seq=tps, seq_rows=seq_rows)
    blk = tm * D * 5 + 2 * D * tn * 2 + tm * tn * 2 + 4 * tn * 4 + 2 * tm * tn * 4
    act, last = pl.pallas_call(
        kern,
        grid=(M // tm, nj),
        in_specs=[pl.BlockSpec((tm, D), lambda i, j: (i, 0)),
                  pl.BlockSpec((1, D), lambda i, j: (0, 0)),
                  pl.BlockSpec((D, tn), lambda i, j: (0, j)),
                  pl.BlockSpec((D, tn), lambda i, j, nj=nj: (0, j + nj)),
                  pl.BlockSpec((CONV_W, tn), lambda i, j: (0, j)),
                  pl.BlockSpec((1, tn), lambda i, j: (0, j)),
                  prev_spec],
        out_specs=[pl.BlockSpec((tm, tn), lambda i, j: (i, j)), last_spec],
        out_shape=[jax.ShapeDtypeStruct((M, Fd), BF16), last_shape],
        scratch_shapes=[pltpu.VMEM((nj, 8, tn), F32), pltpu.VMEM((tm, D), BF16)],
        compiler_params=pltpu.CompilerParams(
            dimension_semantics=("arbitrary", "arbitrary"),
            vmem_limit_bytes=_vmem_limit(blk, 6 * tm * tn * 4)),
        name="ffn_up",
    )(x, gx.reshape(1, D).astype(F32), w_up, w_up, conv_w.astype(F32), conv_b.reshape(1, Fd).astype(F32),
      prev_arg)
    if tps == 0:
        last = last.reshape(n_seq, seq_rows, Fd)[:, seq_rows - 2:]
    else:
        last = last[tps - 1::tps, 6:8]
    return act, last


def _unit(slots, k_tile, outs, q2=None, bias=None, sinks=()):
    return (tuple(slots), k_tile, tuple(outs), q2, bias, tuple(sinks))


def _attn2_kernel(*refs, tq, nq, units, mode, win, sk, sk_valid, n_prev, masked, qbase, kbase, scale,
                  has_q2, has_bias, has_sinks, sinks_per_group, bias_is_mask):
    q_ref, k_ref, v_ref = refs[0], refs[1], refs[2]
    pos = 3
    q2_ref = k2_ref = bias_ref = sinks_ref = None
    if has_q2:
        q2_ref, k2_ref = refs[pos], refs[pos + 1]
        pos += 2
    if has_bias:
        bias_ref = refs[pos]
        pos += 1
    if has_sinks:
        sinks_ref = refs[pos]
        pos += 1
    o_ref = refs[pos]

    g = pl.program_id(1)
    i = pl.program_id(2)
    q0 = i * tq
    lane = lax.broadcasted_iota(I32, (tq, LANES), 1)
    half_mask = {"lo": jnp.where(lane < 64, 1.0, 0.0).astype(BF16),
                 "hi": jnp.where(lane >= 64, 1.0, 0.0).astype(BF16)}
    qc1 = (qbase + q0 + lax.broadcasted_iota(I32, (tq, 1), 0)) >> CHUNK_SHIFT

    def compute(r0, L, mask_from=0):
        if masked:
            krow = r0 + mask_from + lax.broadcasted_iota(I32, (1, L - mask_from), 1)
            kc = (kbase + krow) >> CHUNK_SHIFT
        rows = pl.ds(r0, L) if not isinstance(r0, int) else slice(r0, r0 + L)
        for u, (slots, k_tile, outs, q2, bias_idx, sink_idx) in enumerate(units):
            ns = len(slots)
            qs = []
            for (t, mk) in slots:
                qt = q_ref[:, t * LANES:(t + 1) * LANES]
                if mk is not None:
                    qt = qt * half_mask[mk]
                if q2 is not None:
                    q2t = q2_ref[:, q2[0] * LANES:(q2[0] + 1) * LANES] * half_mask[q2[1]]
                    qt = jnp.concatenate([qt, q2t], axis=1)
                qs.append(qt)
            Q = qs[0] if ns == 1 else jnp.concatenate(qs, axis=0)
            kb = k_ref[rows, k_tile * LANES:(k_tile + 1) * LANES]
            if q2 is not None:
                kb = jnp.concatenate([kb, k2_ref[rows, :]], axis=1)
            vb = v_ref[rows, k_tile * LANES:(k_tile + 1) * LANES]
            log2 = bias_idx is None or bias_is_mask
            expf = jnp.exp2 if log2 else jnp.exp
            s = lax.dot_general(Q, kb, (((1,), (1,)), ((), ())), preferred_element_type=F32) * (
                scale * LOG2E if log2 else scale)
            if bias_idx is not None:
                bt = bias_ref[0, bias_idx]
                s = s + (bt if ns == 1 else jnp.concatenate([bt] * ns, axis=0))
            if masked:
                qc = qc1 if ns == 1 else jnp.concatenate([qc1] * ns, axis=0)
                sm = s[:, mask_from:]
                sm = jnp.where(kc <= qc, sm, NEG_INF)
                if n_prev is not None:
                    sm = jnp.where(kc >= qc - n_prev, sm, NEG_INF)
                if sk_valid < sk:
                    sm = jnp.where(krow < sk_valid, sm, NEG_INF)
                s = sm if mask_from == 0 else jnp.concatenate([s[:, :mask_from], sm], axis=1)
            m = jnp.max(s, axis=-1, keepdims=True)
            if sink_idx:
                sk_rows = [jnp.full((tq, 1), sinks_ref[g * sinks_per_group + si], F32) for si in sink_idx]
                sink = sk_rows[0] if ns == 1 else jnp.concatenate(sk_rows, axis=0)
                sink = sink * LOG2E if log2 else sink
                m = jnp.maximum(m, sink)
            p = expf(s - m)
            l = jnp.sum(p, axis=-1, keepdims=True)
            if sink_idx:
                l = l + expf(sink - m)
            o = jnp.dot(p.astype(BF16), vb, preferred_element_type=F32) / l
            for spec in outs:
                t = spec[0]
                if spec[1] == "slot":
                    ot = o[spec[2] * tq:(spec[2] + 1) * tq]
                else:
                    ot = jnp.where(lane < 64, o[spec[2] * tq:(spec[2] + 1) * tq],
                                   o[spec[3] * tq:(spec[3] + 1) * tq])
                o_ref[:, t * LANES:(t + 1) * LANES] = ot.astype(o_ref.dtype)

    if mode == "window":
        lo_row = ((qbase + q0) >> CHUNK_SHIFT) * CHUNK - n_prev * CHUNK - kbase
        compute(pl.multiple_of(jnp.clip(lo_row, 0, sk - win), 16), win)
    elif mode == "full":
        compute(0, sk)
    else:
        for c in range(nq):
            pl.when(i == c)(functools.partial(compute, 0, min(sk, (c + 1) * tq), c * tq))


def attention2(q, k, v, *, B, Sq, Sk, sk_valid, tq, G, qw, kw, ow, units, scale, mode, win=None,
               n_prev=None, masked=True, qbase=0, kbase=0, q2=None, q2w=0, k2=None, bias=None,
               bias_is_mask=False, sinks=None, sinks_per_group=0, name="attn"):
    nq = Sq // tq
    assert mode != "causal" or (qbase == 0 and kbase == 0 and tq % CHUNK == 0)
    in_specs = [pl.BlockSpec((tq, qw * LANES), lambda b, g, i: (b * nq + i, g)),
                pl.BlockSpec((Sk, kw * LANES), lambda b, g, i: (b, g)),
                pl.BlockSpec((Sk, kw * LANES), lambda b, g, i: (b, g))]
    args = [q, k, v]
    blk = tq * qw * LANES * 2 + 2 * Sk * kw * LANES * 2 + tq * ow * LANES * 2
    if q2 is not None:
        in_specs += [pl.BlockSpec((tq, q2w * LANES), lambda b, g, i: (b * nq + i, g)),
                     pl.BlockSpec((Sk, LANES), lambda b, g, i: (b, 0))]
        args += [q2, k2]
        blk += tq * q2w * LANES * 2 + Sk * LANES * 2
    if bias is not None:
        nv, nbh = bias.shape[0], bias.shape[1] // G
        in_specs.append(pl.BlockSpec((1, nbh, tq, win),
                                     lambda b, g, i, nv=nv: (jnp.minimum(i, nv - 1), g, 0, 0)))
        args.append(bias)
        blk += nbh * tq * win * 4
    if sinks is not None:
        in_specs.append(pl.BlockSpec(memory_space=pltpu.SMEM))
        args.append(sinks.astype(F32))
    L = win if mode == "window" else Sk
    temp = sum(len(u[0]) for u in units) * tq * L * 12
    kern = functools.partial(
        _attn2_kernel, tq=tq, nq=nq, units=tuple(units), mode=mode, win=win, sk=Sk, sk_valid=sk_valid,
        n_prev=n_prev, masked=masked, qbase=qbase, kbase=kbase, scale=scale, has_q2=q2 is not None,
        has_bias=bias is not None, has_sinks=sinks is not None, sinks_per_group=sinks_per_group,
        bias_is_mask=bias_is_mask)
    return pl.pallas_call(
        kern,
        grid=(B, G, nq),
        in_specs=in_specs,
        out_specs=pl.BlockSpec((tq, ow * LANES), lambda b, g, i: (b * nq + i, g)),
        out_shape=jax.ShapeDtypeStruct((B * Sq, G * ow * LANES), BF16),
        compiler_params=pltpu.CompilerParams(
            dimension_semantics=("parallel", "parallel", "arbitrary"),
            vmem_limit_bytes=_vmem_limit(blk, temp)),
        name=name,
    )(*args)


def _toeplitz_kernel(ab_ref, e_ref, o_ref, *, tq, win, n_prev, n_valid, has_table):
    v = pl.program_id(0)
    if has_table:
        wp = e_ref.shape[-1]
        x = jnp.broadcast_to(e_ref[0, 0, 0:1, :], (tq, wp))
        y = pltpu.roll(x, 0, 1, stride=1, stride_axis=0)[:, :win]
    else:
        y = jnp.zeros((tq, win), F32)
    qc = (ab_ref[v, 0] + lax.broadcasted_iota(I32, (tq, win), 0)) >> CHUNK_SHIFT
    kj = lax.broadcasted_iota(I32, (tq, win), 1)
    kc = (ab_ref[v, 1] + kj) >> CHUNK_SHIFT
    y = jnp.where(kc <= qc, y, NEG_INF)
    y = jnp.where(kc >= qc - n_prev, y, NEG_INF)
    if n_valid < win:
        y = jnp.where(kj < n_valid, y, NEG_INF)
    o_ref[0, 0] = y


def band_bias(table, tq, win, pos, n_prev, n_valid):
    nv = len(pos)
    ab = np.array([[q - (k // CHUNK) * CHUNK, k - (k // CHUNK) * CHUNK] for q, k in pos], np.int32)
    has_table = table is not None
    H = table.shape[1] if has_table else 1
    wp = _round_up(win + tq - 1, LANES)
    if has_table:
        m = np.arange(wp)
        m = np.where(m < win, m, m - wp)
        idx = np.stack([np.clip(q - k - m, -MAX_REL, MAX_REL) + MAX_REL for q, k in pos])
        e = jnp.transpose(table.astype(F32)[idx], (0, 2, 1))
        e = jnp.broadcast_to(e[:, :, None, :], (nv, H, 8, wp))
    else:
        e = jnp.zeros((nv, H, 8, wp), F32)
    return pl.pallas_call(
        functools.partial(_toeplitz_kernel, tq=tq, win=win, n_prev=n_prev, n_valid=n_valid,
                          has_table=has_table),
        grid=(nv, H),
        in_specs=[pl.BlockSpec(memory_space=pltpu.SMEM),
                  pl.BlockSpec((1, 1, 8, wp), lambda v, h: (v, h, 0, 0))],
        out_specs=pl.BlockSpec((1, 1, tq, win), lambda v, h: (v, h, 0, 0)),
        out_shape=jax.ShapeDtypeStruct((nv, H, tq, win), F32),
        compiler_params=pltpu.CompilerParams(dimension_semantics=("parallel", "parallel")),
        name="band_bias",
    )(jnp.asarray(ab), e)


def _dsa2_kernel(q_ref, k_ref, v_ref, iq_ref, ik_ref, iw_ref, o_ref, key_ref, mb_ref, m_ref, l_ref,
                 acc_ref, alpha_ref, s_ref, p_ref, *, tq, tk, sk, sk_valid, qbase, n_sel, scale):
    i = pl.program_id(1)
    q0 = i * tq
    qc = (qbase + q0 + lax.broadcasted_iota(I32, (tq, 1), 0)) >> CHUNK_SHIFT
    hi_row = (((qbase + q0 + tq - 1) >> CHUNK_SHIFT) + 1) * CHUNK
    nb = (jnp.minimum(hi_row, sk_valid) + tk - 1) // tk
    lane = lax.broadcasted_iota(I32, (tq, LANES), 1)
    half_mask = (jnp.where(lane < 64, 1.0, 0.0).astype(BF16), jnp.where(lane >= 64, 1.0, 0.0).astype(BF16))
    n_idx_tiles = iq_ref.shape[1] // LANES
    n_lane_tiles = tk // LANES

    def col_of(jb):
        return jb * tk + lax.broadcasted_iota(I32, (1, tk), 1)

    def mask_invisible(x, col):
        x = jnp.where((col >> CHUNK_SHIFT) <= qc, x, NEG_INF)
        if sk_valid < sk:
            x = jnp.where(col < sk_valid, x, NEG_INF)
        return x

    def lane_fold(x):
        acc = x[:, 0:LANES]
        for t in range(1, n_lane_tiles):
            acc = acc + x[:, t * LANES:(t + 1) * LANES]
        return acc

    def score_body(jb, carry):
        r0 = pl.multiple_of(jb * tk, 16)
        ikb = ik_ref[pl.ds(r0, tk), :]
        acc = jnp.zeros((tq, tk), F32)
        for p in range(n_idx_tiles):
            t = iq_ref[:, p * LANES:(p + 1) * LANES]
            for hf in range(2):
                lg = lax.dot_general(t * half_mask[hf], ikb, (((1,), (1,)), ((), ())),
                                     preferred_element_type=F32)
                h = 2 * p + hf
                acc = acc + iw_ref[:, h:h + 1] * jnp.maximum(lg, 0.0)
        sc = mask_invisible(acc, col_of(jb))
        bits = lax.bitcast_convert_type(sc, I32)
        key_ref[jb] = bits ^ ((bits >> 31) & 0x7FFFFFFF)
        return carry

    lax.fori_loop(0, nb, score_body, 0)

    def count(hit):
        def cb(jb, c):
            return c + lane_fold(hit(key_ref[jb], jb))
        part = lax.fori_loop(0, nb, cb, jnp.zeros((tq, LANES), F32))
        return jnp.sum(part, axis=1, keepdims=True)

    def bit_body(it, cand):
        trial = cand | (jnp.int32(1) << (31 - it))
        trial_s = trial ^ INT_MIN
        cnt = count(lambda kk, jb: jnp.where(kk >= trial_s, 1.0, 0.0))
        return jnp.where(cnt >= n_sel, trial, cand)

    cand = lax.fori_loop(0, 32, bit_body, jnp.zeros((tq, 1), I32))
    thr = cand ^ INT_MIN

    cnt_gt = count(lambda kk, jb: jnp.where(kk > thr, 1.0, 0.0))
    n_eq = count(lambda kk, jb: jnp.where(kk == thr, 1.0, 0.0))
    need = n_sel - cnt_gt
    neg_bits = int(np.float32(NEG_INF).view(np.int32))
    neg_key = neg_bits ^ ((neg_bits >> 31) & 0x7FFFFFFF)
    tie = jnp.where(thr > neg_key, jnp.where(n_eq > need, 1.0, 0.0), 0.0)
    any_tie = jnp.max(tie) > 0.0

    def tie_cols():
        def jbit(it, cj):
            trial = cj | (jnp.int32(1) << (30 - it))
            c = count(lambda kk, jb: jnp.where(kk == thr, jnp.where(col_of(jb) < trial, 1.0, 0.0), 0.0))
            return jnp.where(c < need, trial, cj)
        return lax.fori_loop(0, 31, jbit, jnp.zeros((tq, 1), I32))

    jmax = lax.cond(any_tie, tie_cols, lambda: jnp.full((tq, 1), BIG_COL, I32))

    def mask_body(jb, carry):
        kk = key_ref[jb]
        col = col_of(jb)
        sel = jnp.where(kk > thr, 0.0, jnp.where(kk == thr, jnp.where(col <= jmax, 0.0, NEG_INF), NEG_INF))
        mb_ref[jb] = mask_invisible(sel, col)
        return carry

    lax.fori_loop(0, nb, mask_body, 0)

    n_kv = k_ref.shape[1] // LANES
    grp = (q_ref.shape[1] // LANES) // n_kv
    R = grp * tq
    m_ref[...] = jnp.full(m_ref.shape, M_INIT, F32)
    l_ref[...] = jnp.zeros(l_ref.shape, F32)
    acc_ref[...] = jnp.zeros(acc_ref.shape, F32)

    ch = min(SOFTMAX_ROWS, tq)
    n_ch = R // ch

    def body(jb, carry):
        r0 = pl.multiple_of(jb * tk, 16)
        for kvh in range(n_kv):
            Q = jnp.concatenate(
                [q_ref[:, (kvh * grp + gi) * LANES:(kvh * grp + gi + 1) * LANES] for gi in range(grp)], axis=0)
            kb = k_ref[pl.ds(r0, tk), kvh * LANES:(kvh + 1) * LANES]
            vb = v_ref[pl.ds(r0, tk), kvh * LANES:(kvh + 1) * LANES]
            s_ref[kvh] = lax.dot_general(Q, kb, (((1,), (1,)), ((), ())), preferred_element_type=F32)

            def rows_body(c, carry2, kvh=kvh):
                rr = pl.multiple_of(c * ch, ch)
                qr = pl.multiple_of((c * ch) % tq, ch)
                s = s_ref[kvh, pl.ds(rr, ch), :] * (scale * LOG2E) + mb_ref[jb, pl.ds(qr, ch), :]
                m_old = m_ref[kvh, pl.ds(rr, ch), :]
                m_new = jnp.maximum(m_old, jnp.max(s, axis=-1, keepdims=True))
                p = jnp.exp2(s - jnp.concatenate([m_new] * n_lane_tiles, axis=1))
                alpha = jnp.exp2(m_old - m_new)
                l_ref[kvh, pl.ds(rr, ch), :] = alpha * l_ref[kvh, pl.ds(rr, ch), :] + jnp.sum(
                    p, axis=-1, keepdims=True)
                m_ref[kvh, pl.ds(rr, ch), :] = m_new
                alpha_ref[kvh, pl.ds(rr, ch), :] = alpha
                p_ref[kvh, pl.ds(rr, ch), :] = p.astype(BF16)
                return carry2

            lax.fori_loop(0, n_ch, rows_body, 0, unroll=n_ch)
            acc_ref[kvh] = alpha_ref[kvh] * acc_ref[kvh] + jnp.dot(p_ref[kvh], vb,
                                                                   preferred_element_type=F32)
        return carry

    lax.fori_loop(0, nb, body, 0)
    for kvh in range(n_kv):
        o = acc_ref[kvh] / l_ref[kvh]
        for gi in range(grp):
            hh = kvh * grp + gi
            o_ref[:, hh * LANES:(hh + 1) * LANES] = o[gi * tq:(gi + 1) * tq].astype(o_ref.dtype)


def dsa_attention2(q, k, v, iq, ik2, iw, *, B, Sq, Sk, sk_valid, tq, tk, qbase, n_sel):
    nq = Sq // tq
    nkb = Sk // tk
    Hq = q.shape[1]
    Hk = k.shape[1]
    n_kv = Hk // LANES
    R = (Hq // Hk) * tq
    kern = functools.partial(_dsa2_kernel, tq=tq, tk=tk, sk=Sk, sk_valid=sk_valid, qbase=qbase,
                             n_sel=n_sel, scale=DSA_HEAD_DIM ** -0.5)
    blk = tq * (Hq + iq.shape[1]) * 2 + tq * LANES * 4 + Sk * (2 * Hk + LANES) * 2 + tq * Hq * 2
    scratch = 2 * nkb * tq * tk * 4 + 3 * n_kv * R * LANES * 4
    return pl.pallas_call(
        kern,
        grid=(B, nq),
        in_specs=[pl.BlockSpec((tq, Hq), lambda b, i: (b * nq + i, 0)),
                  pl.BlockSpec((Sk, Hk), lambda b, i: (b, 0)),
                  pl.BlockSpec((Sk, Hk), lambda b, i: (b, 0)),
                  pl.BlockSpec((tq, iq.shape[1]), lambda b, i: (b * nq + i, 0)),
                  pl.BlockSpec((Sk, LANES), lambda b, i: (b, 0)),
                  pl.BlockSpec((tq, LANES), lambda b, i: (b * nq + i, 0))],
        out_specs=pl.BlockSpec((tq, Hq), lambda b, i: (b * nq + i, 0)),
        out_shape=jax.ShapeDtypeStruct((B * Sq, Hq), BF16),
        scratch_shapes=[pltpu.VMEM((nkb, tq, tk), I32), pltpu.VMEM((nkb, tq, tk), F32),
                        pltpu.VMEM((n_kv, R, LANES), F32), pltpu.VMEM((n_kv, R, LANES), F32),
                        pltpu.VMEM((n_kv, R, LANES), F32), pltpu.VMEM((n_kv, R, LANES), F32),
                        pltpu.VMEM((n_kv, R, tk), F32), pltpu.VMEM((n_kv, R, tk), BF16)],
        compiler_params=pltpu.CompilerParams(
            dimension_semantics=("parallel", "arbitrary"),
            vmem_limit_bytes=_vmem_limit(blk, scratch + n_kv * R * (LANES * 4 + tk * 6) + (8 << 20))),
        name="dsa_attention",
    )(q, k, v, iq, ik2, iw)


def _pack_cache_kernel(*refs, rows, n_new, n_heads):
    cache_ref, o_ref = refs[0], refs[-1]
    dh = cache_ref.shape[3]
    for h in range(n_heads):
        o_ref[0:rows, h * dh:(h + 1) * dh] = cache_ref[0, :, h, :].astype(o_ref.dtype)
    if n_new:
        o_ref[rows:rows + n_new, :] = refs[1][...].astype(o_ref.dtype)
    n_pad = o_ref.shape[0] - rows - n_new
    if n_pad:
        o_ref[rows + n_new:, :] = jnp.zeros((n_pad, o_ref.shape[1]), o_ref.dtype)


def pack_cache(cache, new, rows_pad, layer=None):
    B, rows, H, dh = cache.shape[-4:]
    n_new = 0 if new is None else new.shape[0] // B
    if layer is None:
        in_specs = [pl.BlockSpec((1, rows, H, dh), lambda b: (b, 0, 0, 0))]
    else:
        in_specs = [pl.BlockSpec((None, 1, rows, H, dh), lambda b: (layer, b, 0, 0, 0))]
    args = [cache]
    if n_new:
        in_specs.append(pl.BlockSpec((n_new, H * dh), lambda b: (b, 0)))
        args.append(new)
    return pl.pallas_call(
        functools.partial(_pack_cache_kernel, rows=rows, n_new=n_new, n_heads=H),
        grid=(B,),
        in_specs=in_specs,
        out_specs=pl.BlockSpec((rows_pad, H * dh), lambda b: (b, 0)),
        out_shape=jax.ShapeDtypeStruct((B * rows_pad, H * dh), BF16),
        compiler_params=pltpu.CompilerParams(
            dimension_semantics=("parallel",),
            vmem_limit_bytes=_vmem_limit(rows * H * dh * 4 + rows_pad * H * dh * 2, rows * H * dh * 4)),
        name="pack_cache",
    )(*args)


def _rope_tables(pos, rot, group):
    width = 2 * LANES
    half = rot // 2
    inv = ROPE_THETA ** (-jnp.arange(half, dtype=F32) / half)
    ang = pos.astype(F32)[:, None] * inv[None, :]
    c, s = jnp.cos(ang), jnp.sin(ang)
    lane = np.arange(width) % group
    is_lo = lane < half
    is_hi = (lane >= half) & (lane < rot)
    idx = np.where(is_lo, lane, np.where(is_hi, lane - half, 0))
    cg, sg = c[:, idx], s[:, idx]
    C = jnp.where(is_lo | is_hi, cg, 1.0)
    S = jnp.where(is_lo, -sg, jnp.where(is_hi, sg, 0.0))
    perm = np.zeros((width, width), np.float32)
    dst = np.arange(width)
    src = np.where(is_lo, dst + half, dst - half)
    perm[src[is_lo | is_hi], dst[is_lo | is_hi]] = 1.0
    return half, C, S, jnp.asarray(perm, BF16)


def _pad_cols(w, n):
    return jnp.pad(w, ((0, 0), (0, n - w.shape[1])))


def _tile_gain(g, n):
    return jnp.tile(g.astype(F32), n // g.shape[0])


def _with_cache(cache, new, B, rows_pad, dtype=F32):
    n = new.shape[0] // B
    P = cache.shape[1]
    parts = [cache.reshape(B, P, -1).astype(dtype), new.reshape(B, n, -1).astype(dtype)]
    if rows_pad > P + n:
        parts.append(jnp.zeros((B, rows_pad - P - n, new.shape[1]), dtype))
    return jnp.concatenate(parts, axis=1).reshape(B * rows_pad, -1)


def _round_up(x, m):
    return (x + m - 1) // m * m


def _window_variants(tq, n_prev):
    return [(t * tq, 0) for t in range(n_prev * CHUNK // tq)] + [(n_prev * CHUNK, 0)]


def _mla_weights(w_in, q_lora_norm, kv_lora_norm, w_uq, w_ukv, q_norm, k_norm):
    H = MLA_HEADS
    wq3 = w_uq.reshape(MLA_Q_LORA, H, MLA_NOPE + MLA_ROPE)
    wkv3 = w_ukv.reshape(MLA_KV_LORA, H, MLA_NOPE + MLA_V)
    return dict(
        w_cq=w_in[:, :MLA_Q_LORA].astype(BF16),
        w_ckv=w_in[:, MLA_Q_LORA:MLA_Q_LORA + MLA_KV_LORA].astype(BF16),
        w_kr=_pad_cols(w_in[:, MLA_Q_LORA + MLA_KV_LORA:], LANES).astype(BF16),
        g_cq=q_lora_norm, g_ckv=kv_lora_norm,
        g_kr=_tile_gain(k_norm[MLA_NOPE:], LANES),
        w_qn=wq3[:, :, :MLA_NOPE].reshape(MLA_Q_LORA, H * MLA_NOPE).astype(BF16),
        w_qr=wq3[:, :, MLA_NOPE:].reshape(MLA_Q_LORA, H * MLA_ROPE).astype(BF16),
        g_qn=_tile_gain(q_norm[:MLA_NOPE], H * MLA_NOPE),
        g_qr=_tile_gain(q_norm[MLA_NOPE:], H * MLA_ROPE),
        w_kn=wkv3[:, :, :MLA_NOPE].reshape(MLA_KV_LORA, H * MLA_NOPE).astype(BF16),
        w_v=wkv3[:, :, MLA_NOPE:].reshape(MLA_KV_LORA, H * MLA_V).astype(BF16),
        g_kn=_tile_gain(k_norm[:MLA_NOPE], H * MLA_NOPE),
    )


def _mla_project(h, W, rope64):
    cq = mm(h, W["w_cq"], gain=W["g_cq"], gs=MLA_Q_LORA, norm=True, out_dtype=BF16, name="mla_cq")
    ckv, ckv_bf = mm(h, W["w_ckv"], gain=W["g_ckv"], gs=MLA_KV_LORA, norm=True, out_dtype=(F32, BF16),
                     name="mla_ckv")
    kr = mm(h, W["w_kr"], gain=W["g_kr"], gs=64, norm=True, rope=rope64, name="mla_kr")
    qn = mm(cq, W["w_qn"], gain=W["g_qn"], gs=LANES, norm=True, out_dtype=BF16, name="mla_qn")
    qr = mm(cq, W["w_qr"], gain=W["g_qr"], gs=64, norm=True, rope=rope64, out_dtype=BF16, name="mla_qr")
    return qn, qr, ckv, ckv_bf, kr


def _mla_expand(ckv_bf, W):
    kn = mm(ckv_bf, W["w_kn"], gain=W["g_kn"], gs=LANES, norm=True, out_dtype=BF16, name="mla_kn")
    v = mm(ckv_bf, W["w_v"], out_dtype=BF16, name="mla_v")
    return kn, v


def _mla_attend(qn, qr, kn, kr, v, *, B, Sq, Sk, sk_valid, tq, hps, mode, qbase):
    kr2 = jnp.concatenate([kr[:, :MLA_ROPE], kr[:, :MLA_ROPE]], axis=1).astype(BF16)
    units = tuple(_unit(((h, None),), h, ((h, "slot", 0),), q2=(h // 2, "lo" if h % 2 == 0 else "hi"))
                  for h in range(hps))
    return attention2(qn, kn, v, B=B, Sq=Sq, Sk=Sk, sk_valid=sk_valid, tq=tq, G=MLA_HEADS // hps,
                      qw=hps, kw=hps, ow=hps, units=units, scale=(MLA_NOPE + MLA_ROPE) ** -0.5,
                      mode=mode, qbase=qbase, q2=qr, q2w=hps // 2, k2=kr2, name="mla_attn")


def kernel(x_prompt, x_sample, mem_prompt, cache_a_ckv, cache_a_krope, cache_b_k, cache_b_v, cache_c_k, cache_c_v, cache_c_idxk, cache_d_k, cache_d_v, cache_mem_k, cache_mem_v, state_ffn_conv, norm_mix, norm_xattn, norm_mem, norm_ffn, a_w_in, a_q_lora_norm, a_kv_lora_norm, a_w_uq, a_w_ukv, a_q_norm, a_k_norm, a_w_out, b_w_in, b_q_norm, b_k_norm, b_rel_bias, b_w_out, c_w_in, c_q_norm, c_k_norm, c_idx_k_norm, c_w_out, d_w_in, d_q_norm, d_k_norm, d_sinks, d_w_out, xa_w_q, xa_w_kv, xa_q_norm, xa_k_norm, xa_w_o, ffn_w_up, ffn_conv_w, ffn_conv_b, ffn_w_down):
    B, S, D = x_prompt.shape
    Bd, n, _ = x_sample.shape
    P = cache_a_ckv.shape[2]
    depth = norm_mix.shape[0]
    Fd = ffn_conv_b.shape[1]
    Mp, Ms = B * S, Bd * n
    pos_p = jnp.arange(S, dtype=I32)
    pos_s = jnp.tile(P + jnp.arange(n, dtype=I32), Bd)
    names = ('a_ckv_p', 'a_krope_p', 'b_k_p', 'b_v_p', 'c_k_p', 'c_v_p', 'c_idxk_p', 'd_k_p', 'd_v_p',
             'mem_k_p', 'mem_v_p', 'conv_p', 'a_ckv_s', 'a_krope_s', 'b_k_s', 'b_v_s', 'c_k_s', 'c_v_s',
             'c_idxk_s', 'd_k_s', 'd_v_s', 'conv_s')
    st = {name: [] for name in names}
    xp = x_prompt.reshape(Mp, D)
    xs = x_sample.reshape(Ms, D)
    tq_p = 128

    def rope_pair(rot, group):
        return _rope_tables(pos_p, rot, group), _rope_tables(pos_s, rot, group)

    for i in range(depth):
        m, j = i % 4, i // 4
        hp = rmsnorm_cast(xp, norm_mix[i])
        hs = rmsnorm_cast(xs, norm_mix[i])
        if m == 0:
            W = _mla_weights(a_w_in[j], a_q_lora_norm[j], a_kv_lora_norm[j], a_w_uq[j], a_w_ukv[j],
                             a_q_norm[j], a_k_norm[j])
            rp, rs = rope_pair(MLA_ROPE, 64)
            w_out = a_w_out[j].astype(BF16)
            qn, qr, ckv, ckv_bf, kr = _mla_project(hp, W, rp)
            kn, v = _mla_expand(ckv_bf, W)
            op = _mla_attend(qn, qr, kn, kr, v, B=B, Sq=S, Sk=S, sk_valid=S, tq=_pick(S, (256, 128)),
                             hps=4, mode="causal", qbase=0)
            xp = mm(op, w_out, res=xp, name="mix_out")
            st['a_ckv_p'].append(ckv.reshape(B, S, MLA_KV_LORA))
            st['a_krope_p'].append(kr[:, :MLA_ROPE].reshape(B, S, MLA_ROPE))
            qn, qr, ckv, ckv_bf, kr = _mla_project(hs, W, rs)
            sk_valid = P + n
            Skp = _round_up(sk_valid, LANES)
            ckv_all = _with_cache(cache_a_ckv[j], ckv_bf, Bd, Skp, BF16)
            kr_all = _with_cache(cache_a_krope[j], kr[:, :MLA_ROPE], Bd, Skp)
            kn, v = _mla_expand(ckv_all, W)
            os_ = _mla_attend(qn, qr, kn, kr_all, v, B=Bd, Sq=n, Sk=Skp, sk_valid=sk_valid, tq=n,
                              hps=MLA_HEADS, mode="full", qbase=P)
            xs = mm(os_, w_out, res=xs, name="mix_out")
            st['a_ckv_s'].append(ckv.reshape(Bd, n, MLA_KV_LORA))
            st['a_krope_s'].append(kr[:, :MLA_ROPE].reshape(Bd, n, MLA_ROPE))
        elif m == 1:
            H, dh = BAND_HEADS, D // BAND_HEADS
            w_in = b_w_in[j]
            wq, wk, wv = (w_in[:, t * H * dh:(t + 1) * H * dh].astype(BF16) for t in range(3))
            gq, gk = _tile_gain(b_q_norm[j], H * dh), _tile_gain(b_k_norm[j], H * dh)
            w_out = b_w_out[j].astype(BF16)
            npv = BAND_PREV_CHUNKS
            def band_units(nh):
                return tuple(_unit(((h, None),), h, ((h, "slot", 0),), bias=h) for h in range(nh))

            q = mm(hp, wq, gain=gq, gs=dh, norm=True, out_dtype=BF16, name="band_q")
            k, k_bf = mm(hp, wk, gain=gk, gs=dh, norm=True, out_dtype=(F32, BF16), name="band_k")
            v, v_bf = mm(hp, wv, out_dtype=(F32, BF16), name="band_v")
            tq_b = _pick(S, (256, 128))
            win = npv * CHUNK + tq_b
            bias = band_bias(b_rel_bias[j], tq_b, win, _window_variants(tq_b, npv), npv, win)
            hps = 8
            op = attention2(q, k_bf, v_bf, B=B, Sq=S, Sk=S, sk_valid=S, tq=tq_b, G=H // hps, qw=hps,
                            kw=hps, ow=hps, units=band_units(hps), scale=dh ** -0.5, mode="window",
                            win=win, n_prev=npv, masked=False, bias=bias, name="band_attn")
            xp = mm(op, w_out, res=xp, name="mix_out")
            r = min(npv * CHUNK, S)
            st['b_k_p'].append(k.reshape(B, S, H, dh)[:, S - r:])
            st['b_v_p'].append(v.reshape(B, S, H, dh)[:, S - r:])
            q = mm(hs, wq, gain=gq, gs=dh, norm=True, out_dtype=BF16, name="band_q")
            k = mm(hs, wk, gain=gk, gs=dh, norm=True, name="band_k")
            v = mm(hs, wv, name="band_v")
            rows = cache_b_k.shape[2]
            sk_valid = rows + n
            Skp = _round_up(sk_valid, LANES)
            k_all = pack_cache(cache_b_k[j], k, Skp)
            v_all = pack_cache(cache_b_v[j], v, Skp)
            bias = band_bias(b_rel_bias[j], n, Skp, [(P, P - rows)], npv, sk_valid)
            os_ = attention2(q, k_all, v_all, B=Bd, Sq=n, Sk=Skp, sk_valid=sk_valid, tq=n, G=1, qw=H,
                             kw=H, ow=H, units=band_units(H), scale=dh ** -0.5, mode="window", win=Skp,
                             n_prev=npv, masked=False, qbase=P, kbase=P - rows, bias=bias,
                             name="band_attn")
            xs = mm(os_, w_out, res=xs, name="mix_out")
            st['b_k_s'].append(k.reshape(Bd, n, H, dh))
            st['b_v_s'].append(v.reshape(Bd, n, H, dh))
        elif m == 2:
            Hq, Hk, dh = DSA_HEADS, DSA_KV_HEADS, DSA_HEAD_DIM
            w_in = c_w_in[j]
            oq, ok_, ov = Hq * dh, Hq * dh + Hk * dh, Hq * dh + 2 * Hk * dh
            oiq = ov + IDX_HEADS * IDX_DIM
            oik = oiq + IDX_DIM
            wq = w_in[:, :oq].astype(BF16)
            wk = w_in[:, oq:ok_].astype(BF16)
            wv = w_in[:, ok_:ov].astype(BF16)
            wiq = w_in[:, ov:oiq].astype(BF16)
            wik = _pad_cols(w_in[:, oiq:oik], LANES).astype(BF16)
            wiw = _pad_cols(w_in[:, oik:], LANES).astype(BF16)
            gq, gk = _tile_gain(c_q_norm[j], oq), _tile_gain(c_k_norm[j], Hk * dh)
            gik = _tile_gain(c_idx_k_norm[j], LANES)
            giw = jnp.full((LANES,), IDX_HEADS ** -0.5 * IDX_DIM ** -0.5, F32)
            w_out = c_w_out[j].astype(BF16)
            rp128, rs128 = rope_pair(dh // 4, LANES)
            rp64, rs64 = rope_pair(IDX_DIM // 4, 64)

            def project(h, r128, r64):
                q = mm(h, wq, gain=gq, gs=dh, norm=True, rope=r128, out_dtype=BF16, name="dsa_q")
                k, k_bf = mm(h, wk, gain=gk, gs=dh, norm=True, rope=r128, out_dtype=(F32, BF16), name="dsa_k")
                v, v_bf = mm(h, wv, out_dtype=(F32, BF16), name="dsa_v")
                iq = mm(h, wiq, rope=r64, out_dtype=BF16, name="dsa_iq")
                ik = mm(h, wik, gain=gik, gs=64, norm=True, rope=r64, name="dsa_ik")
                iw = mm(h, wiw, gain=giw, name="dsa_iw")
                return q, k, v, k_bf, v_bf, iq, ik, iw

            q, k, v, k_bf, v_bf, iq, ik, iw = project(hp, rp128, rp64)
            ik2 = jnp.concatenate([ik[:, :IDX_DIM], ik[:, :IDX_DIM]], axis=1).astype(BF16)
            op = dsa_attention2(q, k_bf, v_bf, iq, ik2, iw, B=B, Sq=S, Sk=S, sk_valid=S,
                                tq=_pick(S, (256, 128)), tk=_pick(S, (512, 256, 128)), qbase=0,
                                n_sel=min(DSA_TOPK, S // 4))
            xp = mm(op, w_out, res=xp, name="mix_out")
            st['c_k_p'].append(k.reshape(B, S, Hk, dh))
            st['c_v_p'].append(v.reshape(B, S, Hk, dh))
            st['c_idxk_p'].append(ik[:, :IDX_DIM].reshape(B, S, IDX_DIM))
            q, k, v, _, _, iq, ik, iw = project(hs, rs128, rs64)
            sk_valid = P + n
            Skp = _round_up(sk_valid, LANES)
            k_all = pack_cache(cache_c_k[j], k, Skp)
            v_all = pack_cache(cache_c_v[j], v, Skp)
            ik_all = _with_cache(cache_c_idxk[j], ik[:, :IDX_DIM], Bd, Skp, BF16)
            ik2 = jnp.concatenate([ik_all, ik_all], axis=1)
            os_ = dsa_attention2(q, k_all, v_all, iq, ik2, iw, B=Bd, Sq=n, Sk=Skp, sk_valid=sk_valid,
                                 tq=n, tk=_pick(Skp, (512, 384, 256, 128)), qbase=P,
                                 n_sel=min(DSA_TOPK, sk_valid // 4))
            xs = mm(os_, w_out, res=xs, name="mix_out")
            st['c_k_s'].append(k.reshape(Bd, n, Hk, dh))
            st['c_v_s'].append(v.reshape(Bd, n, Hk, dh))
            st['c_idxk_s'].append(ik[:, :IDX_DIM].reshape(Bd, n, IDX_DIM))
        else:
            Hq, Hk, dh = SWA_HEADS, SWA_KV_HEADS, SWA_HEAD_DIM
            grp = Hq // Hk
            npairs = Hk // 2
            perm = np.array([[[2 * p * grp + t, (2 * p + 1) * grp + t] for t in range(grp)]
                             for p in range(npairs)]).reshape(-1)
            w_in = d_w_in[j]
            nqc, nkc = Hq * dh, Hk * dh
            wq = w_in[:, :nqc].reshape(D, Hq, dh)[:, perm].reshape(D, nqc).astype(BF16)
            wk = w_in[:, nqc:nqc + nkc].astype(BF16)
            wv = w_in[:, nqc + nkc:].astype(BF16)
            gq, gk = _tile_gain(d_q_norm[j], nqc), _tile_gain(d_k_norm[j], nkc)
            w_out = d_w_out[j].reshape(Hq, dh, D)[perm].reshape(nqc, D).astype(BF16)
            sinks = d_sinks[j][perm]
            rp, rs = rope_pair(dh // 4, 64)
            npv = SWA_PREV_CHUNKS
            def swa_units(npr):
                us = []
                for p in range(npr):
                    slots = tuple((p * grp + t, hf) for t in range(grp) for hf in ("lo", "hi"))
                    outs = tuple((p * grp + t, "pair", 2 * t, 2 * t + 1) for t in range(grp))
                    us.append(_unit(slots, p, outs, sinks=range(2 * grp * p, 2 * grp * (p + 1))))
                return tuple(us)

            def project(h, r):
                q = mm(h, wq, gain=gq, gs=64, norm=True, rope=r, out_dtype=BF16, name="swa_q")
                k, k_bf = mm(h, wk, gain=gk, gs=64, norm=True, rope=r, out_dtype=(F32, BF16), name="swa_k")
                v, v_bf = mm(h, wv, out_dtype=(F32, BF16), name="swa_v")
                return q, k, v, k_bf, v_bf

            q, k, v, k_bf, v_bf = project(hp, rp)
            tq_d = tq_p
            win = npv * CHUNK + tq_d
            pps = npairs
            op = attention2(q, k_bf, v_bf, B=B, Sq=S, Sk=S, sk_valid=S, tq=tq_d, G=npairs // pps,
                            qw=pps * grp, kw=pps, ow=pps * grp, units=swa_units(pps), scale=dh ** -0.5,
                            mode="window", win=win, n_prev=npv, sinks=sinks,
                            sinks_per_group=2 * grp * pps, name="swa_attn")
            xp = mm(op, w_out, res=xp, name="mix_out")
            r = min(npv * CHUNK, S)
            st['d_k_p'].append(k.reshape(B, S, Hk, dh)[:, S - r:])
            st['d_v_p'].append(v.reshape(B, S, Hk, dh)[:, S - r:])
            q, k, v, _, _ = project(hs, rs)
            rows = cache_d_k.shape[2]
            sk_valid = rows + n
            Skp = _round_up(sk_valid, LANES)
            k_all = _with_cache(cache_d_k[j].reshape(Bd, rows, nkc), k, Bd, Skp, BF16)
            v_all = _with_cache(cache_d_v[j].reshape(Bd, rows, nkc), v, Bd, Skp, BF16)
            os_ = attention2(q, k_all, v_all, B=Bd, Sq=n, Sk=Skp, sk_valid=sk_valid, tq=n, G=1,
                             qw=npairs * grp, kw=npairs, ow=npairs * grp, units=swa_units(npairs),
                             scale=dh ** -0.5, mode="window", win=Skp, n_prev=npv, qbase=P,
                             kbase=P - rows, sinks=sinks, sinks_per_group=2 * grp * npairs,
                             name="swa_attn")
            xs = mm(os_, w_out, res=xs, name="mix_out")
            st['d_k_s'].append(k.reshape(Bd, n, Hk, dh))
            st['d_v_s'].append(v.reshape(Bd, n, Hk, dh))

        Hx, dx = XA_HEADS, XA_HEAD_DIM
        Mm = mem_prompt.shape[1]
        hm = rmsnorm_cast(mem_prompt.reshape(B * Mm, D), norm_mem[i])
        w_kv = xa_w_kv[i]
        mk, mk_bf = mm(hm, w_kv[:, :Hx * dx].astype(BF16), gain=_tile_gain(xa_k_norm[i], Hx * dx), gs=dx,
                       norm=True, out_dtype=(F32, BF16), name="xa_k")
        mv, mv_bf = mm(hm, w_kv[:, Hx * dx:].astype(BF16), out_dtype=(F32, BF16), name="xa_v")
        st['mem_k_p'].append(mk.reshape(B, Mm, Hx, dx))
        st['mem_v_p'].append(mv.reshape(B, Mm, Hx, dx))
        wxq = xa_w_q[i].astype(BF16)
        gxq = _tile_gain(xa_q_norm[i], Hx * dx)
        wxo = xa_w_o[i].astype(BF16)
        xa_units = tuple(_unit(((h, None),), h, ((h, "slot", 0),)) for h in range(Hx))

        def cross(x, mk2, mv2, Bx, Sx, tq):
            qx = mm(x, wxq, in_gain=norm_xattn[i], gain=gxq, gs=dx, norm=True, out_dtype=BF16, name="xa_q")
            ox = attention2(qx, mk2, mv2, B=Bx, Sq=Sx, Sk=Mm, sk_valid=Mm, tq=tq, G=1, qw=Hx, kw=Hx,
                            ow=Hx, units=xa_units, scale=dx ** -0.5, mode="full", masked=False,
                            name="xattn")
            return mm(ox, wxo, res=x, name="xa_out")

        xp = cross(xp, mk_bf, mv_bf, B, S, _pick(S, (512, 256, 128)))
        xs = cross(xs, pack_cache(cache_mem_k, None, Mm, layer=i), pack_cache(cache_mem_v, None, Mm, layer=i),
                   Bd, n, n)

        w_up = ffn_w_up[i].astype(BF16)
        w_dn = ffn_w_down[i].astype(BF16)
        act, conv_p = ffn_up(xp, norm_ffn[i], w_up, ffn_conv_w[i], ffn_conv_b[i],
                             jnp.zeros((B, CONV_W - 1, Fd), F32), n_seq=B, seq_rows=S)
        xp = mm(act, w_dn, res=xp, name="ffn_down")
        act, conv_s = ffn_up(xs, norm_ffn[i], w_up, ffn_conv_w[i], ffn_conv_b[i], state_ffn_conv[i],
                             n_seq=Bd, seq_rows=n)
        xs = mm(act, w_dn, res=xs, name="ffn_down")
        st['conv_p'].append(conv_p)
        st['conv_s'].append(conv_s)

    order = ('a_ckv_p', 'a_krope_p', 'b_k_p', 'b_v_p', 'c_k_p', 'c_v_p', 'c_idxk_p', 'd_k_p', 'd_v_p',
             'mem_k_p', 'mem_v_p', 'conv_p', 'a_ckv_s', 'a_krope_s', 'b_k_s', 'b_v_s', 'c_k_s', 'c_v_s',
             'c_idxk_s', 'd_k_s', 'd_v_s', 'conv_s')
    return (xp.reshape(B, S, D), xs.reshape(Bd, n, D)) + tuple(jnp.stack(st[nm]) for nm in order)
```

```python
import functools

import numpy as np
import jax
import jax.numpy as jnp
from jax import lax
from jax.experimental import pallas as pl
from jax.experimental.pallas import tpu as pltpu

F32, BF16, I32 = jnp.float32, jnp.bfloat16, jnp.int32

CHUNK = 64
CHUNK_SHIFT = 6
ROPE_THETA = 500000.0
EPS = 1e-6
NEG_INF = -1e30
MLA_HEADS, MLA_NOPE, MLA_ROPE, MLA_V = 16, 128, 64, 128
MLA_Q_LORA, MLA_KV_LORA = 512, 512
BAND_HEADS, BAND_PREV_CHUNKS, MAX_REL = 16, 8, 128
DSA_HEADS, DSA_KV_HEADS, DSA_HEAD_DIM = 16, 4, 128
IDX_HEADS, IDX_DIM, DSA_TOPK = 16, 64, 256
SWA_HEADS, SWA_KV_HEADS, SWA_HEAD_DIM, SWA_PREV_CHUNKS = 32, 8, 64, 2
XA_HEADS, XA_HEAD_DIM = 4, 128
CONV_W = 3

LANES = 128
VMEM_LIMIT_MAX = 56 * 1024 * 1024
VMEM_LIMIT_MIN = 32 * 1024 * 1024

M_INIT = -3.0e38
INT_MIN = -2 ** 31
BIG_COL = 2 ** 30
LOG2E = 1.4426950408889634
SOFTMAX_ROWS = 64


def _vmem_limit(block_bytes, temp_bytes=0):
    need = 2 * block_bytes + temp_bytes + (4 << 20)
    return int(min(max(need, VMEM_LIMIT_MIN), VMEM_LIMIT_MAX))


def _pick(n, cands):
    for c in cands:
        if n % c == 0:
            return c
    raise ValueError(f"no tile for {n}")


def _rmsnorm_kernel(x_ref, g_ref, o_ref):
    x = x_ref[...]
    ms = jnp.mean(x * x, axis=-1, keepdims=True)
    o_ref[...] = (x * lax.rsqrt(ms + EPS) * g_ref[...]).astype(o_ref.dtype)


def rmsnorm_cast(x, g):
    M, K = x.shape
    tm = _pick(M, (512, 256, 128, 64, 16, 8))
    return pl.pallas_call(
        _rmsnorm_kernel,
        grid=(M // tm,),
        in_specs=[pl.BlockSpec((tm, K), lambda i: (i, 0)), pl.BlockSpec((1, K), lambda i: (0, 0))],
        out_specs=pl.BlockSpec((tm, K), lambda i: (i, 0)),
        out_shape=jax.ShapeDtypeStruct((M, K), BF16),
        compiler_params=pltpu.CompilerParams(
            dimension_semantics=("parallel",), vmem_limit_bytes=_vmem_limit(tm * K * 6, tm * K * 8)),
        name="rmsnorm_cast",
    )(x, g.reshape(1, K).astype(F32))


def _group_norm_rope(y, gain_ref, rope_refs, gs, norm, half):
    tm, tn = y.shape
    if norm and gs == tn:
        ss = jnp.sum(y * y, axis=-1, keepdims=True)
        return y * lax.rsqrt(ss * (1.0 / gs) + EPS) * gain_ref[...]
    outs = []
    for c in range(tn // LANES):
        yc = y[:, c * LANES:(c + 1) * LANES]
        if norm:
            yy = yc * yc
            if gs == LANES:
                ss = jnp.sum(yy, axis=-1, keepdims=True)
            else:
                lo = lax.broadcasted_iota(I32, yc.shape, 1) < gs
                s_lo = jnp.sum(jnp.where(lo, yy, 0.0), axis=-1, keepdims=True)
                s_hi = jnp.sum(jnp.where(lo, 0.0, yy), axis=-1, keepdims=True)
                ss = jnp.where(lo, s_lo, s_hi)
            yc = yc * lax.rsqrt(ss * (1.0 / gs) + EPS) * gain_ref[:, c * LANES:(c + 1) * LANES]
        elif gain_ref is not None:
            yc = yc * gain_ref[:, c * LANES:(c + 1) * LANES]
        outs.append(yc)
    y = outs[0] if len(outs) == 1 else jnp.concatenate(outs, axis=1)
    if not half:
        return y
    c_ref, s_ref, p_ref = rope_refs
    pw = p_ref.shape[0]
    outs = []
    for b in range(tn // pw):
        yb = y[:, b * pw:(b + 1) * pw]
        hi = yb.astype(BF16)
        lo = (yb - hi.astype(F32)).astype(BF16)
        partner = (jnp.dot(hi, p_ref[...], preferred_element_type=F32)
                   + jnp.dot(lo, p_ref[...], preferred_element_type=F32))
        outs.append(yb * c_ref[...] + partner * s_ref[...])
    return outs[0] if len(outs) == 1 else jnp.concatenate(outs, axis=1)


def _normed_rows(x_ref, g_ref, h_ref):
    @pl.when(pl.program_id(1) == 0)
    def _():
        x = x_ref[...]
        ms = jnp.mean(x * x, axis=-1, keepdims=True)
        h_ref[...] = (x * lax.rsqrt(ms + EPS) * g_ref[...]).astype(h_ref.dtype)
    return h_ref[...]


def _mm_kernel(*refs, has_gain, gs, norm, half, has_res, has_in_norm, n_out):
    a_ref, w_ref = refs[0], refs[1]
    pos = 2
    in_gain_ref = gain_ref = rope_refs = res_ref = None
    if has_in_norm:
        in_gain_ref = refs[pos]
        pos += 1
    if has_gain:
        gain_ref = refs[pos]
        pos += 1
    if half:
        rope_refs = refs[pos:pos + 3]
        pos += 3
    if has_res:
        res_ref = refs[pos]
        pos += 1
    a = _normed_rows(a_ref, in_gain_ref, refs[pos + n_out]) if has_in_norm else a_ref[...]
    y = jnp.dot(a, w_ref[...], preferred_element_type=F32)
    if has_gain or half:
        y = _group_norm_rope(y, gain_ref, rope_refs, gs, norm, half)
    if has_res:
        y = res_ref[...] + y
    for o_ref in refs[pos:pos + n_out]:
        o_ref[...] = y.astype(o_ref.dtype)


def mm(a, w, *, gain=None, gs=LANES, norm=False, rope=None, res=None, out_dtype=F32, tn=None,
       in_gain=None, name="mm"):
    M, K = a.shape
    N = w.shape[1]
    assert a.dtype == (BF16 if in_gain is None else F32) and w.dtype == BF16 and N % LANES == 0
    tm_c = (1024, 512, 256, 128, 64, 16, 8)
    if tn is None:
        if norm and gs > LANES:
            tn = gs
        elif K <= 512:
            tn = _pick(N, (2048, 1024, 512, 256, 128))
            if res is not None:
                tm_c = tm_c[1:]
        elif K <= 2048:
            tn = _pick(N, (1024, 512, 256, 128))
        else:
            tn = _pick(N, (512, 256, 128))
    rows_period = M if rope is None else rope[1].shape[0]
    tm = _pick(np.gcd(M, rows_period), tm_c)
    half = 0
    in_specs = [pl.BlockSpec((tm, K), lambda i, j: (i, 0)), pl.BlockSpec((K, tn), lambda i, j: (0, j))]
    args = [a, w]
    blk = tm * K * 2 + K * tn * 2 + tm * tn * 4
    scratch = []
    if in_gain is not None:
        in_specs.append(pl.BlockSpec((1, K), lambda i, j: (0, 0)))
        args.append(in_gain.reshape(1, K).astype(F32))
        scratch.append(pltpu.VMEM((tm, K), BF16))
        blk += tm * K * 4
    if gain is not None:
        in_specs.append(pl.BlockSpec((1, tn), lambda i, j: (0, j)))
        args.append(gain.reshape(1, N).astype(F32))
    if rope is not None:
        half, tab_c, tab_s, perm = rope
        if tn % perm.shape[0] != 0:
            tab_c, tab_s, perm = tab_c[:, :LANES], tab_s[:, :LANES], perm[:LANES, :LANES]
        pw = perm.shape[0]
        assert tn % pw == 0
        nrb = rows_period // tm
        for t in (tab_c, tab_s):
            in_specs.append(pl.BlockSpec((tm, pw), lambda i, j, nrb=nrb: (i % nrb, 0)))
            args.append(t)
        in_specs.append(pl.BlockSpec((pw, pw), lambda i, j: (0, 0)))
        args.append(perm)
        blk += 2 * tm * pw * 4 + pw * pw * 2
    if res is not None:
        in_specs.append(pl.BlockSpec((tm, tn), lambda i, j: (i, j)))
        args.append(res)
        blk += tm * tn * 4
    dtypes = out_dtype if isinstance(out_dtype, tuple) else (out_dtype,)
    kern = functools.partial(_mm_kernel, has_gain=gain is not None, gs=gs, norm=norm, half=half,
                             has_res=res is not None, has_in_norm=in_gain is not None, n_out=len(dtypes))
    outs = pl.pallas_call(
        kern,
        grid=(M // tm, N // tn),
        in_specs=in_specs,
        out_specs=[pl.BlockSpec((tm, tn), lambda i, j: (i, j)) for _ in dtypes],
        out_shape=[jax.ShapeDtypeStruct((M, N), dt) for dt in dtypes],
        scratch_shapes=scratch,
        compiler_params=pltpu.CompilerParams(
            dimension_semantics=("parallel", "arbitrary"),
            vmem_limit_bytes=_vmem_limit(blk + (len(dtypes) - 1) * tm * tn * 4, 3 * tm * tn * 4)),
        name=name,
    )(*args)
    return tuple(outs) if isinstance(out_dtype, tuple) else outs[0]


def _ffn_up_kernel(x_ref, gx_ref, wg_ref, wv_ref, cw_ref, cb_ref, prev_ref, act_ref, last_ref, carry_ref,
                   h_ref, *, tiles_per_seq, seq_rows):
    i = pl.program_id(0)
    h = _normed_rows(x_ref, gx_ref, h_ref)
    g = jnp.dot(h, wg_ref[...], preferred_element_type=F32)
    val = jnp.dot(h, wv_ref[...], preferred_element_type=F32)
    tm, tn = g.shape
    row = lax.broadcasted_iota(I32, (tm, tn), 0)
    g1 = pltpu.roll(g, 1, 0)
    g2 = pltpu.roll(g, 2, 0)
    if tiles_per_seq >= 1:
        first = (i % tiles_per_seq) == 0
        j = pl.program_id(1)
        p_prev = prev_ref[0]
        p_carry = carry_ref[j]
        p2 = jnp.where(first, p_prev[0:1, :], p_carry[6:7, :])
        p1 = jnp.where(first, p_prev[1:2, :], p_carry[7:8, :])
        g1 = jnp.where(row == 0, p1, g1)
        g2 = jnp.where(row == 0, p2, jnp.where(row == 1, p1, g2))
        carry_ref[j] = g[tm - 8:tm, :]
        last_ref[0] = g[tm - 8:tm, :]
    else:
        rmod = row % seq_rows
        g1 = jnp.where(rmod == 0, prev_ref[0], g1)
        g2 = jnp.where(rmod < 2, prev_ref[1], g2)
        last_ref[...] = g
    c = cw_ref[0:1, :] * g2 + cw_ref[1:2, :] * g1 + cw_ref[2:3, :] * g + cb_ref[...]
    act_ref[...] = (c * (1.0 / (1.0 + jnp.exp(-c))) * val).astype(act_ref.dtype)


def ffn_up(x, gx, w_up, conv_w, conv_b, prev, *, n_seq, seq_rows):
    M, D = x.shape
    Fd = w_up.shape[1] // 2
    tn = _pick(Fd, (512, 256, 128))
    nj = Fd // tn
    if seq_rows >= 256:
        tm = _pick(seq_rows, (1024, 512, 256))
        tps = seq_rows // tm
        prev_arg = prev
        prev_spec = pl.BlockSpec((1, 2, tn), lambda i, j: (i // tps, 0, j))
        last_shape = jax.ShapeDtypeStruct((M // tm, 8, Fd), F32)
        last_spec = pl.BlockSpec((1, 8, tn), lambda i, j: (i, 0, j))
    else:
        tm = M
        tps = 0
        z = jnp.zeros((n_seq, seq_rows - 1, Fd), F32)
        inj1 = jnp.concatenate([prev[:, 1:2], z], axis=1).reshape(M, Fd)
        inj2 = jnp.concatenate([prev[:, 0:2], z[:, 1:]], axis=1).reshape(M, Fd)
        prev_arg = jnp.stack([inj1, inj2])
        prev_spec = pl.BlockSpec((2, tm, tn), lambda i, j: (0, i, j))
        last_shape = jax.ShapeDtypeStruct((M, Fd), F32)
        last_spec = pl.BlockSpec((tm, tn), lambda i, j: (i, j))
    kern = functools.partial(_ffn_up_kernel, tiles_per_seq=tps, seq_rows=seq_rows)
    blk = tm * D * 5 + 2 * D * tn * 2 + tm * tn * 2 + 4 * tn * 4 + 2 * tm * tn * 4
    act, last = pl.pallas_call(
        kern,
        grid=(M // tm, nj),
        in_specs=[pl.BlockSpec((tm, D), lambda i, j: (i, 0)),
                  pl.BlockSpec((1, D), lambda i, j: (0, 0)),
                  pl.BlockSpec((D, tn), lambda i, j: (0, j)),
                  pl.BlockSpec((D, tn), lambda i, j, nj=nj: (0, j + nj)),
                  pl.BlockSpec((CONV_W, tn), lambda i, j: (0, j)),
                  pl.BlockSpec((1, tn), lambda i, j: (0, j)),
                  prev_spec],
        out_specs=[pl.BlockSpec((tm, tn), lambda i, j: (i, j)), last_spec],
        out_shape=[jax.ShapeDtypeStruct((M, Fd), BF16), last_shape],
        scratch_shapes=[pltpu.VMEM((nj, 8, tn), F32), pltpu.VMEM((tm, D), BF16)],
        compiler_params=pltpu.CompilerParams(
            dimension_semantics=("arbitrary", "arbitrary"),
            vmem_limit_bytes=_vmem_limit(blk, 6 * tm * tn * 4)),
        name="ffn_up",
    )(x, gx.reshape(1, D).astype(F32), w_up, w_up, conv_w.astype(F32), conv_b.reshape(1, Fd).astype(F32),
      prev_arg)
    if tps == 0:
        last = last.reshape(n_seq, seq_rows, Fd)[:, seq_rows - 2:]
    else:
        last = last[tps - 1::tps, 6:8]
    return act, last


def _unit(slots, k_tile, outs, q2=None, bias=None, sinks=()):
    return (tuple(slots), k_tile, tuple(outs), q2, bias, tuple(sinks))


def _attn2_kernel(*refs, tq, nq, units, mode, win, sk, sk_valid, n_prev, masked, qbase, kbase, scale,
                  has_q2, has_bias, has_sinks, sinks_per_group, bias_is_mask, chunked):
    q_ref, k_ref, v_ref = refs[0], refs[1], refs[2]
    pos = 3
    q2_ref = k2_ref = bias_ref = sinks_ref = None
    if has_q2:
        q2_ref, k2_ref = refs[pos], refs[pos + 1]
        pos += 2
    if has_bias:
        bias_ref = refs[pos]
        pos += 1
    if has_sinks:
        sinks_ref = refs[pos]
        pos += 1
    o_ref = refs[pos]

    g = pl.program_id(1)
    i = pl.program_id(2)
    q0 = i * tq
    lane = lax.broadcasted_iota(I32, (tq, LANES), 1)
    half_mask = {"lo": jnp.where(lane < 64, 1.0, 0.0).astype(BF16),
                 "hi": jnp.where(lane >= 64, 1.0, 0.0).astype(BF16)}
    qc1 = (qbase + q0 + lax.broadcasted_iota(I32, (tq, 1), 0)) >> CHUNK_SHIFT
    scratch_refs = refs[pos + 1:pos + 4]

    def _write_outs(o, outs):
        for spec in outs:
            t = spec[0]
            if spec[1] == "slot":
                ot = o[spec[2] * tq:(spec[2] + 1) * tq]
            else:
                ot = jnp.where(lane < 64, o[spec[2] * tq:(spec[2] + 1) * tq],
                               o[spec[3] * tq:(spec[3] + 1) * tq])
            o_ref[:, t * LANES:(t + 1) * LANES] = ot.astype(o_ref.dtype)

    def compute(r0, L, mask_from=0):
        if masked:
            krow = r0 + mask_from + lax.broadcasted_iota(I32, (1, L - mask_from), 1)
            kc = (kbase + krow) >> CHUNK_SHIFT
        rows = pl.ds(r0, L) if not isinstance(r0, int) else slice(r0, r0 + L)
        for u, (slots, k_tile, outs, q2, bias_idx, sink_idx) in enumerate(units):
            ns = len(slots)
            qs = []
            for (t, mk) in slots:
                qt = q_ref[:, t * LANES:(t + 1) * LANES]
                if mk is not None:
                    qt = qt * half_mask[mk]
                if q2 is not None:
                    q2t = q2_ref[:, q2[0] * LANES:(q2[0] + 1) * LANES] * half_mask[q2[1]]
                    qt = jnp.concatenate([qt, q2t], axis=1)
                qs.append(qt)
            Q = qs[0] if ns == 1 else jnp.concatenate(qs, axis=0)
            kb = k_ref[rows, k_tile * LANES:(k_tile + 1) * LANES]
            if q2 is not None:
                kb = jnp.concatenate([kb, k2_ref[rows, :]], axis=1)
            vb = v_ref[rows, k_tile * LANES:(k_tile + 1) * LANES]
            log2 = bias_idx is None or bias_is_mask
            expf = jnp.exp2 if log2 else jnp.exp
            if chunked:
                s_ref, p_ref, l_ref = scratch_refs
                R = ns * tq
                ch = min(SOFTMAX_ROWS, tq)
                s_ref[u, 0:R, 0:L] = lax.dot_general(Q, kb, (((1,), (1,)), ((), ())),
                                                     preferred_element_type=F32)
                for c in range(R // ch):
                    rr, qr = c * ch, (c * ch) % tq
                    s = s_ref[u, rr:rr + ch, 0:L] * (scale * LOG2E if log2 else scale)
                    if bias_idx is not None:
                        s = s + bias_ref[0, bias_idx, qr:qr + ch, :]
                    if masked:
                        qc = (qbase + q0 + qr + lax.broadcasted_iota(I32, (ch, 1), 0)) >> CHUNK_SHIFT
                        s = jnp.where(kc <= qc, s, NEG_INF)
                        if n_prev is not None:
                            s = jnp.where(kc >= qc - n_prev, s, NEG_INF)
                        if sk_valid < sk:
                            s = jnp.where(krow < sk_valid, s, NEG_INF)
                    m = jnp.max(s, axis=-1, keepdims=True)
                    if sink_idx:
                        sink = sinks_ref[g * sinks_per_group + sink_idx[rr // tq]]
                        sink = sink * LOG2E if log2 else sink
                        m = jnp.maximum(m, sink)
                    p = expf(s - m)
                    l = jnp.sum(p, axis=-1, keepdims=True)
                    if sink_idx:
                        l = l + expf(sink - m)
                    l_ref[u, rr:rr + ch, :] = jnp.broadcast_to(l, (ch, LANES))
                    p_ref[u, rr:rr + ch, 0:L] = p.astype(BF16)
                o = jnp.dot(p_ref[u, 0:R, 0:L], vb, preferred_element_type=F32) / l_ref[u, 0:R, :]
                _write_outs(o, outs)
                continue
            s = lax.dot_general(Q, kb, (((1,), (1,)), ((), ())), preferred_element_type=F32) * (
                scale * LOG2E if log2 else scale)
            if bias_idx is not None:
                bt = bias_ref[0, bias_idx]
                s = s + (bt if ns == 1 else jnp.concatenate([bt] * ns, axis=0))
            if masked:
                qc = qc1 if ns == 1 else jnp.concatenate([qc1] * ns, axis=0)
                sm = s[:, mask_from:]
                sm = jnp.where(kc <= qc, sm, NEG_INF)
                if n_prev is not None:
                    sm = jnp.where(kc >= qc - n_prev, sm, NEG_INF)
                if sk_valid < sk:
                    sm = jnp.where(krow < sk_valid, sm, NEG_INF)
                s = sm if mask_from == 0 else jnp.concatenate([s[:, :mask_from], sm], axis=1)
            m = jnp.max(s, axis=-1, keepdims=True)
            if sink_idx:
                sk_rows = [jnp.full((tq, 1), sinks_ref[g * sinks_per_group + si], F32) for si in sink_idx]
                sink = sk_rows[0] if ns == 1 else jnp.concatenate(sk_rows, axis=0)
                sink = sink * LOG2E if log2 else sink
                m = jnp.maximum(m, sink)
            p = expf(s - m)
            l = jnp.sum(p, axis=-1, keepdims=True)
            if sink_idx:
                l = l + expf(sink - m)
            o = jnp.dot(p.astype(BF16), vb, preferred_element_type=F32) / l
            _write_outs(o, outs)

    if mode == "window":
        lo_row = ((qbase + q0) >> CHUNK_SHIFT) * CHUNK - n_prev * CHUNK - kbase
        compute(pl.multiple_of(jnp.clip(lo_row, 0, sk - win), 16), win)
    elif mode == "full":
        compute(0, sk)
    else:
        for c in range(nq):
            pl.when(i == c)(functools.partial(compute, 0, min(sk, (c + 1) * tq), c * tq))


def attention2(q, k, v, *, B, Sq, Sk, sk_valid, tq, G, qw, kw, ow, units, scale, mode, win=None,
               n_prev=None, masked=True, qbase=0, kbase=0, q2=None, q2w=0, k2=None, bias=None,
               bias_is_mask=False, sinks=None, sinks_per_group=0, chunked=False, name="attn"):
    nq = Sq // tq
    assert mode != "causal" or (qbase == 0 and kbase == 0 and tq % CHUNK == 0)
    in_specs = [pl.BlockSpec((tq, qw * LANES), lambda b, g, i: (b * nq + i, g)),
                pl.BlockSpec((Sk, kw * LANES), lambda b, g, i: (b, g)),
                pl.BlockSpec((Sk, kw * LANES), lambda b, g, i: (b, g))]
    args = [q, k, v]
    blk = tq * qw * LANES * 2 + 2 * Sk * kw * LANES * 2 + tq * ow * LANES * 2
    if q2 is not None:
        in_specs += [pl.BlockSpec((tq, q2w * LANES), lambda b, g, i: (b * nq + i, g)),
                     pl.BlockSpec((Sk, LANES), lambda b, g, i: (b, 0))]
        args += [q2, k2]
        blk += tq * q2w * LANES * 2 + Sk * LANES * 2
    if bias is not None:
        nv, nbh = bias.shape[0], bias.shape[1] // G
        in_specs.append(pl.BlockSpec((1, nbh, tq, win),
                                     lambda b, g, i, nv=nv: (jnp.minimum(i, nv - 1), g, 0, 0)))
        args.append(bias)
        blk += nbh * tq * win * 4
    if sinks is not None:
        in_specs.append(pl.BlockSpec(memory_space=pltpu.SMEM))
        args.append(sinks.astype(F32))
    L = win if mode == "window" else Sk
    temp = sum(len(u[0]) for u in units) * tq * L * 12
    kern = functools.partial(
        _attn2_kernel, tq=tq, nq=nq, units=tuple(units), mode=mode, win=win, sk=Sk, sk_valid=sk_valid,
        n_prev=n_prev, masked=masked, qbase=qbase, kbase=kbase, scale=scale, has_q2=q2 is not None,
        has_bias=bias is not None, has_sinks=sinks is not None, sinks_per_group=sinks_per_group,
        bias_is_mask=bias_is_mask, chunked=chunked)
    scratch = []
    if chunked:
        assert mode != "causal"
        nu, r_max = len(units), max(len(u[0]) for u in units) * tq
        scratch = [pltpu.VMEM((nu, r_max, L), F32), pltpu.VMEM((nu, r_max, L), BF16),
                   pltpu.VMEM((nu, r_max, LANES), F32)]
    return pl.pallas_call(
        kern,
        grid=(B, G, nq),
        in_specs=in_specs,
        out_specs=pl.BlockSpec((tq, ow * LANES), lambda b, g, i: (b * nq + i, g)),
        out_shape=jax.ShapeDtypeStruct((B * Sq, G * ow * LANES), BF16),
        scratch_shapes=scratch,
        compiler_params=pltpu.CompilerParams(
            dimension_semantics=("parallel", "parallel", "arbitrary"),
            vmem_limit_bytes=_vmem_limit(blk, temp)),
        name=name,
    )(*args)


def _toeplitz_kernel(ab_ref, e_ref, o_ref, *, tq, win, n_prev, n_valid, has_table):
    v = pl.program_id(0)
    if has_table:
        wp = e_ref.shape[-1]
        x = jnp.broadcast_to(e_ref[0, 0, 0:1, :], (tq, wp))
        y = pltpu.roll(x, 0, 1, stride=1, stride_axis=0)[:, :win]
    else:
        y = jnp.zeros((tq, win), F32)
    qc = (ab_ref[v, 0] + lax.broadcasted_iota(I32, (tq, win), 0)) >> CHUNK_SHIFT
    kj = lax.broadcasted_iota(I32, (tq, win), 1)
    kc = (ab_ref[v, 1] + kj) >> CHUNK_SHIFT
    y = jnp.where(kc <= qc, y, NEG_INF)
    y = jnp.where(kc >= qc - n_prev, y, NEG_INF)
    if n_valid < win:
        y = jnp.where(kj < n_valid, y, NEG_INF)
    o_ref[0, 0] = y


def band_bias(table, tq, win, pos, n_prev, n_valid):
    nv = len(pos)
    ab = np.array([[q - (k // CHUNK) * CHUNK, k - (k // CHUNK) * CHUNK] for q, k in pos], np.int32)
    has_table = table is not None
    H = table.shape[1] if has_table else 1
    wp = _round_up(win + tq - 1, LANES)
    if has_table:
        m = np.arange(wp)
        m = np.where(m < win, m, m - wp)
        idx = np.stack([np.clip(q - k - m, -MAX_REL, MAX_REL) + MAX_REL for q, k in pos])
        e = jnp.transpose(table.astype(F32)[idx], (0, 2, 1))
        e = jnp.broadcast_to(e[:, :, None, :], (nv, H, 8, wp))
    else:
        e = jnp.zeros((nv, H, 8, wp), F32)
    return pl.pallas_call(
        functools.partial(_toeplitz_kernel, tq=tq, win=win, n_prev=n_prev, n_valid=n_valid,
                          has_table=has_table),
        grid=(nv, H),
        in_specs=[pl.BlockSpec(memory_space=pltpu.SMEM),
                  pl.BlockSpec((1, 1, 8, wp), lambda v, h: (v, h, 0, 0))],
        out_specs=pl.BlockSpec((1, 1, tq, win), lambda v, h: (v, h, 0, 0)),
        out_shape=jax.ShapeDtypeStruct((nv, H, tq, win), F32),
        compiler_params=pltpu.CompilerParams(dimension_semantics=("parallel", "parallel")),
        name="band_bias",
    )(jnp.asarray(ab), e)


def _dsa2_kernel(q_ref, k_ref, v_ref, iq_ref, ik_ref, iw_ref, o_ref, key_ref, mb_ref, m_ref, l_ref,
                 acc_ref, alpha_ref, s_ref, p_ref, *, tq, tk, sk, sk_valid, qbase, n_sel, scale):
    i = pl.program_id(1)
    q0 = i * tq
    qc = (qbase + q0 + lax.broadcasted_iota(I32, (tq, 1), 0)) >> CHUNK_SHIFT
    hi_row = (((qbase + q0 + tq - 1) >> CHUNK_SHIFT) + 1) * CHUNK
    nb = (jnp.minimum(hi_row, sk_valid) + tk - 1) // tk
    lane = lax.broadcasted_iota(I32, (tq, LANES), 1)
    half_mask = (jnp.where(lane < 64, 1.0, 0.0).astype(BF16), jnp.where(lane >= 64, 1.0, 0.0).astype(BF16))
    n_idx_tiles = iq_ref.shape[1] // LANES
    n_lane_tiles = tk // LANES

    def col_of(jb):
        return jb * tk + lax.broadcasted_iota(I32, (1, tk), 1)

    def mask_invisible(x, col):
        x = jnp.where((col >> CHUNK_SHIFT) <= qc, x, NEG_INF)
        if sk_valid < sk:
            x = jnp.where(col < sk_valid, x, NEG_INF)
        return x

    def lane_fold(x):
        acc = x[:, 0:LANES]
        for t in range(1, n_lane_tiles):
            acc = acc + x[:, t * LANES:(t + 1) * LANES]
        return acc

    def score_body(jb, carry):
        r0 = pl.multiple_of(jb * tk, 16)
        ikb = ik_ref[pl.ds(r0, tk), :]
        acc = jnp.zeros((tq, tk), F32)
        for p in range(n_idx_tiles):
            t = iq_ref[:, p * LANES:(p + 1) * LANES]
            for hf in range(2):
                lg = lax.dot_general(t * half_mask[hf], ikb, (((1,), (1,)), ((), ())),
                                     preferred_element_type=F32)
                h = 2 * p + hf
                acc = acc + iw_ref[:, h:h + 1] * jnp.maximum(lg, 0.0)
        sc = mask_invisible(acc, col_of(jb))
        bits = lax.bitcast_convert_type(sc, I32)
        key_ref[jb] = bits ^ ((bits >> 31) & 0x7FFFFFFF)
        return carry

    lax.fori_loop(0, nb, score_body, 0)

    def count(hit):
        def cb(jb, c):
            return c + lane_fold(hit(key_ref[jb], jb))
        part = lax.fori_loop(0, nb, cb, jnp.zeros((tq, LANES), F32))
        return jnp.sum(part, axis=1, keepdims=True)

    def bit_body(it, cand):
        trial = cand | (jnp.int32(1) << (31 - it))
        trial_s = trial ^ INT_MIN
        cnt = count(lambda kk, jb: jnp.where(kk >= trial_s, 1.0, 0.0))
        return jnp.where(cnt >= n_sel, trial, cand)

    cand = lax.fori_loop(0, 32, bit_body, jnp.zeros((tq, 1), I32))
    thr = cand ^ INT_MIN

    cnt_gt = count(lambda kk, jb: jnp.where(kk > thr, 1.0, 0.0))
    n_eq = count(lambda kk, jb: jnp.where(kk == thr, 1.0, 0.0))
    need = n_sel - cnt_gt
    neg_bits = int(np.float32(NEG_INF).view(np.int32))
    neg_key = neg_bits ^ ((neg_bits >> 31) & 0x7FFFFFFF)
    tie = jnp.where(thr > neg_key, jnp.where(n_eq > need, 1.0, 0.0), 0.0)
    any_tie = jnp.max(tie) > 0.0

    def tie_cols():
        def jbit(it, cj):
            trial = cj | (jnp.int32(1) << (30 - it))
            c = count(lambda kk, jb: jnp.where(kk == thr, jnp.where(col_of(jb) < trial, 1.0, 0.0), 0.0))
            return jnp.where(c < need, trial, cj)
        return lax.fori_loop(0, 31, jbit, jnp.zeros((tq, 1), I32))

    jmax = lax.cond(any_tie, tie_cols, lambda: jnp.full((tq, 1), BIG_COL, I32))

    def mask_body(jb, carry):
        kk = key_ref[jb]
        col = col_of(jb)
        sel = jnp.where(kk > thr, 0.0, jnp.where(kk == thr, jnp.where(col <= jmax, 0.0, NEG_INF), NEG_INF))
        mb_ref[jb] = mask_invisible(sel, col)
        return carry

    lax.fori_loop(0, nb, mask_body, 0)

    n_kv = k_ref.shape[1] // LANES
    grp = (q_ref.shape[1] // LANES) // n_kv
    R = grp * tq
    m_ref[...] = jnp.full(m_ref.shape, M_INIT, F32)
    l_ref[...] = jnp.zeros(l_ref.shape, F32)
    acc_ref[...] = jnp.zeros(acc_ref.shape, F32)

    ch = min(SOFTMAX_ROWS, tq)
    n_ch = R // ch

    def body(jb, carry):
        r0 = pl.multiple_of(jb * tk, 16)
        for kvh in range(n_kv):
            Q = jnp.concatenate(
                [q_ref[:, (kvh * grp + gi) * LANES:(kvh * grp + gi + 1) * LANES] for gi in range(grp)], axis=0)
            kb = k_ref[pl.ds(r0, tk), kvh * LANES:(kvh + 1) * LANES]
            vb = v_ref[pl.ds(r0, tk), kvh * LANES:(kvh + 1) * LANES]
            s_ref[kvh] = lax.dot_general(Q, kb, (((1,), (1,)), ((), ())), preferred_element_type=F32)

            def rows_body(c, carry2, kvh=kvh):
                rr = pl.multiple_of(c * ch, ch)
                qr = pl.multiple_of((c * ch) % tq, ch)
                s = s_ref[kvh, pl.ds(rr, ch), :] * (scale * LOG2E) + mb_ref[jb, pl.ds(qr, ch), :]
                m_old = m_ref[kvh, pl.ds(rr, ch), :]
                m_new = jnp.maximum(m_old, jnp.max(s, axis=-1, keepdims=True))
                p = jnp.exp2(s - jnp.concatenate([m_new] * n_lane_tiles, axis=1))
                alpha = jnp.exp2(m_old - m_new)
                l_ref[kvh, pl.ds(rr, ch), :] = alpha * l_ref[kvh, pl.ds(rr, ch), :] + jnp.sum(
                    p, axis=-1, keepdims=True)
                m_ref[kvh, pl.ds(rr, ch), :] = m_new
                alpha_ref[kvh, pl.ds(rr, ch), :] = alpha
                p_ref[kvh, pl.ds(rr, ch), :] = p.astype(BF16)
                return carry2

            lax.fori_loop(0, n_ch, rows_body, 0, unroll=n_ch)
            acc_ref[kvh] = alpha_ref[kvh] * acc_ref[kvh] + jnp.dot(p_ref[kvh], vb,
                                                                   preferred_element_type=F32)
        return carry

    lax.fori_loop(0, nb, body, 0)
    for kvh in range(n_kv):
        o = acc_ref[kvh] / l_ref[kvh]
        for gi in range(grp):
            hh = kvh * grp + gi
            o_ref[:, hh * LANES:(hh + 1) * LANES] = o[gi * tq:(gi + 1) * tq].astype(o_ref.dtype)


def dsa_attention2(q, k, v, iq, ik2, iw, *, B, Sq, Sk, sk_valid, tq, tk, qbase, n_sel):
    nq = Sq // tq
    nkb = Sk // tk
    Hq = q.shape[1]
    Hk = k.shape[1]
    n_kv = Hk // LANES
    R = (Hq // Hk) * tq
    kern = functools.partial(_dsa2_kernel, tq=tq, tk=tk, sk=Sk, sk_valid=sk_valid, qbase=qbase,
                             n_sel=n_sel, scale=DSA_HEAD_DIM ** -0.5)
    blk = tq * (Hq + iq.shape[1]) * 2 + tq * LANES * 4 + Sk * (2 * Hk + LANES) * 2 + tq * Hq * 2
    scratch = 2 * nkb * tq * tk * 4 + 3 * n_kv * R * LANES * 4
    return pl.pallas_call(
        kern,
        grid=(B, nq),
        in_specs=[pl.BlockSpec((tq, Hq), lambda b, i: (b * nq + i, 0)),
                  pl.BlockSpec((Sk, Hk), lambda b, i: (b, 0)),
                  pl.BlockSpec((Sk, Hk), lambda b, i: (b, 0)),
                  pl.BlockSpec((tq, iq.shape[1]), lambda b, i: (b * nq + i, 0)),
                  pl.BlockSpec((Sk, LANES), lambda b, i: (b, 0)),
                  pl.BlockSpec((tq, LANES), lambda b, i: (b * nq + i, 0))],
        out_specs=pl.BlockSpec((tq, Hq), lambda b, i: (b * nq + i, 0)),
        out_shape=jax.ShapeDtypeStruct((B * Sq, Hq), BF16),
        scratch_shapes=[pltpu.VMEM((nkb, tq, tk), I32), pltpu.VMEM((nkb, tq, tk), F32),
                        pltpu.VMEM((n_kv, R, LANES), F32), pltpu.VMEM((n_kv, R, LANES), F32),
                        pltpu.VMEM((n_kv, R, LANES), F32), pltpu.VMEM((n_kv, R, LANES), F32),
                        pltpu.VMEM((n_kv, R, tk), F32), pltpu.VMEM((n_kv, R, tk), BF16)],
        compiler_params=pltpu.CompilerParams(
            dimension_semantics=("parallel", "arbitrary"),
            vmem_limit_bytes=_vmem_limit(blk, scratch + n_kv * R * (LANES * 4 + tk * 6) + (8 << 20))),
        name="dsa_attention",
    )(q, k, v, iq, ik2, iw)


def _pack_cache_kernel(*refs, rows, n_new, n_heads):
    cache_ref, o_ref = refs[0], refs[-1]
    dh = cache_ref.shape[3]
    for h in range(n_heads):
        o_ref[0:rows, h * dh:(h + 1) * dh] = cache_ref[0, :, h, :].astype(o_ref.dtype)
    if n_new:
        o_ref[rows:rows + n_new, :] = refs[1][...].astype(o_ref.dtype)
    n_pad = o_ref.shape[0] - rows - n_new
    if n_pad:
        o_ref[rows + n_new:, :] = jnp.zeros((n_pad, o_ref.shape[1]), o_ref.dtype)


def pack_cache(cache, new, rows_pad, layer=None):
    B, rows, H, dh = cache.shape[-4:]
    n_new = 0 if new is None else new.shape[0] // B
    if layer is None:
        in_specs = [pl.BlockSpec((1, rows, H, dh), lambda b: (b, 0, 0, 0))]
    else:
        in_specs = [pl.BlockSpec((None, 1, rows, H, dh), lambda b: (layer, b, 0, 0, 0))]
    args = [cache]
    if n_new:
        in_specs.append(pl.BlockSpec((n_new, H * dh), lambda b: (b, 0)))
        args.append(new)
    return pl.pallas_call(
        functools.partial(_pack_cache_kernel, rows=rows, n_new=n_new, n_heads=H),
        grid=(B,),
        in_specs=in_specs,
        out_specs=pl.BlockSpec((rows_pad, H * dh), lambda b: (b, 0)),
        out_shape=jax.ShapeDtypeStruct((B * rows_pad, H * dh), BF16),
        compiler_params=pltpu.CompilerParams(
            dimension_semantics=("parallel",),
            vmem_limit_bytes=_vmem_limit(rows * H * dh * 4 + rows_pad * H * dh * 2, rows * H * dh * 4)),
        name="pack_cache",
    )(*args)


def _rope_tables(pos, rot, group):
    width = 2 * LANES
    half = rot // 2
    inv = ROPE_THETA ** (-jnp.arange(half, dtype=F32) / half)
    ang = pos.astype(F32)[:, None] * inv[None, :]
    c, s = jnp.cos(ang), jnp.sin(ang)
    lane = np.arange(width) % group
    is_lo = lane < half
    is_hi = (lane >= half) & (lane < rot)
    idx = np.where(is_lo, lane, np.where(is_hi, lane - half, 0))
    cg, sg = c[:, idx], s[:, idx]
    C = jnp.where(is_lo | is_hi, cg, 1.0)
    S = jnp.where(is_lo, -sg, jnp.where(is_hi, sg, 0.0))
    perm = np.zeros((width, width), np.float32)
    dst = np.arange(width)
    src = np.where(is_lo, dst + half, dst - half)
    perm[src[is_lo | is_hi], dst[is_lo | is_hi]] = 1.0
    return half, C, S, jnp.asarray(perm, BF16)


def _pad_cols(w, n):
    return jnp.pad(w, ((0, 0), (0, n - w.shape[1])))


def _tile_gain(g, n):
    return jnp.tile(g.astype(F32), n // g.shape[0])


def _with_cache(cache, new, B, rows_pad, dtype=F32):
    n = new.shape[0] // B
    P = cache.shape[1]
    parts = [cache.reshape(B, P, -1).astype(dtype), new.reshape(B, n, -1).astype(dtype)]
    if rows_pad > P + n:
        parts.append(jnp.zeros((B, rows_pad - P - n, new.shape[1]), dtype))
    return jnp.concatenate(parts, axis=1).reshape(B * rows_pad, -1)


def _round_up(x, m):
    return (x + m - 1) // m * m


def _window_variants(tq, n_prev):
    return [(t * tq, 0) for t in range(n_prev * CHUNK // tq)] + [(n_prev * CHUNK, 0)]


def _mla_weights(w_in, q_lora_norm, kv_lora_norm, w_uq, w_ukv, q_norm, k_norm):
    H = MLA_HEADS
    wq3 = w_uq.reshape(MLA_Q_LORA, H, MLA_NOPE + MLA_ROPE)
    wkv3 = w_ukv.reshape(MLA_KV_LORA, H, MLA_NOPE + MLA_V)
    return dict(
        w_cq=w_in[:, :MLA_Q_LORA].astype(BF16),
        w_ckv=w_in[:, MLA_Q_LORA:MLA_Q_LORA + MLA_KV_LORA].astype(BF16),
        w_kr=_pad_cols(w_in[:, MLA_Q_LORA + MLA_KV_LORA:], LANES).astype(BF16),
        g_cq=q_lora_norm, g_ckv=kv_lora_norm,
        g_kr=_tile_gain(k_norm[MLA_NOPE:], LANES),
        w_qn=wq3[:, :, :MLA_NOPE].reshape(MLA_Q_LORA, H * MLA_NOPE).astype(BF16),
        w_qr=wq3[:, :, MLA_NOPE:].reshape(MLA_Q_LORA, H * MLA_ROPE).astype(BF16),
        g_qn=_tile_gain(q_norm[:MLA_NOPE], H * MLA_NOPE),
        g_qr=_tile_gain(q_norm[MLA_NOPE:], H * MLA_ROPE),
        w_kn=wkv3[:, :, :MLA_NOPE].reshape(MLA_KV_LORA, H * MLA_NOPE).astype(BF16),
        w_v=wkv3[:, :, MLA_NOPE:].reshape(MLA_KV_LORA, H * MLA_V).astype(BF16),
        g_kn=_tile_gain(k_norm[:MLA_NOPE], H * MLA_NOPE),
    )


def _mla_project(h, W, rope64):
    cq = mm(h, W["w_cq"], gain=W["g_cq"], gs=MLA_Q_LORA, norm=True, out_dtype=BF16, name="mla_cq")
    ckv, ckv_bf = mm(h, W["w_ckv"], gain=W["g_ckv"], gs=MLA_KV_LORA, norm=True, out_dtype=(F32, BF16),
                     name="mla_ckv")
    kr = mm(h, W["w_kr"], gain=W["g_kr"], gs=64, norm=True, rope=rope64, name="mla_kr")
    qn = mm(cq, W["w_qn"], gain=W["g_qn"], gs=LANES, norm=True, out_dtype=BF16, name="mla_qn")
    qr = mm(cq, W["w_qr"], gain=W["g_qr"], gs=64, norm=True, rope=rope64, out_dtype=BF16, name="mla_qr")
    return qn, qr, ckv, ckv_bf, kr


def _mla_expand(ckv_bf, W):
    kn = mm(ckv_bf, W["w_kn"], gain=W["g_kn"], gs=LANES, norm=True, out_dtype=BF16, name="mla_kn")
    v = mm(ckv_bf, W["w_v"], out_dtype=BF16, name="mla_v")
    return kn, v


def _mla_attend(qn, qr, kn, kr, v, *, B, Sq, Sk, sk_valid, tq, hps, mode, qbase):
    kr2 = jnp.concatenate([kr[:, :MLA_ROPE], kr[:, :MLA_ROPE]], axis=1).astype(BF16)
    units = tuple(_unit(((h, None),), h, ((h, "slot", 0),), q2=(h // 2, "lo" if h % 2 == 0 else "hi"))
                  for h in range(hps))
    return attention2(qn, kn, v, B=B, Sq=Sq, Sk=Sk, sk_valid=sk_valid, tq=tq, G=MLA_HEADS // hps,
                      qw=hps, kw=hps, ow=hps, units=units, scale=(MLA_NOPE + MLA_ROPE) ** -0.5,
                      mode=mode, qbase=qbase, q2=qr, q2w=hps // 2, k2=kr2, name="mla_attn")


def kernel(x_prompt, x_sample, mem_prompt, cache_a_ckv, cache_a_krope, cache_b_k, cache_b_v, cache_c_k, cache_c_v, cache_c_idxk, cache_d_k, cache_d_v, cache_mem_k, cache_mem_v, state_ffn_conv, norm_mix, norm_xattn, norm_mem, norm_ffn, a_w_in, a_q_lora_norm, a_kv_lora_norm, a_w_uq, a_w_ukv, a_q_norm, a_k_norm, a_w_out, b_w_in, b_q_norm, b_k_norm, b_rel_bias, b_w_out, c_w_in, c_q_norm, c_k_norm, c_idx_k_norm, c_w_out, d_w_in, d_q_norm, d_k_norm, d_sinks, d_w_out, xa_w_q, xa_w_kv, xa_q_norm, xa_k_norm, xa_w_o, ffn_w_up, ffn_conv_w, ffn_conv_b, ffn_w_down):
    B, S, D = x_prompt.shape
    Bd, n, _ = x_sample.shape
    P = cache_a_ckv.shape[2]
    depth = norm_mix.shape[0]
    Fd = ffn_conv_b.shape[1]
    Mp, Ms = B * S, Bd * n
    pos_p = jnp.arange(S, dtype=I32)
    pos_s = jnp.tile(P + jnp.arange(n, dtype=I32), Bd)
    names = ('a_ckv_p', 'a_krope_p', 'b_k_p', 'b_v_p', 'c_k_p', 'c_v_p', 'c_idxk_p', 'd_k_p', 'd_v_p',
             'mem_k_p', 'mem_v_p', 'conv_p', 'a_ckv_s', 'a_krope_s', 'b_k_s', 'b_v_s', 'c_k_s', 'c_v_s',
             'c_idxk_s', 'd_k_s', 'd_v_s', 'conv_s')
    st = {name: [] for name in names}
    xp = x_prompt.reshape(Mp, D)
    xs = x_sample.reshape(Ms, D)
    tq_p = 128

    def rope_pair(rot, group):
        return _rope_tables(pos_p, rot, group), _rope_tables(pos_s, rot, group)

    for i in range(depth):
        m, j = i % 4, i // 4
        hp = rmsnorm_cast(xp, norm_mix[i])
        hs = rmsnorm_cast(xs, norm_mix[i])
        if m == 0:
            W = _mla_weights(a_w_in[j], a_q_lora_norm[j], a_kv_lora_norm[j], a_w_uq[j], a_w_ukv[j],
                             a_q_norm[j], a_k_norm[j])
            rp, rs = rope_pair(MLA_ROPE, 64)
            w_out = a_w_out[j].astype(BF16)
            qn, qr, ckv, ckv_bf, kr = _mla_project(hp, W, rp)
            kn, v = _mla_expand(ckv_bf, W)
            op = _mla_attend(qn, qr, kn, kr, v, B=B, Sq=S, Sk=S, sk_valid=S, tq=_pick(S, (256, 128)),
                             hps=4, mode="causal", qbase=0)
            xp = mm(op, w_out, res=xp, name="mix_out")
            st['a_ckv_p'].append(ckv.reshape(B, S, MLA_KV_LORA))
            st['a_krope_p'].append(kr[:, :MLA_ROPE].reshape(B, S, MLA_ROPE))
            qn, qr, ckv, ckv_bf, kr = _mla_project(hs, W, rs)
            sk_valid = P + n
            Skp = _round_up(sk_valid, LANES)
            ckv_all = _with_cache(cache_a_ckv[j], ckv_bf, Bd, Skp, BF16)
            kr_all = _with_cache(cache_a_krope[j], kr[:, :MLA_ROPE], Bd, Skp)
            kn, v = _mla_expand(ckv_all, W)
            os_ = _mla_attend(qn, qr, kn, kr_all, v, B=Bd, Sq=n, Sk=Skp, sk_valid=sk_valid, tq=n,
                              hps=MLA_HEADS, mode="full", qbase=P)
            xs = mm(os_, w_out, res=xs, name="mix_out")
            st['a_ckv_s'].append(ckv.reshape(Bd, n, MLA_KV_LORA))
            st['a_krope_s'].append(kr[:, :MLA_ROPE].reshape(Bd, n, MLA_ROPE))
        elif m == 1:
            H, dh = BAND_HEADS, D // BAND_HEADS
            w_in = b_w_in[j]
            wq, wk, wv = (w_in[:, t * H * dh:(t + 1) * H * dh].astype(BF16) for t in range(3))
            gq, gk = _tile_gain(b_q_norm[j], H * dh), _tile_gain(b_k_norm[j], H * dh)
            w_out = b_w_out[j].astype(BF16)
            npv = BAND_PREV_CHUNKS
            def band_units(nh):
                return tuple(_unit(((h, None),), h, ((h, "slot", 0),), bias=h) for h in range(nh))

            q = mm(hp, wq, gain=gq, gs=dh, norm=True, out_dtype=BF16, name="band_q")
            k, k_bf = mm(hp, wk, gain=gk, gs=dh, norm=True, out_dtype=(F32, BF16), name="band_k")
            v, v_bf = mm(hp, wv, out_dtype=(F32, BF16), name="band_v")
            tq_b = _pick(S, (256, 128))
            win = npv * CHUNK + tq_b
            bias = band_bias(b_rel_bias[j], tq_b, win, _window_variants(tq_b, npv), npv, win)
            hps = 8
            op = attention2(q, k_bf, v_bf, B=B, Sq=S, Sk=S, sk_valid=S, tq=tq_b, G=H // hps, qw=hps,
                            kw=hps, ow=hps, units=band_units(hps), scale=dh ** -0.5, mode="window",
                            win=win, n_prev=npv, masked=False, bias=bias, name="band_attn")
            xp = mm(op, w_out, res=xp, name="mix_out")
            r = min(npv * CHUNK, S)
            st['b_k_p'].append(k.reshape(B, S, H, dh)[:, S - r:])
            st['b_v_p'].append(v.reshape(B, S, H, dh)[:, S - r:])
            q = mm(hs, wq, gain=gq, gs=dh, norm=True, out_dtype=BF16, name="band_q")
            k = mm(hs, wk, gain=gk, gs=dh, norm=True, name="band_k")
            v = mm(hs, wv, name="band_v")
            rows = cache_b_k.shape[2]
            sk_valid = rows + n
            Skp = _round_up(sk_valid, LANES)
            k_all = pack_cache(cache_b_k[j], k, Skp)
            v_all = pack_cache(cache_b_v[j], v, Skp)
            bias = band_bias(b_rel_bias[j], n, Skp, [(P, P - rows)], npv, sk_valid)
            os_ = attention2(q, k_all, v_all, B=Bd, Sq=n, Sk=Skp, sk_valid=sk_valid, tq=n, G=1, qw=H,
                             kw=H, ow=H, units=band_units(H), scale=dh ** -0.5, mode="window", win=Skp,
                             n_prev=npv, masked=False, qbase=P, kbase=P - rows, bias=bias,
                             name="band_attn")
            xs = mm(os_, w_out, res=xs, name="mix_out")
            st['b_k_s'].append(k.reshape(Bd, n, H, dh))
            st['b_v_s'].append(v.reshape(Bd, n, H, dh))
        elif m == 2:
            Hq, Hk, dh = DSA_HEADS, DSA_KV_HEADS, DSA_HEAD_DIM
            w_in = c_w_in[j]
            oq, ok_, ov = Hq * dh, Hq * dh + Hk * dh, Hq * dh + 2 * Hk * dh
            oiq = ov + IDX_HEADS * IDX_DIM
            oik = oiq + IDX_DIM
            wq = w_in[:, :oq].astype(BF16)
            wk = w_in[:, oq:ok_].astype(BF16)
            wv = w_in[:, ok_:ov].astype(BF16)
            wiq = w_in[:, ov:oiq].astype(BF16)
            wik = _pad_cols(w_in[:, oiq:oik], LANES).astype(BF16)
            wiw = _pad_cols(w_in[:, oik:], LANES).astype(BF16)
            gq, gk = _tile_gain(c_q_norm[j], oq), _tile_gain(c_k_norm[j], Hk * dh)
            gik = _tile_gain(c_idx_k_norm[j], LANES)
            giw = jnp.full((LANES,), IDX_HEADS ** -0.5 * IDX_DIM ** -0.5, F32)
            w_out = c_w_out[j].astype(BF16)
            rp128, rs128 = rope_pair(dh // 4, LANES)
            rp64, rs64 = rope_pair(IDX_DIM // 4, 64)

            def project(h, r128, r64):
                q = mm(h, wq, gain=gq, gs=dh, norm=True, rope=r128, out_dtype=BF16, name="dsa_q")
                k, k_bf = mm(h, wk, gain=gk, gs=dh, norm=True, rope=r128, out_dtype=(F32, BF16), name="dsa_k")
                v, v_bf = mm(h, wv, out_dtype=(F32, BF16), name="dsa_v")
                iq = mm(h, wiq, rope=r64, out_dtype=BF16, name="dsa_iq")
                ik = mm(h, wik, gain=gik, gs=64, norm=True, rope=r64, name="dsa_ik")
                iw = mm(h, wiw, gain=giw, name="dsa_iw")
                return q, k, v, k_bf, v_bf, iq, ik, iw

            q, k, v, k_bf, v_bf, iq, ik, iw = project(hp, rp128, rp64)
            ik2 = jnp.concatenate([ik[:, :IDX_DIM], ik[:, :IDX_DIM]], axis=1).astype(BF16)
            op = dsa_attention2(q, k_bf, v_bf, iq, ik2, iw, B=B, Sq=S, Sk=S, sk_valid=S,
                                tq=_pick(S, (256, 128)), tk=_pick(S, (512, 256, 128)), qbase=0,
                                n_sel=min(DSA_TOPK, S // 4))
            xp = mm(op, w_out, res=xp, name="mix_out")
            st['c_k_p'].append(k.reshape(B, S, Hk, dh))
            st['c_v_p'].append(v.reshape(B, S, Hk, dh))
            st['c_idxk_p'].append(ik[:, :IDX_DIM].reshape(B, S, IDX_DIM))
            q, k, v, _, _, iq, ik, iw = project(hs, rs128, rs64)
            sk_valid = P + n
            Skp = _round_up(sk_valid, LANES)
            k_all = pack_cache(cache_c_k[j], k, Skp)
            v_all = pack_cache(cache_c_v[j], v, Skp)
            ik_all = _with_cache(cache_c_idxk[j], ik[:, :IDX_DIM], Bd, Skp, BF16)
            ik2 = jnp.concatenate([ik_all, ik_all], axis=1)
            os_ = dsa_attention2(q, k_all, v_all, iq, ik2, iw, B=Bd, Sq=n, Sk=Skp, sk_valid=sk_valid,
                                 tq=n, tk=_pick(Skp, (512, 384, 256, 128)), qbase=P,
                                 n_sel=min(DSA_TOPK, sk_valid // 4))
            xs = mm(os_, w_out, res=xs, name="mix_out")
            st['c_k_s'].append(k.reshape(Bd, n, Hk, dh))
            st['c_v_s'].append(v.reshape(Bd, n, Hk, dh))
            st['c_idxk_s'].append(ik[:, :IDX_DIM].reshape(Bd, n, IDX_DIM))
        else:
            Hq, Hk, dh = SWA_HEADS, SWA_KV_HEADS, SWA_HEAD_DIM
            grp = Hq // Hk
            npairs = Hk // 2
            perm = np.array([[[2 * p * grp + t, (2 * p + 1) * grp + t] for t in range(grp)]
                             for p in range(npairs)]).reshape(-1)
            w_in = d_w_in[j]
            nqc, nkc = Hq * dh, Hk * dh
            wq = w_in[:, :nqc].reshape(D, Hq, dh)[:, perm].reshape(D, nqc).astype(BF16)
            wk = w_in[:, nqc:nqc + nkc].astype(BF16)
            wv = w_in[:, nqc + nkc:].astype(BF16)
            gq, gk = _tile_gain(d_q_norm[j], nqc), _tile_gain(d_k_norm[j], nkc)
            w_out = d_w_out[j].reshape(Hq, dh, D)[perm].reshape(nqc, D).astype(BF16)
            sinks = d_sinks[j][perm]
            rp, rs = rope_pair(dh // 4, 64)
            npv = SWA_PREV_CHUNKS
            def swa_units(npr):
                us = []
                for p in range(npr):
                    slots = tuple((p * grp + t, hf) for t in range(grp) for hf in ("lo", "hi"))
                    outs = tuple((p * grp + t, "pair", 2 * t, 2 * t + 1) for t in range(grp))
                    us.append(_unit(slots, p, outs, sinks=range(2 * grp * p, 2 * grp * (p + 1))))
                return tuple(us)

            def project(h, r):
                q = mm(h, wq, gain=gq, gs=64, norm=True, rope=r, out_dtype=BF16, name="swa_q")
                k, k_bf = mm(h, wk, gain=gk, gs=64, norm=True, rope=r, out_dtype=(F32, BF16), name="swa_k")
                v, v_bf = mm(h, wv, out_dtype=(F32, BF16), name="swa_v")
                return q, k, v, k_bf, v_bf

            q, k, v, k_bf, v_bf = project(hp, rp)
            tq_d = tq_p
            win = npv * CHUNK + tq_d
            pps = npairs
            op = attention2(q, k_bf, v_bf, B=B, Sq=S, Sk=S, sk_valid=S, tq=tq_d, G=npairs // pps,
                            qw=pps * grp, kw=pps, ow=pps * grp, units=swa_units(pps), scale=dh ** -0.5,
                            mode="window", win=win, n_prev=npv, sinks=sinks,
                            sinks_per_group=2 * grp * pps, chunked=True, name="swa_attn")
            xp = mm(op, w_out, res=xp, name="mix_out")
            r = min(npv * CHUNK, S)
            st['d_k_p'].append(k.reshape(B, S, Hk, dh)[:, S - r:])
            st['d_v_p'].append(v.reshape(B, S, Hk, dh)[:, S - r:])
            q, k, v, _, _ = project(hs, rs)
            rows = cache_d_k.shape[2]
            sk_valid = rows + n
            Skp = _round_up(sk_valid, LANES)
            k_all = _with_cache(cache_d_k[j].reshape(Bd, rows, nkc), k, Bd, Skp, BF16)
            v_all = _with_cache(cache_d_v[j].reshape(Bd, rows, nkc), v, Bd, Skp, BF16)
            os_ = attention2(q, k_all, v_all, B=Bd, Sq=n, Sk=Skp, sk_valid=sk_valid, tq=n, G=1,
                             qw=npairs * grp, kw=npairs, ow=npairs * grp, units=swa_units(npairs),
                             scale=dh ** -0.5, mode="window", win=Skp, n_prev=npv, qbase=P,
                             kbase=P - rows, sinks=sinks, sinks_per_group=2 * grp * npairs,
                             name="swa_attn")
            xs = mm(os_, w_out, res=xs, name="mix_out")
            st['d_k_s'].append(k.reshape(Bd, n, Hk, dh))
            st['d_v_s'].append(v.reshape(Bd, n, Hk, dh))

        Hx, dx = XA_HEADS, XA_HEAD_DIM
        Mm = mem_prompt.shape[1]
        hm = rmsnorm_cast(mem_prompt.reshape(B * Mm, D), norm_mem[i])
        w_kv = xa_w_kv[i]
        mk, mk_bf = mm(hm, w_kv[:, :Hx * dx].astype(BF16), gain=_tile_gain(xa_k_norm[i], Hx * dx), gs=dx,
                       norm=True, out_dtype=(F32, BF16), name="xa_k")
        mv, mv_bf = mm(hm, w_kv[:, Hx * dx:].astype(BF16), out_dtype=(F32, BF16), name="xa_v")
        st['mem_k_p'].append(mk.reshape(B, Mm, Hx, dx))
        st['mem_v_p'].append(mv.reshape(B, Mm, Hx, dx))
        wxq = xa_w_q[i].astype(BF16)
        gxq = _tile_gain(xa_q_norm[i], Hx * dx)
        wxo = xa_w_o[i].astype(BF16)
        xa_units = tuple(_unit(((h, None),), h, ((h, "slot", 0),)) for h in range(Hx))

        def cross(x, mk2, mv2, Bx, Sx, tq):
            qx = mm(x, wxq, in_gain=norm_xattn[i], gain=gxq, gs=dx, norm=True, out_dtype=BF16, name="xa_q")
            ox = attention2(qx, mk2, mv2, B=Bx, Sq=Sx, Sk=Mm, sk_valid=Mm, tq=tq, G=1, qw=Hx, kw=Hx,
                            ow=Hx, units=xa_units, scale=dx ** -0.5, mode="full", masked=False,
                            name="xattn")
            return mm(ox, wxo, res=x, name="xa_out")

        xp = cross(xp, mk_bf, mv_bf, B, S, _pick(S, (512, 256, 128)))
        xs = cross(xs, pack_cache(cache_mem_k, None, Mm, layer=i), pack_cache(cache_mem_v, None, Mm, layer=i),
                   Bd, n, n)

        w_up = ffn_w_up[i].astype(BF16)
        w_dn = ffn_w_down[i].astype(BF16)
        act, conv_p = ffn_up(xp, norm_ffn[i], w_up, ffn_conv_w[i], ffn_conv_b[i],
                             jnp.zeros((B, CONV_W - 1, Fd), F32), n_seq=B, seq_rows=S)
        xp = mm(act, w_dn, res=xp, name="ffn_down")
        act, conv_s = ffn_up(xs, norm_ffn[i], w_up, ffn_conv_w[i], ffn_conv_b[i], state_ffn_conv[i],
                             n_seq=Bd, seq_rows=n)
        xs = mm(act, w_dn, res=xs, name="ffn_down")
        st['conv_p'].append(conv_p)
        st['conv_s'].append(conv_s)

    order = ('a_ckv_p', 'a_krope_p', 'b_k_p', 'b_v_p', 'c_k_p', 'c_v_p', 'c_idxk_p', 'd_k_p', 'd_v_p',
             'mem_k_p', 'mem_v_p', 'conv_p', 'a_ckv_s', 'a_krope_s', 'b_k_s', 'b_v_s', 'c_k_s', 'c_v_s',
             'c_idxk_s', 'd_k_s', 'd_v_s', 'conv_s')
    return (xp.reshape(B, S, D), xs.reshape(Bd, n, D)) + tuple(jnp.stack(st[nm]) for nm in order)
```
